```python
import jax
import jax.numpy as jnp
from jax import lax
import numpy as np

D_MODEL = 1024
BATCH = 8
SEQ = 4096
DEPTH = 1
DEC_BATCH = 16
DEC_SEQ = 16
PAST_LEN = 2048

CHUNK = 64
EPS = 1e-6
ROPE_BASE = 10000.0
RET_HEADS = 8
RET_DK = 64
RET_DV = 64
MLA_HEADS = 8
MLA_NOPE = 64
MLA_ROPE = 32
MLA_V = 64
MLA_Q_RANK = 256
MLA_KV_RANK = 256
Q_BLOCK = 128
RET_QK_W = RET_HEADS * RET_DK
RET_V_W = RET_HEADS * RET_DV
MLA_OUT_W = MLA_HEADS * MLA_V
MIX_W = RET_V_W + MLA_OUT_W
IN_COLS = 2 * RET_QK_W + 2 * RET_V_W + MLA_Q_RANK + MLA_KV_RANK + MLA_ROPE
PEER_HEADS = 8
PEER_N_KEYS = 128
PEER_N_EXPERTS = PEER_N_KEYS * PEER_N_KEYS
PEER_DK = 256
PEER_TOPK = 16
PEER_BLOCK = 256

kernel_name = 'hybrid_retention_mla_peer_stream'


def rms_norm(x, w):
    xf = x.astype(jnp.float32)
    y = xf * lax.rsqrt(jnp.mean(xf * xf, axis=-1, keepdims=True) + EPS)
    return (y * w.astype(jnp.float32)).astype(x.dtype)


def rope(x, pos):
    half = x.shape[-1] // 2
    inv = ROPE_BASE ** (-jnp.arange(half, dtype=jnp.float32) / half)
    ang = pos.astype(jnp.float32)[:, None] * inv[None, :]
    shape = (pos.shape[0],) + (1,) * (x.ndim - 3) + (half,)
    cos = jnp.cos(ang).reshape(shape)
    sin = jnp.sin(ang).reshape(shape)
    x1 = x[..., :half].astype(jnp.float32)
    x2 = x[..., half:].astype(jnp.float32)
    return jnp.concatenate([x1 * cos - x2 * sin, x2 * cos + x1 * sin], axis=-1).astype(x.dtype)


def ret_log_decay():
    return jnp.log(1.0 - jnp.exp2(-5.0 - jnp.arange(RET_HEADS, dtype=jnp.float32)))


def retention_chunk(state, q, k, v):
    C = q.shape[2]
    lg = ret_log_decay()[:, None]
    idx = jnp.arange(C, dtype=jnp.float32)[None, :]
    intra = jnp.exp(lg[:, :, None] * jnp.abs(idx[:, :, None] - idx[:, None, :]))
    s = jnp.einsum('bhad,bhcd->bhac', q, k) * intra
    o = jnp.einsum('bhac,bhce->bhae', s, v)
    o = o + jnp.einsum('bhad,bhde->bhae', q * jnp.exp(lg * (idx + 1.0))[..., None], state)
    k_dec = k * jnp.exp(lg * (C - 1.0 - idx))[..., None]
    new_state = jnp.exp(lg * float(C))[..., None] * state + jnp.einsum('bhcd,bhce->bhde', k_dec, v)
    return o, new_state


def retention_prompt(q, k, v):
    B, H, S, _ = q.shape
    nc = S // CHUNK

    def to_chunks(t):
        return t.astype(jnp.float32).reshape(B, H, nc, CHUNK, t.shape[-1]).transpose(2, 0, 1, 3, 4)

    def step(state, qkv):
        o, new_state = retention_chunk(state, *qkv)
        return new_state, o

    state0 = jnp.zeros((B, H, RET_DK, RET_DV), jnp.float32)
    state, o = lax.scan(step, state0, (to_chunks(q), to_chunks(k), to_chunks(v)))
    o = o.transpose(1, 2, 0, 3, 4).reshape(B, H, S, RET_DV)
    return o, state


def mla_attend(q_lat, q_pe, c_kv, k_pe, mask):
    scale = (MLA_NOPE + MLA_ROPE) ** -0.5
    s = (jnp.einsum('bqhc,bkc->bhqk', q_lat, c_kv)
         + jnp.einsum('bqhr,bkr->bhqk', q_pe, k_pe)).astype(jnp.float32) * scale
    if mask is not None:
        s = jnp.where(mask, s, -jnp.inf)
    p = jax.nn.softmax(s, axis=-1).astype(c_kv.dtype)
    return jnp.einsum('bhqk,bkc->bqhc', p, c_kv)


def mla_prompt(q_lat, q_pe, c_kv, k_pe):
    B, S, H, R = q_lat.shape
    nb = S // Q_BLOCK
    k_chunk = jnp.arange(S) // CHUNK

    def one_block(args):
        ql, qp, qc = args
        mask = k_chunk[None, :] <= qc[:, None]
        return mla_attend(ql, qp, c_kv, k_pe, mask)

    ql_b = q_lat.reshape(B, nb, Q_BLOCK, H, R).transpose(1, 0, 2, 3, 4)
    qp_b = q_pe.reshape(B, nb, Q_BLOCK, H, MLA_ROPE).transpose(1, 0, 2, 3, 4)
    qc_b = k_chunk.reshape(nb, Q_BLOCK)
    o = lax.map(one_block, (ql_b, qp_b, qc_b))
    return o.transpose(1, 0, 2, 3, 4).reshape(B, S, H, R)


def mixer_project(xn, pos, w_in, q_norm_w, w_uq, kv_norm_w, w_uk):
    B, S, _ = xn.shape
    proj = xn @ w_in
    o1 = RET_QK_W
    o2 = o1 + RET_QK_W
    o3 = o2 + RET_V_W
    o4 = o3 + RET_V_W
    o5 = o4 + MLA_Q_RANK
    o6 = o5 + MLA_KV_RANK
    q_r, k_r, v_r, gate, c_q, c_kv, k_pe = jnp.split(proj, [o1, o2, o3, o4, o5, o6], axis=-1)
    q_r = rope(q_r.reshape(B, S, RET_HEADS, RET_DK), pos).transpose(0, 2, 1, 3)
    k_r = (rope(k_r.reshape(B, S, RET_HEADS, RET_DK), pos) * (RET_DK ** -0.5)).transpose(0, 2, 1, 3)
    v_r = v_r.reshape(B, S, RET_HEADS, RET_DV).transpose(0, 2, 1, 3)
    c_q = rms_norm(c_q, q_norm_w)
    q_m = jnp.einsum('bsc,chd->bshd', c_q, w_uq)
    q_nope = q_m[..., :MLA_NOPE]
    q_pe = rope(q_m[..., MLA_NOPE:], pos)
    q_lat = jnp.einsum('bshn,chn->bshc', q_nope, w_uk)
    c_kv = rms_norm(c_kv, kv_norm_w)
    k_pe = rope(k_pe, pos)
    return q_r, k_r, v_r, gate, q_lat, q_pe, c_kv, k_pe


def mixer_output(ret_o, gate, mla_lat, gn_w, w_uv, w_o):
    B, H, S, _ = ret_o.shape
    mu = jnp.mean(ret_o, axis=-1, keepdims=True)
    var = jnp.mean(jnp.square(ret_o - mu), axis=-1, keepdims=True)
    y = ((ret_o - mu) * lax.rsqrt(var + EPS)).transpose(0, 2, 1, 3).reshape(B, S, RET_V_W)
    y = (y * gn_w.astype(jnp.float32)).astype(gate.dtype) * jax.nn.silu(gate)
    mla_o = jnp.einsum('bshc,chn->bshn', mla_lat, w_uv).reshape(B, S, MLA_OUT_W)
    return jnp.concatenate([y, mla_o], axis=-1) @ w_o


def peer_ffn(xn, w_q, sub_keys, u_tab, v_tab):
    B, S, D = xn.shape
    xt = xn.reshape(B * S, D)
    n = B * S
    blk = min(PEER_BLOCK, n)
    n_pad = -(-n // blk) * blk
    xt = jnp.pad(xt, ((0, n_pad - n), (0, 0)))

    def one_block(xb):
        q = (xb @ w_q).reshape(blk, PEER_HEADS, 2, PEER_DK // 2)
        s = jnp.einsum('thpd,hpnd->thpn', q, sub_keys)
        top_s, top_i = lax.top_k(s, PEER_TOPK)
        cand = top_s[:, :, 0, :, None] + top_s[:, :, 1, None, :]
        cand_idx = top_i[:, :, 0, :, None] * PEER_N_KEYS + top_i[:, :, 1, None, :]
        cand = cand.reshape(blk, PEER_HEADS, PEER_TOPK * PEER_TOPK)
        cand_idx = cand_idx.reshape(blk, PEER_HEADS, PEER_TOPK * PEER_TOPK)
        best_s, best_j = lax.top_k(cand, PEER_TOPK)
        expert = jnp.take_along_axis(cand_idx, best_j, axis=-1)
        g = jax.nn.softmax(best_s.astype(jnp.float32), axis=-1).astype(xb.dtype)
        u = jnp.take(u_tab, expert, axis=0)
        a = jax.nn.gelu(jnp.einsum('td,thkd->thk', xb, u), approximate=False) * g
        v = jnp.take(v_tab, expert, axis=0)
        return jnp.einsum('thk,thkd->td', a, v)

    out = lax.map(one_block, xt.reshape(n_pad // blk, blk, D)).reshape(n_pad, D)[:n]
    return out.reshape(B, S, D)


def prompt_layer(x, lw):
    (ln1_w, w_in, ret_gn_w, q_norm_w, w_uq, kv_norm_w, w_uk, w_uv, w_o,
     ln2_w, peer_w_q, peer_sub_keys, peer_u, peer_v) = lw
    B, S, _ = x.shape
    pos = jnp.arange(S)
    q_r, k_r, v_r, gate, q_lat, q_pe, c_kv, k_pe = mixer_project(
        rms_norm(x, ln1_w), pos, w_in, q_norm_w, w_uq, kv_norm_w, w_uk)
    ret_o, ret_state = retention_prompt(q_r, k_r, v_r)
    mla_lat = mla_prompt(q_lat, q_pe, c_kv, k_pe)
    h = x + mixer_output(ret_o, gate, mla_lat, ret_gn_w, w_uv, w_o)
    h = h + peer_ffn(rms_norm(h, ln2_w), peer_w_q, peer_sub_keys, peer_u, peer_v)
    return h, c_kv, k_pe, ret_state


def sample_layer(x, ckv_past, kpe_past, ret_state, lw):
    (ln1_w, w_in, ret_gn_w, q_norm_w, w_uq, kv_norm_w, w_uk, w_uv, w_o,
     ln2_w, peer_w_q, peer_sub_keys, peer_u, peer_v) = lw
    B, T, _ = x.shape
    pos = ckv_past.shape[1] + jnp.arange(T)
    q_r, k_r, v_r, gate, q_lat, q_pe, c_kv, k_pe = mixer_project(
        rms_norm(x, ln1_w), pos, w_in, q_norm_w, w_uq, kv_norm_w, w_uk)
    ret_o, new_state = retention_chunk(ret_state.astype(jnp.float32), q_r.astype(jnp.float32),
                                       k_r.astype(jnp.float32), v_r.astype(jnp.float32))
    ckv_all = jnp.concatenate([ckv_past.astype(c_kv.dtype), c_kv], axis=1)
    kpe_all = jnp.concatenate([kpe_past.astype(k_pe.dtype), k_pe], axis=1)
    mla_lat = mla_attend(q_lat, q_pe, ckv_all, kpe_all, None)
    h = x + mixer_output(ret_o, gate, mla_lat, ret_gn_w, w_uv, w_o)
    h = h + peer_ffn(rms_norm(h, ln2_w), peer_w_q, peer_sub_keys, peer_u, peer_v)
    return h, c_kv, k_pe, new_state


def setup_inputs(seed: int = 0) -> dict:
    key = jax.random.key(seed)
    ks = jax.random.split(key, 20)
    f32 = jnp.float32

    def nrm(k, shape, scale):
        return jax.random.normal(k, shape, f32) * scale

    def gain(k, shape):
        return 1.0 + 0.02 * jax.random.normal(k, shape, f32)

    return {
        'x_prompt': nrm(ks[0], (BATCH, SEQ, D_MODEL), 1.0),
        'x_sample': nrm(ks[1], (DEC_BATCH, DEC_SEQ, D_MODEL), 1.0),
        'cache_mla_ckv': nrm(ks[2], (DEPTH, DEC_BATCH, PAST_LEN, MLA_KV_RANK), 1.0),
        'cache_mla_krope': nrm(ks[3], (DEPTH, DEC_BATCH, PAST_LEN, MLA_ROPE), 1.0),
        'state_retention': nrm(ks[4], (DEPTH, DEC_BATCH, RET_HEADS, RET_DK, RET_DV), 0.5),
        'ln1_w': gain(ks[5], (DEPTH, D_MODEL)),
        'w_in': nrm(ks[6], (DEPTH, D_MODEL, IN_COLS), D_MODEL ** -0.5),
        'ret_gn_w': gain(ks[7], (DEPTH, RET_V_W)),
        'mla_q_norm_w': gain(ks[8], (DEPTH, MLA_Q_RANK)),
        'mla_w_uq': nrm(ks[9], (DEPTH, MLA_Q_RANK, MLA_HEADS, MLA_NOPE + MLA_ROPE), MLA_Q_RANK ** -0.5),
        'mla_kv_norm_w': gain(ks[10], (DEPTH, MLA_KV_RANK)),
        'mla_w_uk': nrm(ks[11], (DEPTH, MLA_KV_RANK, MLA_HEADS, MLA_NOPE), MLA_KV_RANK ** -0.5),
        'mla_w_uv': nrm(ks[12], (DEPTH, MLA_KV_RANK, MLA_HEADS, MLA_V), MLA_KV_RANK ** -0.5),
        'w_o': nrm(ks[13], (DEPTH, MIX_W, D_MODEL), MIX_W ** -0.5),
        'ln2_w': gain(ks[14], (DEPTH, D_MODEL)),
        'peer_w_q': nrm(ks[15], (DEPTH, D_MODEL, PEER_HEADS * PEER_DK), D_MODEL ** -0.5),
        'peer_sub_keys': nrm(ks[16], (DEPTH, PEER_HEADS, 2, PEER_N_KEYS, PEER_DK // 2), (PEER_DK // 2) ** -0.5),
        'peer_u': nrm(ks[17], (DEPTH, PEER_N_EXPERTS, D_MODEL), D_MODEL ** -0.5),
        'peer_v': nrm(ks[18], (DEPTH, PEER_N_EXPERTS, D_MODEL), PEER_HEADS ** -0.5),
        'lnf_w': gain(ks[19], (D_MODEL,)),
    }


def reference(x_prompt, x_sample, cache_mla_ckv, cache_mla_krope, state_retention,
              ln1_w, w_in, ret_gn_w, mla_q_norm_w, mla_w_uq, mla_kv_norm_w, mla_w_uk,
              mla_w_uv, w_o, ln2_w, peer_w_q, peer_sub_keys, peer_u, peer_v, lnf_w):
    hp = x_prompt
    hs = x_sample
    ckv_p, kpe_p, st_p, ckv_s, kpe_s, st_s = [], [], [], [], [], []
    for l in range(DEPTH):
        lw = (ln1_w[l], w_in[l], ret_gn_w[l], mla_q_norm_w[l], mla_w_uq[l], mla_kv_norm_w[l],
              mla_w_uk[l], mla_w_uv[l], w_o[l], ln2_w[l], peer_w_q[l], peer_sub_keys[l],
              peer_u[l], peer_v[l])
        hp, c1, k1, s1 = prompt_layer(hp, lw)
        hs, c2, k2, s2 = sample_layer(hs, cache_mla_ckv[l], cache_mla_krope[l], state_retention[l], lw)
        ckv_p.append(c1)
        kpe_p.append(k1)
        st_p.append(s1)
        ckv_s.append(c2)
        kpe_s.append(k2)
        st_s.append(s2)
    y_prompt = rms_norm(hp, lnf_w)
    y_sample = rms_norm(hs, lnf_w)
    return (y_prompt, y_sample, jnp.stack(ckv_p), jnp.stack(kpe_p), jnp.stack(st_p),
            jnp.stack(ckv_s), jnp.stack(kpe_s), jnp.stack(st_s))
```

```python
import functools

import jax
import jax.numpy as jnp
from jax import lax
from jax.experimental import pallas as pl
from jax.experimental.pallas import tpu as pltpu

EPS = 1e-6
ROPE_BASE = 10000.0
CHUNK = 64
PEER_TOPK = 16
LANES = 128
MXU_DTYPE = jnp.bfloat16
VMEM_LIMIT_BYTES = 56 * 1024 * 1024

F32 = jnp.float32
NEG_INF = float("-inf")


def _mm(a, b):
    return jnp.dot(a.astype(MXU_DTYPE), b.astype(MXU_DTYPE), preferred_element_type=F32)


def _mm_nt(a, b):
    return lax.dot_general(a.astype(MXU_DTYPE), b.astype(MXU_DTYPE),
                           (((1,), (1,)), ((), ())), preferred_element_type=F32)


def _mm_tn(a, b):
    return lax.dot_general(a.astype(MXU_DTYPE), b.astype(MXU_DTYPE),
                           (((0,), (0,)), ((), ())), preferred_element_type=F32)


def _rms(x, w):
    return x * lax.rsqrt(jnp.mean(x * x, axis=-1, keepdims=True) + EPS) * w


def _rope(t, c, sa, sb, half):
    n = t.shape[1]
    return t * c + pltpu.roll(t, n - half, 1) * sa + pltpu.roll(t, half, 1) * sb


def _params(*sem):
    return pltpu.CompilerParams(dimension_semantics=sem, vmem_limit_bytes=VMEM_LIMIT_BYTES)


def _inproj_body(x_ref, ln1_ref, win_ref, cr_ref, sar_ref, sbr_ref, cm_ref, sam_ref, sbm_ref,
                 qnw_ref, kvnw_ref, wuq_ref, wuk_ref, wuv_ref,
                 qr_ref, kr_ref, vr_ref, gate_ref, qm_ref, km_ref, vm_ref, ckv_ref, kpe_ref,
                 *, rw, qrank, kvrank, ret_half, mla_half, k_scale, nheads):
    n1 = _rms(x_ref[...], ln1_ref[...])
    proj = _mm(n1, win_ref[...])
    cr, sar, sbr = cr_ref[...], sar_ref[...], sbr_ref[...]
    qr_ref[...] = _rope(proj[:, 0:rw], cr, sar, sbr, ret_half)
    kr_ref[...] = _rope(proj[:, rw:2 * rw], cr, sar, sbr, ret_half) * k_scale
    vr_ref[...] = proj[:, 2 * rw:3 * rw]
    gate_ref[...] = proj[:, 3 * rw:4 * rw]
    o4 = 4 * rw
    o5 = o4 + qrank
    o6 = o5 + kvrank
    cm, sam, sbm = cm_ref[...], sam_ref[...], sbm_ref[...]
    tile = lambda t: jnp.concatenate([t] * nheads, axis=1)
    cq = _rms(proj[:, o4:o5], qnw_ref[...])
    qm = _rope(_mm(cq, wuq_ref[...]), tile(cm), tile(sam), tile(sbm), mla_half)
    qm_ref[...] = qm.astype(qm_ref.dtype)
    ckv = _rms(proj[:, o5:o6], kvnw_ref[...])
    ckv_ref[...] = ckv
    kpe = _rope(proj[:, o6:o6 + LANES], cm, sam, sbm, mla_half)
    kpe_ref[...] = kpe
    km_ref[...] = (_mm(ckv, wuk_ref[...]) + tile(kpe)).astype(km_ref.dtype)
    vm_ref[...] = _mm(ckv, wuv_ref[...]).astype(vm_ref.dtype)


def _inproj(x, tabs, ln1, win_p, qnw, kvnw, wuq_p, wuk_p, wuv_p, *, tm, dims):
    t, d = x.shape
    rw, nheads = dims["rw"], dims["nheads"]
    hp = nheads * LANES
    nblk_tab = tabs[0].shape[0] // tm
    row = lambda i: (i, 0)
    tab = lambda i: (i % nblk_tab, 0)
    full = lambda i: (0, 0)
    fs = lambda a: pl.BlockSpec(a.shape, full)
    in_specs = [pl.BlockSpec((tm, d), row), fs(ln1), fs(win_p)]
    in_specs += [pl.BlockSpec((tm, rw), tab)] * 3 + [pl.BlockSpec((tm, LANES), tab)] * 3
    in_specs += [fs(qnw), fs(kvnw), fs(wuq_p), fs(wuk_p), fs(wuv_p)]
    out_shape = [jax.ShapeDtypeStruct((t, rw), F32)] * 4
    out_shape += [jax.ShapeDtypeStruct((t, hp), MXU_DTYPE)] * 3
    out_shape += [jax.ShapeDtypeStruct((t, dims["kvrank"]), F32), jax.ShapeDtypeStruct((t, LANES), F32)]
    out_specs = [pl.BlockSpec((tm, rw), row)] * 4 + [pl.BlockSpec((tm, hp), row)] * 3
    out_specs += [pl.BlockSpec((tm, dims["kvrank"]), row), pl.BlockSpec((tm, LANES), row)]
    body = functools.partial(
        _inproj_body, rw=rw, qrank=dims["qrank"], kvrank=dims["kvrank"], ret_half=dims["ret_dk"] // 2,
        mla_half=dims["mla_rope"] // 2, k_scale=dims["ret_dk"] ** -0.5, nheads=nheads)
    return pl.pallas_call(
        body, grid=(t // tm,), in_specs=in_specs, out_specs=out_specs, out_shape=out_shape,
        compiler_params=_params("parallel"), name="inproj",
    )(x, ln1, win_p, *tabs, qnw, kvnw, wuq_p, wuk_p, wuv_p)


def _retention_body(lg_ref, q_ref, k_ref, v_ref, s0_ref, o_ref, sout_ref, s_scr, *, rows, chunk, dk):
    hp = pl.program_id(1)
    j = pl.program_id(2)

    @pl.when(j == 0)
    def _():
        s_scr[...] = s0_ref[0, 0]

    lane = lax.broadcasted_iota(jnp.int32, (1, LANES), 1)
    is_a = lane < dk
    lg_a = lg_ref[2 * hp]
    lg_b = lg_ref[2 * hp + 1]
    lgl = jnp.where(is_a, lg_a, lg_b)
    r = lax.broadcasted_iota(jnp.int32, (rows, 1), 0).astype(F32)
    q, k, v = q_ref[...], k_ref[...], v_ref[...]
    q_dec = q * jnp.exp(lgl * (r + 1.0))
    k_dec = k * jnp.exp(lgl * (float(rows) - 1.0 - r))
    ri = lax.broadcasted_iota(jnp.int32, (rows, rows), 0)
    ci = lax.broadcasted_iota(jnp.int32, (rows, rows), 1)
    dist = jnp.abs(ri - ci).astype(F32)
    visible = (ci // chunk) <= (ri // chunk)
    o = _mm(q_dec, s_scr[...])
    for first, lg in ((True, lg_a), (False, lg_b)):
        sel = is_a if first else jnp.logical_not(is_a)
        qh = jnp.where(sel, q, 0.0)
        vh = jnp.where(sel, v, 0.0)
        decay = jnp.where(visible, jnp.exp(lg * dist), 0.0)
        o = o + _mm(_mm_nt(qh, k) * decay, vh)
    o_ref[...] = o
    sr = lax.broadcasted_iota(jnp.int32, (LANES, LANES), 0) < dk
    sc = lax.broadcasted_iota(jnp.int32, (LANES, LANES), 1) < dk
    kv = jnp.where(sr == sc, _mm_tn(k_dec, v), 0.0)
    s_new = jnp.exp(lgl * float(rows)) * s_scr[...] + kv
    s_scr[...] = s_new

    @pl.when(j == pl.num_programs(2) - 1)
    def _():
        sout_ref[0, 0] = s_new


def _retention(lg, q, k, v, s0_pairs, *, nbatch, rows, chunk, dk):
    t, w = q.shape
    npairs = w // LANES
    nblk = t // (nbatch * rows)
    blk = pl.BlockSpec((rows, LANES), lambda b, p, j: (b * nblk + j, p))
    st = pl.BlockSpec((1, 1, LANES, LANES), lambda b, p, j: (b, p, 0, 0))
    return pl.pallas_call(
        functools.partial(_retention_body, rows=rows, chunk=chunk, dk=dk),
        grid=(nbatch, npairs, nblk),
        in_specs=[pl.BlockSpec(memory_space=pltpu.SMEM), blk, blk, blk, st],
        out_specs=[blk, st],
        out_shape=[jax.ShapeDtypeStruct((t, w), F32),
                   jax.ShapeDtypeStruct((nbatch, npairs, LANES, LANES), F32)],
        scratch_shapes=[pltpu.VMEM((LANES, LANES), F32)],
        compiler_params=_params("parallel", "parallel", "arbitrary"), name="retention",
    )(lg, q, k, v, s0_pairs)


def _flash_body(q_ref, k_ref, v_ref, o_ref, *, tq, scale, nheads, chunk):
    i = pl.program_id(1)
    ri = lax.broadcasted_iota(jnp.int32, (tq, tq), 0) // chunk
    ci = lax.broadcasted_iota(jnp.int32, (tq, tq), 1) // chunk
    visible = ci <= ri

    def head(h):
        cols = slice(h * LANES, (h + 1) * LANES)
        q = q_ref[:, cols]

        def step(j, carry, diagonal):
            m, l, acc = carry
            off = pl.multiple_of(j * tq, tq)
            s = _mm_nt(q, k_ref[pl.ds(off, tq), cols]) * scale
            if diagonal:
                s = jnp.where(visible, s, NEG_INF)
            m_new = jnp.maximum(m, jnp.max(s, axis=1, keepdims=True))
            alpha = jnp.exp(m - m_new)
            p = jnp.exp(s - m_new)
            l = alpha * l + jnp.sum(p, axis=1, keepdims=True)
            acc = alpha * acc + _mm(p, v_ref[pl.ds(off, tq), cols])
            return m_new, l, acc

        init = (jnp.full((tq, 1), NEG_INF, F32), jnp.zeros((tq, 1), F32), jnp.zeros((tq, LANES), F32))
        carry = lax.fori_loop(0, i, functools.partial(step, diagonal=False), init)
        _, l, acc = step(i, carry, True)
        return acc / l

    for p in range(nheads // 2):
        o_ref[:, p * LANES:(p + 1) * LANES] = head(2 * p) + head(2 * p + 1)


def _flash(qm, km, vm, *, nbatch, tq, scale, nheads, chunk):
    t, hp = qm.shape
    s = t // nbatch
    nq = s // tq
    ow = nheads // 2 * LANES
    return pl.pallas_call(
        functools.partial(_flash_body, tq=tq, scale=scale, nheads=nheads, chunk=chunk),
        grid=(nbatch, nq),
        in_specs=[pl.BlockSpec((tq, hp), lambda b, i: (b * nq + i, 0)),
                  pl.BlockSpec((s, hp), lambda b, i: (b, 0)),
                  pl.BlockSpec((s, hp), lambda b, i: (b, 0))],
        out_specs=pl.BlockSpec((tq, ow), lambda b, i: (b * nq + i, 0)),
        out_shape=jax.ShapeDtypeStruct((t, ow), F32),
        compiler_params=_params("parallel", "arbitrary"), name="flash_mla",
    )(qm, km, vm)


def _decode_attn_body(q_ref, cpast_ref, kpast_ref, cnew_ref, knew_ref, wuk_ref, wuv_ref, o_ref,
                      *, nheads, nope, rope, scale):
    c_past = cpast_ref[0]
    k_past = kpast_ref[0]
    c_new = cnew_ref[...]
    k_new = knew_ref[:, nope:nope + rope]
    outs = []
    for h in range(nheads):
        q = q_ref[:, h * LANES:(h + 1) * LANES]
        q_lat = _mm(q, wuk_ref[h])
        q_pe = q[:, nope:nope + rope]
        s_p = (_mm_nt(q_lat, c_past) + _mm_nt(q_pe, k_past)) * scale
        s_n = (_mm_nt(q_lat, c_new) + _mm_nt(q_pe, k_new)) * scale
        m = jnp.maximum(jnp.max(s_p, axis=1, keepdims=True), jnp.max(s_n, axis=1, keepdims=True))
        p_p = jnp.exp(s_p - m)
        p_n = jnp.exp(s_n - m)
        l = jnp.sum(p_p, axis=1, keepdims=True) + jnp.sum(p_n, axis=1, keepdims=True)
        o_lat = (_mm(p_p, c_past) + _mm(p_n, c_new)) / l
        outs.append(_mm(o_lat, wuv_ref[h]))
    o_ref[...] = jnp.concatenate(outs, axis=1)


def _decode_attn(qm, c_past, k_past, c_new, kpe_new, wuk3, wuv3, *, nq, nope, rope, scale):
    nb, past, kvr = c_past.shape
    nheads, _, vdim = wuv3.shape
    t, hp = qm.shape
    row = lambda b: (b, 0)
    full3 = lambda b: (0, 0, 0)
    return pl.pallas_call(
        functools.partial(_decode_attn_body, nheads=nheads, nope=nope, rope=rope, scale=scale),
        grid=(nb,),
        in_specs=[pl.BlockSpec((nq, hp), row),
                  pl.BlockSpec((1, past, kvr), lambda b: (b, 0, 0)),
                  pl.BlockSpec((1, past, rope), lambda b: (b, 0, 0)),
                  pl.BlockSpec((nq, kvr), row),
                  pl.BlockSpec((nq, LANES), row),
                  pl.BlockSpec(wuk3.shape, full3),
                  pl.BlockSpec(wuv3.shape, full3)],
        out_specs=pl.BlockSpec((nq, nheads * vdim), row),
        out_shape=jax.ShapeDtypeStruct((t, nheads * vdim), F32),
        compiler_params=_params("parallel"), name="decode_mla",
    )(qm, c_past, k_past, c_new, kpe_new, wuk3, wuv3)


def _split3(x):
    a = x.astype(MXU_DTYPE)
    r = x - a.astype(F32)
    b = r.astype(MXU_DTYPE)
    c = (r - b.astype(F32)).astype(MXU_DTYPE)
    return a, b, c


def _group_mean(x, avg):
    a, b, c = _split3(x)
    dot = lambda t: jnp.dot(t, avg, preferred_element_type=F32)
    return dot(a) + dot(b) + dot(c)


def _topk_rows(s, payload, kk):
    n = s.shape[0]
    rid = lax.broadcasted_iota(jnp.int32, s.shape, 0)
    vals, pays = [], []
    for _ in range(kk):
        mx = jnp.max(s, axis=0, keepdims=True)
        first = jnp.min(jnp.where(s == mx, rid, n), axis=0, keepdims=True)
        hit = rid == first
        vals.append(mx)
        pays.append(jnp.max(jnp.where(hit, payload, -1), axis=0, keepdims=True))
        s = jnp.where(hit, NEG_INF, s)
    return jnp.concatenate(vals, axis=0), jnp.concatenate(pays, axis=0)


def _mix_route_body(x_ref, ret_ref, gate_ref, mla_ref, gnw_ref, avg_ref, wo_ref, ln2_ref, wq_ref, keys_ref,
                    h_ref, idx_ref, g_ref, *, rw, pheads, nkeys, topk):
    ret = ret_ref[...]
    avg = avg_ref[...]
    mu = _group_mean(ret, avg)
    cen = ret - mu
    var = _group_mean(cen * cen, avg)
    gate = gate_ref[...]
    y = cen * lax.rsqrt(var + EPS) * gnw_ref[...] * (gate * jax.nn.sigmoid(gate))
    h = x_ref[...] + _mm(y, wo_ref[0:rw, :]) + _mm(mla_ref[...], wo_ref[rw:, :])
    h_ref[...] = h
    qp = _mm(_rms(h, ln2_ref[...]), wq_ref[...])
    kid = lax.broadcasted_iota(jnp.int32, (nkeys, qp.shape[0]), 0)
    idx_rows, g_rows = [], []
    for hd in range(pheads):
        ts, ti = [], []
        for half in range(2):
            c = (2 * hd + half) * LANES
            st = _mm_nt(keys_ref[2 * hd + half], qp[:, c:c + LANES])
            v, i = _topk_rows(st, kid, topk)
            ts.append(v)
            ti.append(i)
        cand = jnp.concatenate([ts[0][a:a + 1, :] + ts[1] for a in range(topk)], axis=0)
        cidx = jnp.concatenate([ti[0][a:a + 1, :] * nkeys + ti[1] for a in range(topk)], axis=0)
        best, expert = _topk_rows(cand, cidx, topk)
        e = jnp.exp(best - best[0:1, :])
        g_rows.append(e / jnp.sum(e, axis=0, keepdims=True))
        idx_rows.append(expert)
    idx_ref[...] = jnp.concatenate(idx_rows, axis=0).T
    g_ref[...] = jnp.concatenate(g_rows, axis=0).T


def _mix_route(x, ret_o, gate, mla_o, gnw, avg, wo, ln2, wq, keys, *, tm, pheads, nkeys, topk):
    t, d = x.shape
    rw = ret_o.shape[1]
    nsel = pheads * topk
    row = lambda i: (i, 0)
    fs = lambda a: pl.BlockSpec(a.shape, lambda i: (0,) * a.ndim)
    return pl.pallas_call(
        functools.partial(_mix_route_body, rw=rw, pheads=pheads, nkeys=nkeys, topk=topk),
        grid=(t // tm,),
        in_specs=[pl.BlockSpec((tm, d), row), pl.BlockSpec((tm, rw), row), pl.BlockSpec((tm, rw), row),
                  pl.BlockSpec((tm, mla_o.shape[1]), row), fs(gnw), fs(avg), fs(wo), fs(ln2), fs(wq), fs(keys)],
        out_specs=[pl.BlockSpec((tm, d), row), pl.BlockSpec((tm, nsel), row), pl.BlockSpec((tm, nsel), row)],
        out_shape=[jax.ShapeDtypeStruct((t, d), F32), jax.ShapeDtypeStruct((t, nsel), jnp.int32),
                   jax.ShapeDtypeStruct((t, nsel), F32)],
        compiler_params=_params("parallel"), name="mix_route",
    )(x, ret_o, gate, mla_o, gnw, avg, wo, ln2, wq, keys)


SUBLANES = 8


def _peer_body(idx_ref, h_ref, g_ref, ln2_ref, lnf_ref, u_hbm, v_hbm, o_ref,
               ubuf, vbuf, xn_scr, sem_u, sem_v, *, tb, nsel, final_norm):
    h = h_ref[...]
    xn_scr[...] = _rms(h, ln2_ref[...])

    def rows_copy(tab, buf, sem, slot, tok):
        def issue(k, c):
            e = idx_ref[tok, k]
            pltpu.make_async_copy(tab.at[pl.ds(e, 1)], buf.at[slot, pl.ds(k, 1)], sem.at[slot]).start()
            return c
        lax.fori_loop(0, nsel, issue, 0, unroll=8)

    def start(tok, slot):
        rows_copy(u_hbm, ubuf, sem_u, slot, tok)
        rows_copy(v_hbm, vbuf, sem_v, slot, tok)

    def wait(slot):
        pltpu.make_async_copy(u_hbm.at[pl.ds(0, nsel)], ubuf.at[slot], sem_u.at[slot]).wait()
        pltpu.make_async_copy(v_hbm.at[pl.ds(0, nsel)], vbuf.at[slot], sem_v.at[slot]).wait()

    start(0, 0)

    def group(gi, c):
        base = pl.multiple_of(gi * SUBLANES, SUBLANES)
        g_cols = g_ref[pl.ds(base, SUBLANES), :].T
        xn8 = xn_scr[pl.ds(base, SUBLANES), :]
        rows = []
        for r in range(SUBLANES):
            tok = base + r
            slot = r % 2

            @pl.when(tok + 1 < tb)
            def _():
                start(tok + 1, 1 - slot)

            wait(slot)
            hid = jnp.sum(ubuf[slot] * xn8[r:r + 1, :], axis=1, keepdims=True)
            act = 0.5 * hid * (1.0 + lax.erf(hid * (2.0 ** -0.5))) * g_cols[:, r:r + 1]
            rows.append(jnp.sum(vbuf[slot] * act, axis=0, keepdims=True))
        out = h_ref[pl.ds(base, SUBLANES), :] + jnp.concatenate(rows, axis=0)
        if final_norm:
            out = _rms(out, lnf_ref[...])
        o_ref[pl.ds(base, SUBLANES), :] = out
        return c

    lax.fori_loop(0, tb // SUBLANES, group, 0)


def _peer(idx, h, g, ln2, lnf, u_tab, v_tab, *, tb, final_norm):
    t, d = h.shape
    nsel = idx.shape[1]
    row = lambda i: (i, 0)
    one = lambda i: (0, 0)
    return pl.pallas_call(
        functools.partial(_peer_body, tb=tb, nsel=nsel, final_norm=final_norm),
        grid=(t // tb,),
        in_specs=[pl.BlockSpec((tb, nsel), row, memory_space=pltpu.SMEM),
                  pl.BlockSpec((tb, d), row), pl.BlockSpec((tb, nsel), row),
                  pl.BlockSpec((1, d), one), pl.BlockSpec((1, d), one),
                  pl.BlockSpec(memory_space=pl.ANY), pl.BlockSpec(memory_space=pl.ANY)],
        out_specs=pl.BlockSpec((tb, d), row),
        out_shape=jax.ShapeDtypeStruct((t, d), F32),
        scratch_shapes=[pltpu.VMEM((2, nsel, d), F32), pltpu.VMEM((2, nsel, d), F32),
                        pltpu.VMEM((tb, d), F32),
                        pltpu.SemaphoreType.DMA((2,)), pltpu.SemaphoreType.DMA((2,))],
        compiler_params=_params("arbitrary"), name="peer_experts",
    )(idx, h, g, ln2, lnf, u_tab, v_tab)


def _rope_tables(pos, half, group, width, lo):
    inv = ROPE_BASE ** (-jnp.arange(half, dtype=F32) / half)
    ang = pos.astype(F32)[:, None] * inv[None, :]
    cos, sin = jnp.cos(ang), jnp.sin(ang)
    n = pos.shape[0]
    reps = width // group
    pad_hi = group - lo - 2 * half
    blk = lambda a, b, fill: jnp.concatenate(
        [jnp.full((n, lo), fill, F32), a, b, jnp.full((n, pad_hi), fill, F32)], axis=1)
    z = jnp.zeros_like(sin)
    c = blk(cos, cos, 1.0)
    sa = blk(-sin, z, 0.0)
    sb = blk(z, sin, 0.0)
    return [jnp.tile(a, (1, reps)) for a in (c, sa, sb)]


def _ret_log_decay(nheads):
    return jnp.log(1.0 - jnp.exp2(-5.0 - jnp.arange(nheads, dtype=F32)))


def _pair_states(s):
    b, h, dk, dv = s.shape
    s = s.reshape(b, h // 2, 2, dk, dv)
    z = jnp.zeros_like(s[:, :, 0])
    top = jnp.concatenate([s[:, :, 0], z], axis=-1)
    bot = jnp.concatenate([z, s[:, :, 1]], axis=-1)
    return jnp.concatenate([top, bot], axis=-2)


def _unpair_states(sp, dk, dv):
    b, hp = sp.shape[:2]
    return jnp.stack([sp[:, :, :dk, :dv], sp[:, :, dk:, dv:]], axis=2).reshape(b, 2 * hp, dk, dv)


def _layer_weights(ln1_w, w_in, ret_gn_w, q_norm_w, w_uq, kv_norm_w, w_uk, w_uv, w_o, ln2_w,
                   peer_w_q, peer_sub_keys, dims):
    d = w_in.shape[0]
    nheads, nope, rope, vdim = dims["nheads"], dims["nope"], dims["mla_rope"], dims["vdim"]
    o6 = 4 * dims["rw"] + dims["qrank"] + dims["kvrank"]
    zc = lambda r, c: jnp.zeros((r, c), F32)
    win_p = jnp.concatenate([w_in[:, :o6], zc(d, nope), w_in[:, o6:], zc(d, LANES - nope - rope)], axis=1)
    qr, kr = w_uq.shape[0], w_uk.shape[0]
    wuq_p = jnp.concatenate([w_uq, jnp.zeros((qr, nheads, LANES - nope - rope), F32)], axis=2).reshape(qr, -1)
    wuk_p = jnp.concatenate([w_uk, jnp.zeros((kr, nheads, LANES - nope), F32)], axis=2).reshape(kr, -1)
    zv = jnp.zeros((kr, nheads // 2, LANES - vdim), F32)
    wv = w_uv.reshape(kr, nheads // 2, 2, vdim)
    wuv_p = jnp.concatenate([wv[:, :, 0], zv, zv, wv[:, :, 1]], axis=2).reshape(kr, -1)
    wuk3 = jnp.concatenate([jnp.transpose(w_uk, (1, 2, 0)),
                            jnp.zeros((nheads, LANES - nope, kr), F32)], axis=1)
    wuv3 = jnp.transpose(w_uv, (1, 0, 2))
    gidx = jnp.arange(dims["rw"]) // dims["ret_dv"]
    avg = (gidx[:, None] == gidx[None, :]).astype(F32) / dims["ret_dv"]
    keys = peer_sub_keys.reshape(-1, peer_sub_keys.shape[2], peer_sub_keys.shape[3])
    c = lambda a: a.astype(MXU_DTYPE)
    r2 = lambda a: a.reshape(1, -1)
    return dict(ln1=r2(ln1_w), win_p=c(win_p), gnw=r2(ret_gn_w), qnw=r2(q_norm_w), kvnw=r2(kv_norm_w),
                wuq_p=c(wuq_p), wuk_p=c(wuk_p), wuv_p=c(wuv_p), wuk3=c(wuk3), wuv3=c(wuv3), avg=c(avg),
                wo=c(w_o), ln2=r2(ln2_w), wq=c(peer_w_q), keys=c(keys))


def _stream_layer(x, w, tabs, lg, s0, u_tab, v_tab, lnf, dims, *, nbatch, ret_rows, ret_chunk, tm, tb,
                  final_norm, cache=None):
    seq = x.shape[0] // nbatch
    nheads, dk, dv = dims["nheads"], dims["ret_dk"], dims["ret_dv"]
    qr, kr, vr, gate, qm, km, vm, ckv, kpe = _inproj(
        x, tabs, w["ln1"], w["win_p"], w["qnw"], w["kvnw"], w["wuq_p"], w["wuk_p"], w["wuv_p"], tm=tm, dims=dims)
    ret_o, s_pairs = _retention(lg, qr, kr, vr, _pair_states(s0), nbatch=nbatch, rows=ret_rows,
                                chunk=ret_chunk, dk=dk)
    scale = (dims["nope"] + dims["mla_rope"]) ** -0.5
    if cache is None:
        mla_o = _flash(qm, km, vm, nbatch=nbatch, tq=min(256, seq), scale=scale, nheads=nheads, chunk=CHUNK)
    else:
        mla_o = _decode_attn(qm, cache[0], cache[1], ckv, kpe, w["wuk3"], w["wuv3"], nq=seq,
                             nope=dims["nope"], rope=dims["mla_rope"], scale=scale)
    h, idx, g = _mix_route(x, ret_o, gate, mla_o, w["gnw"], w["avg"], w["wo"], w["ln2"], w["wq"], w["keys"],
                           tm=tm, pheads=dims["pheads"], nkeys=dims["nkeys"], topk=dims["topk"])
    out = _peer(idx, h, g, w["ln2"], lnf, u_tab, v_tab, tb=tb, final_norm=final_norm)
    nope, rope = dims["nope"], dims["mla_rope"]
    return out, ckv, kpe[:, nope:nope + rope], _unpair_states(s_pairs, dk, dv)


def kernel(x_prompt, x_sample, cache_mla_ckv, cache_mla_krope, state_retention, ln1_w, w_in, ret_gn_w,
           mla_q_norm_w, mla_w_uq, mla_kv_norm_w, mla_w_uk, mla_w_uv, w_o, ln2_w, peer_w_q, peer_sub_keys,
           peer_u, peer_v, lnf_w):
    depth = w_in.shape[0]
    nb, seq, d = x_prompt.shape
    db, dseq, _ = x_sample.shape
    past = cache_mla_ckv.shape[2]
    rheads, dk, dv = state_retention.shape[2:]
    nkeys = peer_sub_keys.shape[3]
    dims = dict(rw=rheads * dk, ret_dk=dk, ret_dv=dv, qrank=mla_w_uq.shape[1], kvrank=mla_w_uk.shape[1],
                nheads=mla_w_uq.shape[2], nope=mla_w_uk.shape[3], vdim=mla_w_uv.shape[3],
                mla_rope=mla_w_uq.shape[3] - mla_w_uk.shape[3], pheads=peer_sub_keys.shape[1], nkeys=nkeys,
                topk=PEER_TOPK)
    assert rheads * dk == rheads * dv and dims["nheads"] % 2 == 0 and dk * 2 == LANES and dims["vdim"] * 2 == LANES

    def tables(pos):
        return (_rope_tables(pos, dk // 2, dk, dims["rw"], 0)
                + _rope_tables(pos, dims["mla_rope"] // 2, LANES, LANES, dims["nope"]))

    tabs_p = tables(jnp.arange(seq))
    tabs_s = tables(jnp.tile(past + jnp.arange(dseq), db))
    lg = _ret_log_decay(rheads)
    lnf = lnf_w.reshape(1, -1)
    hp = x_prompt.reshape(nb * seq, d)
    hs = x_sample.reshape(db * dseq, d)
    outs = [[] for _ in range(6)]
    for l in range(depth):
        w = _layer_weights(ln1_w[l], w_in[l], ret_gn_w[l], mla_q_norm_w[l], mla_w_uq[l], mla_kv_norm_w[l],
                           mla_w_uk[l], mla_w_uv[l], w_o[l], ln2_w[l], peer_w_q[l], peer_sub_keys[l], dims)
        last = l == depth - 1
        hp, c1, k1, s1 = _stream_layer(
            hp, w, tabs_p, lg, jnp.zeros((nb, rheads, dk, dv), F32), peer_u[l], peer_v[l], lnf, dims,
            nbatch=nb, ret_rows=min(256, seq), ret_chunk=CHUNK, tm=min(256, nb * seq), tb=min(64, nb * seq),
            final_norm=last)
        hs, c2, k2, s2 = _stream_layer(
            hs, w, tabs_s, lg, state_retention[l], peer_u[l], peer_v[l], lnf, dims,
            nbatch=db, ret_rows=dseq, ret_chunk=dseq, tm=min(256, db * dseq), tb=min(64, db * dseq),
            final_norm=last, cache=(cache_mla_ckv[l], cache_mla_krope[l]))
        for acc, val in zip(outs, (c1.reshape(nb, seq, -1), k1.reshape(nb, seq, -1), s1,
                                   c2.reshape(db, dseq, -1), k2.reshape(db, dseq, -1), s2)):
            acc.append(val)
    return (hp.reshape(nb, seq, d), hs.reshape(db, dseq, d), *[jnp.stack(o) for o in outs])
```

```python
import functools

import jax
import jax.numpy as jnp
from jax import lax
from jax.experimental import pallas as pl
from jax.experimental.pallas import tpu as pltpu
from jax.experimental.pallas import tpu_sc as plsc

EPS = 1e-6
ROPE_BASE = 10000.0
CHUNK = 64
PEER_TOPK = 16
LANES = 128
MXU_DTYPE = jnp.bfloat16
VMEM_LIMIT_BYTES = 56 * 1024 * 1024

F32 = jnp.float32
NEG_INF = float("-inf")


def _mm(a, b):
    return jnp.dot(a.astype(MXU_DTYPE), b.astype(MXU_DTYPE), preferred_element_type=F32)


def _mm_nt(a, b):
    return lax.dot_general(a.astype(MXU_DTYPE), b.astype(MXU_DTYPE),
                           (((1,), (1,)), ((), ())), preferred_element_type=F32)


def _mm_tn(a, b):
    return lax.dot_general(a.astype(MXU_DTYPE), b.astype(MXU_DTYPE),
                           (((0,), (0,)), ((), ())), preferred_element_type=F32)


def _rms(x, w):
    return x * lax.rsqrt(jnp.mean(x * x, axis=-1, keepdims=True) + EPS) * w


def _rope(t, c, sa, sb, half):
    n = t.shape[1]
    return t * c + pltpu.roll(t, n - half, 1) * sa + pltpu.roll(t, half, 1) * sb


def _params(*sem):
    return pltpu.CompilerParams(dimension_semantics=sem, vmem_limit_bytes=VMEM_LIMIT_BYTES)


def _inproj_body(x_ref, ln1_ref, win_ref, cr_ref, sar_ref, sbr_ref, cm_ref, sam_ref, sbm_ref,
                 qnw_ref, kvnw_ref, wuq_ref, wuk_ref, wuv_ref,
                 qr_ref, kr_ref, vr_ref, gate_ref, qm_ref, km_ref, vm_ref, ckv_ref, kpe_ref,
                 *, rw, qrank, kvrank, ret_half, mla_half, k_scale, nheads):
    n1 = _rms(x_ref[...], ln1_ref[...])
    proj = _mm(n1, win_ref[...])
    cr, sar, sbr = cr_ref[...], sar_ref[...], sbr_ref[...]
    qr_ref[...] = _rope(proj[:, 0:rw], cr, sar, sbr, ret_half)
    kr_ref[...] = _rope(proj[:, rw:2 * rw], cr, sar, sbr, ret_half) * k_scale
    vr_ref[...] = proj[:, 2 * rw:3 * rw]
    gate_ref[...] = proj[:, 3 * rw:4 * rw]
    o4 = 4 * rw
    o5 = o4 + qrank
    o6 = o5 + kvrank
    cm, sam, sbm = cm_ref[...], sam_ref[...], sbm_ref[...]
    tile = lambda t: jnp.concatenate([t] * nheads, axis=1)
    cq = _rms(proj[:, o4:o5], qnw_ref[...])
    qm = _rope(_mm(cq, wuq_ref[...]), tile(cm), tile(sam), tile(sbm), mla_half)
    qm_ref[...] = qm.astype(qm_ref.dtype)
    ckv = _rms(proj[:, o5:o6], kvnw_ref[...])
    ckv_ref[...] = ckv
    kpe = _rope(proj[:, o6:o6 + LANES], cm, sam, sbm, mla_half)
    kpe_ref[...] = kpe
    km_ref[...] = (_mm(ckv, wuk_ref[...]) + tile(kpe)).astype(km_ref.dtype)
    vm_ref[...] = _mm(ckv, wuv_ref[...]).astype(vm_ref.dtype)


def _inproj(x, tabs, ln1, win_p, qnw, kvnw, wuq_p, wuk_p, wuv_p, *, tm, dims):
    t, d = x.shape
    rw, nheads = dims["rw"], dims["nheads"]
    hp = nheads * LANES
    nblk_tab = tabs[0].shape[0] // tm
    row = lambda i: (i, 0)
    tab = lambda i: (i % nblk_tab, 0)
    full = lambda i: (0, 0)
    fs = lambda a: pl.BlockSpec(a.shape, full)
    in_specs = [pl.BlockSpec((tm, d), row), fs(ln1), fs(win_p)]
    in_specs += [pl.BlockSpec((tm, rw), tab)] * 3 + [pl.BlockSpec((tm, LANES), tab)] * 3
    in_specs += [fs(qnw), fs(kvnw), fs(wuq_p), fs(wuk_p), fs(wuv_p)]
    out_shape = [jax.ShapeDtypeStruct((t, rw), F32)] * 4
    out_shape += [jax.ShapeDtypeStruct((t, hp), MXU_DTYPE)] * 3
    out_shape += [jax.ShapeDtypeStruct((t, dims["kvrank"]), F32), jax.ShapeDtypeStruct((t, LANES), F32)]
    out_specs = [pl.BlockSpec((tm, rw), row)] * 4 + [pl.BlockSpec((tm, hp), row)] * 3
    out_specs += [pl.BlockSpec((tm, dims["kvrank"]), row), pl.BlockSpec((tm, LANES), row)]
    body = functools.partial(
        _inproj_body, rw=rw, qrank=dims["qrank"], kvrank=dims["kvrank"], ret_half=dims["ret_dk"] // 2,
        mla_half=dims["mla_rope"] // 2, k_scale=dims["ret_dk"] ** -0.5, nheads=nheads)
    return pl.pallas_call(
        body, grid=(t // tm,), in_specs=in_specs, out_specs=out_specs, out_shape=out_shape,
        compiler_params=_params("parallel"), name="inproj",
    )(x, ln1, win_p, *tabs, qnw, kvnw, wuq_p, wuk_p, wuv_p)


def _retention_body(lg_ref, q_ref, k_ref, v_ref, s0_ref, o_ref, sout_ref, s_scr, *, rows, chunk, dk):
    hp = pl.program_id(1)
    j = pl.program_id(2)

    @pl.when(j == 0)
    def _():
        s_scr[...] = s0_ref[0, 0]

    lane = lax.broadcasted_iota(jnp.int32, (1, LANES), 1)
    is_a = lane < dk
    lg_a = lg_ref[2 * hp]
    lg_b = lg_ref[2 * hp + 1]
    lgl = jnp.where(is_a, lg_a, lg_b)
    r = lax.broadcasted_iota(jnp.int32, (rows, 1), 0).astype(F32)
    q, k, v = q_ref[...], k_ref[...], v_ref[...]
    q_dec = q * jnp.exp(lgl * (r + 1.0))
    k_dec = k * jnp.exp(lgl * (float(rows) - 1.0 - r))
    ri = lax.broadcasted_iota(jnp.int32, (rows, rows), 0)
    ci = lax.broadcasted_iota(jnp.int32, (rows, rows), 1)
    dist = jnp.abs(ri - ci).astype(F32)
    visible = (ci // chunk) <= (ri // chunk)
    o = _mm(q_dec, s_scr[...])
    for first, lg in ((True, lg_a), (False, lg_b)):
        sel = is_a if first else jnp.logical_not(is_a)
        qh = jnp.where(sel, q, 0.0)
        vh = jnp.where(sel, v, 0.0)
        decay = jnp.where(visible, jnp.exp(lg * dist), 0.0)
        o = o + _mm(_mm_nt(qh, k) * decay, vh)
    o_ref[...] = o
    sr = lax.broadcasted_iota(jnp.int32, (LANES, LANES), 0) < dk
    sc = lax.broadcasted_iota(jnp.int32, (LANES, LANES), 1) < dk
    kv = jnp.where(sr == sc, _mm_tn(k_dec, v), 0.0)
    s_new = jnp.exp(lgl * float(rows)) * s_scr[...] + kv
    s_scr[...] = s_new

    @pl.when(j == pl.num_programs(2) - 1)
    def _():
        sout_ref[0, 0] = s_new


def _retention(lg, q, k, v, s0_pairs, *, nbatch, rows, chunk, dk):
    t, w = q.shape
    npairs = w // LANES
    nblk = t // (nbatch * rows)
    blk = pl.BlockSpec((rows, LANES), lambda b, p, j: (b * nblk + j, p))
    st = pl.BlockSpec((1, 1, LANES, LANES), lambda b, p, j: (b, p, 0, 0))
    return pl.pallas_call(
        functools.partial(_retention_body, rows=rows, chunk=chunk, dk=dk),
        grid=(nbatch, npairs, nblk),
        in_specs=[pl.BlockSpec(memory_space=pltpu.SMEM), blk, blk, blk, st],
        out_specs=[blk, st],
        out_shape=[jax.ShapeDtypeStruct((t, w), F32),
                   jax.ShapeDtypeStruct((nbatch, npairs, LANES, LANES), F32)],
        scratch_shapes=[pltpu.VMEM((LANES, LANES), F32)],
        compiler_params=_params("parallel", "parallel", "arbitrary"), name="retention",
    )(lg, q, k, v, s0_pairs)


def _flash_body(q_ref, k_ref, v_ref, o_ref, *, tq, scale, nheads, chunk):
    i = pl.program_id(1)
    ri = lax.broadcasted_iota(jnp.int32, (tq, tq), 0) // chunk
    ci = lax.broadcasted_iota(jnp.int32, (tq, tq), 1) // chunk
    visible = ci <= ri

    def head(h):
        cols = slice(h * LANES, (h + 1) * LANES)
        q = q_ref[:, cols]

        def step(j, carry, diagonal):
            m, l, acc = carry
            off = pl.multiple_of(j * tq, tq)
            s = _mm_nt(q, k_ref[pl.ds(off, tq), cols]) * scale
            if diagonal:
                s = jnp.where(visible, s, NEG_INF)
            m_new = jnp.maximum(m, jnp.max(s, axis=1, keepdims=True))
            alpha = jnp.exp(m - m_new)
            p = jnp.exp(s - m_new)
            l = alpha * l + jnp.sum(p, axis=1, keepdims=True)
            acc = alpha * acc + _mm(p, v_ref[pl.ds(off, tq), cols])
            return m_new, l, acc

        init = (jnp.full((tq, 1), NEG_INF, F32), jnp.zeros((tq, 1), F32), jnp.zeros((tq, LANES), F32))
        carry = lax.fori_loop(0, i, functools.partial(step, diagonal=False), init)
        _, l, acc = step(i, carry, True)
        return acc / l

    for p in range(nheads // 2):
        o_ref[:, p * LANES:(p + 1) * LANES] = head(2 * p) + head(2 * p + 1)


def _flash(qm, km, vm, *, nbatch, tq, scale, nheads, chunk):
    t, hp = qm.shape
    s = t // nbatch
    nq = s // tq
    ow = nheads // 2 * LANES
    return pl.pallas_call(
        functools.partial(_flash_body, tq=tq, scale=scale, nheads=nheads, chunk=chunk),
        grid=(nbatch, nq),
        in_specs=[pl.BlockSpec((tq, hp), lambda b, i: (b * nq + i, 0)),
                  pl.BlockSpec((s, hp), lambda b, i: (b, 0)),
                  pl.BlockSpec((s, hp), lambda b, i: (b, 0))],
        out_specs=pl.BlockSpec((tq, ow), lambda b, i: (b * nq + i, 0)),
        out_shape=jax.ShapeDtypeStruct((t, ow), F32),
        compiler_params=_params("parallel", "arbitrary"), name="flash_mla",
    )(qm, km, vm)


def _decode_attn_body(q_ref, cpast_ref, kpast_ref, cnew_ref, knew_ref, wuk_ref, wuv_ref, o_ref,
                      *, nheads, nope, rope, scale):
    c_past = cpast_ref[0]
    k_past = kpast_ref[0]
    c_new = cnew_ref[...]
    k_new = knew_ref[:, nope:nope + rope]
    outs = []
    for h in range(nheads):
        q = q_ref[:, h * LANES:(h + 1) * LANES]
        q_lat = _mm(q, wuk_ref[h])
        q_pe = q[:, nope:nope + rope]
        s_p = (_mm_nt(q_lat, c_past) + _mm_nt(q_pe, k_past)) * scale
        s_n = (_mm_nt(q_lat, c_new) + _mm_nt(q_pe, k_new)) * scale
        m = jnp.maximum(jnp.max(s_p, axis=1, keepdims=True), jnp.max(s_n, axis=1, keepdims=True))
        p_p = jnp.exp(s_p - m)
        p_n = jnp.exp(s_n - m)
        l = jnp.sum(p_p, axis=1, keepdims=True) + jnp.sum(p_n, axis=1, keepdims=True)
        o_lat = (_mm(p_p, c_past) + _mm(p_n, c_new)) / l
        outs.append(_mm(o_lat, wuv_ref[h]))
    o_ref[...] = jnp.concatenate(outs, axis=1)


def _decode_attn(qm, c_past, k_past, c_new, kpe_new, wuk3, wuv3, *, nq, nope, rope, scale):
    nb, past, kvr = c_past.shape
    nheads, _, vdim = wuv3.shape
    t, hp = qm.shape
    row = lambda b: (b, 0)
    full3 = lambda b: (0, 0, 0)
    return pl.pallas_call(
        functools.partial(_decode_attn_body, nheads=nheads, nope=nope, rope=rope, scale=scale),
        grid=(nb,),
        in_specs=[pl.BlockSpec((nq, hp), row),
                  pl.BlockSpec((1, past, kvr), lambda b: (b, 0, 0)),
                  pl.BlockSpec((1, past, rope), lambda b: (b, 0, 0)),
                  pl.BlockSpec((nq, kvr), row),
                  pl.BlockSpec((nq, LANES), row),
                  pl.BlockSpec(wuk3.shape, full3),
                  pl.BlockSpec(wuv3.shape, full3)],
        out_specs=pl.BlockSpec((nq, nheads * vdim), row),
        out_shape=jax.ShapeDtypeStruct((t, nheads * vdim), F32),
        compiler_params=_params("parallel"), name="decode_mla",
    )(qm, c_past, k_past, c_new, kpe_new, wuk3, wuv3)


def _split3(x):
    a = x.astype(MXU_DTYPE)
    r = x - a.astype(F32)
    b = r.astype(MXU_DTYPE)
    c = (r - b.astype(F32)).astype(MXU_DTYPE)
    return a, b, c


def _group_mean(x, avg):
    a, b, c = _split3(x)
    dot = lambda t: jnp.dot(t, avg, preferred_element_type=F32)
    return dot(a) + dot(b) + dot(c)


def _topk_rows(s, payload, kk):
    n = s.shape[0]
    rid = lax.broadcasted_iota(jnp.int32, s.shape, 0)
    vals, pays = [], []
    for _ in range(kk):
        mx = jnp.max(s, axis=0, keepdims=True)
        first = jnp.min(jnp.where(s == mx, rid, n), axis=0, keepdims=True)
        hit = rid == first
        vals.append(mx)
        pays.append(jnp.max(jnp.where(hit, payload, -1), axis=0, keepdims=True))
        s = jnp.where(hit, NEG_INF, s)
    return jnp.concatenate(vals, axis=0), jnp.concatenate(pays, axis=0)


def _mix_route_body(x_ref, ret_ref, gate_ref, mla_ref, gnw_ref, avg_ref, wo_ref, ln2_ref, wq_ref, keys_ref,
                    h_ref, hn_ref, idx_ref, g_ref, *, rw, pheads, nkeys, topk):
    ret = ret_ref[...]
    avg = avg_ref[...]
    mu = _group_mean(ret, avg)
    cen = ret - mu
    var = _group_mean(cen * cen, avg)
    gate = gate_ref[...]
    y = cen * lax.rsqrt(var + EPS) * gnw_ref[...] * (gate * jax.nn.sigmoid(gate))
    h = x_ref[...] + _mm(y, wo_ref[0:rw, :]) + _mm(mla_ref[...], wo_ref[rw:, :])
    h_ref[...] = h
    hn = _rms(h, ln2_ref[...])
    hn_ref[...] = hn
    qp = _mm(hn, wq_ref[...])
    kid = lax.broadcasted_iota(jnp.int32, (nkeys, qp.shape[0]), 0)
    idx_rows, g_rows = [], []
    for hd in range(pheads):
        ts, ti = [], []
        for half in range(2):
            c = (2 * hd + half) * LANES
            st = _mm_nt(keys_ref[2 * hd + half], qp[:, c:c + LANES])
            v, i = _topk_rows(st, kid, topk)
            ts.append(v)
            ti.append(i)
        cand = jnp.concatenate([ts[0][a:a + 1, :] + ts[1] for a in range(topk)], axis=0)
        cidx = jnp.concatenate([ti[0][a:a + 1, :] * nkeys + ti[1] for a in range(topk)], axis=0)
        best, expert = _topk_rows(cand, cidx, topk)
        e = jnp.exp(best - best[0:1, :])
        g_rows.append(e / jnp.sum(e, axis=0, keepdims=True))
        idx_rows.append(expert)
    idx_ref[...] = jnp.concatenate(idx_rows, axis=0).T
    g_ref[...] = jnp.concatenate(g_rows, axis=0).T


def _mix_route(x, ret_o, gate, mla_o, gnw, avg, wo, ln2, wq, keys, *, tm, pheads, nkeys, topk):
    t, d = x.shape
    rw = ret_o.shape[1]
    nsel = pheads * topk
    row = lambda i: (i, 0)
    fs = lambda a: pl.BlockSpec(a.shape, lambda i: (0,) * a.ndim)
    return pl.pallas_call(
        functools.partial(_mix_route_body, rw=rw, pheads=pheads, nkeys=nkeys, topk=topk),
        grid=(t // tm,),
        in_specs=[pl.BlockSpec((tm, d), row), pl.BlockSpec((tm, rw), row), pl.BlockSpec((tm, rw), row),
                  pl.BlockSpec((tm, mla_o.shape[1]), row), fs(gnw), fs(avg), fs(wo), fs(ln2), fs(wq), fs(keys)],
        out_specs=[pl.BlockSpec((tm, d), row), pl.BlockSpec((tm, d), row),
                   pl.BlockSpec((tm, nsel), row), pl.BlockSpec((tm, nsel), row)],
        out_shape=[jax.ShapeDtypeStruct((t, d), F32), jax.ShapeDtypeStruct((t, d), F32),
                   jax.ShapeDtypeStruct((t, nsel), jnp.int32),
                   jax.ShapeDtypeStruct((t, nsel), F32)],
        compiler_params=_params("parallel"), name="mix_route",
    )(x, ret_o, gate, mla_o, gnw, avg, wo, ln2, wq, keys)


def _gelu_gate_body(hid_ref, g_ref, a_ref):
    hid = hid_ref[...]
    a_ref[...] = 0.5 * hid * (1.0 + lax.erf(hid * (2.0 ** -0.5))) * g_ref[...]


def _gelu_gate(hid, g, *, tm):
    t, n = hid.shape
    blk = pl.BlockSpec((tm, n), lambda i: (i, 0))
    return pl.pallas_call(
        _gelu_gate_body, grid=(t // tm,), in_specs=[blk, blk], out_specs=blk,
        out_shape=jax.ShapeDtypeStruct((t, n), F32), compiler_params=_params("parallel"), name="gelu_gate",
    )(hid, g)


def _residual_body(h_ref, p_ref, lnf_ref, o_ref, *, final_norm):
    out = h_ref[...] + p_ref[...]
    if final_norm:
        out = _rms(out, lnf_ref[...])
    o_ref[...] = out


def _residual(h, peer, lnf, *, tm, final_norm):
    t, d = h.shape
    blk = pl.BlockSpec((tm, d), lambda i: (i, 0))
    return pl.pallas_call(
        functools.partial(_residual_body, final_norm=final_norm), grid=(t // tm,),
        in_specs=[blk, blk, pl.BlockSpec((1, d), lambda i: (0, 0))], out_specs=blk,
        out_shape=jax.ShapeDtypeStruct((t, d), F32), compiler_params=_params("parallel"), name="residual_norm",
    )(h, peer, lnf)


SC_CORES = 2
SC_SUBCORES = 16
SC_LANES = 16
SC_RING = 4


def _sc_worker_id():
    return lax.axis_index("s") * SC_CORES + lax.axis_index("c")


def _sc_ring(nq, start, wait, compute):
    for s in range(SC_RING - 1):
        start(s, s)

    @pl.loop(0, nq, step=SC_RING)
    def _(q0):
        for s in range(SC_RING):
            q = q0 + s
            nxt = q + SC_RING - 1

            @pl.when(nxt < nq)
            def _():
                start(nxt, (s + SC_RING - 1) % SC_RING)

            wait(q, s)
            compute(q, s)


def _peer_hidden_sc(xn, idx, u_tab):
    t, d = xn.shape
    nsel = idx.shape[1]
    nw = SC_CORES * SC_SUBCORES
    per_w = t // nw
    tb = min(16, per_w)
    nchunk = nsel // SC_LANES
    shift = nchunk.bit_length() - 1
    ncol = d // SC_LANES
    nq = tb * nchunk
    assert per_w * nw == t and per_w % tb == 0 and nchunk == 1 << shift and nq % SC_RING == 0
    mesh = plsc.VectorSubcoreMesh(core_axis_name="c", subcore_axis_name="s")

    def body(x_hbm, idx_hbm, u_hbm, out_hbm, idx_v, x_v, ubuf, hid_v, sem):
        wid = _sc_worker_id()
        lane = lax.iota(jnp.int32, SC_LANES)

        def gather(q, slot):
            tok = lax.shift_right_logical(q, shift)
            ch = q & (nchunk - 1)
            rows = idx_v.at[tok, pl.ds(ch * SC_LANES, SC_LANES)]
            return pltpu.make_async_copy(u_hbm.at[rows], ubuf.at[slot], sem.at[slot])

        def compute(q, slot):
            tok = lax.shift_right_logical(q, shift)
            ch = q & (nchunk - 1)

            def col(c, accs):
                cs = pl.ds(pl.multiple_of(c * SC_LANES, SC_LANES), SC_LANES)
                xc = x_v[tok, cs]
                return tuple(a + xc * ubuf[slot, k, cs] for k, a in enumerate(accs))

            accs = lax.fori_loop(0, ncol, col, tuple(jnp.zeros((SC_LANES,), F32) for _ in range(SC_LANES)))
            out = jnp.zeros((SC_LANES,), F32)
            for k in range(SC_LANES):
                out = jnp.where(lane == k, jnp.sum(accs[k]), out)
            hid_v[tok, pl.ds(ch * SC_LANES, SC_LANES)] = out

        @pl.loop(0, per_w // tb)
        def _(b):
            base = wid * per_w + b * tb
            pltpu.sync_copy(idx_hbm.at[pl.ds(base, tb)], idx_v)
            pltpu.sync_copy(x_hbm.at[pl.ds(base, tb)], x_v)
            _sc_ring(nq, lambda q, s: gather(q, s).start(), lambda q, s: gather(q, s).wait(), compute)
            pltpu.sync_copy(hid_v, out_hbm.at[pl.ds(base, tb)])

    return pl.kernel(
        body, out_type=jax.ShapeDtypeStruct((t, nsel), F32), mesh=mesh,
        scratch_types=[pltpu.VMEM((tb, nsel), jnp.int32), pltpu.VMEM((tb, d), F32),
                       pltpu.VMEM((SC_RING, SC_LANES, d), F32), pltpu.VMEM((tb, nsel), F32),
                       pltpu.SemaphoreType.DMA((SC_RING,))],
        compiler_params=pltpu.CompilerParams(needs_layout_passes=False), name="peer_hidden_sc",
    )(xn, idx, u_tab)


def _peer_mix_sc(act, idx, v_tab):
    t, nsel = act.shape
    d = v_tab.shape[1]
    nw = SC_CORES * SC_SUBCORES
    per_w = t // nw
    tb = min(16, per_w)
    nchunk = nsel // SC_LANES
    shift = nchunk.bit_length() - 1
    ncol = d // SC_LANES
    nq = tb * nchunk
    assert per_w * nw == t and per_w % tb == 0 and nchunk == 1 << shift and nq % SC_RING == 0
    mesh = plsc.VectorSubcoreMesh(core_axis_name="c", subcore_axis_name="s")

    def body(a_hbm, idx_hbm, v_hbm, out_hbm, idx_v, a_v, vbuf, o_v, sem):
        wid = _sc_worker_id()
        zero = jnp.zeros((SC_LANES,), F32)

        def gather(q, slot):
            tok = lax.shift_right_logical(q, shift)
            ch = q & (nchunk - 1)
            rows = idx_v.at[tok, pl.ds(ch * SC_LANES, SC_LANES)]
            return pltpu.make_async_copy(v_hbm.at[rows], vbuf.at[slot], sem.at[slot])

        def compute(q, slot):
            tok = lax.shift_right_logical(q, shift)
            ch = q & (nchunk - 1)
            tok_v = jnp.full((SC_LANES,), tok, jnp.int32)
            col_v = jnp.full((SC_LANES,), ch * SC_LANES, jnp.int32)
            w = [plsc.load_gather(a_v, [tok_v, col_v + k]) for k in range(SC_LANES)]

            @pl.loop(0, ncol)
            def _(c):
                cs = pl.ds(pl.multiple_of(c * SC_LANES, SC_LANES), SC_LANES)
                acc = w[0] * vbuf[slot, 0, cs]
                for k in range(1, SC_LANES):
                    acc = acc + w[k] * vbuf[slot, k, cs]
                o_v[tok, cs] = o_v[tok, cs] + acc

        @pl.loop(0, per_w // tb)
        def _(b):
            base = wid * per_w + b * tb
            pltpu.sync_copy(idx_hbm.at[pl.ds(base, tb)], idx_v)
            pltpu.sync_copy(a_hbm.at[pl.ds(base, tb)], a_v)

            @pl.loop(0, tb)
            def _(r):
                @pl.loop(0, ncol)
                def _(c):
                    o_v[r, pl.ds(pl.multiple_of(c * SC_LANES, SC_LANES), SC_LANES)] = zero

            _sc_ring(nq, lambda q, s: gather(q, s).start(), lambda q, s: gather(q, s).wait(), compute)
            pltpu.sync_copy(o_v, out_hbm.at[pl.ds(base, tb)])

    return pl.kernel(
        body, out_type=jax.ShapeDtypeStruct((t, d), F32), mesh=mesh,
        scratch_types=[pltpu.VMEM((tb, nsel), jnp.int32), pltpu.VMEM((tb, nsel), F32),
                       pltpu.VMEM((SC_RING, SC_LANES, d), F32), pltpu.VMEM((tb, d), F32),
                       pltpu.SemaphoreType.DMA((SC_RING,))],
        compiler_params=pltpu.CompilerParams(needs_layout_passes=False), name="peer_mix_sc",
    )(act, idx, v_tab)


def _rope_tables(pos, half, group, width, lo):
    inv = ROPE_BASE ** (-jnp.arange(half, dtype=F32) / half)
    ang = pos.astype(F32)[:, None] * inv[None, :]
    cos, sin = jnp.cos(ang), jnp.sin(ang)
    n = pos.shape[0]
    reps = width // group
    pad_hi = group - lo - 2 * half
    blk = lambda a, b, fill: jnp.concatenate(
        [jnp.full((n, lo), fill, F32), a, b, jnp.full((n, pad_hi), fill, F32)], axis=1)
    z = jnp.zeros_like(sin)
    c = blk(cos, cos, 1.0)
    sa = blk(-sin, z, 0.0)
    sb = blk(z, sin, 0.0)
    return [jnp.tile(a, (1, reps)) for a in (c, sa, sb)]


def _ret_log_decay(nheads):
    return jnp.log(1.0 - jnp.exp2(-5.0 - jnp.arange(nheads, dtype=F32)))


def _pair_states(s):
    b, h, dk, dv = s.shape
    s = s.reshape(b, h // 2, 2, dk, dv)
    z = jnp.zeros_like(s[:, :, 0])
    top = jnp.concatenate([s[:, :, 0], z], axis=-1)
    bot = jnp.concatenate([z, s[:, :, 1]], axis=-1)
    return jnp.concatenate([top, bot], axis=-2)


def _unpair_states(sp, dk, dv):
    b, hp = sp.shape[:2]
    return jnp.stack([sp[:, :, :dk, :dv], sp[:, :, dk:, dv:]], axis=2).reshape(b, 2 * hp, dk, dv)


def _layer_weights(ln1_w, w_in, ret_gn_w, q_norm_w, w_uq, kv_norm_w, w_uk, w_uv, w_o, ln2_w,
                   peer_w_q, peer_sub_keys, dims):
    d = w_in.shape[0]
    nheads, nope, rope, vdim = dims["nheads"], dims["nope"], dims["mla_rope"], dims["vdim"]
    o6 = 4 * dims["rw"] + dims["qrank"] + dims["kvrank"]
    zc = lambda r, c: jnp.zeros((r, c), F32)
    win_p = jnp.concatenate([w_in[:, :o6], zc(d, nope), w_in[:, o6:], zc(d, LANES - nope - rope)], axis=1)
    qr, kr = w_uq.shape[0], w_uk.shape[0]
    wuq_p = jnp.concatenate([w_uq, jnp.zeros((qr, nheads, LANES - nope - rope), F32)], axis=2).reshape(qr, -1)
    wuk_p = jnp.concatenate([w_uk, jnp.zeros((kr, nheads, LANES - nope), F32)], axis=2).reshape(kr, -1)
    zv = jnp.zeros((kr, nheads // 2, LANES - vdim), F32)
    wv = w_uv.reshape(kr, nheads // 2, 2, vdim)
    wuv_p = jnp.concatenate([wv[:, :, 0], zv, zv, wv[:, :, 1]], axis=2).reshape(kr, -1)
    wuk3 = jnp.concatenate([jnp.transpose(w_uk, (1, 2, 0)),
                            jnp.zeros((nheads, LANES - nope, kr), F32)], axis=1)
    wuv3 = jnp.transpose(w_uv, (1, 0, 2))
    gidx = jnp.arange(dims["rw"]) // dims["ret_dv"]
    avg = (gidx[:, None] == gidx[None, :]).astype(F32) / dims["ret_dv"]
    keys = peer_sub_keys.reshape(-1, peer_sub_keys.shape[2], peer_sub_keys.shape[3])
    c = lambda a: a.astype(MXU_DTYPE)
    r2 = lambda a: a.reshape(1, -1)
    return dict(ln1=r2(ln1_w), win_p=c(win_p), gnw=r2(ret_gn_w), qnw=r2(q_norm_w), kvnw=r2(kv_norm_w),
                wuq_p=c(wuq_p), wuk_p=c(wuk_p), wuv_p=c(wuv_p), wuk3=c(wuk3), wuv3=c(wuv3), avg=c(avg),
                wo=c(w_o), ln2=r2(ln2_w), wq=c(peer_w_q), keys=c(keys))


def _stream_layer(x, w, tabs, lg, s0, u_tab, v_tab, lnf, dims, *, nbatch, ret_rows, ret_chunk, tm,
                  final_norm, cache=None):
    seq = x.shape[0] // nbatch
    nheads, dk, dv = dims["nheads"], dims["ret_dk"], dims["ret_dv"]
    qr, kr, vr, gate, qm, km, vm, ckv, kpe = _inproj(
        x, tabs, w["ln1"], w["win_p"], w["qnw"], w["kvnw"], w["wuq_p"], w["wuk_p"], w["wuv_p"], tm=tm, dims=dims)
    ret_o, s_pairs = _retention(lg, qr, kr, vr, _pair_states(s0), nbatch=nbatch, rows=ret_rows,
                                chunk=ret_chunk, dk=dk)
    scale = (dims["nope"] + dims["mla_rope"]) ** -0.5
    if cache is None:
        mla_o = _flash(qm, km, vm, nbatch=nbatch, tq=min(256, seq), scale=scale, nheads=nheads, chunk=CHUNK)
    else:
        mla_o = _decode_attn(qm, cache[0], cache[1], ckv, kpe, w["wuk3"], w["wuv3"], nq=seq,
                             nope=dims["nope"], rope=dims["mla_rope"], scale=scale)
    h, hn, idx, g = _mix_route(x, ret_o, gate, mla_o, w["gnw"], w["avg"], w["wo"], w["ln2"], w["wq"], w["keys"],
                               tm=tm, pheads=dims["pheads"], nkeys=dims["nkeys"], topk=dims["topk"])
    act = _gelu_gate(_peer_hidden_sc(hn, idx, u_tab), g, tm=tm)
    out = _residual(h, _peer_mix_sc(act, idx, v_tab), lnf, tm=tm, final_norm=final_norm)
    nope, rope = dims["nope"], dims["mla_rope"]
    return out, ckv, kpe[:, nope:nope + rope], _unpair_states(s_pairs, dk, dv)


def kernel(x_prompt, x_sample, cache_mla_ckv, cache_mla_krope, state_retention, ln1_w, w_in, ret_gn_w,
           mla_q_norm_w, mla_w_uq, mla_kv_norm_w, mla_w_uk, mla_w_uv, w_o, ln2_w, peer_w_q, peer_sub_keys,
           peer_u, peer_v, lnf_w):
    depth = w_in.shape[0]
    nb, seq, d = x_prompt.shape
    db, dseq, _ = x_sample.shape
    past = cache_mla_ckv.shape[2]
    rheads, dk, dv = state_retention.shape[2:]
    nkeys = peer_sub_keys.shape[3]
    dims = dict(rw=rheads * dk, ret_dk=dk, ret_dv=dv, qrank=mla_w_uq.shape[1], kvrank=mla_w_uk.shape[1],
                nheads=mla_w_uq.shape[2], nope=mla_w_uk.shape[3], vdim=mla_w_uv.shape[3],
                mla_rope=mla_w_uq.shape[3] - mla_w_uk.shape[3], pheads=peer_sub_keys.shape[1], nkeys=nkeys,
                topk=PEER_TOPK)
    assert rheads * dk == rheads * dv and dims["nheads"] % 2 == 0 and dk * 2 == LANES and dims["vdim"] * 2 == LANES

    def tables(pos):
        return (_rope_tables(pos, dk // 2, dk, dims["rw"], 0)
                + _rope_tables(pos, dims["mla_rope"] // 2, LANES, LANES, dims["nope"]))

    tabs_p = tables(jnp.arange(seq))
    tabs_s = tables(jnp.tile(past + jnp.arange(dseq), db))
    lg = _ret_log_decay(rheads)
    lnf = lnf_w.reshape(1, -1)
    hp = x_prompt.reshape(nb * seq, d)
    hs = x_sample.reshape(db * dseq, d)
    outs = [[] for _ in range(6)]
    for l in range(depth):
        w = _layer_weights(ln1_w[l], w_in[l], ret_gn_w[l], mla_q_norm_w[l], mla_w_uq[l], mla_kv_norm_w[l],
                           mla_w_uk[l], mla_w_uv[l], w_o[l], ln2_w[l], peer_w_q[l], peer_sub_keys[l], dims)
        last = l == depth - 1
        hp, c1, k1, s1 = _stream_layer(
            hp, w, tabs_p, lg, jnp.zeros((nb, rheads, dk, dv), F32), peer_u[l], peer_v[l], lnf, dims,
            nbatch=nb, ret_rows=min(256, seq), ret_chunk=CHUNK, tm=min(256, nb * seq),
            final_norm=last)
        hs, c2, k2, s2 = _stream_layer(
            hs, w, tabs_s, lg, state_retention[l], peer_u[l], peer_v[l], lnf, dims,
            nbatch=db, ret_rows=dseq, ret_chunk=dseq, tm=min(256, db * dseq),
            final_norm=last, cache=(cache_mla_ckv[l], cache_mla_krope[l]))
        for acc, val in zip(outs, (c1.reshape(nb, seq, -1), k1.reshape(nb, seq, -1), s1,
                                   c2.reshape(db, dseq, -1), k2.reshape(db, dseq, -1), s2)):
            acc.append(val)
    return (hp.reshape(nb, seq, d), hs.reshape(db, dseq, d), *[jnp.stack(o) for o in outs])
```

```python
import functools

import jax
import jax.numpy as jnp
from jax import lax
from jax.experimental import pallas as pl
from jax.experimental.pallas import tpu as pltpu
from jax.experimental.pallas import tpu_sc as plsc

EPS = 1e-6
ROPE_BASE = 10000.0
CHUNK = 64
PEER_TOPK = 16
LANES = 128
MXU_DTYPE = jnp.bfloat16
VMEM_LIMIT_BYTES = 56 * 1024 * 1024

F32 = jnp.float32
NEG_INF = float("-inf")


def _mm(a, b):
    return jnp.dot(a.astype(MXU_DTYPE), b.astype(MXU_DTYPE), preferred_element_type=F32)


def _mm_nt(a, b):
    return lax.dot_general(a.astype(MXU_DTYPE), b.astype(MXU_DTYPE),
                           (((1,), (1,)), ((), ())), preferred_element_type=F32)


def _mm_tn(a, b):
    return lax.dot_general(a.astype(MXU_DTYPE), b.astype(MXU_DTYPE),
                           (((0,), (0,)), ((), ())), preferred_element_type=F32)


def _rms(x, w):
    return x * lax.rsqrt(jnp.mean(x * x, axis=-1, keepdims=True) + EPS) * w


def _rope(t, c, sa, sb, half):
    n = t.shape[1]
    return t * c + pltpu.roll(t, n - half, 1) * sa + pltpu.roll(t, half, 1) * sb


def _params(*sem):
    return pltpu.CompilerParams(dimension_semantics=sem, vmem_limit_bytes=VMEM_LIMIT_BYTES)


def _inproj_body(x_ref, ln1_ref, win_ref, cr_ref, sar_ref, sbr_ref, cm_ref, sam_ref, sbm_ref,
                 qnw_ref, kvnw_ref, wuq_ref, wuk_ref, wuv_ref,
                 qr_ref, kr_ref, vr_ref, gate_ref, qm_ref, km_ref, vm_ref, ckv_ref, kpe_ref,
                 *, rw, qrank, kvrank, ret_half, mla_half, k_scale, nheads):
    n1 = _rms(x_ref[...], ln1_ref[...])
    proj = _mm(n1, win_ref[...])
    cr, sar, sbr = cr_ref[...], sar_ref[...], sbr_ref[...]
    qr_ref[...] = _rope(proj[:, 0:rw], cr, sar, sbr, ret_half)
    kr_ref[...] = _rope(proj[:, rw:2 * rw], cr, sar, sbr, ret_half) * k_scale
    vr_ref[...] = proj[:, 2 * rw:3 * rw]
    gate_ref[...] = proj[:, 3 * rw:4 * rw]
    o4 = 4 * rw
    o5 = o4 + qrank
    o6 = o5 + kvrank
    cm, sam, sbm = cm_ref[...], sam_ref[...], sbm_ref[...]
    tile = lambda t: jnp.concatenate([t] * nheads, axis=1)
    cq = _rms(proj[:, o4:o5], qnw_ref[...])
    qm = _rope(_mm(cq, wuq_ref[...]), tile(cm), tile(sam), tile(sbm), mla_half)
    qm_ref[...] = qm.astype(qm_ref.dtype)
    ckv = _rms(proj[:, o5:o6], kvnw_ref[...])
    ckv_ref[...] = ckv
    kpe = _rope(proj[:, o6:o6 + LANES], cm, sam, sbm, mla_half)
    kpe_ref[...] = kpe
    km_ref[...] = (_mm(ckv, wuk_ref[...]) + tile(kpe)).astype(km_ref.dtype)
    vm_ref[...] = _mm(ckv, wuv_ref[...]).astype(vm_ref.dtype)


def _inproj(x, tabs, ln1, win_p, qnw, kvnw, wuq_p, wuk_p, wuv_p, *, tm, dims):
    t, d = x.shape
    rw, nheads = dims["rw"], dims["nheads"]
    hp = nheads * LANES
    nblk_tab = tabs[0].shape[0] // tm
    row = lambda i: (i, 0)
    tab = lambda i: (i % nblk_tab, 0)
    full = lambda i: (0, 0)
    fs = lambda a: pl.BlockSpec(a.shape, full)
    in_specs = [pl.BlockSpec((tm, d), row), fs(ln1), fs(win_p)]
    in_specs += [pl.BlockSpec((tm, rw), tab)] * 3 + [pl.BlockSpec((tm, LANES), tab)] * 3
    in_specs += [fs(qnw), fs(kvnw), fs(wuq_p), fs(wuk_p), fs(wuv_p)]
    out_shape = [jax.ShapeDtypeStruct((t, rw), F32)] * 4
    out_shape += [jax.ShapeDtypeStruct((t, hp), MXU_DTYPE)] * 3
    out_shape += [jax.ShapeDtypeStruct((t, dims["kvrank"]), F32), jax.ShapeDtypeStruct((t, LANES), F32)]
    out_specs = [pl.BlockSpec((tm, rw), row)] * 4 + [pl.BlockSpec((tm, hp), row)] * 3
    out_specs += [pl.BlockSpec((tm, dims["kvrank"]), row), pl.BlockSpec((tm, LANES), row)]
    body = functools.partial(
        _inproj_body, rw=rw, qrank=dims["qrank"], kvrank=dims["kvrank"], ret_half=dims["ret_dk"] // 2,
        mla_half=dims["mla_rope"] // 2, k_scale=dims["ret_dk"] ** -0.5, nheads=nheads)
    return pl.pallas_call(
        body, grid=(t // tm,), in_specs=in_specs, out_specs=out_specs, out_shape=out_shape,
        compiler_params=_params("parallel"), name="inproj",
    )(x, ln1, win_p, *tabs, qnw, kvnw, wuq_p, wuk_p, wuv_p)


def _retention_body(lg_ref, q_ref, k_ref, v_ref, s0_ref, o_ref, sout_ref, s_scr, *, rows, chunk, dk):
    hp = pl.program_id(1)
    j = pl.program_id(2)

    @pl.when(j == 0)
    def _():
        s_scr[...] = s0_ref[0, 0]

    lane = lax.broadcasted_iota(jnp.int32, (1, LANES), 1)
    is_a = lane < dk
    lg_a = lg_ref[2 * hp]
    lg_b = lg_ref[2 * hp + 1]
    lgl = jnp.where(is_a, lg_a, lg_b)
    r = lax.broadcasted_iota(jnp.int32, (rows, 1), 0).astype(F32)
    q, k, v = q_ref[...], k_ref[...], v_ref[...]
    q_dec = q * jnp.exp(lgl * (r + 1.0))
    k_dec = k * jnp.exp(lgl * (float(rows) - 1.0 - r))
    ri = lax.broadcasted_iota(jnp.int32, (rows, rows), 0)
    ci = lax.broadcasted_iota(jnp.int32, (rows, rows), 1)
    dist = jnp.abs(ri - ci).astype(F32)
    visible = (ci // chunk) <= (ri // chunk)
    o = _mm(q_dec, s_scr[...])
    for first, lg in ((True, lg_a), (False, lg_b)):
        sel = is_a if first else jnp.logical_not(is_a)
        qh = jnp.where(sel, q, 0.0)
        vh = jnp.where(sel, v, 0.0)
        decay = jnp.where(visible, jnp.exp(lg * dist), 0.0)
        o = o + _mm(_mm_nt(qh, k) * decay, vh)
    o_ref[...] = o
    sr = lax.broadcasted_iota(jnp.int32, (LANES, LANES), 0) < dk
    sc = lax.broadcasted_iota(jnp.int32, (LANES, LANES), 1) < dk
    kv = jnp.where(sr == sc, _mm_tn(k_dec, v), 0.0)
    s_new = jnp.exp(lgl * float(rows)) * s_scr[...] + kv
    s_scr[...] = s_new

    @pl.when(j == pl.num_programs(2) - 1)
    def _():
        sout_ref[0, 0] = s_new


def _retention(lg, q, k, v, s0_pairs, *, nbatch, rows, chunk, dk):
    t, w = q.shape
    npairs = w // LANES
    nblk = t // (nbatch * rows)
    blk = pl.BlockSpec((rows, LANES), lambda b, p, j: (b * nblk + j, p))
    st = pl.BlockSpec((1, 1, LANES, LANES), lambda b, p, j: (b, p, 0, 0))
    return pl.pallas_call(
        functools.partial(_retention_body, rows=rows, chunk=chunk, dk=dk),
        grid=(nbatch, npairs, nblk),
        in_specs=[pl.BlockSpec(memory_space=pltpu.SMEM), blk, blk, blk, st],
        out_specs=[blk, st],
        out_shape=[jax.ShapeDtypeStruct((t, w), F32),
                   jax.ShapeDtypeStruct((nbatch, npairs, LANES, LANES), F32)],
        scratch_shapes=[pltpu.VMEM((LANES, LANES), F32)],
        compiler_params=_params("parallel", "parallel", "arbitrary"), name="retention",
    )(lg, q, k, v, s0_pairs)


def _flash_body(q_ref, k_ref, v_ref, o_ref, *, tq, scale, nheads, chunk):
    i = pl.program_id(1)
    ri = lax.broadcasted_iota(jnp.int32, (tq, tq), 0) // chunk
    ci = lax.broadcasted_iota(jnp.int32, (tq, tq), 1) // chunk
    visible = ci <= ri

    def head(h):
        cols = slice(h * LANES, (h + 1) * LANES)
        q = q_ref[:, cols]

        def step(j, carry, diagonal):
            m, l, acc = carry
            off = pl.multiple_of(j * tq, tq)
            s = _mm_nt(q, k_ref[pl.ds(off, tq), cols]) * scale
            if diagonal:
                s = jnp.where(visible, s, NEG_INF)
            m_new = jnp.maximum(m, jnp.max(s, axis=1, keepdims=True))
            alpha = jnp.exp(m - m_new)
            p = jnp.exp(s - m_new)
            l = alpha * l + jnp.sum(p, axis=1, keepdims=True)
            acc = alpha * acc + _mm(p, v_ref[pl.ds(off, tq), cols])
            return m_new, l, acc

        init = (jnp.full((tq, 1), NEG_INF, F32), jnp.zeros((tq, 1), F32), jnp.zeros((tq, LANES), F32))
        carry = lax.fori_loop(0, i, functools.partial(step, diagonal=False), init)
        _, l, acc = step(i, carry, True)
        return acc / l

    for p in range(nheads // 2):
        o_ref[:, p * LANES:(p + 1) * LANES] = head(2 * p) + head(2 * p + 1)


def _flash(qm, km, vm, *, nbatch, tq, scale, nheads, chunk):
    t, hp = qm.shape
    s = t // nbatch
    nq = s // tq
    ow = nheads // 2 * LANES
    return pl.pallas_call(
        functools.partial(_flash_body, tq=tq, scale=scale, nheads=nheads, chunk=chunk),
        grid=(nbatch, nq),
        in_specs=[pl.BlockSpec((tq, hp), lambda b, i: (b * nq + i, 0)),
                  pl.BlockSpec((s, hp), lambda b, i: (b, 0)),
                  pl.BlockSpec((s, hp), lambda b, i: (b, 0))],
        out_specs=pl.BlockSpec((tq, ow), lambda b, i: (b * nq + i, 0)),
        out_shape=jax.ShapeDtypeStruct((t, ow), F32),
        compiler_params=_params("parallel", "arbitrary"), name="flash_mla",
    )(qm, km, vm)


def _decode_attn_body(q_ref, cpast_ref, kpast_ref, cnew_ref, knew_ref, wuk_ref, wuv_ref, o_ref,
                      *, nheads, nope, rope, scale):
    c_past = cpast_ref[0]
    k_past = kpast_ref[0]
    c_new = cnew_ref[...]
    k_new = knew_ref[:, nope:nope + rope]
    outs = []
    for h in range(nheads):
        q = q_ref[:, h * LANES:(h + 1) * LANES]
        q_lat = _mm(q, wuk_ref[h])
        q_pe = q[:, nope:nope + rope]
        s_p = (_mm_nt(q_lat, c_past) + _mm_nt(q_pe, k_past)) * scale
        s_n = (_mm_nt(q_lat, c_new) + _mm_nt(q_pe, k_new)) * scale
        m = jnp.maximum(jnp.max(s_p, axis=1, keepdims=True), jnp.max(s_n, axis=1, keepdims=True))
        p_p = jnp.exp(s_p - m)
        p_n = jnp.exp(s_n - m)
        l = jnp.sum(p_p, axis=1, keepdims=True) + jnp.sum(p_n, axis=1, keepdims=True)
        o_lat = (_mm(p_p, c_past) + _mm(p_n, c_new)) / l
        outs.append(_mm(o_lat, wuv_ref[h]))
    o_ref[...] = jnp.concatenate(outs, axis=1)


def _decode_attn(qm, c_past, k_past, c_new, kpe_new, wuk3, wuv3, *, nq, nope, rope, scale):
    nb, past, kvr = c_past.shape
    nheads, _, vdim = wuv3.shape
    t, hp = qm.shape
    row = lambda b: (b, 0)
    full3 = lambda b: (0, 0, 0)
    return pl.pallas_call(
        functools.partial(_decode_attn_body, nheads=nheads, nope=nope, rope=rope, scale=scale),
        grid=(nb,),
        in_specs=[pl.BlockSpec((nq, hp), row),
                  pl.BlockSpec((1, past, kvr), lambda b: (b, 0, 0)),
                  pl.BlockSpec((1, past, rope), lambda b: (b, 0, 0)),
                  pl.BlockSpec((nq, kvr), row),
                  pl.BlockSpec((nq, LANES), row),
                  pl.BlockSpec(wuk3.shape, full3),
                  pl.BlockSpec(wuv3.shape, full3)],
        out_specs=pl.BlockSpec((nq, nheads * vdim), row),
        out_shape=jax.ShapeDtypeStruct((t, nheads * vdim), F32),
        compiler_params=_params("parallel"), name="decode_mla",
    )(qm, c_past, k_past, c_new, kpe_new, wuk3, wuv3)


def _split3(x):
    a = x.astype(MXU_DTYPE)
    r = x - a.astype(F32)
    b = r.astype(MXU_DTYPE)
    c = (r - b.astype(F32)).astype(MXU_DTYPE)
    return a, b, c


def _group_mean(x, avg):
    a, b, c = _split3(x)
    dot = lambda t: jnp.dot(t, avg, preferred_element_type=F32)
    return dot(a) + dot(b) + dot(c)


def _topk_rows(s, payload, kk):
    n = s.shape[0]
    rid = lax.broadcasted_iota(jnp.int32, s.shape, 0)
    vals, pays = [], []
    for _ in range(kk):
        mx = jnp.max(s, axis=0, keepdims=True)
        first = jnp.min(jnp.where(s == mx, rid, n), axis=0, keepdims=True)
        hit = rid == first
        vals.append(mx)
        pays.append(jnp.max(jnp.where(hit, payload, -1), axis=0, keepdims=True))
        s = jnp.where(hit, NEG_INF, s)
    return jnp.concatenate(vals, axis=0), jnp.concatenate(pays, axis=0)


def _mix_route_body(x_ref, ret_ref, gate_ref, mla_ref, gnw_ref, avg_ref, wo_ref, ln2_ref, wq_ref, keys_ref,
                    h_ref, hn_ref, idx_ref, g_ref, *, rw, pheads, nkeys, topk):
    ret = ret_ref[...]
    avg = avg_ref[...]
    mu = _group_mean(ret, avg)
    cen = ret - mu
    var = _group_mean(cen * cen, avg)
    gate = gate_ref[...]
    y = cen * lax.rsqrt(var + EPS) * gnw_ref[...] * (gate * jax.nn.sigmoid(gate))
    h = x_ref[...] + _mm(y, wo_ref[0:rw, :]) + _mm(mla_ref[...], wo_ref[rw:, :])
    h_ref[...] = h
    hn = _rms(h, ln2_ref[...])
    hn_ref[...] = hn
    qp = _mm(hn, wq_ref[...])
    kid = lax.broadcasted_iota(jnp.int32, (nkeys, qp.shape[0]), 0)
    idx_rows, g_rows = [], []
    for hd in range(pheads):
        ts, ti = [], []
        for half in range(2):
            c = (2 * hd + half) * LANES
            st = _mm_nt(keys_ref[2 * hd + half], qp[:, c:c + LANES])
            v, i = _topk_rows(st, kid, topk)
            ts.append(v)
            ti.append(i)
        cand = jnp.concatenate([ts[0][a:a + 1, :] + ts[1] for a in range(topk)], axis=0)
        cidx = jnp.concatenate([ti[0][a:a + 1, :] * nkeys + ti[1] for a in range(topk)], axis=0)
        best, expert = _topk_rows(cand, cidx, topk)
        e = jnp.exp(best - best[0:1, :])
        g_rows.append(e / jnp.sum(e, axis=0, keepdims=True))
        idx_rows.append(expert)
    idx_ref[...] = jnp.concatenate(idx_rows, axis=0).T
    g_ref[...] = jnp.concatenate(g_rows, axis=0).T


def _mix_route(x, ret_o, gate, mla_o, gnw, avg, wo, ln2, wq, keys, *, tm, pheads, nkeys, topk):
    t, d = x.shape
    rw = ret_o.shape[1]
    nsel = pheads * topk
    row = lambda i: (i, 0)
    fs = lambda a: pl.BlockSpec(a.shape, lambda i: (0,) * a.ndim)
    return pl.pallas_call(
        functools.partial(_mix_route_body, rw=rw, pheads=pheads, nkeys=nkeys, topk=topk),
        grid=(t // tm,),
        in_specs=[pl.BlockSpec((tm, d), row), pl.BlockSpec((tm, rw), row), pl.BlockSpec((tm, rw), row),
                  pl.BlockSpec((tm, mla_o.shape[1]), row), fs(gnw), fs(avg), fs(wo), fs(ln2), fs(wq), fs(keys)],
        out_specs=[pl.BlockSpec((tm, d), row), pl.BlockSpec((tm, d), row),
                   pl.BlockSpec((tm, nsel), row), pl.BlockSpec((tm, nsel), row)],
        out_shape=[jax.ShapeDtypeStruct((t, d), F32), jax.ShapeDtypeStruct((t, d), F32),
                   jax.ShapeDtypeStruct((t, nsel), jnp.int32),
                   jax.ShapeDtypeStruct((t, nsel), F32)],
        compiler_params=_params("parallel"), name="mix_route",
    )(x, ret_o, gate, mla_o, gnw, avg, wo, ln2, wq, keys)


def _gelu_gate_body(hid_ref, g_ref, a_ref):
    hid = hid_ref[...]
    a_ref[...] = 0.5 * hid * (1.0 + lax.erf(hid * (2.0 ** -0.5))) * g_ref[...]


def _gelu_gate(hid, g, *, tm):
    t, n = hid.shape
    blk = pl.BlockSpec((tm, n), lambda i: (i, 0))
    return pl.pallas_call(
        _gelu_gate_body, grid=(t // tm,), in_specs=[blk, blk], out_specs=blk,
        out_shape=jax.ShapeDtypeStruct((t, n), F32), compiler_params=_params("parallel"), name="gelu_gate",
    )(hid, g)


def _residual_body(h_ref, p_ref, lnf_ref, o_ref, *, final_norm):
    out = h_ref[...] + p_ref[...]
    if final_norm:
        out = _rms(out, lnf_ref[...])
    o_ref[...] = out


def _residual(h, peer, lnf, *, tm, final_norm):
    t, d = h.shape
    blk = pl.BlockSpec((tm, d), lambda i: (i, 0))
    return pl.pallas_call(
        functools.partial(_residual_body, final_norm=final_norm), grid=(t // tm,),
        in_specs=[blk, blk, pl.BlockSpec((1, d), lambda i: (0, 0))], out_specs=blk,
        out_shape=jax.ShapeDtypeStruct((t, d), F32), compiler_params=_params("parallel"), name="residual_norm",
    )(h, peer, lnf)


SC_CORES = 2
SC_SUBCORES = 16
SC_LANES = 16
SC_RING = 4


def _sc_worker_id():
    return lax.axis_index("s") * SC_CORES + lax.axis_index("c")


def _sc_ring(nq, start, wait, compute):
    for s in range(SC_RING - 1):
        start(s, s)

    @pl.loop(0, nq, step=SC_RING)
    def _(q0):
        for s in range(SC_RING):
            q = q0 + s
            nxt = q + SC_RING - 1

            @pl.when(nxt < nq)
            def _():
                start(nxt, (s + SC_RING - 1) % SC_RING)

            wait(q, s)
            compute(q, s)


def _peer_hidden_sc(xn, idx, u_tab):
    t, d = xn.shape
    nsel = idx.shape[1]
    nw = SC_CORES * SC_SUBCORES
    per_w = t // nw
    tb = min(16, per_w)
    nchunk = nsel // SC_LANES
    shift = nchunk.bit_length() - 1
    ncol = d // SC_LANES
    nq = tb * nchunk
    assert per_w * nw == t and per_w % tb == 0 and nchunk == 1 << shift and nq % SC_RING == 0
    mesh = plsc.VectorSubcoreMesh(core_axis_name="c", subcore_axis_name="s")

    def body(x_hbm, idx_hbm, u_hbm, out_hbm, idx_v, x_v, ubuf, hid_v, sem):
        wid = _sc_worker_id()
        lane = lax.iota(jnp.int32, SC_LANES)

        def gather(q, slot):
            tok = lax.shift_right_logical(q, shift)
            ch = q & (nchunk - 1)
            rows = idx_v.at[tok, pl.ds(ch * SC_LANES, SC_LANES)]
            return pltpu.make_async_copy(u_hbm.at[rows], ubuf.at[slot], sem.at[slot])

        def compute(q, slot):
            tok = lax.shift_right_logical(q, shift)
            ch = q & (nchunk - 1)

            def col(c, accs):
                cs = pl.ds(pl.multiple_of(c * SC_LANES, SC_LANES), SC_LANES)
                xc = x_v[tok, cs]
                return tuple(a + xc * ubuf[slot, k, cs] for k, a in enumerate(accs))

            accs = lax.fori_loop(0, ncol, col, tuple(jnp.zeros((SC_LANES,), F32) for _ in range(SC_LANES)),
                                 unroll=2)
            out = jnp.zeros((SC_LANES,), F32)
            for k in range(SC_LANES):
                out = jnp.where(lane == k, jnp.sum(accs[k]), out)
            hid_v[tok, pl.ds(ch * SC_LANES, SC_LANES)] = out

        @pl.loop(0, per_w // tb)
        def _(b):
            base = wid * per_w + b * tb
            pltpu.sync_copy(idx_hbm.at[pl.ds(base, tb)], idx_v)
            pltpu.sync_copy(x_hbm.at[pl.ds(base, tb)], x_v)
            _sc_ring(nq, lambda q, s: gather(q, s).start(), lambda q, s: gather(q, s).wait(), compute)
            pltpu.sync_copy(hid_v, out_hbm.at[pl.ds(base, tb)])

    return pl.kernel(
        body, out_type=jax.ShapeDtypeStruct((t, nsel), F32), mesh=mesh,
        scratch_types=[pltpu.VMEM((tb, nsel), jnp.int32), pltpu.VMEM((tb, d), F32),
                       pltpu.VMEM((SC_RING, SC_LANES, d), F32), pltpu.VMEM((tb, nsel), F32),
                       pltpu.SemaphoreType.DMA((SC_RING,))],
        compiler_params=pltpu.CompilerParams(needs_layout_passes=False), name="peer_hidden_sc",
    )(xn, idx, u_tab)


def _peer_mix_sc(act, idx, v_tab):
    t, nsel = act.shape
    d = v_tab.shape[1]
    nw = SC_CORES * SC_SUBCORES
    per_w = t // nw
    tb = min(16, per_w)
    nchunk = nsel // SC_LANES
    shift = nchunk.bit_length() - 1
    ncol = d // SC_LANES
    nq = tb * nchunk
    assert per_w * nw == t and per_w % tb == 0 and nchunk == 1 << shift and nq % SC_RING == 0
    mesh = plsc.VectorSubcoreMesh(core_axis_name="c", subcore_axis_name="s")

    def body(a_hbm, idx_hbm, v_hbm, out_hbm, idx_v, a_v, vbuf, o_v, sem):
        wid = _sc_worker_id()
        zero = jnp.zeros((SC_LANES,), F32)

        def gather(q, slot):
            tok = lax.shift_right_logical(q, shift)
            ch = q & (nchunk - 1)
            rows = idx_v.at[tok, pl.ds(ch * SC_LANES, SC_LANES)]
            return pltpu.make_async_copy(v_hbm.at[rows], vbuf.at[slot], sem.at[slot])

        def compute(q, slot):
            tok = lax.shift_right_logical(q, shift)
            ch = q & (nchunk - 1)
            tok_v = jnp.full((SC_LANES,), tok, jnp.int32)
            col_v = jnp.full((SC_LANES,), ch * SC_LANES, jnp.int32)
            w = [plsc.load_gather(a_v, [tok_v, col_v + k]) for k in range(SC_LANES)]

            @pl.loop(0, ncol, step=2)
            def _(c):
                for u in range(2):
                    cs = pl.ds(pl.multiple_of((c + u) * SC_LANES, SC_LANES), SC_LANES)
                    terms = [w[k] * vbuf[slot, k, cs] for k in range(SC_LANES)]
                    while len(terms) > 1:
                        terms = [a + b for a, b in zip(terms[0::2], terms[1::2])]
                    plsc.addupdate(o_v.at[tok, cs], terms[0])

        @pl.loop(0, per_w // tb)
        def _(b):
            base = wid * per_w + b * tb
            pltpu.sync_copy(idx_hbm.at[pl.ds(base, tb)], idx_v)
            pltpu.sync_copy(a_hbm.at[pl.ds(base, tb)], a_v)

            @pl.loop(0, tb)
            def _(r):
                @pl.loop(0, ncol)
                def _(c):
                    o_v[r, pl.ds(pl.multiple_of(c * SC_LANES, SC_LANES), SC_LANES)] = zero

            _sc_ring(nq, lambda q, s: gather(q, s).start(), lambda q, s: gather(q, s).wait(), compute)
            pltpu.sync_copy(o_v, out_hbm.at[pl.ds(base, tb)])

    return pl.kernel(
        body, out_type=jax.ShapeDtypeStruct((t, d), F32), mesh=mesh,
        scratch_types=[pltpu.VMEM((tb, nsel), jnp.int32), pltpu.VMEM((tb, nsel), F32),
                       pltpu.VMEM((SC_RING, SC_LANES, d), F32), pltpu.VMEM((tb, d), F32),
                       pltpu.SemaphoreType.DMA((SC_RING,))],
        compiler_params=pltpu.CompilerParams(needs_layout_passes=False), name="peer_mix_sc",
    )(act, idx, v_tab)


def _rope_tables(pos, half, group, width, lo):
    inv = ROPE_BASE ** (-jnp.arange(half, dtype=F32) / half)
    ang = pos.astype(F32)[:, None] * inv[None, :]
    cos, sin = jnp.cos(ang), jnp.sin(ang)
    n = pos.shape[0]
    reps = width // group
    pad_hi = group - lo - 2 * half
    blk = lambda a, b, fill: jnp.concatenate(
        [jnp.full((n, lo), fill, F32), a, b, jnp.full((n, pad_hi), fill, F32)], axis=1)
    z = jnp.zeros_like(sin)
    c = blk(cos, cos, 1.0)
    sa = blk(-sin, z, 0.0)
    sb = blk(z, sin, 0.0)
    return [jnp.tile(a, (1, reps)) for a in (c, sa, sb)]


def _ret_log_decay(nheads):
    return jnp.log(1.0 - jnp.exp2(-5.0 - jnp.arange(nheads, dtype=F32)))


def _pair_states(s):
    b, h, dk, dv = s.shape
    s = s.reshape(b, h // 2, 2, dk, dv)
    z = jnp.zeros_like(s[:, :, 0])
    top = jnp.concatenate([s[:, :, 0], z], axis=-1)
    bot = jnp.concatenate([z, s[:, :, 1]], axis=-1)
    return jnp.concatenate([top, bot], axis=-2)


def _unpair_states(sp, dk, dv):
    b, hp = sp.shape[:2]
    return jnp.stack([sp[:, :, :dk, :dv], sp[:, :, dk:, dv:]], axis=2).reshape(b, 2 * hp, dk, dv)


def _layer_weights(ln1_w, w_in, ret_gn_w, q_norm_w, w_uq, kv_norm_w, w_uk, w_uv, w_o, ln2_w,
                   peer_w_q, peer_sub_keys, dims):
    d = w_in.shape[0]
    nheads, nope, rope, vdim = dims["nheads"], dims["nope"], dims["mla_rope"], dims["vdim"]
    o6 = 4 * dims["rw"] + dims["qrank"] + dims["kvrank"]
    zc = lambda r, c: jnp.zeros((r, c), F32)
    win_p = jnp.concatenate([w_in[:, :o6], zc(d, nope), w_in[:, o6:], zc(d, LANES - nope - rope)], axis=1)
    qr, kr = w_uq.shape[0], w_uk.shape[0]
    wuq_p = jnp.concatenate([w_uq, jnp.zeros((qr, nheads, LANES - nope - rope), F32)], axis=2).reshape(qr, -1)
    wuk_p = jnp.concatenate([w_uk, jnp.zeros((kr, nheads, LANES - nope), F32)], axis=2).reshape(kr, -1)
    zv = jnp.zeros((kr, nheads // 2, LANES - vdim), F32)
    wv = w_uv.reshape(kr, nheads // 2, 2, vdim)
    wuv_p = jnp.concatenate([wv[:, :, 0], zv, zv, wv[:, :, 1]], axis=2).reshape(kr, -1)
    wuk3 = jnp.concatenate([jnp.transpose(w_uk, (1, 2, 0)),
                            jnp.zeros((nheads, LANES - nope, kr), F32)], axis=1)
    wuv3 = jnp.transpose(w_uv, (1, 0, 2))
    gidx = jnp.arange(dims["rw"]) // dims["ret_dv"]
    avg = (gidx[:, None] == gidx[None, :]).astype(F32) / dims["ret_dv"]
    keys = peer_sub_keys.reshape(-1, peer_sub_keys.shape[2], peer_sub_keys.shape[3])
    c = lambda a: a.astype(MXU_DTYPE)
    r2 = lambda a: a.reshape(1, -1)
    return dict(ln1=r2(ln1_w), win_p=c(win_p), gnw=r2(ret_gn_w), qnw=r2(q_norm_w), kvnw=r2(kv_norm_w),
                wuq_p=c(wuq_p), wuk_p=c(wuk_p), wuv_p=c(wuv_p), wuk3=c(wuk3), wuv3=c(wuv3), avg=c(avg),
                wo=c(w_o), ln2=r2(ln2_w), wq=c(peer_w_q), keys=c(keys))


def _stream_layer(x, w, tabs, lg, s0, u_tab, v_tab, lnf, dims, *, nbatch, ret_rows, ret_chunk, tm,
                  final_norm, cache=None):
    seq = x.shape[0] // nbatch
    nheads, dk, dv = dims["nheads"], dims["ret_dk"], dims["ret_dv"]
    qr, kr, vr, gate, qm, km, vm, ckv, kpe = _inproj(
        x, tabs, w["ln1"], w["win_p"], w["qnw"], w["kvnw"], w["wuq_p"], w["wuk_p"], w["wuv_p"], tm=tm, dims=dims)
    ret_o, s_pairs = _retention(lg, qr, kr, vr, _pair_states(s0), nbatch=nbatch, rows=ret_rows,
                                chunk=ret_chunk, dk=dk)
    scale = (dims["nope"] + dims["mla_rope"]) ** -0.5
    if cache is None:
        mla_o = _flash(qm, km, vm, nbatch=nbatch, tq=min(256, seq), scale=scale, nheads=nheads, chunk=CHUNK)
    else:
        mla_o = _decode_attn(qm, cache[0], cache[1], ckv, kpe, w["wuk3"], w["wuv3"], nq=seq,
                             nope=dims["nope"], rope=dims["mla_rope"], scale=scale)
    h, hn, idx, g = _mix_route(x, ret_o, gate, mla_o, w["gnw"], w["avg"], w["wo"], w["ln2"], w["wq"], w["keys"],
                               tm=tm, pheads=dims["pheads"], nkeys=dims["nkeys"], topk=dims["topk"])
    act = _gelu_gate(_peer_hidden_sc(hn, idx, u_tab), g, tm=tm)
    out = _residual(h, _peer_mix_sc(act, idx, v_tab), lnf, tm=tm, final_norm=final_norm)
    nope, rope = dims["nope"], dims["mla_rope"]
    return out, ckv, kpe[:, nope:nope + rope], _unpair_states(s_pairs, dk, dv)


def kernel(x_prompt, x_sample, cache_mla_ckv, cache_mla_krope, state_retention, ln1_w, w_in, ret_gn_w,
           mla_q_norm_w, mla_w_uq, mla_kv_norm_w, mla_w_uk, mla_w_uv, w_o, ln2_w, peer_w_q, peer_sub_keys,
           peer_u, peer_v, lnf_w):
    depth = w_in.shape[0]
    nb, seq, d = x_prompt.shape
    db, dseq, _ = x_sample.shape
    past = cache_mla_ckv.shape[2]
    rheads, dk, dv = state_retention.shape[2:]
    nkeys = peer_sub_keys.shape[3]
    dims = dict(rw=rheads * dk, ret_dk=dk, ret_dv=dv, qrank=mla_w_uq.shape[1], kvrank=mla_w_uk.shape[1],
                nheads=mla_w_uq.shape[2], nope=mla_w_uk.shape[3], vdim=mla_w_uv.shape[3],
                mla_rope=mla_w_uq.shape[3] - mla_w_uk.shape[3], pheads=peer_sub_keys.shape[1], nkeys=nkeys,
                topk=PEER_TOPK)
    assert rheads * dk == rheads * dv and dims["nheads"] % 2 == 0 and dk * 2 == LANES and dims["vdim"] * 2 == LANES

    def tables(pos):
        return (_rope_tables(pos, dk // 2, dk, dims["rw"], 0)
                + _rope_tables(pos, dims["mla_rope"] // 2, LANES, LANES, dims["nope"]))

    tabs_p = tables(jnp.arange(seq))
    tabs_s = tables(jnp.tile(past + jnp.arange(dseq), db))
    lg = _ret_log_decay(rheads)
    lnf = lnf_w.reshape(1, -1)
    hp = x_prompt.reshape(nb * seq, d)
    hs = x_sample.reshape(db * dseq, d)
    outs = [[] for _ in range(6)]
    for l in range(depth):
        w = _layer_weights(ln1_w[l], w_in[l], ret_gn_w[l], mla_q_norm_w[l], mla_w_uq[l], mla_kv_norm_w[l],
                           mla_w_uk[l], mla_w_uv[l], w_o[l], ln2_w[l], peer_w_q[l], peer_sub_keys[l], dims)
        last = l == depth - 1
        hp, c1, k1, s1 = _stream_layer(
            hp, w, tabs_p, lg, jnp.zeros((nb, rheads, dk, dv), F32), peer_u[l], peer_v[l], lnf, dims,
            nbatch=nb, ret_rows=min(256, seq), ret_chunk=CHUNK, tm=min(256, nb * seq),
            final_norm=last)
        hs, c2, k2, s2 = _stream_layer(
            hs, w, tabs_s, lg, state_retention[l], peer_u[l], peer_v[l], lnf, dims,
            nbatch=db, ret_rows=dseq, ret_chunk=dseq, tm=min(256, db * dseq),
            final_norm=last, cache=(cache_mla_ckv[l], cache_mla_krope[l]))
        for acc, val in zip(outs, (c1.reshape(nb, seq, -1), k1.reshape(nb, seq, -1), s1,
                                   c2.reshape(db, dseq, -1), k2.reshape(db, dseq, -1), s2)):
            acc.append(val)
    return (hp.reshape(nb, seq, d), hs.reshape(db, dseq, d), *[jnp.stack(o) for o in outs])
```

```python
import functools

import jax
import jax.numpy as jnp
from jax import lax
from jax.experimental import pallas as pl
from jax.experimental.pallas import tpu as pltpu
from jax.experimental.pallas import tpu_sc as plsc

EPS = 1e-6
ROPE_BASE = 10000.0
CHUNK = 64
PEER_TOPK = 16
PROMPT_GROUPS = 4
LANES = 128
MXU_DTYPE = jnp.bfloat16
VMEM_LIMIT_BYTES = 56 * 1024 * 1024

F32 = jnp.float32
NEG_INF = float("-inf")


def _mm(a, b):
    return jnp.dot(a.astype(MXU_DTYPE), b.astype(MXU_DTYPE), preferred_element_type=F32)


def _mm_nt(a, b):
    return lax.dot_general(a.astype(MXU_DTYPE), b.astype(MXU_DTYPE),
                           (((1,), (1,)), ((), ())), preferred_element_type=F32)


def _mm_tn(a, b):
    return lax.dot_general(a.astype(MXU_DTYPE), b.astype(MXU_DTYPE),
                           (((0,), (0,)), ((), ())), preferred_element_type=F32)


def _rms(x, w):
    return x * lax.rsqrt(jnp.mean(x * x, axis=-1, keepdims=True) + EPS) * w


def _rope(t, c, sa, sb, half):
    n = t.shape[1]
    return t * c + pltpu.roll(t, n - half, 1) * sa + pltpu.roll(t, half, 1) * sb


def _params(*sem):
    return pltpu.CompilerParams(dimension_semantics=sem, vmem_limit_bytes=VMEM_LIMIT_BYTES)


def _inproj_body(x_ref, ln1_ref, win_ref, cr_ref, sar_ref, sbr_ref, cm_ref, sam_ref, sbm_ref,
                 qnw_ref, kvnw_ref, wuq_ref, wuk_ref, wuv_ref,
                 qr_ref, kr_ref, vr_ref, gate_ref, qm_ref, km_ref, vm_ref, ckv_ref, kpe_ref,
                 *, rw, qrank, kvrank, ret_half, mla_half, k_scale, nheads):
    n1 = _rms(x_ref[...], ln1_ref[...])
    proj = _mm(n1, win_ref[...])
    cr, sar, sbr = cr_ref[...], sar_ref[...], sbr_ref[...]
    qr_ref[...] = _rope(proj[:, 0:rw], cr, sar, sbr, ret_half)
    kr_ref[...] = _rope(proj[:, rw:2 * rw], cr, sar, sbr, ret_half) * k_scale
    vr_ref[...] = proj[:, 2 * rw:3 * rw]
    gate_ref[...] = proj[:, 3 * rw:4 * rw]
    o4 = 4 * rw
    o5 = o4 + qrank
    o6 = o5 + kvrank
    cm, sam, sbm = cm_ref[...], sam_ref[...], sbm_ref[...]
    tile = lambda t: jnp.concatenate([t] * nheads, axis=1)
    cq = _rms(proj[:, o4:o5], qnw_ref[...])
    qm = _rope(_mm(cq, wuq_ref[...]), tile(cm), tile(sam), tile(sbm), mla_half)
    qm_ref[...] = qm.astype(qm_ref.dtype)
    ckv = _rms(proj[:, o5:o6], kvnw_ref[...])
    ckv_ref[...] = ckv
    kpe = _rope(proj[:, o6:o6 + LANES], cm, sam, sbm, mla_half)
    kpe_ref[...] = kpe
    km_ref[...] = (_mm(ckv, wuk_ref[...]) + tile(kpe)).astype(km_ref.dtype)
    vm_ref[...] = _mm(ckv, wuv_ref[...]).astype(vm_ref.dtype)


def _inproj(x, tabs, ln1, win_p, qnw, kvnw, wuq_p, wuk_p, wuv_p, *, tm, dims):
    t, d = x.shape
    rw, nheads = dims["rw"], dims["nheads"]
    hp = nheads * LANES
    nblk_tab = tabs[0].shape[0] // tm
    row = lambda i: (i, 0)
    tab = lambda i: (i % nblk_tab, 0)
    full = lambda i: (0, 0)
    fs = lambda a: pl.BlockSpec(a.shape, full)
    in_specs = [pl.BlockSpec((tm, d), row), fs(ln1), fs(win_p)]
    in_specs += [pl.BlockSpec((tm, rw), tab)] * 3 + [pl.BlockSpec((tm, LANES), tab)] * 3
    in_specs += [fs(qnw), fs(kvnw), fs(wuq_p), fs(wuk_p), fs(wuv_p)]
    out_shape = [jax.ShapeDtypeStruct((t, rw), F32)] * 4
    out_shape += [jax.ShapeDtypeStruct((t, hp), MXU_DTYPE)] * 3
    out_shape += [jax.ShapeDtypeStruct((t, dims["kvrank"]), F32), jax.ShapeDtypeStruct((t, LANES), F32)]
    out_specs = [pl.BlockSpec((tm, rw), row)] * 4 + [pl.BlockSpec((tm, hp), row)] * 3
    out_specs += [pl.BlockSpec((tm, dims["kvrank"]), row), pl.BlockSpec((tm, LANES), row)]
    body = functools.partial(
        _inproj_body, rw=rw, qrank=dims["qrank"], kvrank=dims["kvrank"], ret_half=dims["ret_dk"] // 2,
        mla_half=dims["mla_rope"] // 2, k_scale=dims["ret_dk"] ** -0.5, nheads=nheads)
    return pl.pallas_call(
        body, grid=(t // tm,), in_specs=in_specs, out_specs=out_specs, out_shape=out_shape,
        compiler_params=_params("parallel"), name="inproj",
    )(x, ln1, win_p, *tabs, qnw, kvnw, wuq_p, wuk_p, wuv_p)


def _retention_body(lg_ref, q_ref, k_ref, v_ref, s0_ref, o_ref, sout_ref, s_scr, *, rows, chunk, dk):
    hp = pl.program_id(1)
    j = pl.program_id(2)

    @pl.when(j == 0)
    def _():
        s_scr[...] = s0_ref[0, 0]

    lane = lax.broadcasted_iota(jnp.int32, (1, LANES), 1)
    is_a = lane < dk
    lg_a = lg_ref[2 * hp]
    lg_b = lg_ref[2 * hp + 1]
    lgl = jnp.where(is_a, lg_a, lg_b)
    r = lax.broadcasted_iota(jnp.int32, (rows, 1), 0).astype(F32)
    q, k, v = q_ref[...], k_ref[...], v_ref[...]
    q_dec = q * jnp.exp(lgl * (r + 1.0))
    k_dec = k * jnp.exp(lgl * (float(rows) - 1.0 - r))
    ri = lax.broadcasted_iota(jnp.int32, (rows, rows), 0)
    ci = lax.broadcasted_iota(jnp.int32, (rows, rows), 1)
    dist = jnp.abs(ri - ci).astype(F32)
    visible = (ci // chunk) <= (ri // chunk)
    o = _mm(q_dec, s_scr[...])
    for first, lg in ((True, lg_a), (False, lg_b)):
        sel = is_a if first else jnp.logical_not(is_a)
        qh = jnp.where(sel, q, 0.0)
        vh = jnp.where(sel, v, 0.0)
        decay = jnp.where(visible, jnp.exp(lg * dist), 0.0)
        o = o + _mm(_mm_nt(qh, k) * decay, vh)
    o_ref[...] = o
    sr = lax.broadcasted_iota(jnp.int32, (LANES, LANES), 0) < dk
    sc = lax.broadcasted_iota(jnp.int32, (LANES, LANES), 1) < dk
    kv = jnp.where(sr == sc, _mm_tn(k_dec, v), 0.0)
    s_new = jnp.exp(lgl * float(rows)) * s_scr[...] + kv
    s_scr[...] = s_new

    @pl.when(j == pl.num_programs(2) - 1)
    def _():
        sout_ref[0, 0] = s_new


def _retention(lg, q, k, v, s0_pairs, *, nbatch, rows, chunk, dk):
    t, w = q.shape
    npairs = w // LANES
    nblk = t // (nbatch * rows)
    blk = pl.BlockSpec((rows, LANES), lambda b, p, j: (b * nblk + j, p))
    st = pl.BlockSpec((1, 1, LANES, LANES), lambda b, p, j: (b, p, 0, 0))
    return pl.pallas_call(
        functools.partial(_retention_body, rows=rows, chunk=chunk, dk=dk),
        grid=(nbatch, npairs, nblk),
        in_specs=[pl.BlockSpec(memory_space=pltpu.SMEM), blk, blk, blk, st],
        out_specs=[blk, st],
        out_shape=[jax.ShapeDtypeStruct((t, w), F32),
                   jax.ShapeDtypeStruct((nbatch, npairs, LANES, LANES), F32)],
        scratch_shapes=[pltpu.VMEM((LANES, LANES), F32)],
        compiler_params=_params("parallel", "parallel", "arbitrary"), name="retention",
    )(lg, q, k, v, s0_pairs)


def _flash_body(q_ref, k_ref, v_ref, o_ref, *, tq, scale, nheads, chunk):
    i = pl.program_id(1)
    ri = lax.broadcasted_iota(jnp.int32, (tq, tq), 0) // chunk
    ci = lax.broadcasted_iota(jnp.int32, (tq, tq), 1) // chunk
    visible = ci <= ri

    def head(h):
        cols = slice(h * LANES, (h + 1) * LANES)
        q = q_ref[:, cols]

        def step(j, carry, diagonal):
            m, l, acc = carry
            off = pl.multiple_of(j * tq, tq)
            s = _mm_nt(q, k_ref[pl.ds(off, tq), cols]) * scale
            if diagonal:
                s = jnp.where(visible, s, NEG_INF)
            m_new = jnp.maximum(m, jnp.max(s, axis=1, keepdims=True))
            alpha = jnp.exp(m - m_new)
            p = jnp.exp(s - m_new)
            l = alpha * l + jnp.sum(p, axis=1, keepdims=True)
            acc = alpha * acc + _mm(p, v_ref[pl.ds(off, tq), cols])
            return m_new, l, acc

        init = (jnp.full((tq, 1), NEG_INF, F32), jnp.zeros((tq, 1), F32), jnp.zeros((tq, LANES), F32))
        carry = lax.fori_loop(0, i, functools.partial(step, diagonal=False), init)
        _, l, acc = step(i, carry, True)
        return acc / l

    for p in range(nheads // 2):
        o_ref[:, p * LANES:(p + 1) * LANES] = head(2 * p) + head(2 * p + 1)


def _flash(qm, km, vm, *, nbatch, tq, scale, nheads, chunk):
    t, hp = qm.shape
    s = t // nbatch
    nq = s // tq
    ow = nheads // 2 * LANES
    return pl.pallas_call(
        functools.partial(_flash_body, tq=tq, scale=scale, nheads=nheads, chunk=chunk),
        grid=(nbatch, nq),
        in_specs=[pl.BlockSpec((tq, hp), lambda b, i: (b * nq + i, 0)),
                  pl.BlockSpec((s, hp), lambda b, i: (b, 0)),
                  pl.BlockSpec((s, hp), lambda b, i: (b, 0))],
        out_specs=pl.BlockSpec((tq, ow), lambda b, i: (b * nq + i, 0)),
        out_shape=jax.ShapeDtypeStruct((t, ow), F32),
        compiler_params=_params("parallel", "arbitrary"), name="flash_mla",
    )(qm, km, vm)


def _decode_attn_body(q_ref, cpast_ref, kpast_ref, cnew_ref, knew_ref, wuk_ref, wuv_ref, o_ref,
                      *, nheads, nope, rope, scale):
    c_past = cpast_ref[0]
    k_past = kpast_ref[0]
    c_new = cnew_ref[...]
    k_new = knew_ref[:, nope:nope + rope]
    outs = []
    for h in range(nheads):
        q = q_ref[:, h * LANES:(h + 1) * LANES]
        q_lat = _mm(q, wuk_ref[h])
        q_pe = q[:, nope:nope + rope]
        s_p = (_mm_nt(q_lat, c_past) + _mm_nt(q_pe, k_past)) * scale
        s_n = (_mm_nt(q_lat, c_new) + _mm_nt(q_pe, k_new)) * scale
        m = jnp.maximum(jnp.max(s_p, axis=1, keepdims=True), jnp.max(s_n, axis=1, keepdims=True))
        p_p = jnp.exp(s_p - m)
        p_n = jnp.exp(s_n - m)
        l = jnp.sum(p_p, axis=1, keepdims=True) + jnp.sum(p_n, axis=1, keepdims=True)
        o_lat = (_mm(p_p, c_past) + _mm(p_n, c_new)) / l
        outs.append(_mm(o_lat, wuv_ref[h]))
    o_ref[...] = jnp.concatenate(outs, axis=1)


def _decode_attn(qm, c_past, k_past, c_new, kpe_new, wuk3, wuv3, *, nq, nope, rope, scale):
    nb, past, kvr = c_past.shape
    nheads, _, vdim = wuv3.shape
    t, hp = qm.shape
    row = lambda b: (b, 0)
    full3 = lambda b: (0, 0, 0)
    return pl.pallas_call(
        functools.partial(_decode_attn_body, nheads=nheads, nope=nope, rope=rope, scale=scale),
        grid=(nb,),
        in_specs=[pl.BlockSpec((nq, hp), row),
                  pl.BlockSpec((1, past, kvr), lambda b: (b, 0, 0)),
                  pl.BlockSpec((1, past, rope), lambda b: (b, 0, 0)),
                  pl.BlockSpec((nq, kvr), row),
                  pl.BlockSpec((nq, LANES), row),
                  pl.BlockSpec(wuk3.shape, full3),
                  pl.BlockSpec(wuv3.shape, full3)],
        out_specs=pl.BlockSpec((nq, nheads * vdim), row),
        out_shape=jax.ShapeDtypeStruct((t, nheads * vdim), F32),
        compiler_params=_params("parallel"), name="decode_mla",
    )(qm, c_past, k_past, c_new, kpe_new, wuk3, wuv3)


def _split3(x):
    a = x.astype(MXU_DTYPE)
    r = x - a.astype(F32)
    b = r.astype(MXU_DTYPE)
    c = (r - b.astype(F32)).astype(MXU_DTYPE)
    return a, b, c


def _group_mean(x, avg):
    a, b, c = _split3(x)
    dot = lambda t: jnp.dot(t, avg, preferred_element_type=F32)
    return dot(a) + dot(b) + dot(c)


def _topk_rows(s, payload, kk):
    n = s.shape[0]
    rid = lax.broadcasted_iota(jnp.int32, s.shape, 0)
    vals, pays = [], []
    for _ in range(kk):
        mx = jnp.max(s, axis=0, keepdims=True)
        first = jnp.min(jnp.where(s == mx, rid, n), axis=0, keepdims=True)
        hit = rid == first
        vals.append(mx)
        pays.append(jnp.max(jnp.where(hit, payload, -1), axis=0, keepdims=True))
        s = jnp.where(hit, NEG_INF, s)
    return jnp.concatenate(vals, axis=0), jnp.concatenate(pays, axis=0)


def _mix_route_body(x_ref, ret_ref, gate_ref, mla_ref, gnw_ref, avg_ref, wo_ref, ln2_ref, wq_ref, keys_ref,
                    h_ref, hn_ref, idx_ref, g_ref, *, rw, pheads, nkeys, topk):
    ret = ret_ref[...]
    avg = avg_ref[...]
    mu = _group_mean(ret, avg)
    cen = ret - mu
    var = _group_mean(cen * cen, avg)
    gate = gate_ref[...]
    y = cen * lax.rsqrt(var + EPS) * gnw_ref[...] * (gate * jax.nn.sigmoid(gate))
    h = x_ref[...] + _mm(y, wo_ref[0:rw, :]) + _mm(mla_ref[...], wo_ref[rw:, :])
    h_ref[...] = h
    hn = _rms(h, ln2_ref[...])
    hn_ref[...] = hn
    qp = _mm(hn, wq_ref[...])
    kid = lax.broadcasted_iota(jnp.int32, (nkeys, qp.shape[0]), 0)
    idx_rows, g_rows = [], []
    for hd in range(pheads):
        ts, ti = [], []
        for half in range(2):
            c = (2 * hd + half) * LANES
            st = _mm_nt(keys_ref[2 * hd + half], qp[:, c:c + LANES])
            v, i = _topk_rows(st, kid, topk)
            ts.append(v)
            ti.append(i)
        cand = jnp.concatenate([ts[0][a:a + 1, :] + ts[1] for a in range(topk)], axis=0)
        cidx = jnp.concatenate([ti[0][a:a + 1, :] * nkeys + ti[1] for a in range(topk)], axis=0)
        best, expert = _topk_rows(cand, cidx, topk)
        e = jnp.exp(best - best[0:1, :])
        g_rows.append(e / jnp.sum(e, axis=0, keepdims=True))
        idx_rows.append(expert)
    idx_ref[...] = jnp.concatenate(idx_rows, axis=0).T
    g_ref[...] = jnp.concatenate(g_rows, axis=0).T


def _mix_route(x, ret_o, gate, mla_o, gnw, avg, wo, ln2, wq, keys, *, tm, pheads, nkeys, topk):
    t, d = x.shape
    rw = ret_o.shape[1]
    nsel = pheads * topk
    row = lambda i: (i, 0)
    fs = lambda a: pl.BlockSpec(a.shape, lambda i: (0,) * a.ndim)
    return pl.pallas_call(
        functools.partial(_mix_route_body, rw=rw, pheads=pheads, nkeys=nkeys, topk=topk),
        grid=(t // tm,),
        in_specs=[pl.BlockSpec((tm, d), row), pl.BlockSpec((tm, rw), row), pl.BlockSpec((tm, rw), row),
                  pl.BlockSpec((tm, mla_o.shape[1]), row), fs(gnw), fs(avg), fs(wo), fs(ln2), fs(wq), fs(keys)],
        out_specs=[pl.BlockSpec((tm, d), row), pl.BlockSpec((tm, d), row),
                   pl.BlockSpec((tm, nsel), row), pl.BlockSpec((tm, nsel), row)],
        out_shape=[jax.ShapeDtypeStruct((t, d), F32), jax.ShapeDtypeStruct((t, d), F32),
                   jax.ShapeDtypeStruct((t, nsel), jnp.int32),
                   jax.ShapeDtypeStruct((t, nsel), F32)],
        compiler_params=_params("parallel"), name="mix_route",
    )(x, ret_o, gate, mla_o, gnw, avg, wo, ln2, wq, keys)


def _gelu_gate_body(hid_ref, g_ref, a_ref):
    hid = hid_ref[...]
    a_ref[...] = 0.5 * hid * (1.0 + lax.erf(hid * (2.0 ** -0.5))) * g_ref[...]


def _gelu_gate(hid, g, *, tm):
    t, n = hid.shape
    blk = pl.BlockSpec((tm, n), lambda i: (i, 0))
    return pl.pallas_call(
        _gelu_gate_body, grid=(t // tm,), in_specs=[blk, blk], out_specs=blk,
        out_shape=jax.ShapeDtypeStruct((t, n), F32), compiler_params=_params("parallel"), name="gelu_gate",
    )(hid, g)


def _residual_body(h_ref, p_ref, lnf_ref, o_ref, *, final_norm):
    out = h_ref[...] + p_ref[...]
    if final_norm:
        out = _rms(out, lnf_ref[...])
    o_ref[...] = out


def _residual(h, peer, lnf, *, tm, final_norm):
    t, d = h.shape
    blk = pl.BlockSpec((tm, d), lambda i: (i, 0))
    return pl.pallas_call(
        functools.partial(_residual_body, final_norm=final_norm), grid=(t // tm,),
        in_specs=[blk, blk, pl.BlockSpec((1, d), lambda i: (0, 0))], out_specs=blk,
        out_shape=jax.ShapeDtypeStruct((t, d), F32), compiler_params=_params("parallel"), name="residual_norm",
    )(h, peer, lnf)


SC_CORES = 2
SC_SUBCORES = 16
SC_LANES = 16
SC_RING = 4


def _sc_worker_id():
    return lax.axis_index("s") * SC_CORES + lax.axis_index("c")


def _sc_ring(nq, start, wait, compute):
    for s in range(SC_RING - 1):
        start(s, s)

    @pl.loop(0, nq, step=SC_RING)
    def _(q0):
        for s in range(SC_RING):
            q = q0 + s
            nxt = q + SC_RING - 1

            @pl.when(nxt < nq)
            def _():
                start(nxt, (s + SC_RING - 1) % SC_RING)

            wait(q, s)
            compute(q, s)


def _peer_hidden_sc(xn, idx, u_tab):
    t, d = xn.shape
    nsel = idx.shape[1]
    nw = SC_CORES * SC_SUBCORES
    per_w = t // nw
    tb = min(16, per_w)
    nchunk = nsel // SC_LANES
    shift = nchunk.bit_length() - 1
    ncol = d // SC_LANES
    nq = tb * nchunk
    assert per_w * nw == t and per_w % tb == 0 and nchunk == 1 << shift and nq % SC_RING == 0
    mesh = plsc.VectorSubcoreMesh(core_axis_name="c", subcore_axis_name="s")

    def body(x_hbm, idx_hbm, u_hbm, out_hbm, idx_v, x_v, ubuf, hid_v, sem):
        wid = _sc_worker_id()
        lane = lax.iota(jnp.int32, SC_LANES)

        def gather(q, slot):
            tok = lax.shift_right_logical(q, shift)
            ch = q & (nchunk - 1)
            rows = idx_v.at[tok, pl.ds(ch * SC_LANES, SC_LANES)]
            return pltpu.make_async_copy(u_hbm.at[rows], ubuf.at[slot], sem.at[slot])

        def compute(q, slot):
            tok = lax.shift_right_logical(q, shift)
            ch = q & (nchunk - 1)

            def col(c, accs):
                cs = pl.ds(pl.multiple_of(c * SC_LANES, SC_LANES), SC_LANES)
                xc = x_v[tok, cs]
                return tuple(a + xc * ubuf[slot, k, cs] for k, a in enumerate(accs))

            accs = lax.fori_loop(0, ncol, col, tuple(jnp.zeros((SC_LANES,), F32) for _ in range(SC_LANES)),
                                 unroll=2)
            out = jnp.zeros((SC_LANES,), F32)
            for k in range(SC_LANES):
                out = jnp.where(lane == k, jnp.sum(accs[k]), out)
            hid_v[tok, pl.ds(ch * SC_LANES, SC_LANES)] = out

        @pl.loop(0, per_w // tb)
        def _(b):
            base = wid * per_w + b * tb
            pltpu.sync_copy(idx_hbm.at[pl.ds(base, tb)], idx_v)
            pltpu.sync_copy(x_hbm.at[pl.ds(base, tb)], x_v)
            _sc_ring(nq, lambda q, s: gather(q, s).start(), lambda q, s: gather(q, s).wait(), compute)
            pltpu.sync_copy(hid_v, out_hbm.at[pl.ds(base, tb)])

    return pl.kernel(
        body, out_type=jax.ShapeDtypeStruct((t, nsel), F32), mesh=mesh,
        scratch_types=[pltpu.VMEM((tb, nsel), jnp.int32), pltpu.VMEM((tb, d), F32),
                       pltpu.VMEM((SC_RING, SC_LANES, d), F32), pltpu.VMEM((tb, nsel), F32),
                       pltpu.SemaphoreType.DMA((SC_RING,))],
        compiler_params=pltpu.CompilerParams(needs_layout_passes=False), name="peer_hidden_sc",
    )(xn, idx, u_tab)


def _peer_mix_sc(act, idx, v_tab):
    t, nsel = act.shape
    d = v_tab.shape[1]
    nw = SC_CORES * SC_SUBCORES
    per_w = t // nw
    tb = min(16, per_w)
    nchunk = nsel // SC_LANES
    shift = nchunk.bit_length() - 1
    ncol = d // SC_LANES
    nq = tb * nchunk
    assert per_w * nw == t and per_w % tb == 0 and nchunk == 1 << shift and nq % SC_RING == 0
    mesh = plsc.VectorSubcoreMesh(core_axis_name="c", subcore_axis_name="s")

    def body(a_hbm, idx_hbm, v_hbm, out_hbm, idx_v, a_v, vbuf, o_v, sem):
        wid = _sc_worker_id()
        zero = jnp.zeros((SC_LANES,), F32)

        def gather(q, slot):
            tok = lax.shift_right_logical(q, shift)
            ch = q & (nchunk - 1)
            rows = idx_v.at[tok, pl.ds(ch * SC_LANES, SC_LANES)]
            return pltpu.make_async_copy(v_hbm.at[rows], vbuf.at[slot], sem.at[slot])

        def compute(q, slot):
            tok = lax.shift_right_logical(q, shift)
            ch = q & (nchunk - 1)
            tok_v = jnp.full((SC_LANES,), tok, jnp.int32)
            col_v = jnp.full((SC_LANES,), ch * SC_LANES, jnp.int32)
            w = [plsc.load_gather(a_v, [tok_v, col_v + k]) for k in range(SC_LANES)]

            @plsc.parallel_loop(0, ncol, unroll=4)
            def _(c):
                cs = pl.ds(pl.multiple_of(c * SC_LANES, SC_LANES), SC_LANES)
                terms = [w[k] * vbuf[slot, k, cs] for k in range(SC_LANES)]
                while len(terms) > 1:
                    terms = [a + b for a, b in zip(terms[0::2], terms[1::2])]
                plsc.addupdate(o_v.at[tok, cs], terms[0])

        @pl.loop(0, per_w // tb)
        def _(b):
            base = wid * per_w + b * tb
            pltpu.sync_copy(idx_hbm.at[pl.ds(base, tb)], idx_v)
            pltpu.sync_copy(a_hbm.at[pl.ds(base, tb)], a_v)

            @pl.loop(0, tb)
            def _(r):
                @pl.loop(0, ncol)
                def _(c):
                    o_v[r, pl.ds(pl.multiple_of(c * SC_LANES, SC_LANES), SC_LANES)] = zero

            _sc_ring(nq, lambda q, s: gather(q, s).start(), lambda q, s: gather(q, s).wait(), compute)
            pltpu.sync_copy(o_v, out_hbm.at[pl.ds(base, tb)])

    return pl.kernel(
        body, out_type=jax.ShapeDtypeStruct((t, d), F32), mesh=mesh,
        scratch_types=[pltpu.VMEM((tb, nsel), jnp.int32), pltpu.VMEM((tb, nsel), F32),
                       pltpu.VMEM((SC_RING, SC_LANES, d), F32), pltpu.VMEM((tb, d), F32),
                       pltpu.SemaphoreType.DMA((SC_RING,))],
        compiler_params=pltpu.CompilerParams(needs_layout_passes=False), name="peer_mix_sc",
    )(act, idx, v_tab)


def _rope_tables(pos, half, group, width, lo):
    inv = ROPE_BASE ** (-jnp.arange(half, dtype=F32) / half)
    ang = pos.astype(F32)[:, None] * inv[None, :]
    cos, sin = jnp.cos(ang), jnp.sin(ang)
    n = pos.shape[0]
    reps = width // group
    pad_hi = group - lo - 2 * half
    blk = lambda a, b, fill: jnp.concatenate(
        [jnp.full((n, lo), fill, F32), a, b, jnp.full((n, pad_hi), fill, F32)], axis=1)
    z = jnp.zeros_like(sin)
    c = blk(cos, cos, 1.0)
    sa = blk(-sin, z, 0.0)
    sb = blk(z, sin, 0.0)
    return [jnp.tile(a, (1, reps)) for a in (c, sa, sb)]


def _ret_log_decay(nheads):
    return jnp.log(1.0 - jnp.exp2(-5.0 - jnp.arange(nheads, dtype=F32)))


def _pair_states(s):
    b, h, dk, dv = s.shape
    s = s.reshape(b, h // 2, 2, dk, dv)
    z = jnp.zeros_like(s[:, :, 0])
    top = jnp.concatenate([s[:, :, 0], z], axis=-1)
    bot = jnp.concatenate([z, s[:, :, 1]], axis=-1)
    return jnp.concatenate([top, bot], axis=-2)


def _unpair_states(sp, dk, dv):
    b, hp = sp.shape[:2]
    return jnp.stack([sp[:, :, :dk, :dv], sp[:, :, dk:, dv:]], axis=2).reshape(b, 2 * hp, dk, dv)


def _layer_weights(ln1_w, w_in, ret_gn_w, q_norm_w, w_uq, kv_norm_w, w_uk, w_uv, w_o, ln2_w,
                   peer_w_q, peer_sub_keys, dims):
    d = w_in.shape[0]
    nheads, nope, rope, vdim = dims["nheads"], dims["nope"], dims["mla_rope"], dims["vdim"]
    o6 = 4 * dims["rw"] + dims["qrank"] + dims["kvrank"]
    zc = lambda r, c: jnp.zeros((r, c), F32)
    win_p = jnp.concatenate([w_in[:, :o6], zc(d, nope), w_in[:, o6:], zc(d, LANES - nope - rope)], axis=1)
    qr, kr = w_uq.shape[0], w_uk.shape[0]
    wuq_p = jnp.concatenate([w_uq, jnp.zeros((qr, nheads, LANES - nope - rope), F32)], axis=2).reshape(qr, -1)
    wuk_p = jnp.concatenate([w_uk, jnp.zeros((kr, nheads, LANES - nope), F32)], axis=2).reshape(kr, -1)
    zv = jnp.zeros((kr, nheads // 2, LANES - vdim), F32)
    wv = w_uv.reshape(kr, nheads // 2, 2, vdim)
    wuv_p = jnp.concatenate([wv[:, :, 0], zv, zv, wv[:, :, 1]], axis=2).reshape(kr, -1)
    wuk3 = jnp.concatenate([jnp.transpose(w_uk, (1, 2, 0)),
                            jnp.zeros((nheads, LANES - nope, kr), F32)], axis=1)
    wuv3 = jnp.transpose(w_uv, (1, 0, 2))
    gidx = jnp.arange(dims["rw"]) // dims["ret_dv"]
    avg = (gidx[:, None] == gidx[None, :]).astype(F32) / dims["ret_dv"]
    keys = peer_sub_keys.reshape(-1, peer_sub_keys.shape[2], peer_sub_keys.shape[3])
    c = lambda a: a.astype(MXU_DTYPE)
    r2 = lambda a: a.reshape(1, -1)
    return dict(ln1=r2(ln1_w), win_p=c(win_p), gnw=r2(ret_gn_w), qnw=r2(q_norm_w), kvnw=r2(kv_norm_w),
                wuq_p=c(wuq_p), wuk_p=c(wuk_p), wuv_p=c(wuv_p), wuk3=c(wuk3), wuv3=c(wuv3), avg=c(avg),
                wo=c(w_o), ln2=r2(ln2_w), wq=c(peer_w_q), keys=c(keys))


def _stream_layer(x, w, tabs, lg, s0, u_tab, v_tab, lnf, dims, *, nbatch, ret_rows, ret_chunk, tm,
                  final_norm, cache=None):
    seq = x.shape[0] // nbatch
    nheads, dk, dv = dims["nheads"], dims["ret_dk"], dims["ret_dv"]
    qr, kr, vr, gate, qm, km, vm, ckv, kpe = _inproj(
        x, tabs, w["ln1"], w["win_p"], w["qnw"], w["kvnw"], w["wuq_p"], w["wuk_p"], w["wuv_p"], tm=tm, dims=dims)
    ret_o, s_pairs = _retention(lg, qr, kr, vr, _pair_states(s0), nbatch=nbatch, rows=ret_rows,
                                chunk=ret_chunk, dk=dk)
    scale = (dims["nope"] + dims["mla_rope"]) ** -0.5
    if cache is None:
        mla_o = _flash(qm, km, vm, nbatch=nbatch, tq=min(256, seq), scale=scale, nheads=nheads, chunk=CHUNK)
    else:
        mla_o = _decode_attn(qm, cache[0], cache[1], ckv, kpe, w["wuk3"], w["wuv3"], nq=seq,
                             nope=dims["nope"], rope=dims["mla_rope"], scale=scale)
    h, hn, idx, g = _mix_route(x, ret_o, gate, mla_o, w["gnw"], w["avg"], w["wo"], w["ln2"], w["wq"], w["keys"],
                               tm=tm, pheads=dims["pheads"], nkeys=dims["nkeys"], topk=dims["topk"])
    act = _gelu_gate(_peer_hidden_sc(hn, idx, u_tab), g, tm=tm)
    out = _residual(h, _peer_mix_sc(act, idx, v_tab), lnf, tm=tm, final_norm=final_norm)
    nope, rope = dims["nope"], dims["mla_rope"]
    return out, ckv, kpe[:, nope:nope + rope], _unpair_states(s_pairs, dk, dv)


def kernel(x_prompt, x_sample, cache_mla_ckv, cache_mla_krope, state_retention, ln1_w, w_in, ret_gn_w,
           mla_q_norm_w, mla_w_uq, mla_kv_norm_w, mla_w_uk, mla_w_uv, w_o, ln2_w, peer_w_q, peer_sub_keys,
           peer_u, peer_v, lnf_w):
    depth = w_in.shape[0]
    nb, seq, d = x_prompt.shape
    db, dseq, _ = x_sample.shape
    past = cache_mla_ckv.shape[2]
    rheads, dk, dv = state_retention.shape[2:]
    nkeys = peer_sub_keys.shape[3]
    dims = dict(rw=rheads * dk, ret_dk=dk, ret_dv=dv, qrank=mla_w_uq.shape[1], kvrank=mla_w_uk.shape[1],
                nheads=mla_w_uq.shape[2], nope=mla_w_uk.shape[3], vdim=mla_w_uv.shape[3],
                mla_rope=mla_w_uq.shape[3] - mla_w_uk.shape[3], pheads=peer_sub_keys.shape[1], nkeys=nkeys,
                topk=PEER_TOPK)
    assert rheads * dk == rheads * dv and dims["nheads"] % 2 == 0 and dk * 2 == LANES and dims["vdim"] * 2 == LANES

    def tables(pos):
        return (_rope_tables(pos, dk // 2, dk, dims["rw"], 0)
                + _rope_tables(pos, dims["mla_rope"] // 2, LANES, LANES, dims["nope"]))

    tabs_p = tables(jnp.arange(seq))
    tabs_s = tables(jnp.tile(past + jnp.arange(dseq), db))
    lg = _ret_log_decay(rheads)
    lnf = lnf_w.reshape(1, -1)
    hp = x_prompt.reshape(nb * seq, d)
    hs = x_sample.reshape(db * dseq, d)
    outs = [[] for _ in range(6)]
    for l in range(depth):
        w = _layer_weights(ln1_w[l], w_in[l], ret_gn_w[l], mla_q_norm_w[l], mla_w_uq[l], mla_kv_norm_w[l],
                           mla_w_uk[l], mla_w_uv[l], w_o[l], ln2_w[l], peer_w_q[l], peer_sub_keys[l], dims)
        last = l == depth - 1
        gb = nb // PROMPT_GROUPS if nb % PROMPT_GROUPS == 0 else nb
        parts = [_stream_layer(
            hp[g * gb * seq:(g + 1) * gb * seq], w, tabs_p, lg, jnp.zeros((gb, rheads, dk, dv), F32),
            peer_u[l], peer_v[l], lnf, dims, nbatch=gb, ret_rows=min(256, seq), ret_chunk=CHUNK,
            tm=min(256, gb * seq), final_norm=last) for g in range(nb // gb)]
        hp, c1, k1, s1 = (jnp.concatenate(p, axis=0) for p in zip(*parts))
        hs, c2, k2, s2 = _stream_layer(
            hs, w, tabs_s, lg, state_retention[l], peer_u[l], peer_v[l], lnf, dims,
            nbatch=db, ret_rows=dseq, ret_chunk=dseq, tm=min(256, db * dseq),
            final_norm=last, cache=(cache_mla_ckv[l], cache_mla_krope[l]))
        for acc, val in zip(outs, (c1.reshape(nb, seq, -1), k1.reshape(nb, seq, -1), s1,
                                   c2.reshape(db, dseq, -1), k2.reshape(db, dseq, -1), s2)):
            acc.append(val)
    return (hp.reshape(nb, seq, d), hs.reshape(db, dseq, d), *[jnp.stack(o) for o in outs])
```

```python
import functools

import jax
import jax.numpy as jnp
from jax import lax
from jax.experimental import pallas as pl
from jax.experimental.pallas import tpu as pltpu
from jax.experimental.pallas import tpu_sc as plsc

EPS = 1e-6
ROPE_BASE = 10000.0
CHUNK = 64
PEER_TOPK = 16
PROMPT_GROUPS = 8
LANES = 128
MXU_DTYPE = jnp.bfloat16
VMEM_LIMIT_BYTES = 56 * 1024 * 1024

F32 = jnp.float32
NEG_INF = float("-inf")


def _mm(a, b):
    return jnp.dot(a.astype(MXU_DTYPE), b.astype(MXU_DTYPE), preferred_element_type=F32)


def _mm_nt(a, b):
    return lax.dot_general(a.astype(MXU_DTYPE), b.astype(MXU_DTYPE),
                           (((1,), (1,)), ((), ())), preferred_element_type=F32)


def _mm_tn(a, b):
    return lax.dot_general(a.astype(MXU_DTYPE), b.astype(MXU_DTYPE),
                           (((0,), (0,)), ((), ())), preferred_element_type=F32)


def _rms(x, w):
    return x * lax.rsqrt(jnp.mean(x * x, axis=-1, keepdims=True) + EPS) * w


def _rope(t, c, sa, sb, half):
    n = t.shape[1]
    return t * c + pltpu.roll(t, n - half, 1) * sa + pltpu.roll(t, half, 1) * sb


def _params(*sem):
    return pltpu.CompilerParams(dimension_semantics=sem, vmem_limit_bytes=VMEM_LIMIT_BYTES)


def _inproj_body(x_ref, ln1_ref, win_ref, cr_ref, sar_ref, sbr_ref, cm_ref, sam_ref, sbm_ref,
                 qnw_ref, kvnw_ref, wuq_ref, wuk_ref, wuv_ref,
                 qr_ref, kr_ref, vr_ref, gate_ref, qm_ref, km_ref, vm_ref, ckv_ref, kpe_ref,
                 *, rw, qrank, kvrank, ret_half, mla_half, k_scale, nheads):
    n1 = _rms(x_ref[...], ln1_ref[...])
    proj = _mm(n1, win_ref[...])
    cr, sar, sbr = cr_ref[...], sar_ref[...], sbr_ref[...]
    qr_ref[...] = _rope(proj[:, 0:rw], cr, sar, sbr, ret_half)
    kr_ref[...] = _rope(proj[:, rw:2 * rw], cr, sar, sbr, ret_half) * k_scale
    vr_ref[...] = proj[:, 2 * rw:3 * rw]
    gate_ref[...] = proj[:, 3 * rw:4 * rw]
    o4 = 4 * rw
    o5 = o4 + qrank
    o6 = o5 + kvrank
    cm, sam, sbm = cm_ref[...], sam_ref[...], sbm_ref[...]
    tile = lambda t: jnp.concatenate([t] * nheads, axis=1)
    cq = _rms(proj[:, o4:o5], qnw_ref[...])
    qm = _rope(_mm(cq, wuq_ref[...]), tile(cm), tile(sam), tile(sbm), mla_half)
    qm_ref[...] = qm.astype(qm_ref.dtype)
    ckv = _rms(proj[:, o5:o6], kvnw_ref[...])
    ckv_ref[...] = ckv
    kpe = _rope(proj[:, o6:o6 + LANES], cm, sam, sbm, mla_half)
    kpe_ref[...] = kpe
    km_ref[...] = (_mm(ckv, wuk_ref[...]) + tile(kpe)).astype(km_ref.dtype)
    vm_ref[...] = _mm(ckv, wuv_ref[...]).astype(vm_ref.dtype)


def _inproj(x, tabs, ln1, win_p, qnw, kvnw, wuq_p, wuk_p, wuv_p, *, tm, dims):
    t, d = x.shape
    rw, nheads = dims["rw"], dims["nheads"]
    hp = nheads * LANES
    nblk_tab = tabs[0].shape[0] // tm
    row = lambda i: (i, 0)
    tab = lambda i: (i % nblk_tab, 0)
    full = lambda i: (0, 0)
    fs = lambda a: pl.BlockSpec(a.shape, full)
    in_specs = [pl.BlockSpec((tm, d), row), fs(ln1), fs(win_p)]
    in_specs += [pl.BlockSpec((tm, rw), tab)] * 3 + [pl.BlockSpec((tm, LANES), tab)] * 3
    in_specs += [fs(qnw), fs(kvnw), fs(wuq_p), fs(wuk_p), fs(wuv_p)]
    out_shape = [jax.ShapeDtypeStruct((t, rw), F32)] * 4
    out_shape += [jax.ShapeDtypeStruct((t, hp), MXU_DTYPE)] * 3
    out_shape += [jax.ShapeDtypeStruct((t, dims["kvrank"]), F32), jax.ShapeDtypeStruct((t, LANES), F32)]
    out_specs = [pl.BlockSpec((tm, rw), row)] * 4 + [pl.BlockSpec((tm, hp), row)] * 3
    out_specs += [pl.BlockSpec((tm, dims["kvrank"]), row), pl.BlockSpec((tm, LANES), row)]
    body = functools.partial(
        _inproj_body, rw=rw, qrank=dims["qrank"], kvrank=dims["kvrank"], ret_half=dims["ret_dk"] // 2,
        mla_half=dims["mla_rope"] // 2, k_scale=dims["ret_dk"] ** -0.5, nheads=nheads)
    return pl.pallas_call(
        body, grid=(t // tm,), in_specs=in_specs, out_specs=out_specs, out_shape=out_shape,
        compiler_params=_params("parallel"), name="inproj",
    )(x, ln1, win_p, *tabs, qnw, kvnw, wuq_p, wuk_p, wuv_p)


def _retention_body(lg_ref, q_ref, k_ref, v_ref, s0_ref, o_ref, sout_ref, s_scr, *, rows, chunk, dk):
    hp = pl.program_id(1)
    j = pl.program_id(2)

    @pl.when(j == 0)
    def _():
        s_scr[...] = s0_ref[0, 0]

    lane = lax.broadcasted_iota(jnp.int32, (1, LANES), 1)
    is_a = lane < dk
    lg_a = lg_ref[2 * hp]
    lg_b = lg_ref[2 * hp + 1]
    lgl = jnp.where(is_a, lg_a, lg_b)
    r = lax.broadcasted_iota(jnp.int32, (rows, 1), 0).astype(F32)
    q, k, v = q_ref[...], k_ref[...], v_ref[...]
    q_dec = q * jnp.exp(lgl * (r + 1.0))
    k_dec = k * jnp.exp(lgl * (float(rows) - 1.0 - r))
    ri = lax.broadcasted_iota(jnp.int32, (rows, rows), 0)
    ci = lax.broadcasted_iota(jnp.int32, (rows, rows), 1)
    dist = jnp.abs(ri - ci).astype(F32)
    visible = (ci // chunk) <= (ri // chunk)
    o = _mm(q_dec, s_scr[...])
    for first, lg in ((True, lg_a), (False, lg_b)):
        sel = is_a if first else jnp.logical_not(is_a)
        qh = jnp.where(sel, q, 0.0)
        vh = jnp.where(sel, v, 0.0)
        decay = jnp.where(visible, jnp.exp(lg * dist), 0.0)
        o = o + _mm(_mm_nt(qh, k) * decay, vh)
    o_ref[...] = o
    sr = lax.broadcasted_iota(jnp.int32, (LANES, LANES), 0) < dk
    sc = lax.broadcasted_iota(jnp.int32, (LANES, LANES), 1) < dk
    kv = jnp.where(sr == sc, _mm_tn(k_dec, v), 0.0)
    s_new = jnp.exp(lgl * float(rows)) * s_scr[...] + kv
    s_scr[...] = s_new

    @pl.when(j == pl.num_programs(2) - 1)
    def _():
        sout_ref[0, 0] = s_new


def _retention(lg, q, k, v, s0_pairs, *, nbatch, rows, chunk, dk):
    t, w = q.shape
    npairs = w // LANES
    nblk = t // (nbatch * rows)
    blk = pl.BlockSpec((rows, LANES), lambda b, p, j: (b * nblk + j, p))
    st = pl.BlockSpec((1, 1, LANES, LANES), lambda b, p, j: (b, p, 0, 0))
    return pl.pallas_call(
        functools.partial(_retention_body, rows=rows, chunk=chunk, dk=dk),
        grid=(nbatch, npairs, nblk),
        in_specs=[pl.BlockSpec(memory_space=pltpu.SMEM), blk, blk, blk, st],
        out_specs=[blk, st],
        out_shape=[jax.ShapeDtypeStruct((t, w), F32),
                   jax.ShapeDtypeStruct((nbatch, npairs, LANES, LANES), F32)],
        scratch_shapes=[pltpu.VMEM((LANES, LANES), F32)],
        compiler_params=_params("parallel", "parallel", "arbitrary"), name="retention",
    )(lg, q, k, v, s0_pairs)


def _flash_body(q_ref, k_ref, v_ref, o_ref, *, tq, scale, nheads, chunk):
    i = pl.program_id(1)
    ri = lax.broadcasted_iota(jnp.int32, (tq, tq), 0) // chunk
    ci = lax.broadcasted_iota(jnp.int32, (tq, tq), 1) // chunk
    visible = ci <= ri

    def head(h):
        cols = slice(h * LANES, (h + 1) * LANES)
        q = q_ref[:, cols]

        def step(j, carry, diagonal):
            m, l, acc = carry
            off = pl.multiple_of(j * tq, tq)
            s = _mm_nt(q, k_ref[pl.ds(off, tq), cols]) * scale
            if diagonal:
                s = jnp.where(visible, s, NEG_INF)
            m_new = jnp.maximum(m, jnp.max(s, axis=1, keepdims=True))
            alpha = jnp.exp(m - m_new)
            p = jnp.exp(s - m_new)
            l = alpha * l + jnp.sum(p, axis=1, keepdims=True)
            acc = alpha * acc + _mm(p, v_ref[pl.ds(off, tq), cols])
            return m_new, l, acc

        init = (jnp.full((tq, 1), NEG_INF, F32), jnp.zeros((tq, 1), F32), jnp.zeros((tq, LANES), F32))
        carry = lax.fori_loop(0, i, functools.partial(step, diagonal=False), init)
        _, l, acc = step(i, carry, True)
        return acc / l

    for p in range(nheads // 2):
        o_ref[:, p * LANES:(p + 1) * LANES] = head(2 * p) + head(2 * p + 1)


def _flash(qm, km, vm, *, nbatch, tq, scale, nheads, chunk):
    t, hp = qm.shape
    s = t // nbatch
    nq = s // tq
    ow = nheads // 2 * LANES
    return pl.pallas_call(
        functools.partial(_flash_body, tq=tq, scale=scale, nheads=nheads, chunk=chunk),
        grid=(nbatch, nq),
        in_specs=[pl.BlockSpec((tq, hp), lambda b, i: (b * nq + i, 0)),
                  pl.BlockSpec((s, hp), lambda b, i: (b, 0)),
                  pl.BlockSpec((s, hp), lambda b, i: (b, 0))],
        out_specs=pl.BlockSpec((tq, ow), lambda b, i: (b * nq + i, 0)),
        out_shape=jax.ShapeDtypeStruct((t, ow), F32),
        compiler_params=_params("parallel", "arbitrary"), name="flash_mla",
    )(qm, km, vm)


def _decode_attn_body(q_ref, cpast_ref, kpast_ref, cnew_ref, knew_ref, wuk_ref, wuv_ref, o_ref,
                      *, nheads, nope, rope, scale):
    c_past = cpast_ref[0]
    k_past = kpast_ref[0]
    c_new = cnew_ref[...]
    k_new = knew_ref[:, nope:nope + rope]
    outs = []
    for h in range(nheads):
        q = q_ref[:, h * LANES:(h + 1) * LANES]
        q_lat = _mm(q, wuk_ref[h])
        q_pe = q[:, nope:nope + rope]
        s_p = (_mm_nt(q_lat, c_past) + _mm_nt(q_pe, k_past)) * scale
        s_n = (_mm_nt(q_lat, c_new) + _mm_nt(q_pe, k_new)) * scale
        m = jnp.maximum(jnp.max(s_p, axis=1, keepdims=True), jnp.max(s_n, axis=1, keepdims=True))
        p_p = jnp.exp(s_p - m)
        p_n = jnp.exp(s_n - m)
        l = jnp.sum(p_p, axis=1, keepdims=True) + jnp.sum(p_n, axis=1, keepdims=True)
        o_lat = (_mm(p_p, c_past) + _mm(p_n, c_new)) / l
        outs.append(_mm(o_lat, wuv_ref[h]))
    o_ref[...] = jnp.concatenate(outs, axis=1)


def _decode_attn(qm, c_past, k_past, c_new, kpe_new, wuk3, wuv3, *, nq, nope, rope, scale):
    nb, past, kvr = c_past.shape
    nheads, _, vdim = wuv3.shape
    t, hp = qm.shape
    row = lambda b: (b, 0)
    full3 = lambda b: (0, 0, 0)
    return pl.pallas_call(
        functools.partial(_decode_attn_body, nheads=nheads, nope=nope, rope=rope, scale=scale),
        grid=(nb,),
        in_specs=[pl.BlockSpec((nq, hp), row),
                  pl.BlockSpec((1, past, kvr), lambda b: (b, 0, 0)),
                  pl.BlockSpec((1, past, rope), lambda b: (b, 0, 0)),
                  pl.BlockSpec((nq, kvr), row),
                  pl.BlockSpec((nq, LANES), row),
                  pl.BlockSpec(wuk3.shape, full3),
                  pl.BlockSpec(wuv3.shape, full3)],
        out_specs=pl.BlockSpec((nq, nheads * vdim), row),
        out_shape=jax.ShapeDtypeStruct((t, nheads * vdim), F32),
        compiler_params=_params("parallel"), name="decode_mla",
    )(qm, c_past, k_past, c_new, kpe_new, wuk3, wuv3)


def _split3(x):
    a = x.astype(MXU_DTYPE)
    r = x - a.astype(F32)
    b = r.astype(MXU_DTYPE)
    c = (r - b.astype(F32)).astype(MXU_DTYPE)
    return a, b, c


def _group_mean(x, avg):
    a, b, c = _split3(x)
    dot = lambda t: jnp.dot(t, avg, preferred_element_type=F32)
    return dot(a) + dot(b) + dot(c)


def _topk_rows(s, payload, kk):
    n = s.shape[0]
    rid = lax.broadcasted_iota(jnp.int32, s.shape, 0)
    vals, pays = [], []
    for _ in range(kk):
        mx = jnp.max(s, axis=0, keepdims=True)
        first = jnp.min(jnp.where(s == mx, rid, n), axis=0, keepdims=True)
        hit = rid == first
        vals.append(mx)
        pays.append(jnp.max(jnp.where(hit, payload, -1), axis=0, keepdims=True))
        s = jnp.where(hit, NEG_INF, s)
    return jnp.concatenate(vals, axis=0), jnp.concatenate(pays, axis=0)


def _mix_route_body(x_ref, ret_ref, gate_ref, mla_ref, gnw_ref, avg_ref, wo_ref, ln2_ref, wq_ref, keys_ref,
                    h_ref, hn_ref, idx_ref, g_ref, *, rw, pheads, nkeys, topk):
    ret = ret_ref[...]
    avg = avg_ref[...]
    mu = _group_mean(ret, avg)
    cen = ret - mu
    var = _group_mean(cen * cen, avg)
    gate = gate_ref[...]
    y = cen * lax.rsqrt(var + EPS) * gnw_ref[...] * (gate * jax.nn.sigmoid(gate))
    h = x_ref[...] + _mm(y, wo_ref[0:rw, :]) + _mm(mla_ref[...], wo_ref[rw:, :])
    h_ref[...] = h
    hn = _rms(h, ln2_ref[...])
    hn_ref[...] = hn
    qp = _mm(hn, wq_ref[...])
    kid = lax.broadcasted_iota(jnp.int32, (nkeys, qp.shape[0]), 0)
    idx_rows, g_rows = [], []
    for hd in range(pheads):
        ts, ti = [], []
        for half in range(2):
            c = (2 * hd + half) * LANES
            st = _mm_nt(keys_ref[2 * hd + half], qp[:, c:c + LANES])
            v, i = _topk_rows(st, kid, topk)
            ts.append(v)
            ti.append(i)
        cand = jnp.concatenate([ts[0][a:a + 1, :] + ts[1] for a in range(topk)], axis=0)
        cidx = jnp.concatenate([ti[0][a:a + 1, :] * nkeys + ti[1] for a in range(topk)], axis=0)
        best, expert = _topk_rows(cand, cidx, topk)
        e = jnp.exp(best - best[0:1, :])
        g_rows.append(e / jnp.sum(e, axis=0, keepdims=True))
        idx_rows.append(expert)
    idx_ref[...] = jnp.concatenate(idx_rows, axis=0).T
    g_ref[...] = jnp.concatenate(g_rows, axis=0).T


def _mix_route(x, ret_o, gate, mla_o, gnw, avg, wo, ln2, wq, keys, *, tm, pheads, nkeys, topk):
    t, d = x.shape
    rw = ret_o.shape[1]
    nsel = pheads * topk
    row = lambda i: (i, 0)
    fs = lambda a: pl.BlockSpec(a.shape, lambda i: (0,) * a.ndim)
    return pl.pallas_call(
        functools.partial(_mix_route_body, rw=rw, pheads=pheads, nkeys=nkeys, topk=topk),
        grid=(t // tm,),
        in_specs=[pl.BlockSpec((tm, d), row), pl.BlockSpec((tm, rw), row), pl.BlockSpec((tm, rw), row),
                  pl.BlockSpec((tm, mla_o.shape[1]), row), fs(gnw), fs(avg), fs(wo), fs(ln2), fs(wq), fs(keys)],
        out_specs=[pl.BlockSpec((tm, d), row), pl.BlockSpec((tm, d), row),
                   pl.BlockSpec((tm, nsel), row), pl.BlockSpec((tm, nsel), row)],
        out_shape=[jax.ShapeDtypeStruct((t, d), F32), jax.ShapeDtypeStruct((t, d), F32),
                   jax.ShapeDtypeStruct((t, nsel), jnp.int32),
                   jax.ShapeDtypeStruct((t, nsel), F32)],
        compiler_params=_params("parallel"), name="mix_route",
    )(x, ret_o, gate, mla_o, gnw, avg, wo, ln2, wq, keys)


def _gelu_gate_body(hid_ref, g_ref, a_ref):
    hid = hid_ref[...]
    a_ref[...] = 0.5 * hid * (1.0 + lax.erf(hid * (2.0 ** -0.5))) * g_ref[...]


def _gelu_gate(hid, g, *, tm):
    t, n = hid.shape
    blk = pl.BlockSpec((tm, n), lambda i: (i, 0))
    return pl.pallas_call(
        _gelu_gate_body, grid=(t // tm,), in_specs=[blk, blk], out_specs=blk,
        out_shape=jax.ShapeDtypeStruct((t, n), F32), compiler_params=_params("parallel"), name="gelu_gate",
    )(hid, g)


def _residual_body(h_ref, p_ref, lnf_ref, o_ref, *, final_norm):
    out = h_ref[...] + p_ref[...]
    if final_norm:
        out = _rms(out, lnf_ref[...])
    o_ref[...] = out


def _residual(h, peer, lnf, *, tm, final_norm):
    t, d = h.shape
    blk = pl.BlockSpec((tm, d), lambda i: (i, 0))
    return pl.pallas_call(
        functools.partial(_residual_body, final_norm=final_norm), grid=(t // tm,),
        in_specs=[blk, blk, pl.BlockSpec((1, d), lambda i: (0, 0))], out_specs=blk,
        out_shape=jax.ShapeDtypeStruct((t, d), F32), compiler_params=_params("parallel"), name="residual_norm",
    )(h, peer, lnf)


SC_CORES = 2
SC_SUBCORES = 16
SC_LANES = 16
SC_RING = 4


def _sc_worker_id():
    return lax.axis_index("s") * SC_CORES + lax.axis_index("c")


def _sc_ring(nq, start, wait, compute):
    for s in range(SC_RING - 1):
        start(s, s)

    @pl.loop(0, nq, step=SC_RING)
    def _(q0):
        for s in range(SC_RING):
            q = q0 + s
            nxt = q + SC_RING - 1

            @pl.when(nxt < nq)
            def _():
                start(nxt, (s + SC_RING - 1) % SC_RING)

            wait(q, s)
            compute(q, s)


def _peer_hidden_sc(xn, idx, u_tab, after):
    t, d = xn.shape
    nsel = idx.shape[1]
    nw = SC_CORES * SC_SUBCORES
    per_w = t // nw
    tb = min(16, per_w)
    nchunk = nsel // SC_LANES
    shift = nchunk.bit_length() - 1
    ncol = d // SC_LANES
    nq = tb * nchunk
    assert per_w * nw == t and per_w % tb == 0 and nchunk == 1 << shift and nq % SC_RING == 0
    mesh = plsc.VectorSubcoreMesh(core_axis_name="c", subcore_axis_name="s")

    def body(x_hbm, idx_hbm, u_hbm, _after_hbm, out_hbm, idx_v, x_v, ubuf, hid_v, sem):
        wid = _sc_worker_id()
        lane = lax.iota(jnp.int32, SC_LANES)

        def gather(q, slot):
            tok = lax.shift_right_logical(q, shift)
            ch = q & (nchunk - 1)
            rows = idx_v.at[tok, pl.ds(ch * SC_LANES, SC_LANES)]
            return pltpu.make_async_copy(u_hbm.at[rows], ubuf.at[slot], sem.at[slot])

        def compute(q, slot):
            tok = lax.shift_right_logical(q, shift)
            ch = q & (nchunk - 1)

            def col(c, accs):
                cs = pl.ds(pl.multiple_of(c * SC_LANES, SC_LANES), SC_LANES)
                xc = x_v[tok, cs]
                return tuple(a + xc * ubuf[slot, k, cs] for k, a in enumerate(accs))

            accs = lax.fori_loop(0, ncol, col, tuple(jnp.zeros((SC_LANES,), F32) for _ in range(SC_LANES)),
                                 unroll=2)
            out = jnp.zeros((SC_LANES,), F32)
            for k in range(SC_LANES):
                out = jnp.where(lane == k, jnp.sum(accs[k]), out)
            hid_v[tok, pl.ds(ch * SC_LANES, SC_LANES)] = out

        @pl.loop(0, per_w // tb)
        def _(b):
            base = wid * per_w + b * tb
            pltpu.sync_copy(idx_hbm.at[pl.ds(base, tb)], idx_v)
            pltpu.sync_copy(x_hbm.at[pl.ds(base, tb)], x_v)
            _sc_ring(nq, lambda q, s: gather(q, s).start(), lambda q, s: gather(q, s).wait(), compute)
            pltpu.sync_copy(hid_v, out_hbm.at[pl.ds(base, tb)])

    return pl.kernel(
        body, out_type=jax.ShapeDtypeStruct((t, nsel), F32), mesh=mesh,
        scratch_types=[pltpu.VMEM((tb, nsel), jnp.int32), pltpu.VMEM((tb, d), F32),
                       pltpu.VMEM((SC_RING, SC_LANES, d), F32), pltpu.VMEM((tb, nsel), F32),
                       pltpu.SemaphoreType.DMA((SC_RING,))],
        compiler_params=pltpu.CompilerParams(needs_layout_passes=False), name="peer_hidden_sc",
    )(xn, idx, u_tab, after)


def _peer_mix_sc(act, idx, v_tab):
    t, nsel = act.shape
    d = v_tab.shape[1]
    nw = SC_CORES * SC_SUBCORES
    per_w = t // nw
    tb = min(16, per_w)
    nchunk = nsel // SC_LANES
    shift = nchunk.bit_length() - 1
    ncol = d // SC_LANES
    nq = tb * nchunk
    assert per_w * nw == t and per_w % tb == 0 and nchunk == 1 << shift and nq % SC_RING == 0
    mesh = plsc.VectorSubcoreMesh(core_axis_name="c", subcore_axis_name="s")

    def body(a_hbm, idx_hbm, v_hbm, out_hbm, idx_v, a_v, vbuf, o_v, sem):
        wid = _sc_worker_id()
        zero = jnp.zeros((SC_LANES,), F32)

        def gather(q, slot):
            tok = lax.shift_right_logical(q, shift)
            ch = q & (nchunk - 1)
            rows = idx_v.at[tok, pl.ds(ch * SC_LANES, SC_LANES)]
            return pltpu.make_async_copy(v_hbm.at[rows], vbuf.at[slot], sem.at[slot])

        def compute(q, slot):
            tok = lax.shift_right_logical(q, shift)
            ch = q & (nchunk - 1)
            tok_v = jnp.full((SC_LANES,), tok, jnp.int32)
            col_v = jnp.full((SC_LANES,), ch * SC_LANES, jnp.int32)
            w = [plsc.load_gather(a_v, [tok_v, col_v + k]) for k in range(SC_LANES)]

            @plsc.parallel_loop(0, ncol, unroll=4)
            def _(c):
                cs = pl.ds(pl.multiple_of(c * SC_LANES, SC_LANES), SC_LANES)
                terms = [w[k] * vbuf[slot, k, cs] for k in range(SC_LANES)]
                while len(terms) > 1:
                    terms = [a + b for a, b in zip(terms[0::2], terms[1::2])]
                plsc.addupdate(o_v.at[tok, cs], terms[0])

        @pl.loop(0, per_w // tb)
        def _(b):
            base = wid * per_w + b * tb
            pltpu.sync_copy(idx_hbm.at[pl.ds(base, tb)], idx_v)
            pltpu.sync_copy(a_hbm.at[pl.ds(base, tb)], a_v)

            @pl.loop(0, tb)
            def _(r):
                @pl.loop(0, ncol)
                def _(c):
                    o_v[r, pl.ds(pl.multiple_of(c * SC_LANES, SC_LANES), SC_LANES)] = zero

            _sc_ring(nq, lambda q, s: gather(q, s).start(), lambda q, s: gather(q, s).wait(), compute)
            pltpu.sync_copy(o_v, out_hbm.at[pl.ds(base, tb)])

    return pl.kernel(
        body, out_type=jax.ShapeDtypeStruct((t, d), F32), mesh=mesh,
        scratch_types=[pltpu.VMEM((tb, nsel), jnp.int32), pltpu.VMEM((tb, nsel), F32),
                       pltpu.VMEM((SC_RING, SC_LANES, d), F32), pltpu.VMEM((tb, d), F32),
                       pltpu.SemaphoreType.DMA((SC_RING,))],
        compiler_params=pltpu.CompilerParams(needs_layout_passes=False), name="peer_mix_sc",
    )(act, idx, v_tab)


def _rope_tables(pos, half, group, width, lo):
    inv = ROPE_BASE ** (-jnp.arange(half, dtype=F32) / half)
    ang = pos.astype(F32)[:, None] * inv[None, :]
    cos, sin = jnp.cos(ang), jnp.sin(ang)
    n = pos.shape[0]
    reps = width // group
    pad_hi = group - lo - 2 * half
    blk = lambda a, b, fill: jnp.concatenate(
        [jnp.full((n, lo), fill, F32), a, b, jnp.full((n, pad_hi), fill, F32)], axis=1)
    z = jnp.zeros_like(sin)
    c = blk(cos, cos, 1.0)
    sa = blk(-sin, z, 0.0)
    sb = blk(z, sin, 0.0)
    return [jnp.tile(a, (1, reps)) for a in (c, sa, sb)]


def _ret_log_decay(nheads):
    return jnp.log(1.0 - jnp.exp2(-5.0 - jnp.arange(nheads, dtype=F32)))


def _pair_states(s):
    b, h, dk, dv = s.shape
    s = s.reshape(b, h // 2, 2, dk, dv)
    z = jnp.zeros_like(s[:, :, 0])
    top = jnp.concatenate([s[:, :, 0], z], axis=-1)
    bot = jnp.concatenate([z, s[:, :, 1]], axis=-1)
    return jnp.concatenate([top, bot], axis=-2)


def _unpair_states(sp, dk, dv):
    b, hp = sp.shape[:2]
    return jnp.stack([sp[:, :, :dk, :dv], sp[:, :, dk:, dv:]], axis=2).reshape(b, 2 * hp, dk, dv)


def _layer_weights(ln1_w, w_in, ret_gn_w, q_norm_w, w_uq, kv_norm_w, w_uk, w_uv, w_o, ln2_w,
                   peer_w_q, peer_sub_keys, dims):
    d = w_in.shape[0]
    nheads, nope, rope, vdim = dims["nheads"], dims["nope"], dims["mla_rope"], dims["vdim"]
    o6 = 4 * dims["rw"] + dims["qrank"] + dims["kvrank"]
    zc = lambda r, c: jnp.zeros((r, c), F32)
    win_p = jnp.concatenate([w_in[:, :o6], zc(d, nope), w_in[:, o6:], zc(d, LANES - nope - rope)], axis=1)
    qr, kr = w_uq.shape[0], w_uk.shape[0]
    wuq_p = jnp.concatenate([w_uq, jnp.zeros((qr, nheads, LANES - nope - rope), F32)], axis=2).reshape(qr, -1)
    wuk_p = jnp.concatenate([w_uk, jnp.zeros((kr, nheads, LANES - nope), F32)], axis=2).reshape(kr, -1)
    zv = jnp.zeros((kr, nheads // 2, LANES - vdim), F32)
    wv = w_uv.reshape(kr, nheads // 2, 2, vdim)
    wuv_p = jnp.concatenate([wv[:, :, 0], zv, zv, wv[:, :, 1]], axis=2).reshape(kr, -1)
    wuk3 = jnp.concatenate([jnp.transpose(w_uk, (1, 2, 0)),
                            jnp.zeros((nheads, LANES - nope, kr), F32)], axis=1)
    wuv3 = jnp.transpose(w_uv, (1, 0, 2))
    gidx = jnp.arange(dims["rw"]) // dims["ret_dv"]
    avg = (gidx[:, None] == gidx[None, :]).astype(F32) / dims["ret_dv"]
    keys = peer_sub_keys.reshape(-1, peer_sub_keys.shape[2], peer_sub_keys.shape[3])
    c = lambda a: a.astype(MXU_DTYPE)
    r2 = lambda a: a.reshape(1, -1)
    return dict(ln1=r2(ln1_w), win_p=c(win_p), gnw=r2(ret_gn_w), qnw=r2(q_norm_w), kvnw=r2(kv_norm_w),
                wuq_p=c(wuq_p), wuk_p=c(wuk_p), wuv_p=c(wuv_p), wuk3=c(wuk3), wuv3=c(wuv3), avg=c(avg),
                wo=c(w_o), ln2=r2(ln2_w), wq=c(peer_w_q), keys=c(keys))


def _stream_layer(x, w, tabs, lg, s0, u_tab, v_tab, lnf, dims, *, nbatch, ret_rows, ret_chunk, tm,
                  final_norm, cache=None, after=None):
    seq = x.shape[0] // nbatch
    nheads, dk, dv = dims["nheads"], dims["ret_dk"], dims["ret_dv"]
    qr, kr, vr, gate, qm, km, vm, ckv, kpe = _inproj(
        x, tabs, w["ln1"], w["win_p"], w["qnw"], w["kvnw"], w["wuq_p"], w["wuk_p"], w["wuv_p"], tm=tm, dims=dims)
    ret_o, s_pairs = _retention(lg, qr, kr, vr, _pair_states(s0), nbatch=nbatch, rows=ret_rows,
                                chunk=ret_chunk, dk=dk)
    scale = (dims["nope"] + dims["mla_rope"]) ** -0.5
    if cache is None:
        mla_o = _flash(qm, km, vm, nbatch=nbatch, tq=min(256, seq), scale=scale, nheads=nheads, chunk=CHUNK)
    else:
        mla_o = _decode_attn(qm, cache[0], cache[1], ckv, kpe, w["wuk3"], w["wuv3"], nq=seq,
                             nope=dims["nope"], rope=dims["mla_rope"], scale=scale)
    h, hn, idx, g = _mix_route(x, ret_o, gate, mla_o, w["gnw"], w["avg"], w["wo"], w["ln2"], w["wq"], w["keys"],
                               tm=tm, pheads=dims["pheads"], nkeys=dims["nkeys"], topk=dims["topk"])
    act = _gelu_gate(_peer_hidden_sc(hn, idx, u_tab, idx if after is None else after), g, tm=tm)
    peer = _peer_mix_sc(act, idx, v_tab)
    out = _residual(h, peer, lnf, tm=tm, final_norm=final_norm)
    nope, rope = dims["nope"], dims["mla_rope"]
    return (out, ckv, kpe[:, nope:nope + rope], _unpair_states(s_pairs, dk, dv)), peer


def kernel(x_prompt, x_sample, cache_mla_ckv, cache_mla_krope, state_retention, ln1_w, w_in, ret_gn_w,
           mla_q_norm_w, mla_w_uq, mla_kv_norm_w, mla_w_uk, mla_w_uv, w_o, ln2_w, peer_w_q, peer_sub_keys,
           peer_u, peer_v, lnf_w):
    depth = w_in.shape[0]
    nb, seq, d = x_prompt.shape
    db, dseq, _ = x_sample.shape
    past = cache_mla_ckv.shape[2]
    rheads, dk, dv = state_retention.shape[2:]
    nkeys = peer_sub_keys.shape[3]
    dims = dict(rw=rheads * dk, ret_dk=dk, ret_dv=dv, qrank=mla_w_uq.shape[1], kvrank=mla_w_uk.shape[1],
                nheads=mla_w_uq.shape[2], nope=mla_w_uk.shape[3], vdim=mla_w_uv.shape[3],
                mla_rope=mla_w_uq.shape[3] - mla_w_uk.shape[3], pheads=peer_sub_keys.shape[1], nkeys=nkeys,
                topk=PEER_TOPK)
    assert rheads * dk == rheads * dv and dims["nheads"] % 2 == 0 and dk * 2 == LANES and dims["vdim"] * 2 == LANES

    def tables(pos):
        return (_rope_tables(pos, dk // 2, dk, dims["rw"], 0)
                + _rope_tables(pos, dims["mla_rope"] // 2, LANES, LANES, dims["nope"]))

    tabs_p = tables(jnp.arange(seq))
    tabs_s = tables(jnp.tile(past + jnp.arange(dseq), db))
    lg = _ret_log_decay(rheads)
    lnf = lnf_w.reshape(1, -1)
    hp = x_prompt.reshape(nb * seq, d)
    hs = x_sample.reshape(db * dseq, d)
    outs = [[] for _ in range(6)]
    for l in range(depth):
        w = _layer_weights(ln1_w[l], w_in[l], ret_gn_w[l], mla_q_norm_w[l], mla_w_uq[l], mla_kv_norm_w[l],
                           mla_w_uk[l], mla_w_uv[l], w_o[l], ln2_w[l], peer_w_q[l], peer_sub_keys[l], dims)
        last = l == depth - 1
        gb = nb // PROMPT_GROUPS if nb % PROMPT_GROUPS == 0 else nb
        parts, peers = [], []
        for g in range(nb // gb):
            part, peer = _stream_layer(
                hp[g * gb * seq:(g + 1) * gb * seq], w, tabs_p, lg, jnp.zeros((gb, rheads, dk, dv), F32),
                peer_u[l], peer_v[l], lnf, dims, nbatch=gb, ret_rows=min(256, seq), ret_chunk=CHUNK,
                tm=min(256, gb * seq), final_norm=last, after=peers[g - 2] if g >= 2 else None)
            parts.append(part)
            peers.append(peer)
        hp, c1, k1, s1 = (jnp.concatenate(p, axis=0) for p in zip(*parts))
        (hs, c2, k2, s2), _ = _stream_layer(
            hs, w, tabs_s, lg, state_retention[l], peer_u[l], peer_v[l], lnf, dims,
            nbatch=db, ret_rows=dseq, ret_chunk=dseq, tm=min(256, db * dseq),
            final_norm=last, cache=(cache_mla_ckv[l], cache_mla_krope[l]), after=peers[0])
        for acc, val in zip(outs, (c1.reshape(nb, seq, -1), k1.reshape(nb, seq, -1), s1,
                                   c2.reshape(db, dseq, -1), k2.reshape(db, dseq, -1), s2)):
            acc.append(val)
    return (hp.reshape(nb, seq, d), hs.reshape(db, dseq, d), *[jnp.stack(o) for o in outs])
```

```python
import functools

import jax
import jax.numpy as jnp
from jax import lax
from jax.experimental import pallas as pl
from jax.experimental.pallas import tpu as pltpu
from jax.experimental.pallas import tpu_sc as plsc

EPS = 1e-6
ROPE_BASE = 10000.0
CHUNK = 64
PEER_TOPK = 16
PROMPT_GROUPS = 8
LANES = 128
MXU_DTYPE = jnp.bfloat16
VMEM_LIMIT_BYTES = 56 * 1024 * 1024

F32 = jnp.float32
NEG_INF = float("-inf")


def _mm(a, b):
    return jnp.dot(a.astype(MXU_DTYPE), b.astype(MXU_DTYPE), preferred_element_type=F32)


def _mm_nt(a, b):
    return lax.dot_general(a.astype(MXU_DTYPE), b.astype(MXU_DTYPE),
                           (((1,), (1,)), ((), ())), preferred_element_type=F32)


def _mm_tn(a, b):
    return lax.dot_general(a.astype(MXU_DTYPE), b.astype(MXU_DTYPE),
                           (((0,), (0,)), ((), ())), preferred_element_type=F32)


def _rms(x, w):
    return x * lax.rsqrt(jnp.mean(x * x, axis=-1, keepdims=True) + EPS) * w


def _rope(t, c, sa, sb, half):
    n = t.shape[1]
    return t * c + pltpu.roll(t, n - half, 1) * sa + pltpu.roll(t, half, 1) * sb


def _params(*sem):
    return pltpu.CompilerParams(dimension_semantics=sem, vmem_limit_bytes=VMEM_LIMIT_BYTES)


def _inproj_body(x_ref, ln1_ref, win_ref, cr_ref, sar_ref, sbr_ref, cm_ref, sam_ref, sbm_ref,
                 qnw_ref, kvnw_ref, wuq_ref, wuk_ref, wuv_ref,
                 qr_ref, kr_ref, vr_ref, gate_ref, qm_ref, km_ref, vm_ref, ckv_ref, kpe_ref,
                 *, rw, qrank, kvrank, ret_half, mla_half, k_scale, nheads):
    n1 = _rms(x_ref[...], ln1_ref[...])
    proj = _mm(n1, win_ref[...])
    cr, sar, sbr = cr_ref[...], sar_ref[...], sbr_ref[...]
    qr_ref[...] = _rope(proj[:, 0:rw], cr, sar, sbr, ret_half)
    kr_ref[...] = _rope(proj[:, rw:2 * rw], cr, sar, sbr, ret_half) * k_scale
    vr_ref[...] = proj[:, 2 * rw:3 * rw]
    gate_ref[...] = proj[:, 3 * rw:4 * rw]
    o4 = 4 * rw
    o5 = o4 + qrank
    o6 = o5 + kvrank
    cm, sam, sbm = cm_ref[...], sam_ref[...], sbm_ref[...]
    tile = lambda t: jnp.concatenate([t] * nheads, axis=1)
    cq = _rms(proj[:, o4:o5], qnw_ref[...])
    qm = _rope(_mm(cq, wuq_ref[...]), tile(cm), tile(sam), tile(sbm), mla_half)
    qm_ref[...] = qm.astype(qm_ref.dtype)
    ckv = _rms(proj[:, o5:o6], kvnw_ref[...])
    ckv_ref[...] = ckv
    kpe = _rope(proj[:, o6:o6 + LANES], cm, sam, sbm, mla_half)
    kpe_ref[...] = kpe
    km_ref[...] = (_mm(ckv, wuk_ref[...]) + tile(kpe)).astype(km_ref.dtype)
    vm_ref[...] = _mm(ckv, wuv_ref[...]).astype(vm_ref.dtype)


def _inproj(x, tabs, ln1, win_p, qnw, kvnw, wuq_p, wuk_p, wuv_p, *, tm, dims):
    t, d = x.shape
    rw, nheads = dims["rw"], dims["nheads"]
    hp = nheads * LANES
    nblk_tab = tabs[0].shape[0] // tm
    row = lambda i: (i, 0)
    tab = lambda i: (i % nblk_tab, 0)
    full = lambda i: (0, 0)
    fs = lambda a: pl.BlockSpec(a.shape, full)
    in_specs = [pl.BlockSpec((tm, d), row), fs(ln1), fs(win_p)]
    in_specs += [pl.BlockSpec((tm, rw), tab)] * 3 + [pl.BlockSpec((tm, LANES), tab)] * 3
    in_specs += [fs(qnw), fs(kvnw), fs(wuq_p), fs(wuk_p), fs(wuv_p)]
    out_shape = [jax.ShapeDtypeStruct((t, rw), F32)] * 4
    out_shape += [jax.ShapeDtypeStruct((t, hp), MXU_DTYPE)] * 3
    out_shape += [jax.ShapeDtypeStruct((t, dims["kvrank"]), F32), jax.ShapeDtypeStruct((t, LANES), F32)]
    out_specs = [pl.BlockSpec((tm, rw), row)] * 4 + [pl.BlockSpec((tm, hp), row)] * 3
    out_specs += [pl.BlockSpec((tm, dims["kvrank"]), row), pl.BlockSpec((tm, LANES), row)]
    body = functools.partial(
        _inproj_body, rw=rw, qrank=dims["qrank"], kvrank=dims["kvrank"], ret_half=dims["ret_dk"] // 2,
        mla_half=dims["mla_rope"] // 2, k_scale=dims["ret_dk"] ** -0.5, nheads=nheads)
    return pl.pallas_call(
        body, grid=(t // tm,), in_specs=in_specs, out_specs=out_specs, out_shape=out_shape,
        compiler_params=_params("parallel"), name="inproj",
    )(x, ln1, win_p, *tabs, qnw, kvnw, wuq_p, wuk_p, wuv_p)


def _retention_body(lg_ref, q_ref, k_ref, v_ref, s0_ref, o_ref, sout_ref, s_scr, *, rows, chunk, dk):
    hp = pl.program_id(1)
    j = pl.program_id(2)

    @pl.when(j == 0)
    def _():
        s_scr[...] = s0_ref[0, 0]

    lane = lax.broadcasted_iota(jnp.int32, (1, LANES), 1)
    is_a = lane < dk
    lg_a = lg_ref[2 * hp]
    lg_b = lg_ref[2 * hp + 1]
    lgl = jnp.where(is_a, lg_a, lg_b)
    r = lax.broadcasted_iota(jnp.int32, (rows, 1), 0).astype(F32)
    q, k, v = q_ref[...], k_ref[...], v_ref[...]
    q_dec = q * jnp.exp(lgl * (r + 1.0))
    k_dec = k * jnp.exp(lgl * (float(rows) - 1.0 - r))
    ri = lax.broadcasted_iota(jnp.int32, (rows, rows), 0)
    ci = lax.broadcasted_iota(jnp.int32, (rows, rows), 1)
    dist = jnp.abs(ri - ci).astype(F32)
    visible = (ci // chunk) <= (ri // chunk)
    o = _mm(q_dec, s_scr[...])
    for first, lg in ((True, lg_a), (False, lg_b)):
        sel = is_a if first else jnp.logical_not(is_a)
        qh = jnp.where(sel, q, 0.0)
        vh = jnp.where(sel, v, 0.0)
        decay = jnp.where(visible, jnp.exp(lg * dist), 0.0)
        o = o + _mm(_mm_nt(qh, k) * decay, vh)
    o_ref[...] = o
    sr = lax.broadcasted_iota(jnp.int32, (LANES, LANES), 0) < dk
    sc = lax.broadcasted_iota(jnp.int32, (LANES, LANES), 1) < dk
    kv = jnp.where(sr == sc, _mm_tn(k_dec, v), 0.0)
    s_new = jnp.exp(lgl * float(rows)) * s_scr[...] + kv
    s_scr[...] = s_new

    @pl.when(j == pl.num_programs(2) - 1)
    def _():
        sout_ref[0, 0] = s_new


def _retention(lg, q, k, v, s0_pairs, *, nbatch, rows, chunk, dk):
    t, w = q.shape
    npairs = w // LANES
    nblk = t // (nbatch * rows)
    blk = pl.BlockSpec((rows, LANES), lambda b, p, j: (b * nblk + j, p))
    st = pl.BlockSpec((1, 1, LANES, LANES), lambda b, p, j: (b, p, 0, 0))
    return pl.pallas_call(
        functools.partial(_retention_body, rows=rows, chunk=chunk, dk=dk),
        grid=(nbatch, npairs, nblk),
        in_specs=[pl.BlockSpec(memory_space=pltpu.SMEM), blk, blk, blk, st],
        out_specs=[blk, st],
        out_shape=[jax.ShapeDtypeStruct((t, w), F32),
                   jax.ShapeDtypeStruct((nbatch, npairs, LANES, LANES), F32)],
        scratch_shapes=[pltpu.VMEM((LANES, LANES), F32)],
        compiler_params=_params("parallel", "parallel", "arbitrary"), name="retention",
    )(lg, q, k, v, s0_pairs)


def _flash_body(q_ref, k_ref, v_ref, o_ref, *, tq, scale, nheads, chunk):
    i = pl.program_id(1)
    ri = lax.broadcasted_iota(jnp.int32, (tq, tq), 0) // chunk
    ci = lax.broadcasted_iota(jnp.int32, (tq, tq), 1) // chunk
    visible = ci <= ri

    def head(h):
        cols = slice(h * LANES, (h + 1) * LANES)
        q = q_ref[:, cols]

        def step(j, carry, diagonal):
            m, l, acc = carry
            off = pl.multiple_of(j * tq, tq)
            s = _mm_nt(q, k_ref[pl.ds(off, tq), cols]) * scale
            if diagonal:
                s = jnp.where(visible, s, NEG_INF)
            m_new = jnp.maximum(m, jnp.max(s, axis=1, keepdims=True))
            alpha = jnp.exp(m - m_new)
            p = jnp.exp(s - m_new)
            l = alpha * l + jnp.sum(p, axis=1, keepdims=True)
            acc = alpha * acc + _mm(p, v_ref[pl.ds(off, tq), cols])
            return m_new, l, acc

        init = (jnp.full((tq, 1), NEG_INF, F32), jnp.zeros((tq, 1), F32), jnp.zeros((tq, LANES), F32))
        carry = lax.fori_loop(0, i, functools.partial(step, diagonal=False), init)
        _, l, acc = step(i, carry, True)
        return acc / l

    for p in range(nheads // 2):
        o_ref[:, p * LANES:(p + 1) * LANES] = head(2 * p) + head(2 * p + 1)


def _flash(qm, km, vm, *, nbatch, tq, scale, nheads, chunk):
    t, hp = qm.shape
    s = t // nbatch
    nq = s // tq
    ow = nheads // 2 * LANES
    return pl.pallas_call(
        functools.partial(_flash_body, tq=tq, scale=scale, nheads=nheads, chunk=chunk),
        grid=(nbatch, nq),
        in_specs=[pl.BlockSpec((tq, hp), lambda b, i: (b * nq + i, 0)),
                  pl.BlockSpec((s, hp), lambda b, i: (b, 0)),
                  pl.BlockSpec((s, hp), lambda b, i: (b, 0))],
        out_specs=pl.BlockSpec((tq, ow), lambda b, i: (b * nq + i, 0)),
        out_shape=jax.ShapeDtypeStruct((t, ow), F32),
        compiler_params=_params("parallel", "arbitrary"), name="flash_mla",
    )(qm, km, vm)


def _decode_attn_body(q_ref, cpast_ref, kpast_ref, cnew_ref, knew_ref, wuk_ref, wuv_ref, o_ref,
                      *, nheads, nope, rope, scale):
    c_past = cpast_ref[0]
    k_past = kpast_ref[0]
    c_new = cnew_ref[...]
    k_new = knew_ref[:, nope:nope + rope]
    outs = []
    for h in range(nheads):
        q = q_ref[:, h * LANES:(h + 1) * LANES]
        q_lat = _mm(q, wuk_ref[h])
        q_pe = q[:, nope:nope + rope]
        s_p = (_mm_nt(q_lat, c_past) + _mm_nt(q_pe, k_past)) * scale
        s_n = (_mm_nt(q_lat, c_new) + _mm_nt(q_pe, k_new)) * scale
        m = jnp.maximum(jnp.max(s_p, axis=1, keepdims=True), jnp.max(s_n, axis=1, keepdims=True))
        p_p = jnp.exp(s_p - m)
        p_n = jnp.exp(s_n - m)
        l = jnp.sum(p_p, axis=1, keepdims=True) + jnp.sum(p_n, axis=1, keepdims=True)
        o_lat = (_mm(p_p, c_past) + _mm(p_n, c_new)) / l
        outs.append(_mm(o_lat, wuv_ref[h]))
    o_ref[...] = jnp.concatenate(outs, axis=1)


def _decode_attn(qm, c_past, k_past, c_new, kpe_new, wuk3, wuv3, *, nq, nope, rope, scale):
    nb, past, kvr = c_past.shape
    nheads, _, vdim = wuv3.shape
    t, hp = qm.shape
    row = lambda b: (b, 0)
    full3 = lambda b: (0, 0, 0)
    return pl.pallas_call(
        functools.partial(_decode_attn_body, nheads=nheads, nope=nope, rope=rope, scale=scale),
        grid=(nb,),
        in_specs=[pl.BlockSpec((nq, hp), row),
                  pl.BlockSpec((1, past, kvr), lambda b: (b, 0, 0)),
                  pl.BlockSpec((1, past, rope), lambda b: (b, 0, 0)),
                  pl.BlockSpec((nq, kvr), row),
                  pl.BlockSpec((nq, LANES), row),
                  pl.BlockSpec(wuk3.shape, full3),
                  pl.BlockSpec(wuv3.shape, full3)],
        out_specs=pl.BlockSpec((nq, nheads * vdim), row),
        out_shape=jax.ShapeDtypeStruct((t, nheads * vdim), F32),
        compiler_params=_params("parallel"), name="decode_mla",
    )(qm, c_past, k_past, c_new, kpe_new, wuk3, wuv3)


def _split3(x):
    a = x.astype(MXU_DTYPE)
    r = x - a.astype(F32)
    b = r.astype(MXU_DTYPE)
    c = (r - b.astype(F32)).astype(MXU_DTYPE)
    return a, b, c


def _group_mean(x, avg):
    a, b, c = _split3(x)
    dot = lambda t: jnp.dot(t, avg, preferred_element_type=F32)
    return dot(a) + dot(b) + dot(c)


def _topk_rows(s, payload, kk):
    n = s.shape[0]
    rid = lax.broadcasted_iota(jnp.int32, s.shape, 0)
    vals, pays = [], []
    for _ in range(kk):
        mx = jnp.max(s, axis=0, keepdims=True)
        first = jnp.min(jnp.where(s == mx, rid, n), axis=0, keepdims=True)
        hit = rid == first
        vals.append(mx)
        pays.append(jnp.max(jnp.where(hit, payload, -1), axis=0, keepdims=True))
        s = jnp.where(hit, NEG_INF, s)
    return jnp.concatenate(vals, axis=0), jnp.concatenate(pays, axis=0)


def _mix_route_body(x_ref, ret_ref, gate_ref, mla_ref, gnw_ref, avg_ref, wo_ref, ln2_ref, wq_ref, keys_ref,
                    h_ref, hn_ref, idx_ref, g_ref, *, rw, pheads, nkeys, topk):
    ret = ret_ref[...]
    avg = avg_ref[...]
    mu = _group_mean(ret, avg)
    cen = ret - mu
    var = _group_mean(cen * cen, avg)
    gate = gate_ref[...]
    y = cen * lax.rsqrt(var + EPS) * gnw_ref[...] * (gate * jax.nn.sigmoid(gate))
    h = x_ref[...] + _mm(y, wo_ref[0:rw, :]) + _mm(mla_ref[...], wo_ref[rw:, :])
    h_ref[...] = h
    hn = _rms(h, ln2_ref[...])
    hn_ref[...] = hn
    qp = _mm(hn, wq_ref[...])
    kid = lax.broadcasted_iota(jnp.int32, (nkeys, qp.shape[0]), 0)
    idx_rows, g_rows = [], []
    for hd in range(pheads):
        ts, ti = [], []
        for half in range(2):
            c = (2 * hd + half) * LANES
            st = _mm_nt(keys_ref[2 * hd + half], qp[:, c:c + LANES])
            v, i = _topk_rows(st, kid, topk)
            ts.append(v)
            ti.append(i)
        cand = jnp.concatenate([ts[0][a:a + 1, :] + ts[1] for a in range(topk)], axis=0)
        cidx = jnp.concatenate([ti[0][a:a + 1, :] * nkeys + ti[1] for a in range(topk)], axis=0)
        best, expert = _topk_rows(cand, cidx, topk)
        e = jnp.exp(best - best[0:1, :])
        g_rows.append(e / jnp.sum(e, axis=0, keepdims=True))
        idx_rows.append(expert)
    idx_ref[...] = jnp.concatenate(idx_rows, axis=0).T
    g_ref[...] = jnp.concatenate(g_rows, axis=0).T


def _mix_route(x, ret_o, gate, mla_o, gnw, avg, wo, ln2, wq, keys, *, tm, pheads, nkeys, topk):
    t, d = x.shape
    rw = ret_o.shape[1]
    nsel = pheads * topk
    row = lambda i: (i, 0)
    fs = lambda a: pl.BlockSpec(a.shape, lambda i: (0,) * a.ndim)
    return pl.pallas_call(
        functools.partial(_mix_route_body, rw=rw, pheads=pheads, nkeys=nkeys, topk=topk),
        grid=(t // tm,),
        in_specs=[pl.BlockSpec((tm, d), row), pl.BlockSpec((tm, rw), row), pl.BlockSpec((tm, rw), row),
                  pl.BlockSpec((tm, mla_o.shape[1]), row), fs(gnw), fs(avg), fs(wo), fs(ln2), fs(wq), fs(keys)],
        out_specs=[pl.BlockSpec((tm, d), row), pl.BlockSpec((tm, d), row),
                   pl.BlockSpec((tm, nsel), row), pl.BlockSpec((tm, nsel), row)],
        out_shape=[jax.ShapeDtypeStruct((t, d), F32), jax.ShapeDtypeStruct((t, d), F32),
                   jax.ShapeDtypeStruct((t, nsel), jnp.int32),
                   jax.ShapeDtypeStruct((t, nsel), F32)],
        compiler_params=_params("parallel"), name="mix_route",
    )(x, ret_o, gate, mla_o, gnw, avg, wo, ln2, wq, keys)


def _gelu_gate_body(hid_ref, g_ref, a_ref):
    hid = hid_ref[...]
    a_ref[...] = 0.5 * hid * (1.0 + lax.erf(hid * (2.0 ** -0.5))) * g_ref[...]


def _gelu_gate(hid, g, *, tm):
    t, n = hid.shape
    blk = pl.BlockSpec((tm, n), lambda i: (i, 0))
    return pl.pallas_call(
        _gelu_gate_body, grid=(t // tm,), in_specs=[blk, blk], out_specs=blk,
        out_shape=jax.ShapeDtypeStruct((t, n), F32), compiler_params=_params("parallel"), name="gelu_gate",
    )(hid, g)


def _residual_body(h_ref, p_ref, lnf_ref, o_ref, *, final_norm):
    out = h_ref[...] + p_ref[...]
    if final_norm:
        out = _rms(out, lnf_ref[...])
    o_ref[...] = out


def _residual(h, peer, lnf, *, tm, final_norm):
    t, d = h.shape
    blk = pl.BlockSpec((tm, d), lambda i: (i, 0))
    return pl.pallas_call(
        functools.partial(_residual_body, final_norm=final_norm), grid=(t // tm,),
        in_specs=[blk, blk, pl.BlockSpec((1, d), lambda i: (0, 0))], out_specs=blk,
        out_shape=jax.ShapeDtypeStruct((t, d), F32), compiler_params=_params("parallel"), name="residual_norm",
    )(h, peer, lnf)


SC_CORES = 2
SC_SUBCORES = 16
SC_LANES = 16
SC_RING = 4
SC_BATCH = 32


def _sc_worker_id():
    return lax.axis_index("s") * SC_CORES + lax.axis_index("c")


def _sc_ring(nq, start, wait, compute):
    for s in range(SC_RING - 1):
        start(s, s)

    @pl.loop(0, nq, step=SC_RING)
    def _(q0):
        for s in range(SC_RING):
            q = q0 + s
            nxt = q + SC_RING - 1

            @pl.when(nxt < nq)
            def _():
                start(nxt, (s + SC_RING - 1) % SC_RING)

            wait(q, s)
            compute(q, s)


def _peer_hidden_sc(xn, idx, u_tab, after):
    t, d = xn.shape
    nsel = idx.shape[1]
    nw = SC_CORES * SC_SUBCORES
    per_w = t // nw
    tb = min(SC_BATCH, per_w)
    nchunk = nsel // SC_LANES
    shift = nchunk.bit_length() - 1
    ncol = d // SC_LANES
    nq = tb * nchunk
    assert per_w * nw == t and per_w % tb == 0 and nchunk == 1 << shift and nq % SC_RING == 0
    mesh = plsc.VectorSubcoreMesh(core_axis_name="c", subcore_axis_name="s")

    def body(x_hbm, idx_hbm, u_hbm, _after_hbm, out_hbm, idx_v, x_v, ubuf, hid_v, sem):
        wid = _sc_worker_id()
        lane = lax.iota(jnp.int32, SC_LANES)

        def gather(q, slot):
            tok = lax.shift_right_logical(q, shift)
            ch = q & (nchunk - 1)
            rows = idx_v.at[tok, pl.ds(ch * SC_LANES, SC_LANES)]
            return pltpu.make_async_copy(u_hbm.at[rows], ubuf.at[slot], sem.at[slot])

        def compute(q, slot):
            tok = lax.shift_right_logical(q, shift)
            ch = q & (nchunk - 1)

            def col(c, accs):
                cs = pl.ds(pl.multiple_of(c * SC_LANES, SC_LANES), SC_LANES)
                xc = x_v[tok, cs]
                return tuple(a + xc * ubuf[slot, k, cs] for k, a in enumerate(accs))

            accs = lax.fori_loop(0, ncol, col, tuple(jnp.zeros((SC_LANES,), F32) for _ in range(SC_LANES)),
                                 unroll=2)
            out = jnp.zeros((SC_LANES,), F32)
            for k in range(SC_LANES):
                out = jnp.where(lane == k, jnp.sum(accs[k]), out)
            hid_v[tok, pl.ds(ch * SC_LANES, SC_LANES)] = out

        @pl.loop(0, per_w // tb)
        def _(b):
            base = wid * per_w + b * tb
            pltpu.sync_copy(idx_hbm.at[pl.ds(base, tb)], idx_v)
            pltpu.sync_copy(x_hbm.at[pl.ds(base, tb)], x_v)
            _sc_ring(nq, lambda q, s: gather(q, s).start(), lambda q, s: gather(q, s).wait(), compute)
            pltpu.sync_copy(hid_v, out_hbm.at[pl.ds(base, tb)])

    return pl.kernel(
        body, out_type=jax.ShapeDtypeStruct((t, nsel), F32), mesh=mesh,
        scratch_types=[pltpu.VMEM((tb, nsel), jnp.int32), pltpu.VMEM((tb, d), F32),
                       pltpu.VMEM((SC_RING, SC_LANES, d), F32), pltpu.VMEM((tb, nsel), F32),
                       pltpu.SemaphoreType.DMA((SC_RING,))],
        compiler_params=pltpu.CompilerParams(needs_layout_passes=False), name="peer_hidden_sc",
    )(xn, idx, u_tab, after)


def _peer_mix_sc(act, idx, v_tab):
    t, nsel = act.shape
    d = v_tab.shape[1]
    nw = SC_CORES * SC_SUBCORES
    per_w = t // nw
    tb = min(SC_BATCH, per_w)
    nchunk = nsel // SC_LANES
    shift = nchunk.bit_length() - 1
    ncol = d // SC_LANES
    nq = tb * nchunk
    assert per_w * nw == t and per_w % tb == 0 and nchunk == 1 << shift and nq % SC_RING == 0
    mesh = plsc.VectorSubcoreMesh(core_axis_name="c", subcore_axis_name="s")

    def body(a_hbm, idx_hbm, v_hbm, out_hbm, idx_v, a_v, vbuf, o_v, sem):
        wid = _sc_worker_id()
        zero = jnp.zeros((SC_LANES,), F32)

        def gather(q, slot):
            tok = lax.shift_right_logical(q, shift)
            ch = q & (nchunk - 1)
            rows = idx_v.at[tok, pl.ds(ch * SC_LANES, SC_LANES)]
            return pltpu.make_async_copy(v_hbm.at[rows], vbuf.at[slot], sem.at[slot])

        def compute(q, slot):
            tok = lax.shift_right_logical(q, shift)
            ch = q & (nchunk - 1)
            tok_v = jnp.full((SC_LANES,), tok, jnp.int32)
            col_v = jnp.full((SC_LANES,), ch * SC_LANES, jnp.int32)
            w = [plsc.load_gather(a_v, [tok_v, col_v + k]) for k in range(SC_LANES)]

            @plsc.parallel_loop(0, ncol, unroll=4)
            def _(c):
                cs = pl.ds(pl.multiple_of(c * SC_LANES, SC_LANES), SC_LANES)
                terms = [w[k] * vbuf[slot, k, cs] for k in range(SC_LANES)]
                while len(terms) > 1:
                    terms = [a + b for a, b in zip(terms[0::2], terms[1::2])]
                o_v[tok, cs] = o_v[tok, cs] + terms[0]

        @pl.loop(0, per_w // tb)
        def _(b):
            base = wid * per_w + b * tb
            pltpu.sync_copy(idx_hbm.at[pl.ds(base, tb)], idx_v)
            pltpu.sync_copy(a_hbm.at[pl.ds(base, tb)], a_v)

            @pl.loop(0, tb)
            def _(r):
                @pl.loop(0, ncol)
                def _(c):
                    o_v[r, pl.ds(pl.multiple_of(c * SC_LANES, SC_LANES), SC_LANES)] = zero

            _sc_ring(nq, lambda q, s: gather(q, s).start(), lambda q, s: gather(q, s).wait(), compute)
            pltpu.sync_copy(o_v, out_hbm.at[pl.ds(base, tb)])

    return pl.kernel(
        body, out_type=jax.ShapeDtypeStruct((t, d), F32), mesh=mesh,
        scratch_types=[pltpu.VMEM((tb, nsel), jnp.int32), pltpu.VMEM((tb, nsel), F32),
                       pltpu.VMEM((SC_RING, SC_LANES, d), F32), pltpu.VMEM((tb, d), F32),
                       pltpu.SemaphoreType.DMA((SC_RING,))],
        compiler_params=pltpu.CompilerParams(needs_layout_passes=False), name="peer_mix_sc",
    )(act, idx, v_tab)


def _rope_tables(pos, half, group, width, lo):
    inv = ROPE_BASE ** (-jnp.arange(half, dtype=F32) / half)
    ang = pos.astype(F32)[:, None] * inv[None, :]
    cos, sin = jnp.cos(ang), jnp.sin(ang)
    n = pos.shape[0]
    reps = width // group
    pad_hi = group - lo - 2 * half
    blk = lambda a, b, fill: jnp.concatenate(
        [jnp.full((n, lo), fill, F32), a, b, jnp.full((n, pad_hi), fill, F32)], axis=1)
    z = jnp.zeros_like(sin)
    c = blk(cos, cos, 1.0)
    sa = blk(-sin, z, 0.0)
    sb = blk(z, sin, 0.0)
    return [jnp.tile(a, (1, reps)) for a in (c, sa, sb)]


def _ret_log_decay(nheads):
    return jnp.log(1.0 - jnp.exp2(-5.0 - jnp.arange(nheads, dtype=F32)))


def _pair_states(s):
    b, h, dk, dv = s.shape
    s = s.reshape(b, h // 2, 2, dk, dv)
    z = jnp.zeros_like(s[:, :, 0])
    top = jnp.concatenate([s[:, :, 0], z], axis=-1)
    bot = jnp.concatenate([z, s[:, :, 1]], axis=-1)
    return jnp.concatenate([top, bot], axis=-2)


def _unpair_states(sp, dk, dv):
    b, hp = sp.shape[:2]
    return jnp.stack([sp[:, :, :dk, :dv], sp[:, :, dk:, dv:]], axis=2).reshape(b, 2 * hp, dk, dv)


def _layer_weights(ln1_w, w_in, ret_gn_w, q_norm_w, w_uq, kv_norm_w, w_uk, w_uv, w_o, ln2_w,
                   peer_w_q, peer_sub_keys, dims):
    d = w_in.shape[0]
    nheads, nope, rope, vdim = dims["nheads"], dims["nope"], dims["mla_rope"], dims["vdim"]
    o6 = 4 * dims["rw"] + dims["qrank"] + dims["kvrank"]
    zc = lambda r, c: jnp.zeros((r, c), F32)
    win_p = jnp.concatenate([w_in[:, :o6], zc(d, nope), w_in[:, o6:], zc(d, LANES - nope - rope)], axis=1)
    qr, kr = w_uq.shape[0], w_uk.shape[0]
    wuq_p = jnp.concatenate([w_uq, jnp.zeros((qr, nheads, LANES - nope - rope), F32)], axis=2).reshape(qr, -1)
    wuk_p = jnp.concatenate([w_uk, jnp.zeros((kr, nheads, LANES - nope), F32)], axis=2).reshape(kr, -1)
    zv = jnp.zeros((kr, nheads // 2, LANES - vdim), F32)
    wv = w_uv.reshape(kr, nheads // 2, 2, vdim)
    wuv_p = jnp.concatenate([wv[:, :, 0], zv, zv, wv[:, :, 1]], axis=2).reshape(kr, -1)
    wuk3 = jnp.concatenate([jnp.transpose(w_uk, (1, 2, 0)),
                            jnp.zeros((nheads, LANES - nope, kr), F32)], axis=1)
    wuv3 = jnp.transpose(w_uv, (1, 0, 2))
    gidx = jnp.arange(dims["rw"]) // dims["ret_dv"]
    avg = (gidx[:, None] == gidx[None, :]).astype(F32) / dims["ret_dv"]
    keys = peer_sub_keys.reshape(-1, peer_sub_keys.shape[2], peer_sub_keys.shape[3])
    c = lambda a: a.astype(MXU_DTYPE)
    r2 = lambda a: a.reshape(1, -1)
    return dict(ln1=r2(ln1_w), win_p=c(win_p), gnw=r2(ret_gn_w), qnw=r2(q_norm_w), kvnw=r2(kv_norm_w),
                wuq_p=c(wuq_p), wuk_p=c(wuk_p), wuv_p=c(wuv_p), wuk3=c(wuk3), wuv3=c(wuv3), avg=c(avg),
                wo=c(w_o), ln2=r2(ln2_w), wq=c(peer_w_q), keys=c(keys))


def _stream_layer(x, w, tabs, lg, s0, u_tab, v_tab, lnf, dims, *, nbatch, ret_rows, ret_chunk, tm,
                  final_norm, cache=None, after=None):
    seq = x.shape[0] // nbatch
    nheads, dk, dv = dims["nheads"], dims["ret_dk"], dims["ret_dv"]
    qr, kr, vr, gate, qm, km, vm, ckv, kpe = _inproj(
        x, tabs, w["ln1"], w["win_p"], w["qnw"], w["kvnw"], w["wuq_p"], w["wuk_p"], w["wuv_p"], tm=tm, dims=dims)
    ret_o, s_pairs = _retention(lg, qr, kr, vr, _pair_states(s0), nbatch=nbatch, rows=ret_rows,
                                chunk=ret_chunk, dk=dk)
    scale = (dims["nope"] + dims["mla_rope"]) ** -0.5
    if cache is None:
        mla_o = _flash(qm, km, vm, nbatch=nbatch, tq=min(256, seq), scale=scale, nheads=nheads, chunk=CHUNK)
    else:
        mla_o = _decode_attn(qm, cache[0], cache[1], ckv, kpe, w["wuk3"], w["wuv3"], nq=seq,
                             nope=dims["nope"], rope=dims["mla_rope"], scale=scale)
    h, hn, idx, g = _mix_route(x, ret_o, gate, mla_o, w["gnw"], w["avg"], w["wo"], w["ln2"], w["wq"], w["keys"],
                               tm=tm, pheads=dims["pheads"], nkeys=dims["nkeys"], topk=dims["topk"])
    act = _gelu_gate(_peer_hidden_sc(hn, idx, u_tab, idx if after is None else after), g, tm=tm)
    peer = _peer_mix_sc(act, idx, v_tab)
    out = _residual(h, peer, lnf, tm=tm, final_norm=final_norm)
    nope, rope = dims["nope"], dims["mla_rope"]
    return (out, ckv, kpe[:, nope:nope + rope], _unpair_states(s_pairs, dk, dv)), peer


def kernel(x_prompt, x_sample, cache_mla_ckv, cache_mla_krope, state_retention, ln1_w, w_in, ret_gn_w,
           mla_q_norm_w, mla_w_uq, mla_kv_norm_w, mla_w_uk, mla_w_uv, w_o, ln2_w, peer_w_q, peer_sub_keys,
           peer_u, peer_v, lnf_w):
    depth = w_in.shape[0]
    nb, seq, d = x_prompt.shape
    db, dseq, _ = x_sample.shape
    past = cache_mla_ckv.shape[2]
    rheads, dk, dv = state_retention.shape[2:]
    nkeys = peer_sub_keys.shape[3]
    dims = dict(rw=rheads * dk, ret_dk=dk, ret_dv=dv, qrank=mla_w_uq.shape[1], kvrank=mla_w_uk.shape[1],
                nheads=mla_w_uq.shape[2], nope=mla_w_uk.shape[3], vdim=mla_w_uv.shape[3],
                mla_rope=mla_w_uq.shape[3] - mla_w_uk.shape[3], pheads=peer_sub_keys.shape[1], nkeys=nkeys,
                topk=PEER_TOPK)
    assert rheads * dk == rheads * dv and dims["nheads"] % 2 == 0 and dk * 2 == LANES and dims["vdim"] * 2 == LANES

    def tables(pos):
        return (_rope_tables(pos, dk // 2, dk, dims["rw"], 0)
                + _rope_tables(pos, dims["mla_rope"] // 2, LANES, LANES, dims["nope"]))

    tabs_p = tables(jnp.arange(seq))
    tabs_s = tables(jnp.tile(past + jnp.arange(dseq), db))
    lg = _ret_log_decay(rheads)
    lnf = lnf_w.reshape(1, -1)
    hp = x_prompt.reshape(nb * seq, d)
    hs = x_sample.reshape(db * dseq, d)
    outs = [[] for _ in range(6)]
    for l in range(depth):
        w = _layer_weights(ln1_w[l], w_in[l], ret_gn_w[l], mla_q_norm_w[l], mla_w_uq[l], mla_kv_norm_w[l],
                           mla_w_uk[l], mla_w_uv[l], w_o[l], ln2_w[l], peer_w_q[l], peer_sub_keys[l], dims)
        last = l == depth - 1
        gb = nb // PROMPT_GROUPS if nb % PROMPT_GROUPS == 0 else nb
        parts, peers = [], []
        for g in range(nb // gb):
            part, peer = _stream_layer(
                hp[g * gb * seq:(g + 1) * gb * seq], w, tabs_p, lg, jnp.zeros((gb, rheads, dk, dv), F32),
                peer_u[l], peer_v[l], lnf, dims, nbatch=gb, ret_rows=min(256, seq), ret_chunk=CHUNK,
                tm=min(256, gb * seq), final_norm=last, after=peers[g - 2] if g >= 2 else None)
            parts.append(part)
            peers.append(peer)
        hp, c1, k1, s1 = (jnp.concatenate(p, axis=0) for p in zip(*parts))
        (hs, c2, k2, s2), _ = _stream_layer(
            hs, w, tabs_s, lg, state_retention[l], peer_u[l], peer_v[l], lnf, dims,
            nbatch=db, ret_rows=dseq, ret_chunk=dseq, tm=min(256, db * dseq),
            final_norm=last, cache=(cache_mla_ckv[l], cache_mla_krope[l]), after=peers[0])
        for acc, val in zip(outs, (c1.reshape(nb, seq, -1), k1.reshape(nb, seq, -1), s1,
                                   c2.reshape(db, dseq, -1), k2.reshape(db, dseq, -1), s2)):
            acc.append(val)
    return (hp.reshape(nb, seq, d), hs.reshape(db, dseq, d), *[jnp.stack(o) for o in outs])
```

```python
import functools

import jax
import jax.numpy as jnp
from jax import lax
from jax.experimental import pallas as pl
from jax.experimental.pallas import tpu as pltpu
from jax.experimental.pallas import tpu_sc as plsc

EPS = 1e-6
ROPE_BASE = 10000.0
CHUNK = 64
PEER_TOPK = 16
PROMPT_GROUPS = 8
PROMPT_HEAD_SPLIT = 4
LANES = 128
MXU_DTYPE = jnp.bfloat16
VMEM_LIMIT_BYTES = 56 * 1024 * 1024

F32 = jnp.float32
NEG_INF = float("-inf")


def _mm(a, b):
    return jnp.dot(a.astype(MXU_DTYPE), b.astype(MXU_DTYPE), preferred_element_type=F32)


def _mm_nt(a, b):
    return lax.dot_general(a.astype(MXU_DTYPE), b.astype(MXU_DTYPE),
                           (((1,), (1,)), ((), ())), preferred_element_type=F32)


def _mm_tn(a, b):
    return lax.dot_general(a.astype(MXU_DTYPE), b.astype(MXU_DTYPE),
                           (((0,), (0,)), ((), ())), preferred_element_type=F32)


def _rms(x, w):
    return x * lax.rsqrt(jnp.mean(x * x, axis=-1, keepdims=True) + EPS) * w


def _rope(t, c, sa, sb, half):
    n = t.shape[1]
    return t * c + pltpu.roll(t, n - half, 1) * sa + pltpu.roll(t, half, 1) * sb


def _params(*sem):
    return pltpu.CompilerParams(dimension_semantics=sem, vmem_limit_bytes=VMEM_LIMIT_BYTES)


def _inproj_body(x_ref, ln1_ref, win_ref, cr_ref, sar_ref, sbr_ref, cm_ref, sam_ref, sbm_ref,
                 qnw_ref, kvnw_ref, wuq_ref, wuk_ref, wuv_ref,
                 qr_ref, kr_ref, vr_ref, gate_ref, qm_ref, km_ref, vm_ref, ckv_ref, kpe_ref,
                 *, rw, qrank, kvrank, ret_half, mla_half, k_scale, nheads):
    n1 = _rms(x_ref[...], ln1_ref[...])
    proj = _mm(n1, win_ref[...])
    cr, sar, sbr = cr_ref[...], sar_ref[...], sbr_ref[...]
    qr_ref[...] = _rope(proj[:, 0:rw], cr, sar, sbr, ret_half)
    kr_ref[...] = _rope(proj[:, rw:2 * rw], cr, sar, sbr, ret_half) * k_scale
    vr_ref[...] = proj[:, 2 * rw:3 * rw]
    gate_ref[...] = proj[:, 3 * rw:4 * rw]
    o4 = 4 * rw
    o5 = o4 + qrank
    o6 = o5 + kvrank
    cm, sam, sbm = cm_ref[...], sam_ref[...], sbm_ref[...]
    tile = lambda t: jnp.concatenate([t] * nheads, axis=1)
    cq = _rms(proj[:, o4:o5], qnw_ref[...])
    qm = _rope(_mm(cq, wuq_ref[...]), tile(cm), tile(sam), tile(sbm), mla_half)
    qm_ref[...] = qm.astype(qm_ref.dtype)
    ckv = _rms(proj[:, o5:o6], kvnw_ref[...])
    ckv_ref[...] = ckv
    kpe = _rope(proj[:, o6:o6 + LANES], cm, sam, sbm, mla_half)
    kpe_ref[...] = kpe
    km_ref[...] = (_mm(ckv, wuk_ref[...]) + tile(kpe)).astype(km_ref.dtype)
    vm_ref[...] = _mm(ckv, wuv_ref[...]).astype(vm_ref.dtype)


def _inproj(x, tabs, ln1, win_p, qnw, kvnw, wuq_p, wuk_p, wuv_p, *, tm, dims):
    t, d = x.shape
    rw, nheads = dims["rw"], dims["nheads"]
    hp = nheads * LANES
    nblk_tab = tabs[0].shape[0] // tm
    row = lambda i: (i, 0)
    tab = lambda i: (i % nblk_tab, 0)
    full = lambda i: (0, 0)
    fs = lambda a: pl.BlockSpec(a.shape, full)
    in_specs = [pl.BlockSpec((tm, d), row), fs(ln1), fs(win_p)]
    in_specs += [pl.BlockSpec((tm, rw), tab)] * 3 + [pl.BlockSpec((tm, LANES), tab)] * 3
    in_specs += [fs(qnw), fs(kvnw), fs(wuq_p), fs(wuk_p), fs(wuv_p)]
    out_shape = [jax.ShapeDtypeStruct((t, rw), F32)] * 4
    out_shape += [jax.ShapeDtypeStruct((t, hp), MXU_DTYPE)] * 3
    out_shape += [jax.ShapeDtypeStruct((t, dims["kvrank"]), F32), jax.ShapeDtypeStruct((t, LANES), F32)]
    out_specs = [pl.BlockSpec((tm, rw), row)] * 4 + [pl.BlockSpec((tm, hp), row)] * 3
    out_specs += [pl.BlockSpec((tm, dims["kvrank"]), row), pl.BlockSpec((tm, LANES), row)]
    body = functools.partial(
        _inproj_body, rw=rw, qrank=dims["qrank"], kvrank=dims["kvrank"], ret_half=dims["ret_dk"] // 2,
        mla_half=dims["mla_rope"] // 2, k_scale=dims["ret_dk"] ** -0.5, nheads=nheads)
    return pl.pallas_call(
        body, grid=(t // tm,), in_specs=in_specs, out_specs=out_specs, out_shape=out_shape,
        compiler_params=_params("parallel"), name="inproj",
    )(x, ln1, win_p, *tabs, qnw, kvnw, wuq_p, wuk_p, wuv_p)


def _retention_body(lg_ref, q_ref, k_ref, v_ref, s0_ref, o_ref, sout_ref, s_scr, *, rows, chunk, dk):
    hp = pl.program_id(1)
    j = pl.program_id(2)

    @pl.when(j == 0)
    def _():
        s_scr[...] = s0_ref[0, 0]

    lane = lax.broadcasted_iota(jnp.int32, (1, LANES), 1)
    is_a = lane < dk
    lg_a = lg_ref[2 * hp]
    lg_b = lg_ref[2 * hp + 1]
    lgl = jnp.where(is_a, lg_a, lg_b)
    r = lax.broadcasted_iota(jnp.int32, (rows, 1), 0).astype(F32)
    q, k, v = q_ref[...], k_ref[...], v_ref[...]
    q_dec = q * jnp.exp(lgl * (r + 1.0))
    k_dec = k * jnp.exp(lgl * (float(rows) - 1.0 - r))
    ri = lax.broadcasted_iota(jnp.int32, (rows, rows), 0)
    ci = lax.broadcasted_iota(jnp.int32, (rows, rows), 1)
    dist = jnp.abs(ri - ci).astype(F32)
    visible = (ci // chunk) <= (ri // chunk)
    o = _mm(q_dec, s_scr[...])
    for first, lg in ((True, lg_a), (False, lg_b)):
        sel = is_a if first else jnp.logical_not(is_a)
        qh = jnp.where(sel, q, 0.0)
        vh = jnp.where(sel, v, 0.0)
        decay = jnp.where(visible, jnp.exp(lg * dist), 0.0)
        o = o + _mm(_mm_nt(qh, k) * decay, vh)
    o_ref[...] = o
    sr = lax.broadcasted_iota(jnp.int32, (LANES, LANES), 0) < dk
    sc = lax.broadcasted_iota(jnp.int32, (LANES, LANES), 1) < dk
    kv = jnp.where(sr == sc, _mm_tn(k_dec, v), 0.0)
    s_new = jnp.exp(lgl * float(rows)) * s_scr[...] + kv
    s_scr[...] = s_new

    @pl.when(j == pl.num_programs(2) - 1)
    def _():
        sout_ref[0, 0] = s_new


def _retention(lg, q, k, v, s0_pairs, *, nbatch, rows, chunk, dk):
    t, w = q.shape
    npairs = w // LANES
    nblk = t // (nbatch * rows)
    blk = pl.BlockSpec((rows, LANES), lambda b, p, j: (b * nblk + j, p))
    st = pl.BlockSpec((1, 1, LANES, LANES), lambda b, p, j: (b, p, 0, 0))
    return pl.pallas_call(
        functools.partial(_retention_body, rows=rows, chunk=chunk, dk=dk),
        grid=(nbatch, npairs, nblk),
        in_specs=[pl.BlockSpec(memory_space=pltpu.SMEM), blk, blk, blk, st],
        out_specs=[blk, st],
        out_shape=[jax.ShapeDtypeStruct((t, w), F32),
                   jax.ShapeDtypeStruct((nbatch, npairs, LANES, LANES), F32)],
        scratch_shapes=[pltpu.VMEM((LANES, LANES), F32)],
        compiler_params=_params("parallel", "parallel", "arbitrary"), name="retention",
    )(lg, q, k, v, s0_pairs)


def _flash_body(q_ref, k_ref, v_ref, o_ref, *, tq, tile0, scale, nheads, chunk):
    i = pl.program_id(1) + tile0
    ri = lax.broadcasted_iota(jnp.int32, (tq, tq), 0) // chunk
    ci = lax.broadcasted_iota(jnp.int32, (tq, tq), 1) // chunk
    visible = ci <= ri

    def head(h):
        cols = slice(h * LANES, (h + 1) * LANES)
        q = q_ref[:, cols]

        def step(j, carry, diagonal):
            m, l, acc = carry
            off = pl.multiple_of(j * tq, tq)
            s = _mm_nt(q, k_ref[pl.ds(off, tq), cols]) * scale
            if diagonal:
                s = jnp.where(visible, s, NEG_INF)
            m_new = jnp.maximum(m, jnp.max(s, axis=1, keepdims=True))
            alpha = jnp.exp(m - m_new)
            p = jnp.exp(s - m_new)
            l = alpha * l + jnp.sum(p, axis=1, keepdims=True)
            acc = alpha * acc + _mm(p, v_ref[pl.ds(off, tq), cols])
            return m_new, l, acc

        init = (jnp.full((tq, 1), NEG_INF, F32), jnp.zeros((tq, 1), F32), jnp.zeros((tq, LANES), F32))
        carry = lax.fori_loop(0, i, functools.partial(step, diagonal=False), init)
        _, l, acc = step(i, carry, True)
        return acc / l

    for p in range(nheads // 2):
        o_ref[:, p * LANES:(p + 1) * LANES] = head(2 * p) + head(2 * p + 1)


def _flash(qm, km, vm, *, nbatch, tq, lo, hi, scale, nheads, chunk):
    t, hp = qm.shape
    s = t // nbatch
    nq = s // tq
    tile0 = lo // tq
    nqr = (hi - lo) // tq
    ow = nheads // 2 * LANES
    return pl.pallas_call(
        functools.partial(_flash_body, tq=tq, tile0=tile0, scale=scale, nheads=nheads, chunk=chunk),
        grid=(nbatch, nqr),
        in_specs=[pl.BlockSpec((tq, hp), lambda b, i: (b * nq + tile0 + i, 0)),
                  pl.BlockSpec((s, hp), lambda b, i: (b, 0)),
                  pl.BlockSpec((s, hp), lambda b, i: (b, 0))],
        out_specs=pl.BlockSpec((tq, ow), lambda b, i: (b * nqr + i, 0)),
        out_shape=jax.ShapeDtypeStruct((nbatch * (hi - lo), ow), F32),
        compiler_params=_params("parallel", "arbitrary"), name="flash_mla",
    )(qm, km, vm)


def _decode_attn_body(q_ref, cpast_ref, kpast_ref, cnew_ref, knew_ref, wuk_ref, wuv_ref, o_ref,
                      *, nheads, nope, rope, scale):
    c_past = cpast_ref[0]
    k_past = kpast_ref[0]
    c_new = cnew_ref[...]
    k_new = knew_ref[:, nope:nope + rope]
    outs = []
    for h in range(nheads):
        q = q_ref[:, h * LANES:(h + 1) * LANES]
        q_lat = _mm(q, wuk_ref[h])
        q_pe = q[:, nope:nope + rope]
        s_p = (_mm_nt(q_lat, c_past) + _mm_nt(q_pe, k_past)) * scale
        s_n = (_mm_nt(q_lat, c_new) + _mm_nt(q_pe, k_new)) * scale
        m = jnp.maximum(jnp.max(s_p, axis=1, keepdims=True), jnp.max(s_n, axis=1, keepdims=True))
        p_p = jnp.exp(s_p - m)
        p_n = jnp.exp(s_n - m)
        l = jnp.sum(p_p, axis=1, keepdims=True) + jnp.sum(p_n, axis=1, keepdims=True)
        o_lat = (_mm(p_p, c_past) + _mm(p_n, c_new)) / l
        outs.append(_mm(o_lat, wuv_ref[h]))
    o_ref[...] = jnp.concatenate(outs, axis=1)


def _decode_attn(qm, c_past, k_past, c_new, kpe_new, wuk3, wuv3, *, nq, nope, rope, scale):
    nb, past, kvr = c_past.shape
    nheads, _, vdim = wuv3.shape
    t, hp = qm.shape
    row = lambda b: (b, 0)
    full3 = lambda b: (0, 0, 0)
    return pl.pallas_call(
        functools.partial(_decode_attn_body, nheads=nheads, nope=nope, rope=rope, scale=scale),
        grid=(nb,),
        in_specs=[pl.BlockSpec((nq, hp), row),
                  pl.BlockSpec((1, past, kvr), lambda b: (b, 0, 0)),
                  pl.BlockSpec((1, past, rope), lambda b: (b, 0, 0)),
                  pl.BlockSpec((nq, kvr), row),
                  pl.BlockSpec((nq, LANES), row),
                  pl.BlockSpec(wuk3.shape, full3),
                  pl.BlockSpec(wuv3.shape, full3)],
        out_specs=pl.BlockSpec((nq, nheads * vdim), row),
        out_shape=jax.ShapeDtypeStruct((t, nheads * vdim), F32),
        compiler_params=_params("parallel"), name="decode_mla",
    )(qm, c_past, k_past, c_new, kpe_new, wuk3, wuv3)


def _split3(x):
    a = x.astype(MXU_DTYPE)
    r = x - a.astype(F32)
    b = r.astype(MXU_DTYPE)
    c = (r - b.astype(F32)).astype(MXU_DTYPE)
    return a, b, c


def _group_mean(x, avg):
    a, b, c = _split3(x)
    dot = lambda t: jnp.dot(t, avg, preferred_element_type=F32)
    return dot(a) + dot(b) + dot(c)


def _topk_rows(s, payload, kk):
    n = s.shape[0]
    rid = lax.broadcasted_iota(jnp.int32, s.shape, 0)
    vals, pays = [], []
    for _ in range(kk):
        mx = jnp.max(s, axis=0, keepdims=True)
        first = jnp.min(jnp.where(s == mx, rid, n), axis=0, keepdims=True)
        hit = rid == first
        vals.append(mx)
        pays.append(jnp.max(jnp.where(hit, payload, -1), axis=0, keepdims=True))
        s = jnp.where(hit, NEG_INF, s)
    return jnp.concatenate(vals, axis=0), jnp.concatenate(pays, axis=0)


def _mix_route_body(x_ref, ret_ref, gate_ref, mla_ref, gnw_ref, avg_ref, wo_ref, ln2_ref, wq_ref, keys_ref,
                    h_ref, hn_ref, idx_ref, g_ref, *, rw, pheads, nkeys, topk):
    ret = ret_ref[...]
    avg = avg_ref[...]
    mu = _group_mean(ret, avg)
    cen = ret - mu
    var = _group_mean(cen * cen, avg)
    gate = gate_ref[...]
    y = cen * lax.rsqrt(var + EPS) * gnw_ref[...] * (gate * jax.nn.sigmoid(gate))
    h = x_ref[...] + _mm(y, wo_ref[0:rw, :]) + _mm(mla_ref[...], wo_ref[rw:, :])
    h_ref[...] = h
    hn = _rms(h, ln2_ref[...])
    hn_ref[...] = hn
    qp = _mm(hn, wq_ref[...])
    kid = lax.broadcasted_iota(jnp.int32, (nkeys, qp.shape[0]), 0)
    idx_rows, g_rows = [], []
    for hd in range(pheads):
        ts, ti = [], []
        for half in range(2):
            c = (2 * hd + half) * LANES
            st = _mm_nt(keys_ref[2 * hd + half], qp[:, c:c + LANES])
            v, i = _topk_rows(st, kid, topk)
            ts.append(v)
            ti.append(i)
        cand = jnp.concatenate([ts[0][a:a + 1, :] + ts[1] for a in range(topk)], axis=0)
        cidx = jnp.concatenate([ti[0][a:a + 1, :] * nkeys + ti[1] for a in range(topk)], axis=0)
        best, expert = _topk_rows(cand, cidx, topk)
        e = jnp.exp(best - best[0:1, :])
        g_rows.append(e / jnp.sum(e, axis=0, keepdims=True))
        idx_rows.append(expert)
    idx_ref[...] = jnp.concatenate(idx_rows, axis=0).T
    g_ref[...] = jnp.concatenate(g_rows, axis=0).T


def _mix_route(x, ret_o, gate, mla_o, gnw, avg, wo, ln2, wq, keys, *, tm, row0, pheads, nkeys, topk):
    t = mla_o.shape[0]
    d = x.shape[1]
    rw = ret_o.shape[1]
    nsel = pheads * topk
    blk0 = row0 // tm
    row = lambda i: (i, 0)
    src = lambda i: (blk0 + i, 0)
    fs = lambda a: pl.BlockSpec(a.shape, lambda i: (0,) * a.ndim)
    return pl.pallas_call(
        functools.partial(_mix_route_body, rw=rw, pheads=pheads, nkeys=nkeys, topk=topk),
        grid=(t // tm,),
        in_specs=[pl.BlockSpec((tm, d), src), pl.BlockSpec((tm, rw), src), pl.BlockSpec((tm, rw), src),
                  pl.BlockSpec((tm, mla_o.shape[1]), row), fs(gnw), fs(avg), fs(wo), fs(ln2), fs(wq), fs(keys)],
        out_specs=[pl.BlockSpec((tm, d), row), pl.BlockSpec((tm, d), row),
                   pl.BlockSpec((tm, nsel), row), pl.BlockSpec((tm, nsel), row)],
        out_shape=[jax.ShapeDtypeStruct((t, d), F32), jax.ShapeDtypeStruct((t, d), F32),
                   jax.ShapeDtypeStruct((t, nsel), jnp.int32),
                   jax.ShapeDtypeStruct((t, nsel), F32)],
        compiler_params=_params("parallel"), name="mix_route",
    )(x, ret_o, gate, mla_o, gnw, avg, wo, ln2, wq, keys)


def _gelu_gate_body(hid_ref, g_ref, a_ref):
    hid = hid_ref[...]
    a_ref[...] = 0.5 * hid * (1.0 + lax.erf(hid * (2.0 ** -0.5))) * g_ref[...]


def _gelu_gate(hid, g, *, tm):
    t, n = hid.shape
    blk = pl.BlockSpec((tm, n), lambda i: (i, 0))
    return pl.pallas_call(
        _gelu_gate_body, grid=(t // tm,), in_specs=[blk, blk], out_specs=blk,
        out_shape=jax.ShapeDtypeStruct((t, n), F32), compiler_params=_params("parallel"), name="gelu_gate",
    )(hid, g)


def _residual_body(h_ref, p_ref, lnf_ref, o_ref, *, final_norm):
    out = h_ref[...] + p_ref[...]
    if final_norm:
        out = _rms(out, lnf_ref[...])
    o_ref[...] = out


def _residual(h, peer, lnf, *, tm, final_norm):
    t, d = h.shape
    blk = pl.BlockSpec((tm, d), lambda i: (i, 0))
    return pl.pallas_call(
        functools.partial(_residual_body, final_norm=final_norm), grid=(t // tm,),
        in_specs=[blk, blk, pl.BlockSpec((1, d), lambda i: (0, 0))], out_specs=blk,
        out_shape=jax.ShapeDtypeStruct((t, d), F32), compiler_params=_params("parallel"), name="residual_norm",
    )(h, peer, lnf)


SC_CORES = 2
SC_SUBCORES = 16
SC_LANES = 16
SC_RING = 4
SC_BATCH = 32


def _sc_worker_id():
    return lax.axis_index("s") * SC_CORES + lax.axis_index("c")


def _sc_ring(nq, start, wait, compute):
    for s in range(SC_RING - 1):
        start(s, s)

    @pl.loop(0, nq, step=SC_RING)
    def _(q0):
        for s in range(SC_RING):
            q = q0 + s
            nxt = q + SC_RING - 1

            @pl.when(nxt < nq)
            def _():
                start(nxt, (s + SC_RING - 1) % SC_RING)

            wait(q, s)
            compute(q, s)


def _peer_hidden_sc(xn, idx, u_tab, after):
    t, d = xn.shape
    nsel = idx.shape[1]
    nw = SC_CORES * SC_SUBCORES
    per_w = t // nw
    tb = min(SC_BATCH, per_w)
    nchunk = nsel // SC_LANES
    shift = nchunk.bit_length() - 1
    ncol = d // SC_LANES
    nq = tb * nchunk
    assert per_w * nw == t and per_w % tb == 0 and nchunk == 1 << shift and nq % SC_RING == 0
    mesh = plsc.VectorSubcoreMesh(core_axis_name="c", subcore_axis_name="s")

    def body(x_hbm, idx_hbm, u_hbm, _after_hbm, out_hbm, idx_v, x_v, ubuf, hid_v, sem):
        wid = _sc_worker_id()
        lane = lax.iota(jnp.int32, SC_LANES)

        def gather(q, slot):
            tok = lax.shift_right_logical(q, shift)
            ch = q & (nchunk - 1)
            rows = idx_v.at[tok, pl.ds(ch * SC_LANES, SC_LANES)]
            return pltpu.make_async_copy(u_hbm.at[rows], ubuf.at[slot], sem.at[slot])

        def compute(q, slot):
            tok = lax.shift_right_logical(q, shift)
            ch = q & (nchunk - 1)

            @plsc.parallel_loop(0, ncol, carry=tuple(jnp.zeros((SC_LANES,), F32) for _ in range(SC_LANES)))
            def accs(c, acc):
                cs = pl.ds(pl.multiple_of(c * SC_LANES, SC_LANES), SC_LANES)
                xc = x_v[tok, cs]
                return tuple(a + xc * ubuf[slot, k, cs] for k, a in enumerate(acc))

            out = jnp.zeros((SC_LANES,), F32)
            for k in range(SC_LANES):
                out = jnp.where(lane == k, jnp.sum(accs[k]), out)
            hid_v[tok, pl.ds(ch * SC_LANES, SC_LANES)] = out

        @pl.loop(0, per_w // tb)
        def _(b):
            base = wid * per_w + b * tb
            pltpu.sync_copy(idx_hbm.at[pl.ds(base, tb)], idx_v)
            pltpu.sync_copy(x_hbm.at[pl.ds(base, tb)], x_v)
            _sc_ring(nq, lambda q, s: gather(q, s).start(), lambda q, s: gather(q, s).wait(), compute)
            pltpu.sync_copy(hid_v, out_hbm.at[pl.ds(base, tb)])

    return pl.kernel(
        body, out_type=jax.ShapeDtypeStruct((t, nsel), F32), mesh=mesh,
        scratch_types=[pltpu.VMEM((tb, nsel), jnp.int32), pltpu.VMEM((tb, d), F32),
                       pltpu.VMEM((SC_RING, SC_LANES, d), F32), pltpu.VMEM((tb, nsel), F32),
                       pltpu.SemaphoreType.DMA((SC_RING,))],
        compiler_params=pltpu.CompilerParams(needs_layout_passes=False), name="peer_hidden_sc",
    )(xn, idx, u_tab, after)


def _peer_mix_sc(act, idx, v_tab):
    t, nsel = act.shape
    d = v_tab.shape[1]
    nw = SC_CORES * SC_SUBCORES
    per_w = t // nw
    tb = min(SC_BATCH, per_w)
    nchunk = nsel // SC_LANES
    shift = nchunk.bit_length() - 1
    ncol = d // SC_LANES
    nq = tb * nchunk
    assert per_w * nw == t and per_w % tb == 0 and nchunk == 1 << shift and nq % SC_RING == 0
    mesh = plsc.VectorSubcoreMesh(core_axis_name="c", subcore_axis_name="s")

    def body(a_hbm, idx_hbm, v_hbm, out_hbm, idx_v, a_v, vbuf, o_v, sem):
        wid = _sc_worker_id()
        zero = jnp.zeros((SC_LANES,), F32)

        def gather(q, slot):
            tok = lax.shift_right_logical(q, shift)
            ch = q & (nchunk - 1)
            rows = idx_v.at[tok, pl.ds(ch * SC_LANES, SC_LANES)]
            return pltpu.make_async_copy(v_hbm.at[rows], vbuf.at[slot], sem.at[slot])

        def compute(q, slot):
            tok = lax.shift_right_logical(q, shift)
            ch = q & (nchunk - 1)
            tok_v = jnp.full((SC_LANES,), tok, jnp.int32)
            col_v = jnp.full((SC_LANES,), ch * SC_LANES, jnp.int32)
            w = [plsc.load_gather(a_v, [tok_v, col_v + k]) for k in range(SC_LANES)]

            @plsc.parallel_loop(0, ncol)
            def _(c):
                cs = pl.ds(pl.multiple_of(c * SC_LANES, SC_LANES), SC_LANES)
                terms = [w[k] * vbuf[slot, k, cs] for k in range(SC_LANES)]
                while len(terms) > 1:
                    terms = [a + b for a, b in zip(terms[0::2], terms[1::2])]
                o_v[tok, cs] = o_v[tok, cs] + terms[0]

        @pl.loop(0, per_w // tb)
        def _(b):
            base = wid * per_w + b * tb
            pltpu.sync_copy(idx_hbm.at[pl.ds(base, tb)], idx_v)
            pltpu.sync_copy(a_hbm.at[pl.ds(base, tb)], a_v)

            @pl.loop(0, tb)
            def _(r):
                @pl.loop(0, ncol)
                def _(c):
                    o_v[r, pl.ds(pl.multiple_of(c * SC_LANES, SC_LANES), SC_LANES)] = zero

            _sc_ring(nq, lambda q, s: gather(q, s).start(), lambda q, s: gather(q, s).wait(), compute)
            pltpu.sync_copy(o_v, out_hbm.at[pl.ds(base, tb)])

    return pl.kernel(
        body, out_type=jax.ShapeDtypeStruct((t, d), F32), mesh=mesh,
        scratch_types=[pltpu.VMEM((tb, nsel), jnp.int32), pltpu.VMEM((tb, nsel), F32),
                       pltpu.VMEM((SC_RING, SC_LANES, d), F32), pltpu.VMEM((tb, d), F32),
                       pltpu.SemaphoreType.DMA((SC_RING,))],
        compiler_params=pltpu.CompilerParams(needs_layout_passes=False), name="peer_mix_sc",
    )(act, idx, v_tab)


def _rope_tables(pos, half, group, width, lo):
    inv = ROPE_BASE ** (-jnp.arange(half, dtype=F32) / half)
    ang = pos.astype(F32)[:, None] * inv[None, :]
    cos, sin = jnp.cos(ang), jnp.sin(ang)
    n = pos.shape[0]
    reps = width // group
    pad_hi = group - lo - 2 * half
    blk = lambda a, b, fill: jnp.concatenate(
        [jnp.full((n, lo), fill, F32), a, b, jnp.full((n, pad_hi), fill, F32)], axis=1)
    z = jnp.zeros_like(sin)
    c = blk(cos, cos, 1.0)
    sa = blk(-sin, z, 0.0)
    sb = blk(z, sin, 0.0)
    return [jnp.tile(a, (1, reps)) for a in (c, sa, sb)]


def _ret_log_decay(nheads):
    return jnp.log(1.0 - jnp.exp2(-5.0 - jnp.arange(nheads, dtype=F32)))


def _pair_states(s):
    b, h, dk, dv = s.shape
    s = s.reshape(b, h // 2, 2, dk, dv)
    z = jnp.zeros_like(s[:, :, 0])
    top = jnp.concatenate([s[:, :, 0], z], axis=-1)
    bot = jnp.concatenate([z, s[:, :, 1]], axis=-1)
    return jnp.concatenate([top, bot], axis=-2)


def _unpair_states(sp, dk, dv):
    b, hp = sp.shape[:2]
    return jnp.stack([sp[:, :, :dk, :dv], sp[:, :, dk:, dv:]], axis=2).reshape(b, 2 * hp, dk, dv)


def _layer_weights(ln1_w, w_in, ret_gn_w, q_norm_w, w_uq, kv_norm_w, w_uk, w_uv, w_o, ln2_w,
                   peer_w_q, peer_sub_keys, dims):
    d = w_in.shape[0]
    nheads, nope, rope, vdim = dims["nheads"], dims["nope"], dims["mla_rope"], dims["vdim"]
    o6 = 4 * dims["rw"] + dims["qrank"] + dims["kvrank"]
    zc = lambda r, c: jnp.zeros((r, c), F32)
    win_p = jnp.concatenate([w_in[:, :o6], zc(d, nope), w_in[:, o6:], zc(d, LANES - nope - rope)], axis=1)
    qr, kr = w_uq.shape[0], w_uk.shape[0]
    wuq_p = jnp.concatenate([w_uq, jnp.zeros((qr, nheads, LANES - nope - rope), F32)], axis=2).reshape(qr, -1)
    wuk_p = jnp.concatenate([w_uk, jnp.zeros((kr, nheads, LANES - nope), F32)], axis=2).reshape(kr, -1)
    zv = jnp.zeros((kr, nheads // 2, LANES - vdim), F32)
    wv = w_uv.reshape(kr, nheads // 2, 2, vdim)
    wuv_p = jnp.concatenate([wv[:, :, 0], zv, zv, wv[:, :, 1]], axis=2).reshape(kr, -1)
    wuk3 = jnp.concatenate([jnp.transpose(w_uk, (1, 2, 0)),
                            jnp.zeros((nheads, LANES - nope, kr), F32)], axis=1)
    wuv3 = jnp.transpose(w_uv, (1, 0, 2))
    gidx = jnp.arange(dims["rw"]) // dims["ret_dv"]
    avg = (gidx[:, None] == gidx[None, :]).astype(F32) / dims["ret_dv"]
    keys = peer_sub_keys.reshape(-1, peer_sub_keys.shape[2], peer_sub_keys.shape[3])
    c = lambda a: a.astype(MXU_DTYPE)
    r2 = lambda a: a.reshape(1, -1)
    return dict(ln1=r2(ln1_w), win_p=c(win_p), gnw=r2(ret_gn_w), qnw=r2(q_norm_w), kvnw=r2(kv_norm_w),
                wuq_p=c(wuq_p), wuk_p=c(wuk_p), wuv_p=c(wuv_p), wuk3=c(wuk3), wuv3=c(wuv3), avg=c(avg),
                wo=c(w_o), ln2=r2(ln2_w), wq=c(peer_w_q), keys=c(keys))


def _stream_layer(x, w, tabs, lg, s0, u_tab, v_tab, lnf, dims, *, nbatch, ret_rows, ret_chunk, tm,
                  final_norm, ranges, sc_order, cache=None):
    seq = x.shape[0] // nbatch
    assert nbatch == 1 or ranges == [(0, seq)]
    nheads, dk, dv = dims["nheads"], dims["ret_dk"], dims["ret_dv"]
    qr, kr, vr, gate, qm, km, vm, ckv, kpe = _inproj(
        x, tabs, w["ln1"], w["win_p"], w["qnw"], w["kvnw"], w["wuq_p"], w["wuk_p"], w["wuv_p"], tm=tm, dims=dims)
    ret_o, s_pairs = _retention(lg, qr, kr, vr, _pair_states(s0), nbatch=nbatch, rows=ret_rows,
                                chunk=ret_chunk, dk=dk)
    scale = (dims["nope"] + dims["mla_rope"]) ** -0.5
    outs = []
    for lo, hi in ranges:
        if cache is None:
            mla_o = _flash(qm, km, vm, nbatch=nbatch, tq=min(256, seq), lo=lo, hi=hi, scale=scale,
                           nheads=nheads, chunk=CHUNK)
        else:
            mla_o = _decode_attn(qm, cache[0], cache[1], ckv, kpe, w["wuk3"], w["wuv3"], nq=seq,
                                 nope=dims["nope"], rope=dims["mla_rope"], scale=scale)
        h, hn, idx, g = _mix_route(x, ret_o, gate, mla_o, w["gnw"], w["avg"], w["wo"], w["ln2"], w["wq"],
                                   w["keys"], tm=tm, row0=lo, pheads=dims["pheads"], nkeys=dims["nkeys"],
                                   topk=dims["topk"])
        after = sc_order[-2] if len(sc_order) >= 2 else idx
        act = _gelu_gate(_peer_hidden_sc(hn, idx, u_tab, after), g, tm=tm)
        peer = _peer_mix_sc(act, idx, v_tab)
        sc_order.append(peer)
        outs.append(_residual(h, peer, lnf, tm=tm, final_norm=final_norm))
    out = outs[0] if len(outs) == 1 else jnp.concatenate(outs, axis=0)
    nope, rope = dims["nope"], dims["mla_rope"]
    return out, ckv, kpe[:, nope:nope + rope], _unpair_states(s_pairs, dk, dv)


def kernel(x_prompt, x_sample, cache_mla_ckv, cache_mla_krope, state_retention, ln1_w, w_in, ret_gn_w,
           mla_q_norm_w, mla_w_uq, mla_kv_norm_w, mla_w_uk, mla_w_uv, w_o, ln2_w, peer_w_q, peer_sub_keys,
           peer_u, peer_v, lnf_w):
    depth = w_in.shape[0]
    nb, seq, d = x_prompt.shape
    db, dseq, _ = x_sample.shape
    past = cache_mla_ckv.shape[2]
    rheads, dk, dv = state_retention.shape[2:]
    nkeys = peer_sub_keys.shape[3]
    dims = dict(rw=rheads * dk, ret_dk=dk, ret_dv=dv, qrank=mla_w_uq.shape[1], kvrank=mla_w_uk.shape[1],
                nheads=mla_w_uq.shape[2], nope=mla_w_uk.shape[3], vdim=mla_w_uv.shape[3],
                mla_rope=mla_w_uq.shape[3] - mla_w_uk.shape[3], pheads=peer_sub_keys.shape[1], nkeys=nkeys,
                topk=PEER_TOPK)
    assert rheads * dk == rheads * dv and dims["nheads"] % 2 == 0 and dk * 2 == LANES and dims["vdim"] * 2 == LANES

    def tables(pos):
        return (_rope_tables(pos, dk // 2, dk, dims["rw"], 0)
                + _rope_tables(pos, dims["mla_rope"] // 2, LANES, LANES, dims["nope"]))

    tabs_p = tables(jnp.arange(seq))
    tabs_s = tables(jnp.tile(past + jnp.arange(dseq), db))
    lg = _ret_log_decay(rheads)
    lnf = lnf_w.reshape(1, -1)
    hp = x_prompt.reshape(nb * seq, d)
    hs = x_sample.reshape(db * dseq, d)
    outs = [[] for _ in range(6)]
    for l in range(depth):
        w = _layer_weights(ln1_w[l], w_in[l], ret_gn_w[l], mla_q_norm_w[l], mla_w_uq[l], mla_kv_norm_w[l],
                           mla_w_uk[l], mla_w_uv[l], w_o[l], ln2_w[l], peer_w_q[l], peer_sub_keys[l], dims)
        last = l == depth - 1
        gb = nb // PROMPT_GROUPS if nb % PROMPT_GROUPS == 0 else nb
        head = seq // PROMPT_HEAD_SPLIT
        split = gb == 1 and head % 256 == 0 and head > 0
        parts, sc_order = [], []
        for g in range(nb // gb):
            ranges = [(0, head), (head, seq)] if split and g == 0 else [(0, seq)]
            parts.append(_stream_layer(
                hp[g * gb * seq:(g + 1) * gb * seq], w, tabs_p, lg, jnp.zeros((gb, rheads, dk, dv), F32),
                peer_u[l], peer_v[l], lnf, dims, nbatch=gb, ret_rows=min(256, seq), ret_chunk=CHUNK,
                tm=min(256, gb * seq), final_norm=last, ranges=ranges, sc_order=sc_order))
        hp, c1, k1, s1 = (jnp.concatenate(p, axis=0) for p in zip(*parts))
        hs, c2, k2, s2 = _stream_layer(
            hs, w, tabs_s, lg, state_retention[l], peer_u[l], peer_v[l], lnf, dims,
            nbatch=db, ret_rows=dseq, ret_chunk=dseq, tm=min(256, db * dseq), final_norm=last,
            ranges=[(0, dseq)], sc_order=[sc_order[0], sc_order[0]],
            cache=(cache_mla_ckv[l], cache_mla_krope[l]))
        for acc, val in zip(outs, (c1.reshape(nb, seq, -1), k1.reshape(nb, seq, -1), s1,
                                   c2.reshape(db, dseq, -1), k2.reshape(db, dseq, -1), s2)):
            acc.append(val)
    return (hp.reshape(nb, seq, d), hs.reshape(db, dseq, d), *[jnp.stack(o) for o in outs])
```

```python
import functools

import jax
import jax.numpy as jnp
from jax import lax
from jax.experimental import pallas as pl
from jax.experimental.pallas import tpu as pltpu
from jax.experimental.pallas import tpu_sc as plsc

EPS = 1e-6
ROPE_BASE = 10000.0
CHUNK = 64
PEER_TOPK = 16
PROMPT_GROUPS = 8
PROMPT_HEAD_SPLIT = 4
LANES = 128
MXU_DTYPE = jnp.bfloat16
VMEM_LIMIT_BYTES = 56 * 1024 * 1024

F32 = jnp.float32
NEG_INF = float("-inf")


def _mm(a, b):
    return jnp.dot(a.astype(MXU_DTYPE), b.astype(MXU_DTYPE), preferred_element_type=F32)


def _mm_nt(a, b):
    return lax.dot_general(a.astype(MXU_DTYPE), b.astype(MXU_DTYPE),
                           (((1,), (1,)), ((), ())), preferred_element_type=F32)


def _mm_tn(a, b):
    return lax.dot_general(a.astype(MXU_DTYPE), b.astype(MXU_DTYPE),
                           (((0,), (0,)), ((), ())), preferred_element_type=F32)


def _rms(x, w):
    return x * lax.rsqrt(jnp.mean(x * x, axis=-1, keepdims=True) + EPS) * w


def _rope(t, c, sa, sb, half):
    n = t.shape[1]
    return t * c + pltpu.roll(t, n - half, 1) * sa + pltpu.roll(t, half, 1) * sb


def _params(*sem):
    return pltpu.CompilerParams(dimension_semantics=sem, vmem_limit_bytes=VMEM_LIMIT_BYTES)


ORDER_SPEC = pl.BlockSpec(memory_space=pl.ANY)


def _ordered(body):
    def run(_order_ref, *refs):
        body(*refs)
    return run


def _inproj_body(x_ref, ln1_ref, win_ref, cr_ref, sar_ref, sbr_ref, cm_ref, sam_ref, sbm_ref,
                 qnw_ref, kvnw_ref, wuq_ref, wuk_ref, wuv_ref,
                 qr_ref, kr_ref, vr_ref, gate_ref, qm_ref, km_ref, vm_ref, ckv_ref, kpe_ref,
                 *, rw, qrank, kvrank, ret_half, mla_half, k_scale, nheads):
    n1 = _rms(x_ref[...], ln1_ref[...])
    proj = _mm(n1, win_ref[...])
    cr, sar, sbr = cr_ref[...], sar_ref[...], sbr_ref[...]
    qr_ref[...] = _rope(proj[:, 0:rw], cr, sar, sbr, ret_half)
    kr_ref[...] = _rope(proj[:, rw:2 * rw], cr, sar, sbr, ret_half) * k_scale
    vr_ref[...] = proj[:, 2 * rw:3 * rw]
    gate_ref[...] = proj[:, 3 * rw:4 * rw]
    o4 = 4 * rw
    o5 = o4 + qrank
    o6 = o5 + kvrank
    cm, sam, sbm = cm_ref[...], sam_ref[...], sbm_ref[...]
    tile = lambda t: jnp.concatenate([t] * nheads, axis=1)
    cq = _rms(proj[:, o4:o5], qnw_ref[...])
    qm = _rope(_mm(cq, wuq_ref[...]), tile(cm), tile(sam), tile(sbm), mla_half)
    qm_ref[...] = qm.astype(qm_ref.dtype)
    ckv = _rms(proj[:, o5:o6], kvnw_ref[...])
    ckv_ref[...] = ckv
    kpe = _rope(proj[:, o6:o6 + LANES], cm, sam, sbm, mla_half)
    kpe_ref[...] = kpe
    km_ref[...] = (_mm(ckv, wuk_ref[...]) + tile(kpe)).astype(km_ref.dtype)
    vm_ref[...] = _mm(ckv, wuv_ref[...]).astype(vm_ref.dtype)


def _inproj(order, x, tabs, ln1, win_p, qnw, kvnw, wuq_p, wuk_p, wuv_p, *, tm, dims):
    t, d = x.shape
    rw, nheads = dims["rw"], dims["nheads"]
    hp = nheads * LANES
    nblk_tab = tabs[0].shape[0] // tm
    row = lambda i: (i, 0)
    tab = lambda i: (i % nblk_tab, 0)
    full = lambda i: (0, 0)
    fs = lambda a: pl.BlockSpec(a.shape, full)
    in_specs = [pl.BlockSpec((tm, d), row), fs(ln1), fs(win_p)]
    in_specs += [pl.BlockSpec((tm, rw), tab)] * 3 + [pl.BlockSpec((tm, LANES), tab)] * 3
    in_specs += [fs(qnw), fs(kvnw), fs(wuq_p), fs(wuk_p), fs(wuv_p)]
    out_shape = [jax.ShapeDtypeStruct((t, rw), F32)] * 4
    out_shape += [jax.ShapeDtypeStruct((t, hp), MXU_DTYPE)] * 3
    out_shape += [jax.ShapeDtypeStruct((t, dims["kvrank"]), F32), jax.ShapeDtypeStruct((t, LANES), F32)]
    out_specs = [pl.BlockSpec((tm, rw), row)] * 4 + [pl.BlockSpec((tm, hp), row)] * 3
    out_specs += [pl.BlockSpec((tm, dims["kvrank"]), row), pl.BlockSpec((tm, LANES), row)]
    body = functools.partial(
        _inproj_body, rw=rw, qrank=dims["qrank"], kvrank=dims["kvrank"], ret_half=dims["ret_dk"] // 2,
        mla_half=dims["mla_rope"] // 2, k_scale=dims["ret_dk"] ** -0.5, nheads=nheads)
    return pl.pallas_call(
        _ordered(body), grid=(t // tm,), in_specs=[ORDER_SPEC] + in_specs, out_specs=out_specs,
        out_shape=out_shape, compiler_params=_params("parallel"), name="inproj",
    )(order, x, ln1, win_p, *tabs, qnw, kvnw, wuq_p, wuk_p, wuv_p)


def _retention_body(lg_ref, q_ref, k_ref, v_ref, s0_ref, o_ref, sout_ref, s_scr, *, rows, chunk, dk):
    hp = pl.program_id(1)
    j = pl.program_id(2)

    @pl.when(j == 0)
    def _():
        s_scr[...] = s0_ref[0, 0]

    lane = lax.broadcasted_iota(jnp.int32, (1, LANES), 1)
    is_a = lane < dk
    lg_a = lg_ref[2 * hp]
    lg_b = lg_ref[2 * hp + 1]
    lgl = jnp.where(is_a, lg_a, lg_b)
    r = lax.broadcasted_iota(jnp.int32, (rows, 1), 0).astype(F32)
    q, k, v = q_ref[...], k_ref[...], v_ref[...]
    q_dec = q * jnp.exp(lgl * (r + 1.0))
    k_dec = k * jnp.exp(lgl * (float(rows) - 1.0 - r))
    ri = lax.broadcasted_iota(jnp.int32, (rows, rows), 0)
    ci = lax.broadcasted_iota(jnp.int32, (rows, rows), 1)
    dist = jnp.abs(ri - ci).astype(F32)
    visible = (ci // chunk) <= (ri // chunk)
    o = _mm(q_dec, s_scr[...])
    for first, lg in ((True, lg_a), (False, lg_b)):
        sel = is_a if first else jnp.logical_not(is_a)
        qh = jnp.where(sel, q, 0.0)
        vh = jnp.where(sel, v, 0.0)
        decay = jnp.where(visible, jnp.exp(lg * dist), 0.0)
        o = o + _mm(_mm_nt(qh, k) * decay, vh)
    o_ref[...] = o
    sr = lax.broadcasted_iota(jnp.int32, (LANES, LANES), 0) < dk
    sc = lax.broadcasted_iota(jnp.int32, (LANES, LANES), 1) < dk
    kv = jnp.where(sr == sc, _mm_tn(k_dec, v), 0.0)
    s_new = jnp.exp(lgl * float(rows)) * s_scr[...] + kv
    s_scr[...] = s_new

    @pl.when(j == pl.num_programs(2) - 1)
    def _():
        sout_ref[0, 0] = s_new


def _retention(lg, q, k, v, s0_pairs, *, nbatch, rows, chunk, dk):
    t, w = q.shape
    npairs = w // LANES
    nblk = t // (nbatch * rows)
    blk = pl.BlockSpec((rows, LANES), lambda b, p, j: (b * nblk + j, p))
    st = pl.BlockSpec((1, 1, LANES, LANES), lambda b, p, j: (b, p, 0, 0))
    return pl.pallas_call(
        functools.partial(_retention_body, rows=rows, chunk=chunk, dk=dk),
        grid=(nbatch, npairs, nblk),
        in_specs=[pl.BlockSpec(memory_space=pltpu.SMEM), blk, blk, blk, st],
        out_specs=[blk, st],
        out_shape=[jax.ShapeDtypeStruct((t, w), F32),
                   jax.ShapeDtypeStruct((nbatch, npairs, LANES, LANES), F32)],
        scratch_shapes=[pltpu.VMEM((LANES, LANES), F32)],
        compiler_params=_params("parallel", "parallel", "arbitrary"), name="retention",
    )(lg, q, k, v, s0_pairs)


def _flash_body(q_ref, k_ref, v_ref, o_ref, *, tq, tile0, scale, nheads, chunk):
    i = pl.program_id(1) + tile0
    ri = lax.broadcasted_iota(jnp.int32, (tq, tq), 0) // chunk
    ci = lax.broadcasted_iota(jnp.int32, (tq, tq), 1) // chunk
    visible = ci <= ri

    def head(h):
        cols = slice(h * LANES, (h + 1) * LANES)
        q = q_ref[:, cols]

        def step(j, carry, diagonal):
            m, l, acc = carry
            off = pl.multiple_of(j * tq, tq)
            s = _mm_nt(q, k_ref[pl.ds(off, tq), cols]) * scale
            if diagonal:
                s = jnp.where(visible, s, NEG_INF)
            m_new = jnp.maximum(m, jnp.max(s, axis=1, keepdims=True))
            alpha = jnp.exp(m - m_new)
            p = jnp.exp(s - m_new)
            l = alpha * l + jnp.sum(p, axis=1, keepdims=True)
            acc = alpha * acc + _mm(p, v_ref[pl.ds(off, tq), cols])
            return m_new, l, acc

        init = (jnp.full((tq, 1), NEG_INF, F32), jnp.zeros((tq, 1), F32), jnp.zeros((tq, LANES), F32))
        carry = lax.fori_loop(0, i, functools.partial(step, diagonal=False), init)
        _, l, acc = step(i, carry, True)
        return acc / l

    for p in range(nheads // 2):
        o_ref[:, p * LANES:(p + 1) * LANES] = head(2 * p) + head(2 * p + 1)


def _flash(order, qm, km, vm, *, nbatch, tq, lo, hi, scale, nheads, chunk):
    t, hp = qm.shape
    s = t // nbatch
    nq = s // tq
    tile0 = lo // tq
    nqr = (hi - lo) // tq
    ow = nheads // 2 * LANES
    return pl.pallas_call(
        _ordered(functools.partial(_flash_body, tq=tq, tile0=tile0, scale=scale, nheads=nheads, chunk=chunk)),
        grid=(nbatch, nqr),
        in_specs=[ORDER_SPEC,
                  pl.BlockSpec((tq, hp), lambda b, i: (b * nq + tile0 + i, 0)),
                  pl.BlockSpec((s, hp), lambda b, i: (b, 0)),
                  pl.BlockSpec((s, hp), lambda b, i: (b, 0))],
        out_specs=pl.BlockSpec((tq, ow), lambda b, i: (b * nqr + i, 0)),
        out_shape=jax.ShapeDtypeStruct((nbatch * (hi - lo), ow), F32),
        compiler_params=_params("parallel", "arbitrary"), name="flash_mla",
    )(order, qm, km, vm)


def _decode_attn_body(q_ref, cpast_ref, kpast_ref, cnew_ref, knew_ref, wuk_ref, wuv_ref, o_ref,
                      *, nheads, nope, rope, scale):
    c_past = cpast_ref[0]
    k_past = kpast_ref[0]
    c_new = cnew_ref[...]
    k_new = knew_ref[:, nope:nope + rope]
    outs = []
    for h in range(nheads):
        q = q_ref[:, h * LANES:(h + 1) * LANES]
        q_lat = _mm(q, wuk_ref[h])
        q_pe = q[:, nope:nope + rope]
        s_p = (_mm_nt(q_lat, c_past) + _mm_nt(q_pe, k_past)) * scale
        s_n = (_mm_nt(q_lat, c_new) + _mm_nt(q_pe, k_new)) * scale
        m = jnp.maximum(jnp.max(s_p, axis=1, keepdims=True), jnp.max(s_n, axis=1, keepdims=True))
        p_p = jnp.exp(s_p - m)
        p_n = jnp.exp(s_n - m)
        l = jnp.sum(p_p, axis=1, keepdims=True) + jnp.sum(p_n, axis=1, keepdims=True)
        o_lat = (_mm(p_p, c_past) + _mm(p_n, c_new)) / l
        outs.append(_mm(o_lat, wuv_ref[h]))
    o_ref[...] = jnp.concatenate(outs, axis=1)


def _decode_attn(qm, c_past, k_past, c_new, kpe_new, wuk3, wuv3, *, nq, nope, rope, scale):
    nb, past, kvr = c_past.shape
    nheads, _, vdim = wuv3.shape
    t, hp = qm.shape
    row = lambda b: (b, 0)
    full3 = lambda b: (0, 0, 0)
    return pl.pallas_call(
        functools.partial(_decode_attn_body, nheads=nheads, nope=nope, rope=rope, scale=scale),
        grid=(nb,),
        in_specs=[pl.BlockSpec((nq, hp), row),
                  pl.BlockSpec((1, past, kvr), lambda b: (b, 0, 0)),
                  pl.BlockSpec((1, past, rope), lambda b: (b, 0, 0)),
                  pl.BlockSpec((nq, kvr), row),
                  pl.BlockSpec((nq, LANES), row),
                  pl.BlockSpec(wuk3.shape, full3),
                  pl.BlockSpec(wuv3.shape, full3)],
        out_specs=pl.BlockSpec((nq, nheads * vdim), row),
        out_shape=jax.ShapeDtypeStruct((t, nheads * vdim), F32),
        compiler_params=_params("parallel"), name="decode_mla",
    )(qm, c_past, k_past, c_new, kpe_new, wuk3, wuv3)


def _split3(x):
    a = x.astype(MXU_DTYPE)
    r = x - a.astype(F32)
    b = r.astype(MXU_DTYPE)
    c = (r - b.astype(F32)).astype(MXU_DTYPE)
    return a, b, c


def _group_mean(x, avg):
    a, b, c = _split3(x)
    dot = lambda t: jnp.dot(t, avg, preferred_element_type=F32)
    return dot(a) + dot(b) + dot(c)


def _topk_rows(s, payload, kk):
    n = s.shape[0]
    rid = lax.broadcasted_iota(jnp.int32, s.shape, 0)
    vals, pays = [], []
    for _ in range(kk):
        mx = jnp.max(s, axis=0, keepdims=True)
        first = jnp.min(jnp.where(s == mx, rid, n), axis=0, keepdims=True)
        hit = rid == first
        vals.append(mx)
        pays.append(jnp.max(jnp.where(hit, payload, -1), axis=0, keepdims=True))
        s = jnp.where(hit, NEG_INF, s)
    return jnp.concatenate(vals, axis=0), jnp.concatenate(pays, axis=0)


def _mix_route_body(x_ref, ret_ref, gate_ref, mla_ref, gnw_ref, avg_ref, wo_ref, ln2_ref, wq_ref, keys_ref,
                    h_ref, hn_ref, idx_ref, g_ref, *, rw, pheads, nkeys, topk):
    ret = ret_ref[...]
    avg = avg_ref[...]
    mu = _group_mean(ret, avg)
    cen = ret - mu
    var = _group_mean(cen * cen, avg)
    gate = gate_ref[...]
    y = cen * lax.rsqrt(var + EPS) * gnw_ref[...] * (gate * jax.nn.sigmoid(gate))
    h = x_ref[...] + _mm(y, wo_ref[0:rw, :]) + _mm(mla_ref[...], wo_ref[rw:, :])
    h_ref[...] = h
    hn = _rms(h, ln2_ref[...])
    hn_ref[...] = hn
    qp = _mm(hn, wq_ref[...])
    kid = lax.broadcasted_iota(jnp.int32, (nkeys, qp.shape[0]), 0)
    idx_rows, g_rows = [], []
    for hd in range(pheads):
        ts, ti = [], []
        for half in range(2):
            c = (2 * hd + half) * LANES
            st = _mm_nt(keys_ref[2 * hd + half], qp[:, c:c + LANES])
            v, i = _topk_rows(st, kid, topk)
            ts.append(v)
            ti.append(i)
        cand = jnp.concatenate([ts[0][a:a + 1, :] + ts[1] for a in range(topk)], axis=0)
        cidx = jnp.concatenate([ti[0][a:a + 1, :] * nkeys + ti[1] for a in range(topk)], axis=0)
        best, expert = _topk_rows(cand, cidx, topk)
        e = jnp.exp(best - best[0:1, :])
        g_rows.append(e / jnp.sum(e, axis=0, keepdims=True))
        idx_rows.append(expert)
    idx_ref[...] = jnp.concatenate(idx_rows, axis=0).T
    g_ref[...] = jnp.concatenate(g_rows, axis=0).T


def _mix_route(x, ret_o, gate, mla_o, gnw, avg, wo, ln2, wq, keys, *, tm, row0, pheads, nkeys, topk):
    t = mla_o.shape[0]
    d = x.shape[1]
    rw = ret_o.shape[1]
    nsel = pheads * topk
    blk0 = row0 // tm
    row = lambda i: (i, 0)
    src = lambda i: (blk0 + i, 0)
    fs = lambda a: pl.BlockSpec(a.shape, lambda i: (0,) * a.ndim)
    return pl.pallas_call(
        functools.partial(_mix_route_body, rw=rw, pheads=pheads, nkeys=nkeys, topk=topk),
        grid=(t // tm,),
        in_specs=[pl.BlockSpec((tm, d), src), pl.BlockSpec((tm, rw), src), pl.BlockSpec((tm, rw), src),
                  pl.BlockSpec((tm, mla_o.shape[1]), row), fs(gnw), fs(avg), fs(wo), fs(ln2), fs(wq), fs(keys)],
        out_specs=[pl.BlockSpec((tm, d), row), pl.BlockSpec((tm, d), row),
                   pl.BlockSpec((tm, nsel), row), pl.BlockSpec((tm, nsel), row)],
        out_shape=[jax.ShapeDtypeStruct((t, d), F32), jax.ShapeDtypeStruct((t, d), F32),
                   jax.ShapeDtypeStruct((t, nsel), jnp.int32),
                   jax.ShapeDtypeStruct((t, nsel), F32)],
        compiler_params=_params("parallel"), name="mix_route",
    )(x, ret_o, gate, mla_o, gnw, avg, wo, ln2, wq, keys)


def _gelu_gate_body(hid_ref, g_ref, a_ref):
    hid = hid_ref[...]
    a_ref[...] = 0.5 * hid * (1.0 + lax.erf(hid * (2.0 ** -0.5))) * g_ref[...]


def _gelu_gate(order, hid, g, *, tm):
    t, n = hid.shape
    blk = pl.BlockSpec((tm, n), lambda i: (i, 0))
    return pl.pallas_call(
        _ordered(_gelu_gate_body), grid=(t // tm,), in_specs=[ORDER_SPEC, blk, blk], out_specs=blk,
        out_shape=jax.ShapeDtypeStruct((t, n), F32), compiler_params=_params("parallel"), name="gelu_gate",
    )(order, hid, g)


def _residual_body(h_ref, p_ref, lnf_ref, o_ref, *, final_norm):
    out = h_ref[...] + p_ref[...]
    if final_norm:
        out = _rms(out, lnf_ref[...])
    o_ref[...] = out


def _residual(order, h, peer, lnf, *, tm, final_norm):
    t, d = h.shape
    blk = pl.BlockSpec((tm, d), lambda i: (i, 0))
    return pl.pallas_call(
        _ordered(functools.partial(_residual_body, final_norm=final_norm)), grid=(t // tm,),
        in_specs=[ORDER_SPEC, blk, blk, pl.BlockSpec((1, d), lambda i: (0, 0))], out_specs=blk,
        out_shape=jax.ShapeDtypeStruct((t, d), F32), compiler_params=_params("parallel"), name="residual_norm",
    )(order, h, peer, lnf)


SC_CORES = 2
SC_SUBCORES = 16
SC_LANES = 16
SC_RING = 4
SC_BATCH = 32


def _sc_worker_id():
    return lax.axis_index("s") * SC_CORES + lax.axis_index("c")


def _sc_ring(nq, start, wait, compute):
    for s in range(SC_RING - 1):
        start(s, s)

    @pl.loop(0, nq, step=SC_RING)
    def _(q0):
        for s in range(SC_RING):
            q = q0 + s
            nxt = q + SC_RING - 1

            @pl.when(nxt < nq)
            def _():
                start(nxt, (s + SC_RING - 1) % SC_RING)

            wait(q, s)
            compute(q, s)


def _peer_hidden_sc(xn, idx, u_tab):
    t, d = xn.shape
    nsel = idx.shape[1]
    nw = SC_CORES * SC_SUBCORES
    per_w = t // nw
    tb = min(SC_BATCH, per_w)
    nchunk = nsel // SC_LANES
    shift = nchunk.bit_length() - 1
    ncol = d // SC_LANES
    nq = tb * nchunk
    assert per_w * nw == t and per_w % tb == 0 and nchunk == 1 << shift and nq % SC_RING == 0
    mesh = plsc.VectorSubcoreMesh(core_axis_name="c", subcore_axis_name="s")

    def body(x_hbm, idx_hbm, u_hbm, out_hbm, idx_v, x_v, ubuf, hid_v, sem):
        wid = _sc_worker_id()
        lane = lax.iota(jnp.int32, SC_LANES)

        def gather(q, slot):
            tok = lax.shift_right_logical(q, shift)
            ch = q & (nchunk - 1)
            rows = idx_v.at[tok, pl.ds(ch * SC_LANES, SC_LANES)]
            return pltpu.make_async_copy(u_hbm.at[rows], ubuf.at[slot], sem.at[slot])

        def compute(q, slot):
            tok = lax.shift_right_logical(q, shift)
            ch = q & (nchunk - 1)

            @plsc.parallel_loop(0, ncol, carry=tuple(jnp.zeros((SC_LANES,), F32) for _ in range(SC_LANES)))
            def accs(c, acc):
                cs = pl.ds(pl.multiple_of(c * SC_LANES, SC_LANES), SC_LANES)
                xc = x_v[tok, cs]
                return tuple(a + xc * ubuf[slot, k, cs] for k, a in enumerate(acc))

            out = jnp.zeros((SC_LANES,), F32)
            for k in range(SC_LANES):
                out = jnp.where(lane == k, jnp.sum(accs[k]), out)
            hid_v[tok, pl.ds(ch * SC_LANES, SC_LANES)] = out

        @pl.loop(0, per_w // tb)
        def _(b):
            base = wid * per_w + b * tb
            pltpu.sync_copy(idx_hbm.at[pl.ds(base, tb)], idx_v)
            pltpu.sync_copy(x_hbm.at[pl.ds(base, tb)], x_v)
            _sc_ring(nq, lambda q, s: gather(q, s).start(), lambda q, s: gather(q, s).wait(), compute)
            pltpu.sync_copy(hid_v, out_hbm.at[pl.ds(base, tb)])

    return pl.kernel(
        body, out_type=jax.ShapeDtypeStruct((t, nsel), F32), mesh=mesh,
        scratch_types=[pltpu.VMEM((tb, nsel), jnp.int32), pltpu.VMEM((tb, d), F32),
                       pltpu.VMEM((SC_RING, SC_LANES, d), F32), pltpu.VMEM((tb, nsel), F32),
                       pltpu.SemaphoreType.DMA((SC_RING,))],
        compiler_params=pltpu.CompilerParams(needs_layout_passes=False), name="peer_hidden_sc",
    )(xn, idx, u_tab)


def _peer_mix_sc(act, idx, v_tab):
    t, nsel = act.shape
    d = v_tab.shape[1]
    nw = SC_CORES * SC_SUBCORES
    per_w = t // nw
    tb = min(SC_BATCH, per_w)
    nchunk = nsel // SC_LANES
    shift = nchunk.bit_length() - 1
    ncol = d // SC_LANES
    nq = tb * nchunk
    assert per_w * nw == t and per_w % tb == 0 and nchunk == 1 << shift and nq % SC_RING == 0
    mesh = plsc.VectorSubcoreMesh(core_axis_name="c", subcore_axis_name="s")

    def body(a_hbm, idx_hbm, v_hbm, out_hbm, idx_v, a_v, vbuf, o_v, sem):
        wid = _sc_worker_id()
        zero = jnp.zeros((SC_LANES,), F32)

        def gather(q, slot):
            tok = lax.shift_right_logical(q, shift)
            ch = q & (nchunk - 1)
            rows = idx_v.at[tok, pl.ds(ch * SC_LANES, SC_LANES)]
            return pltpu.make_async_copy(v_hbm.at[rows], vbuf.at[slot], sem.at[slot])

        def compute(q, slot):
            tok = lax.shift_right_logical(q, shift)
            ch = q & (nchunk - 1)
            tok_v = jnp.full((SC_LANES,), tok, jnp.int32)
            col_v = jnp.full((SC_LANES,), ch * SC_LANES, jnp.int32)
            w = [plsc.load_gather(a_v, [tok_v, col_v + k]) for k in range(SC_LANES)]

            @plsc.parallel_loop(0, ncol)
            def _(c):
                cs = pl.ds(pl.multiple_of(c * SC_LANES, SC_LANES), SC_LANES)
                terms = [w[k] * vbuf[slot, k, cs] for k in range(SC_LANES)]
                while len(terms) > 1:
                    terms = [a + b for a, b in zip(terms[0::2], terms[1::2])]
                o_v[tok, cs] = o_v[tok, cs] + terms[0]

        @pl.loop(0, per_w // tb)
        def _(b):
            base = wid * per_w + b * tb
            pltpu.sync_copy(idx_hbm.at[pl.ds(base, tb)], idx_v)
            pltpu.sync_copy(a_hbm.at[pl.ds(base, tb)], a_v)

            @pl.loop(0, tb)
            def _(r):
                @pl.loop(0, ncol)
                def _(c):
                    o_v[r, pl.ds(pl.multiple_of(c * SC_LANES, SC_LANES), SC_LANES)] = zero

            _sc_ring(nq, lambda q, s: gather(q, s).start(), lambda q, s: gather(q, s).wait(), compute)
            pltpu.sync_copy(o_v, out_hbm.at[pl.ds(base, tb)])

    return pl.kernel(
        body, out_type=jax.ShapeDtypeStruct((t, d), F32), mesh=mesh,
        scratch_types=[pltpu.VMEM((tb, nsel), jnp.int32), pltpu.VMEM((tb, nsel), F32),
                       pltpu.VMEM((SC_RING, SC_LANES, d), F32), pltpu.VMEM((tb, d), F32),
                       pltpu.SemaphoreType.DMA((SC_RING,))],
        compiler_params=pltpu.CompilerParams(needs_layout_passes=False), name="peer_mix_sc",
    )(act, idx, v_tab)


def _rope_tables(pos, half, group, width, lo):
    inv = ROPE_BASE ** (-jnp.arange(half, dtype=F32) / half)
    ang = pos.astype(F32)[:, None] * inv[None, :]
    cos, sin = jnp.cos(ang), jnp.sin(ang)
    n = pos.shape[0]
    reps = width // group
    pad_hi = group - lo - 2 * half
    blk = lambda a, b, fill: jnp.concatenate(
        [jnp.full((n, lo), fill, F32), a, b, jnp.full((n, pad_hi), fill, F32)], axis=1)
    z = jnp.zeros_like(sin)
    c = blk(cos, cos, 1.0)
    sa = blk(-sin, z, 0.0)
    sb = blk(z, sin, 0.0)
    return [jnp.tile(a, (1, reps)) for a in (c, sa, sb)]


def _ret_log_decay(nheads):
    return jnp.log(1.0 - jnp.exp2(-5.0 - jnp.arange(nheads, dtype=F32)))


def _pair_states(s):
    b, h, dk, dv = s.shape
    s = s.reshape(b, h // 2, 2, dk, dv)
    z = jnp.zeros_like(s[:, :, 0])
    top = jnp.concatenate([s[:, :, 0], z], axis=-1)
    bot = jnp.concatenate([z, s[:, :, 1]], axis=-1)
    return jnp.concatenate([top, bot], axis=-2)


def _unpair_states(sp, dk, dv):
    b, hp = sp.shape[:2]
    return jnp.stack([sp[:, :, :dk, :dv], sp[:, :, dk:, dv:]], axis=2).reshape(b, 2 * hp, dk, dv)


def _layer_weights(ln1_w, w_in, ret_gn_w, q_norm_w, w_uq, kv_norm_w, w_uk, w_uv, w_o, ln2_w,
                   peer_w_q, peer_sub_keys, dims):
    d = w_in.shape[0]
    nheads, nope, rope, vdim = dims["nheads"], dims["nope"], dims["mla_rope"], dims["vdim"]
    o6 = 4 * dims["rw"] + dims["qrank"] + dims["kvrank"]
    zc = lambda r, c: jnp.zeros((r, c), F32)
    win_p = jnp.concatenate([w_in[:, :o6], zc(d, nope), w_in[:, o6:], zc(d, LANES - nope - rope)], axis=1)
    qr, kr = w_uq.shape[0], w_uk.shape[0]
    wuq_p = jnp.concatenate([w_uq, jnp.zeros((qr, nheads, LANES - nope - rope), F32)], axis=2).reshape(qr, -1)
    wuk_p = jnp.concatenate([w_uk, jnp.zeros((kr, nheads, LANES - nope), F32)], axis=2).reshape(kr, -1)
    zv = jnp.zeros((kr, nheads // 2, LANES - vdim), F32)
    wv = w_uv.reshape(kr, nheads // 2, 2, vdim)
    wuv_p = jnp.concatenate([wv[:, :, 0], zv, zv, wv[:, :, 1]], axis=2).reshape(kr, -1)
    wuk3 = jnp.concatenate([jnp.transpose(w_uk, (1, 2, 0)),
                            jnp.zeros((nheads, LANES - nope, kr), F32)], axis=1)
    wuv3 = jnp.transpose(w_uv, (1, 0, 2))
    gidx = jnp.arange(dims["rw"]) // dims["ret_dv"]
    avg = (gidx[:, None] == gidx[None, :]).astype(F32) / dims["ret_dv"]
    keys = peer_sub_keys.reshape(-1, peer_sub_keys.shape[2], peer_sub_keys.shape[3])
    c = lambda a: a.astype(MXU_DTYPE)
    r2 = lambda a: a.reshape(1, -1)
    return dict(ln1=r2(ln1_w), win_p=c(win_p), gnw=r2(ret_gn_w), qnw=r2(q_norm_w), kvnw=r2(kv_norm_w),
                wuq_p=c(wuq_p), wuk_p=c(wuk_p), wuv_p=c(wuv_p), wuk3=c(wuk3), wuv3=c(wuv3), avg=c(avg),
                wo=c(w_o), ln2=r2(ln2_w), wq=c(peer_w_q), keys=c(keys))


class _Stream:
    def __init__(self, x, tabs, s0, *, nbatch, ret_rows, ret_chunk, tm, ranges, cache=None):
        self.x, self.tabs, self.s0, self.cache = x, tabs, s0, cache
        self.nbatch, self.ret_rows, self.ret_chunk, self.tm, self.ranges = nbatch, ret_rows, ret_chunk, tm, ranges
        self.seq = x.shape[0] // nbatch
        assert nbatch == 1 or ranges == [(0, self.seq)]
        self.pre = None
        self.outs = []


def _layer(streams, w, lg, u_tab, v_tab, lnf, dims, *, final_norm):
    units = [(st, lo, hi) for st in streams for lo, hi in st.ranges]
    n = len(units)
    nheads, dk = dims["nheads"], dims["ret_dk"]
    scale = (dims["nope"] + dims["mla_rope"]) ** -0.5
    built = [None] * n

    def build(i, order):
        st, lo, hi = units[i]
        if st.pre is None:
            proj = _inproj(order, st.x, st.tabs, w["ln1"], w["win_p"], w["qnw"], w["kvnw"], w["wuq_p"], w["wuk_p"],
                           w["wuv_p"], tm=st.tm, dims=dims)
            ret_o, s_pairs = _retention(lg, *proj[:3], _pair_states(st.s0), nbatch=st.nbatch, rows=st.ret_rows,
                                        chunk=st.ret_chunk, dk=dk)
            st.pre = list(proj) + [ret_o, s_pairs]
        qr, kr, vr, gate, qm, km, vm, ckv, kpe, ret_o, s_pairs = st.pre
        if st.cache is None:
            mla_o = _flash(order, qm, km, vm, nbatch=st.nbatch, tq=min(256, st.seq), lo=lo, hi=hi, scale=scale,
                           nheads=nheads, chunk=CHUNK)
        else:
            mla_o = _decode_attn(qm, st.cache[0], st.cache[1], ckv, kpe, w["wuk3"], w["wuv3"], nq=st.seq,
                                 nope=dims["nope"], rope=dims["mla_rope"], scale=scale)
        h, hn, idx, g = _mix_route(st.x, ret_o, gate, mla_o, w["gnw"], w["avg"], w["wo"], w["ln2"], w["wq"],
                                   w["keys"], tm=st.tm, row0=lo, pheads=dims["pheads"], nkeys=dims["nkeys"],
                                   topk=dims["topk"])
        built[i] = (h, idx, g, _peer_hidden_sc(hn, idx, u_tab))

    for i in range(min(2, n)):
        build(i, lnf)
    acts, peers = [], []
    for i in range(n):
        h, idx, g, hid = built[i]
        acts.append(_gelu_gate(built[i + 1][1] if i + 1 < n else lnf, hid, g, tm=units[i][0].tm))
        peers.append(_peer_mix_sc(acts[i], idx, v_tab))
        if i + 2 < n:
            build(i + 2, acts[i])
    for i, (st, lo, hi) in enumerate(units):
        st.outs.append(_residual(acts[min(i + 2, n - 1)], built[i][0], peers[i], lnf, tm=st.tm,
                                 final_norm=final_norm))
    nope, rope, dv = dims["nope"], dims["mla_rope"], dims["ret_dv"]
    results = []
    for st in streams:
        out = st.outs[0] if len(st.outs) == 1 else jnp.concatenate(st.outs, axis=0)
        results.append((out, st.pre[7], st.pre[8][:, nope:nope + rope], _unpair_states(st.pre[10], dk, dv)))
    return results


def kernel(x_prompt, x_sample, cache_mla_ckv, cache_mla_krope, state_retention, ln1_w, w_in, ret_gn_w,
           mla_q_norm_w, mla_w_uq, mla_kv_norm_w, mla_w_uk, mla_w_uv, w_o, ln2_w, peer_w_q, peer_sub_keys,
           peer_u, peer_v, lnf_w):
    depth = w_in.shape[0]
    nb, seq, d = x_prompt.shape
    db, dseq, _ = x_sample.shape
    past = cache_mla_ckv.shape[2]
    rheads, dk, dv = state_retention.shape[2:]
    nkeys = peer_sub_keys.shape[3]
    dims = dict(rw=rheads * dk, ret_dk=dk, ret_dv=dv, qrank=mla_w_uq.shape[1], kvrank=mla_w_uk.shape[1],
                nheads=mla_w_uq.shape[2], nope=mla_w_uk.shape[3], vdim=mla_w_uv.shape[3],
                mla_rope=mla_w_uq.shape[3] - mla_w_uk.shape[3], pheads=peer_sub_keys.shape[1], nkeys=nkeys,
                topk=PEER_TOPK)
    assert rheads * dk == rheads * dv and dims["nheads"] % 2 == 0 and dk * 2 == LANES and dims["vdim"] * 2 == LANES

    def tables(pos):
        return (_rope_tables(pos, dk // 2, dk, dims["rw"], 0)
                + _rope_tables(pos, dims["mla_rope"] // 2, LANES, LANES, dims["nope"]))

    tabs_p = tables(jnp.arange(seq))
    tabs_s = tables(jnp.tile(past + jnp.arange(dseq), db))
    lg = _ret_log_decay(rheads)
    lnf = lnf_w.reshape(1, -1)
    hp = x_prompt.reshape(nb * seq, d)
    hs = x_sample.reshape(db * dseq, d)
    outs = [[] for _ in range(6)]
    for l in range(depth):
        w = _layer_weights(ln1_w[l], w_in[l], ret_gn_w[l], mla_q_norm_w[l], mla_w_uq[l], mla_kv_norm_w[l],
                           mla_w_uk[l], mla_w_uv[l], w_o[l], ln2_w[l], peer_w_q[l], peer_sub_keys[l], dims)
        last = l == depth - 1
        gb = nb // PROMPT_GROUPS if nb % PROMPT_GROUPS == 0 else nb
        step = seq // PROMPT_HEAD_SPLIT
        split = gb == 1 and step > 0 and step % 256 == 0
        streams = [_Stream(hp[g * gb * seq:(g + 1) * gb * seq], tabs_p, jnp.zeros((gb, rheads, dk, dv), F32),
                           nbatch=gb, ret_rows=min(256, seq), ret_chunk=CHUNK, tm=min(256, gb * seq),
                           ranges=([(lo, lo + step) for lo in range(0, seq, step)] if split and g == 0
                                   else [(0, seq)]))
                   for g in range(nb // gb)]
        streams.append(_Stream(hs, tabs_s, state_retention[l], nbatch=db, ret_rows=dseq, ret_chunk=dseq,
                               tm=min(256, db * dseq), ranges=[(0, dseq)],
                               cache=(cache_mla_ckv[l], cache_mla_krope[l])))
        results = _layer(streams, w, lg, peer_u[l], peer_v[l], lnf, dims, final_norm=last)
        hp, c1, k1, s1 = (jnp.concatenate(p, axis=0) for p in zip(*results[:-1]))
        hs, c2, k2, s2 = results[-1]
        for acc, val in zip(outs, (c1.reshape(nb, seq, -1), k1.reshape(nb, seq, -1), s1,
                                   c2.reshape(db, dseq, -1), k2.reshape(db, dseq, -1), s2)):
            acc.append(val)
    return (hp.reshape(nb, seq, d), hs.reshape(db, dseq, d), *[jnp.stack(o) for o in outs])
```

```python
import functools

import jax
import jax.numpy as jnp
from jax import lax
from jax.experimental import pallas as pl
from jax.experimental.pallas import tpu as pltpu
from jax.experimental.pallas import tpu_sc as plsc

EPS = 1e-6
ROPE_BASE = 10000.0
CHUNK = 64
PEER_TOPK = 16
PROMPT_GROUPS = 8
PROMPT_HEAD_SPLIT = 4
LANES = 128
MXU_DTYPE = jnp.bfloat16
VMEM_LIMIT_BYTES = 56 * 1024 * 1024

F32 = jnp.float32
NEG_INF = float("-inf")


def _mm(a, b):
    return jnp.dot(a.astype(MXU_DTYPE), b.astype(MXU_DTYPE), preferred_element_type=F32)


def _mm_nt(a, b):
    return lax.dot_general(a.astype(MXU_DTYPE), b.astype(MXU_DTYPE),
                           (((1,), (1,)), ((), ())), preferred_element_type=F32)


def _mm_tn(a, b):
    return lax.dot_general(a.astype(MXU_DTYPE), b.astype(MXU_DTYPE),
                           (((0,), (0,)), ((), ())), preferred_element_type=F32)


def _rms(x, w):
    return x * lax.rsqrt(jnp.mean(x * x, axis=-1, keepdims=True) + EPS) * w


def _rope(t, c, sa, sb, half):
    n = t.shape[1]
    return t * c + pltpu.roll(t, n - half, 1) * sa + pltpu.roll(t, half, 1) * sb


def _params(*sem):
    return pltpu.CompilerParams(dimension_semantics=sem, vmem_limit_bytes=VMEM_LIMIT_BYTES)


ORDER_SPEC = pl.BlockSpec(memory_space=pl.ANY)


def _ordered(body):
    def run(_order_ref, *refs):
        body(*refs)
    return run


def _inproj_body(x_ref, ln1_ref, win_ref, cr_ref, sar_ref, sbr_ref, cm_ref, sam_ref, sbm_ref,
                 qnw_ref, kvnw_ref, wuq_ref, wuk_ref, wuv_ref,
                 qr_ref, kr_ref, vr_ref, gate_ref, qm_ref, km_ref, vm_ref, ckv_ref, kpe_ref,
                 *, rw, qrank, kvrank, ret_half, mla_half, k_scale, nheads):
    n1 = _rms(x_ref[...], ln1_ref[...])
    proj = _mm(n1, win_ref[...])
    cr, sar, sbr = cr_ref[...], sar_ref[...], sbr_ref[...]
    qr_ref[...] = _rope(proj[:, 0:rw], cr, sar, sbr, ret_half)
    kr_ref[...] = _rope(proj[:, rw:2 * rw], cr, sar, sbr, ret_half) * k_scale
    vr_ref[...] = proj[:, 2 * rw:3 * rw]
    gate_ref[...] = proj[:, 3 * rw:4 * rw]
    o4 = 4 * rw
    o5 = o4 + qrank
    o6 = o5 + kvrank
    cm, sam, sbm = cm_ref[...], sam_ref[...], sbm_ref[...]
    tile = lambda t: jnp.concatenate([t] * nheads, axis=1)
    cq = _rms(proj[:, o4:o5], qnw_ref[...])
    qm = _rope(_mm(cq, wuq_ref[...]), tile(cm), tile(sam), tile(sbm), mla_half)
    qm_ref[...] = qm.astype(qm_ref.dtype)
    ckv = _rms(proj[:, o5:o6], kvnw_ref[...])
    ckv_ref[...] = ckv
    kpe = _rope(proj[:, o6:o6 + LANES], cm, sam, sbm, mla_half)
    kpe_ref[...] = kpe
    km_ref[...] = (_mm(ckv, wuk_ref[...]) + tile(kpe)).astype(km_ref.dtype)
    vm_ref[...] = _mm(ckv, wuv_ref[...]).astype(vm_ref.dtype)


def _inproj(order, x, tabs, ln1, win_p, qnw, kvnw, wuq_p, wuk_p, wuv_p, *, tm, dims):
    t, d = x.shape
    rw, nheads = dims["rw"], dims["nheads"]
    hp = nheads * LANES
    nblk_tab = tabs[0].shape[0] // tm
    row = lambda i: (i, 0)
    tab = lambda i: (i % nblk_tab, 0)
    full = lambda i: (0, 0)
    fs = lambda a: pl.BlockSpec(a.shape, full)
    in_specs = [pl.BlockSpec((tm, d), row), fs(ln1), fs(win_p)]
    in_specs += [pl.BlockSpec((tm, rw), tab)] * 3 + [pl.BlockSpec((tm, LANES), tab)] * 3
    in_specs += [fs(qnw), fs(kvnw), fs(wuq_p), fs(wuk_p), fs(wuv_p)]
    out_shape = [jax.ShapeDtypeStruct((t, rw), F32)] * 4
    out_shape += [jax.ShapeDtypeStruct((t, hp), MXU_DTYPE)] * 3
    out_shape += [jax.ShapeDtypeStruct((t, dims["kvrank"]), F32), jax.ShapeDtypeStruct((t, LANES), F32)]
    out_specs = [pl.BlockSpec((tm, rw), row)] * 4 + [pl.BlockSpec((tm, hp), row)] * 3
    out_specs += [pl.BlockSpec((tm, dims["kvrank"]), row), pl.BlockSpec((tm, LANES), row)]
    body = functools.partial(
        _inproj_body, rw=rw, qrank=dims["qrank"], kvrank=dims["kvrank"], ret_half=dims["ret_dk"] // 2,
        mla_half=dims["mla_rope"] // 2, k_scale=dims["ret_dk"] ** -0.5, nheads=nheads)
    return pl.pallas_call(
        _ordered(body), grid=(t // tm,), in_specs=[ORDER_SPEC] + in_specs, out_specs=out_specs,
        out_shape=out_shape, compiler_params=_params("parallel"), name="inproj",
    )(order, x, ln1, win_p, *tabs, qnw, kvnw, wuq_p, wuk_p, wuv_p)


def _retention_body(lg_ref, q_ref, k_ref, v_ref, s0_ref, o_ref, sout_ref, s_scr, *, rows, chunk, dk):
    hp = pl.program_id(1)
    j = pl.program_id(2)

    @pl.when(j == 0)
    def _():
        s_scr[...] = s0_ref[0, 0]

    lane = lax.broadcasted_iota(jnp.int32, (1, LANES), 1)
    is_a = lane < dk
    lg_a = lg_ref[2 * hp]
    lg_b = lg_ref[2 * hp + 1]
    lgl = jnp.where(is_a, lg_a, lg_b)
    r = lax.broadcasted_iota(jnp.int32, (rows, 1), 0).astype(F32)
    q, k, v = q_ref[...], k_ref[...], v_ref[...]
    q_dec = q * jnp.exp(lgl * (r + 1.0))
    k_dec = k * jnp.exp(lgl * (float(rows) - 1.0 - r))
    ri = lax.broadcasted_iota(jnp.int32, (rows, rows), 0)
    ci = lax.broadcasted_iota(jnp.int32, (rows, rows), 1)
    dist = jnp.abs(ri - ci).astype(F32)
    visible = (ci // chunk) <= (ri // chunk)
    o = _mm(q_dec, s_scr[...])
    for first, lg in ((True, lg_a), (False, lg_b)):
        sel = is_a if first else jnp.logical_not(is_a)
        qh = jnp.where(sel, q, 0.0)
        vh = jnp.where(sel, v, 0.0)
        decay = jnp.where(visible, jnp.exp(lg * dist), 0.0)
        o = o + _mm(_mm_nt(qh, k) * decay, vh)
    o_ref[...] = o
    sr = lax.broadcasted_iota(jnp.int32, (LANES, LANES), 0) < dk
    sc = lax.broadcasted_iota(jnp.int32, (LANES, LANES), 1) < dk
    kv = jnp.where(sr == sc, _mm_tn(k_dec, v), 0.0)
    s_new = jnp.exp(lgl * float(rows)) * s_scr[...] + kv
    s_scr[...] = s_new

    @pl.when(j == pl.num_programs(2) - 1)
    def _():
        sout_ref[0, 0] = s_new


def _retention(lg, q, k, v, s0_pairs, *, nbatch, rows, chunk, dk):
    t, w = q.shape
    npairs = w // LANES
    nblk = t // (nbatch * rows)
    blk = pl.BlockSpec((rows, LANES), lambda b, p, j: (b * nblk + j, p))
    st = pl.BlockSpec((1, 1, LANES, LANES), lambda b, p, j: (b, p, 0, 0))
    return pl.pallas_call(
        functools.partial(_retention_body, rows=rows, chunk=chunk, dk=dk),
        grid=(nbatch, npairs, nblk),
        in_specs=[pl.BlockSpec(memory_space=pltpu.SMEM), blk, blk, blk, st],
        out_specs=[blk, st],
        out_shape=[jax.ShapeDtypeStruct((t, w), F32),
                   jax.ShapeDtypeStruct((nbatch, npairs, LANES, LANES), F32)],
        scratch_shapes=[pltpu.VMEM((LANES, LANES), F32)],
        compiler_params=_params("parallel", "parallel", "arbitrary"), name="retention",
    )(lg, q, k, v, s0_pairs)


def _flash_body(q_ref, k_ref, v_ref, o_ref, *, tq, tile0, scale, nheads, chunk):
    i = pl.program_id(1) + tile0
    ri = lax.broadcasted_iota(jnp.int32, (tq, tq), 0) // chunk
    ci = lax.broadcasted_iota(jnp.int32, (tq, tq), 1) // chunk
    visible = ci <= ri

    def head(h):
        cols = slice(h * LANES, (h + 1) * LANES)
        q = q_ref[:, cols]

        def step(j, carry, diagonal):
            m, l, acc = carry
            off = pl.multiple_of(j * tq, tq)
            s = _mm_nt(q, k_ref[pl.ds(off, tq), cols]) * scale
            if diagonal:
                s = jnp.where(visible, s, NEG_INF)
            m_new = jnp.maximum(m, jnp.max(s, axis=1, keepdims=True))
            alpha = jnp.exp(m - m_new)
            p = jnp.exp(s - m_new)
            l = alpha * l + jnp.sum(p, axis=1, keepdims=True)
            acc = alpha * acc + _mm(p, v_ref[pl.ds(off, tq), cols])
            return m_new, l, acc

        init = (jnp.full((tq, 1), NEG_INF, F32), jnp.zeros((tq, 1), F32), jnp.zeros((tq, LANES), F32))
        carry = lax.fori_loop(0, i, functools.partial(step, diagonal=False), init)
        _, l, acc = step(i, carry, True)
        return acc / l

    for p in range(nheads // 2):
        o_ref[:, p * LANES:(p + 1) * LANES] = head(2 * p) + head(2 * p + 1)


def _flash(order, qm, km, vm, *, nbatch, tq, lo, hi, scale, nheads, chunk):
    t, hp = qm.shape
    s = t // nbatch
    nq = s // tq
    tile0 = lo // tq
    nqr = (hi - lo) // tq
    ow = nheads // 2 * LANES
    return pl.pallas_call(
        _ordered(functools.partial(_flash_body, tq=tq, tile0=tile0, scale=scale, nheads=nheads, chunk=chunk)),
        grid=(nbatch, nqr),
        in_specs=[ORDER_SPEC,
                  pl.BlockSpec((tq, hp), lambda b, i: (b * nq + tile0 + i, 0)),
                  pl.BlockSpec((s, hp), lambda b, i: (b, 0)),
                  pl.BlockSpec((s, hp), lambda b, i: (b, 0))],
        out_specs=pl.BlockSpec((tq, ow), lambda b, i: (b * nqr + i, 0)),
        out_shape=jax.ShapeDtypeStruct((nbatch * (hi - lo), ow), F32),
        compiler_params=_params("parallel", "arbitrary"), name="flash_mla",
    )(order, qm, km, vm)


def _decode_attn_body(q_ref, cpast_ref, kpast_ref, cnew_ref, knew_ref, wuk_ref, wuv_ref, o_ref,
                      *, nheads, nope, rope, scale):
    c_past = cpast_ref[0]
    k_past = kpast_ref[0]
    c_new = cnew_ref[...]
    k_new = knew_ref[:, nope:nope + rope]
    outs = []
    for h in range(nheads):
        q = q_ref[:, h * LANES:(h + 1) * LANES]
        q_lat = _mm(q, wuk_ref[h])
        q_pe = q[:, nope:nope + rope]
        s_p = (_mm_nt(q_lat, c_past) + _mm_nt(q_pe, k_past)) * scale
        s_n = (_mm_nt(q_lat, c_new) + _mm_nt(q_pe, k_new)) * scale
        m = jnp.maximum(jnp.max(s_p, axis=1, keepdims=True), jnp.max(s_n, axis=1, keepdims=True))
        p_p = jnp.exp(s_p - m)
        p_n = jnp.exp(s_n - m)
        l = jnp.sum(p_p, axis=1, keepdims=True) + jnp.sum(p_n, axis=1, keepdims=True)
        o_lat = (_mm(p_p, c_past) + _mm(p_n, c_new)) / l
        outs.append(_mm(o_lat, wuv_ref[h]))
    o_ref[...] = jnp.concatenate(outs, axis=1)


def _decode_attn(qm, c_past, k_past, c_new, kpe_new, wuk3, wuv3, *, nq, nope, rope, scale):
    nb, past, kvr = c_past.shape
    nheads, _, vdim = wuv3.shape
    t, hp = qm.shape
    row = lambda b: (b, 0)
    full3 = lambda b: (0, 0, 0)
    return pl.pallas_call(
        functools.partial(_decode_attn_body, nheads=nheads, nope=nope, rope=rope, scale=scale),
        grid=(nb,),
        in_specs=[pl.BlockSpec((nq, hp), row),
                  pl.BlockSpec((1, past, kvr), lambda b: (b, 0, 0)),
                  pl.BlockSpec((1, past, rope), lambda b: (b, 0, 0)),
                  pl.BlockSpec((nq, kvr), row),
                  pl.BlockSpec((nq, LANES), row),
                  pl.BlockSpec(wuk3.shape, full3),
                  pl.BlockSpec(wuv3.shape, full3)],
        out_specs=pl.BlockSpec((nq, nheads * vdim), row),
        out_shape=jax.ShapeDtypeStruct((t, nheads * vdim), F32),
        compiler_params=_params("parallel"), name="decode_mla",
    )(qm, c_past, k_past, c_new, kpe_new, wuk3, wuv3)


def _split3(x):
    a = x.astype(MXU_DTYPE)
    r = x - a.astype(F32)
    b = r.astype(MXU_DTYPE)
    c = (r - b.astype(F32)).astype(MXU_DTYPE)
    return a, b, c


def _group_mean(x, avg):
    a, b, c = _split3(x)
    dot = lambda t: jnp.dot(t, avg, preferred_element_type=F32)
    return dot(a) + dot(b) + dot(c)


def _topk_rows(s, payload, kk):
    n = s.shape[0]
    rid = lax.broadcasted_iota(jnp.int32, s.shape, 0)
    vals, pays = [], []
    for _ in range(kk):
        mx = jnp.max(s, axis=0, keepdims=True)
        first = jnp.min(jnp.where(s == mx, rid, n), axis=0, keepdims=True)
        hit = rid == first
        vals.append(mx)
        pays.append(jnp.max(jnp.where(hit, payload, -1), axis=0, keepdims=True))
        s = jnp.where(hit, NEG_INF, s)
    return jnp.concatenate(vals, axis=0), jnp.concatenate(pays, axis=0)


def _mix_route_body(x_ref, ret_ref, gate_ref, mla_ref, gnw_ref, avg_ref, wo_ref, ln2_ref, wq_ref, keys_ref,
                    h_ref, hn_ref, idx_ref, g_ref, *, rw, pheads, nkeys, topk):
    ret = ret_ref[...]
    avg = avg_ref[...]
    mu = _group_mean(ret, avg)
    cen = ret - mu
    var = _group_mean(cen * cen, avg)
    gate = gate_ref[...]
    y = cen * lax.rsqrt(var + EPS) * gnw_ref[...] * (gate * jax.nn.sigmoid(gate))
    h = x_ref[...] + _mm(y, wo_ref[0:rw, :]) + _mm(mla_ref[...], wo_ref[rw:, :])
    h_ref[...] = h
    hn = _rms(h, ln2_ref[...])
    hn_ref[...] = hn
    qp = _mm(hn, wq_ref[...])
    kid = lax.broadcasted_iota(jnp.int32, (nkeys, qp.shape[0]), 0)
    idx_rows, g_rows = [], []
    for hd in range(pheads):
        ts, ti = [], []
        for half in range(2):
            c = (2 * hd + half) * LANES
            st = _mm_nt(keys_ref[2 * hd + half], qp[:, c:c + LANES])
            v, i = _topk_rows(st, kid, topk)
            ts.append(v)
            ti.append(i)
        cand = jnp.concatenate([ts[0][a:a + 1, :] + ts[1] for a in range(topk)], axis=0)
        cidx = jnp.concatenate([ti[0][a:a + 1, :] * nkeys + ti[1] for a in range(topk)], axis=0)
        best, expert = _topk_rows(cand, cidx, topk)
        e = jnp.exp(best - best[0:1, :])
        g_rows.append(e / jnp.sum(e, axis=0, keepdims=True))
        idx_rows.append(expert)
    idx_ref[...] = jnp.concatenate(idx_rows, axis=0).T
    g_ref[...] = jnp.concatenate(g_rows, axis=0).T


def _mix_route(x, ret_o, gate, mla_o, gnw, avg, wo, ln2, wq, keys, *, tm, row0, pheads, nkeys, topk):
    t = mla_o.shape[0]
    d = x.shape[1]
    rw = ret_o.shape[1]
    nsel = pheads * topk
    blk0 = row0 // tm
    row = lambda i: (i, 0)
    src = lambda i: (blk0 + i, 0)
    fs = lambda a: pl.BlockSpec(a.shape, lambda i: (0,) * a.ndim)
    return pl.pallas_call(
        functools.partial(_mix_route_body, rw=rw, pheads=pheads, nkeys=nkeys, topk=topk),
        grid=(t // tm,),
        in_specs=[pl.BlockSpec((tm, d), src), pl.BlockSpec((tm, rw), src), pl.BlockSpec((tm, rw), src),
                  pl.BlockSpec((tm, mla_o.shape[1]), row), fs(gnw), fs(avg), fs(wo), fs(ln2), fs(wq), fs(keys)],
        out_specs=[pl.BlockSpec((tm, d), row), pl.BlockSpec((tm, d), row),
                   pl.BlockSpec((tm, nsel), row), pl.BlockSpec((tm, nsel), row)],
        out_shape=[jax.ShapeDtypeStruct((t, d), F32), jax.ShapeDtypeStruct((t, d), F32),
                   jax.ShapeDtypeStruct((t, nsel), jnp.int32),
                   jax.ShapeDtypeStruct((t, nsel), F32)],
        compiler_params=_params("parallel"), name="mix_route",
    )(x, ret_o, gate, mla_o, gnw, avg, wo, ln2, wq, keys)


def _gelu_gate_body(hid_ref, g_ref, a_ref):
    hid = hid_ref[...]
    a_ref[...] = 0.5 * hid * (1.0 + lax.erf(hid * (2.0 ** -0.5))) * g_ref[...]


def _gelu_gate(order, hid, g, *, tm):
    t, n = hid.shape
    blk = pl.BlockSpec((tm, n), lambda i: (i, 0))
    return pl.pallas_call(
        _ordered(_gelu_gate_body), grid=(t // tm,), in_specs=[ORDER_SPEC, blk, blk], out_specs=blk,
        out_shape=jax.ShapeDtypeStruct((t, n), F32), compiler_params=_params("parallel"), name="gelu_gate",
    )(order, hid, g)


def _residual_body(h_ref, p_ref, lnf_ref, o_ref, *, final_norm):
    out = h_ref[...] + p_ref[...]
    if final_norm:
        out = _rms(out, lnf_ref[...])
    o_ref[...] = out


def _residual(order, h, peer, lnf, *, tm, final_norm):
    t, d = h.shape
    blk = pl.BlockSpec((tm, d), lambda i: (i, 0))
    return pl.pallas_call(
        _ordered(functools.partial(_residual_body, final_norm=final_norm)), grid=(t // tm,),
        in_specs=[ORDER_SPEC, blk, blk, pl.BlockSpec((1, d), lambda i: (0, 0))], out_specs=blk,
        out_shape=jax.ShapeDtypeStruct((t, d), F32), compiler_params=_params("parallel"), name="residual_norm",
    )(order, h, peer, lnf)


SC_CORES = 2
SC_SUBCORES = 16
SC_LANES = 16
SC_RING = 4
SC_BATCH = 32


def _sc_worker_id():
    return lax.axis_index("s") * SC_CORES + lax.axis_index("c")


def _sc_ring(nq, start, wait, compute):
    for s in range(SC_RING - 1):
        start(s, s)

    @pl.loop(0, nq, step=SC_RING)
    def _(q0):
        for s in range(SC_RING):
            q = q0 + s
            nxt = q + SC_RING - 1

            @pl.when(nxt < nq)
            def _():
                start(nxt, (s + SC_RING - 1) % SC_RING)

            wait(q, s)
            compute(q, s)


def _peer_hidden_sc(order, xn, idx, u_tab):
    t, d = xn.shape
    nsel = idx.shape[1]
    nw = SC_CORES * SC_SUBCORES
    per_w = t // nw
    tb = min(SC_BATCH, per_w)
    nchunk = nsel // SC_LANES
    shift = nchunk.bit_length() - 1
    ncol = d // SC_LANES
    nq = tb * nchunk
    assert per_w * nw == t and per_w % tb == 0 and nchunk == 1 << shift and nq % SC_RING == 0
    mesh = plsc.VectorSubcoreMesh(core_axis_name="c", subcore_axis_name="s")

    def body(_order_hbm, x_hbm, idx_hbm, u_hbm, out_hbm, idx_v, x_v, ubuf, hid_v, sem):
        wid = _sc_worker_id()
        lane = lax.iota(jnp.int32, SC_LANES)

        def gather(q, slot):
            tok = lax.shift_right_logical(q, shift)
            ch = q & (nchunk - 1)
            rows = idx_v.at[tok, pl.ds(ch * SC_LANES, SC_LANES)]
            return pltpu.make_async_copy(u_hbm.at[rows], ubuf.at[slot], sem.at[slot])

        def compute(q, slot):
            tok = lax.shift_right_logical(q, shift)
            ch = q & (nchunk - 1)

            @plsc.parallel_loop(0, ncol, carry=tuple(jnp.zeros((SC_LANES,), F32) for _ in range(SC_LANES)))
            def accs(c, acc):
                cs = pl.ds(pl.multiple_of(c * SC_LANES, SC_LANES), SC_LANES)
                xc = x_v[tok, cs]
                return tuple(a + xc * ubuf[slot, k, cs] for k, a in enumerate(acc))

            out = jnp.zeros((SC_LANES,), F32)
            for k in range(SC_LANES):
                out = jnp.where(lane == k, jnp.sum(accs[k]), out)
            hid_v[tok, pl.ds(ch * SC_LANES, SC_LANES)] = out

        @pl.loop(0, per_w // tb)
        def _(b):
            base = wid * per_w + b * tb
            pltpu.sync_copy(idx_hbm.at[pl.ds(base, tb)], idx_v)
            pltpu.sync_copy(x_hbm.at[pl.ds(base, tb)], x_v)
            _sc_ring(nq, lambda q, s: gather(q, s).start(), lambda q, s: gather(q, s).wait(), compute)
            pltpu.sync_copy(hid_v, out_hbm.at[pl.ds(base, tb)])

    return pl.kernel(
        body, out_type=jax.ShapeDtypeStruct((t, nsel), F32), mesh=mesh,
        scratch_types=[pltpu.VMEM((tb, nsel), jnp.int32), pltpu.VMEM((tb, d), F32),
                       pltpu.VMEM((SC_RING, SC_LANES, d), F32), pltpu.VMEM((tb, nsel), F32),
                       pltpu.SemaphoreType.DMA((SC_RING,))],
        compiler_params=pltpu.CompilerParams(needs_layout_passes=False), name="peer_hidden_sc",
    )(order, xn, idx, u_tab)


def _peer_mix_sc(act, idx, v_tab):
    t, nsel = act.shape
    d = v_tab.shape[1]
    nw = SC_CORES * SC_SUBCORES
    per_w = t // nw
    tb = min(SC_BATCH, per_w)
    nchunk = nsel // SC_LANES
    shift = nchunk.bit_length() - 1
    ncol = d // SC_LANES
    nq = tb * nchunk
    assert per_w * nw == t and per_w % tb == 0 and nchunk == 1 << shift and nq % SC_RING == 0
    mesh = plsc.VectorSubcoreMesh(core_axis_name="c", subcore_axis_name="s")

    def body(a_hbm, idx_hbm, v_hbm, out_hbm, idx_v, a_v, vbuf, o_v, sem):
        wid = _sc_worker_id()
        zero = jnp.zeros((SC_LANES,), F32)

        def gather(q, slot):
            tok = lax.shift_right_logical(q, shift)
            ch = q & (nchunk - 1)
            rows = idx_v.at[tok, pl.ds(ch * SC_LANES, SC_LANES)]
            return pltpu.make_async_copy(v_hbm.at[rows], vbuf.at[slot], sem.at[slot])

        def compute(q, slot):
            tok = lax.shift_right_logical(q, shift)
            ch = q & (nchunk - 1)
            tok_v = jnp.full((SC_LANES,), tok, jnp.int32)
            col_v = jnp.full((SC_LANES,), ch * SC_LANES, jnp.int32)
            w = [plsc.load_gather(a_v, [tok_v, col_v + k]) for k in range(SC_LANES)]

            @plsc.parallel_loop(0, ncol)
            def _(c):
                cs = pl.ds(pl.multiple_of(c * SC_LANES, SC_LANES), SC_LANES)
                terms = [w[k] * vbuf[slot, k, cs] for k in range(SC_LANES)]
                while len(terms) > 1:
                    terms = [a + b for a, b in zip(terms[0::2], terms[1::2])]
                o_v[tok, cs] = o_v[tok, cs] + terms[0]

        @pl.loop(0, per_w // tb)
        def _(b):
            base = wid * per_w + b * tb
            pltpu.sync_copy(idx_hbm.at[pl.ds(base, tb)], idx_v)
            pltpu.sync_copy(a_hbm.at[pl.ds(base, tb)], a_v)

            @pl.loop(0, tb)
            def _(r):
                @pl.loop(0, ncol)
                def _(c):
                    o_v[r, pl.ds(pl.multiple_of(c * SC_LANES, SC_LANES), SC_LANES)] = zero

            _sc_ring(nq, lambda q, s: gather(q, s).start(), lambda q, s: gather(q, s).wait(), compute)
            pltpu.sync_copy(o_v, out_hbm.at[pl.ds(base, tb)])

    return pl.kernel(
        body, out_type=jax.ShapeDtypeStruct((t, d), F32), mesh=mesh,
        scratch_types=[pltpu.VMEM((tb, nsel), jnp.int32), pltpu.VMEM((tb, nsel), F32),
                       pltpu.VMEM((SC_RING, SC_LANES, d), F32), pltpu.VMEM((tb, d), F32),
                       pltpu.SemaphoreType.DMA((SC_RING,))],
        compiler_params=pltpu.CompilerParams(needs_layout_passes=False), name="peer_mix_sc",
    )(act, idx, v_tab)


def _rope_tables(pos, half, group, width, lo):
    inv = ROPE_BASE ** (-jnp.arange(half, dtype=F32) / half)
    ang = pos.astype(F32)[:, None] * inv[None, :]
    cos, sin = jnp.cos(ang), jnp.sin(ang)
    n = pos.shape[0]
    reps = width // group
    pad_hi = group - lo - 2 * half
    blk = lambda a, b, fill: jnp.concatenate(
        [jnp.full((n, lo), fill, F32), a, b, jnp.full((n, pad_hi), fill, F32)], axis=1)
    z = jnp.zeros_like(sin)
    c = blk(cos, cos, 1.0)
    sa = blk(-sin, z, 0.0)
    sb = blk(z, sin, 0.0)
    return [jnp.tile(a, (1, reps)) for a in (c, sa, sb)]


def _ret_log_decay(nheads):
    return jnp.log(1.0 - jnp.exp2(-5.0 - jnp.arange(nheads, dtype=F32)))


def _pair_states(s):
    b, h, dk, dv = s.shape
    s = s.reshape(b, h // 2, 2, dk, dv)
    z = jnp.zeros_like(s[:, :, 0])
    top = jnp.concatenate([s[:, :, 0], z], axis=-1)
    bot = jnp.concatenate([z, s[:, :, 1]], axis=-1)
    return jnp.concatenate([top, bot], axis=-2)


def _unpair_states(sp, dk, dv):
    b, hp = sp.shape[:2]
    return jnp.stack([sp[:, :, :dk, :dv], sp[:, :, dk:, dv:]], axis=2).reshape(b, 2 * hp, dk, dv)


def _layer_weights(ln1_w, w_in, ret_gn_w, q_norm_w, w_uq, kv_norm_w, w_uk, w_uv, w_o, ln2_w,
                   peer_w_q, peer_sub_keys, dims):
    d = w_in.shape[0]
    nheads, nope, rope, vdim = dims["nheads"], dims["nope"], dims["mla_rope"], dims["vdim"]
    o6 = 4 * dims["rw"] + dims["qrank"] + dims["kvrank"]
    zc = lambda r, c: jnp.zeros((r, c), F32)
    win_p = jnp.concatenate([w_in[:, :o6], zc(d, nope), w_in[:, o6:], zc(d, LANES - nope - rope)], axis=1)
    qr, kr = w_uq.shape[0], w_uk.shape[0]
    wuq_p = jnp.concatenate([w_uq, jnp.zeros((qr, nheads, LANES - nope - rope), F32)], axis=2).reshape(qr, -1)
    wuk_p = jnp.concatenate([w_uk, jnp.zeros((kr, nheads, LANES - nope), F32)], axis=2).reshape(kr, -1)
    zv = jnp.zeros((kr, nheads // 2, LANES - vdim), F32)
    wv = w_uv.reshape(kr, nheads // 2, 2, vdim)
    wuv_p = jnp.concatenate([wv[:, :, 0], zv, zv, wv[:, :, 1]], axis=2).reshape(kr, -1)
    wuk3 = jnp.concatenate([jnp.transpose(w_uk, (1, 2, 0)),
                            jnp.zeros((nheads, LANES - nope, kr), F32)], axis=1)
    wuv3 = jnp.transpose(w_uv, (1, 0, 2))
    gidx = jnp.arange(dims["rw"]) // dims["ret_dv"]
    avg = (gidx[:, None] == gidx[None, :]).astype(F32) / dims["ret_dv"]
    keys = peer_sub_keys.reshape(-1, peer_sub_keys.shape[2], peer_sub_keys.shape[3])
    c = lambda a: a.astype(MXU_DTYPE)
    r2 = lambda a: a.reshape(1, -1)
    return dict(ln1=r2(ln1_w), win_p=c(win_p), gnw=r2(ret_gn_w), qnw=r2(q_norm_w), kvnw=r2(kv_norm_w),
                wuq_p=c(wuq_p), wuk_p=c(wuk_p), wuv_p=c(wuv_p), wuk3=c(wuk3), wuv3=c(wuv3), avg=c(avg),
                wo=c(w_o), ln2=r2(ln2_w), wq=c(peer_w_q), keys=c(keys))


class _Stream:
    def __init__(self, x, tabs, s0, *, nbatch, ret_rows, ret_chunk, tm, ranges, cache=None):
        self.x, self.tabs, self.s0, self.cache = x, tabs, s0, cache
        self.nbatch, self.ret_rows, self.ret_chunk, self.tm, self.ranges = nbatch, ret_rows, ret_chunk, tm, ranges
        self.seq = x.shape[0] // nbatch
        assert nbatch == 1 or ranges == [(0, self.seq)]
        self.pre = None
        self.outs = []


def _layer(streams, w, lg, u_tab, v_tab, lnf, dims, *, final_norm):
    units = [(st, lo, hi) for st in streams for lo, hi in st.ranges]
    n = len(units)
    nheads, dk = dims["nheads"], dims["ret_dk"]
    scale = (dims["nope"] + dims["mla_rope"]) ** -0.5
    built = [None] * n

    def build(i, order):
        st, lo, hi = units[i]
        if st.pre is None:
            proj = _inproj(order, st.x, st.tabs, w["ln1"], w["win_p"], w["qnw"], w["kvnw"], w["wuq_p"], w["wuk_p"],
                           w["wuv_p"], tm=st.tm, dims=dims)
            ret_o, s_pairs = _retention(lg, *proj[:3], _pair_states(st.s0), nbatch=st.nbatch, rows=st.ret_rows,
                                        chunk=st.ret_chunk, dk=dk)
            st.pre = list(proj) + [ret_o, s_pairs]
        qr, kr, vr, gate, qm, km, vm, ckv, kpe, ret_o, s_pairs = st.pre
        if st.cache is None:
            mla_o = _flash(order, qm, km, vm, nbatch=st.nbatch, tq=min(256, st.seq), lo=lo, hi=hi, scale=scale,
                           nheads=nheads, chunk=CHUNK)
        else:
            mla_o = _decode_attn(qm, st.cache[0], st.cache[1], ckv, kpe, w["wuk3"], w["wuv3"], nq=st.seq,
                                 nope=dims["nope"], rope=dims["mla_rope"], scale=scale)
        h, hn, idx, g = _mix_route(st.x, ret_o, gate, mla_o, w["gnw"], w["avg"], w["wo"], w["ln2"], w["wq"],
                                   w["keys"], tm=st.tm, row0=lo, pheads=dims["pheads"], nkeys=dims["nkeys"],
                                   topk=dims["topk"])
        built[i] = (h, idx, g, _peer_hidden_sc(peers[i - 2] if i >= 2 else lnf, hn, idx, u_tab))

    for i in range(min(2, n)):
        build(i, lnf)
    acts, peers = [], []
    for i in range(n):
        h, idx, g, hid = built[i]
        acts.append(_gelu_gate(built[i + 1][1] if i + 1 < n else lnf, hid, g, tm=units[i][0].tm))
        peers.append(_peer_mix_sc(acts[i], idx, v_tab))
        if i + 2 < n:
            build(i + 2, acts[i])
    for i, (st, lo, hi) in enumerate(units):
        st.outs.append(_residual(acts[min(i + 2, n - 1)], built[i][0], peers[i], lnf, tm=st.tm,
                                 final_norm=final_norm))
    nope, rope, dv = dims["nope"], dims["mla_rope"], dims["ret_dv"]
    results = []
    for st in streams:
        out = st.outs[0] if len(st.outs) == 1 else jnp.concatenate(st.outs, axis=0)
        results.append((out, st.pre[7], st.pre[8][:, nope:nope + rope], _unpair_states(st.pre[10], dk, dv)))
    return results


def kernel(x_prompt, x_sample, cache_mla_ckv, cache_mla_krope, state_retention, ln1_w, w_in, ret_gn_w,
           mla_q_norm_w, mla_w_uq, mla_kv_norm_w, mla_w_uk, mla_w_uv, w_o, ln2_w, peer_w_q, peer_sub_keys,
           peer_u, peer_v, lnf_w):
    depth = w_in.shape[0]
    nb, seq, d = x_prompt.shape
    db, dseq, _ = x_sample.shape
    past = cache_mla_ckv.shape[2]
    rheads, dk, dv = state_retention.shape[2:]
    nkeys = peer_sub_keys.shape[3]
    dims = dict(rw=rheads * dk, ret_dk=dk, ret_dv=dv, qrank=mla_w_uq.shape[1], kvrank=mla_w_uk.shape[1],
                nheads=mla_w_uq.shape[2], nope=mla_w_uk.shape[3], vdim=mla_w_uv.shape[3],
                mla_rope=mla_w_uq.shape[3] - mla_w_uk.shape[3], pheads=peer_sub_keys.shape[1], nkeys=nkeys,
                topk=PEER_TOPK)
    assert rheads * dk == rheads * dv and dims["nheads"] % 2 == 0 and dk * 2 == LANES and dims["vdim"] * 2 == LANES

    def tables(pos):
        return (_rope_tables(pos, dk // 2, dk, dims["rw"], 0)
                + _rope_tables(pos, dims["mla_rope"] // 2, LANES, LANES, dims["nope"]))

    tabs_p = tables(jnp.arange(seq))
    tabs_s = tables(jnp.tile(past + jnp.arange(dseq), db))
    lg = _ret_log_decay(rheads)
    lnf = lnf_w.reshape(1, -1)
    hp = x_prompt.reshape(nb * seq, d)
    hs = x_sample.reshape(db * dseq, d)
    outs = [[] for _ in range(6)]
    for l in range(depth):
        w = _layer_weights(ln1_w[l], w_in[l], ret_gn_w[l], mla_q_norm_w[l], mla_w_uq[l], mla_kv_norm_w[l],
                           mla_w_uk[l], mla_w_uv[l], w_o[l], ln2_w[l], peer_w_q[l], peer_sub_keys[l], dims)
        last = l == depth - 1
        gb = nb // PROMPT_GROUPS if nb % PROMPT_GROUPS == 0 else nb
        step = seq // PROMPT_HEAD_SPLIT
        split = gb == 1 and step > 0 and step % 256 == 0
        streams = [_Stream(hp[g * gb * seq:(g + 1) * gb * seq], tabs_p, jnp.zeros((gb, rheads, dk, dv), F32),
                           nbatch=gb, ret_rows=min(256, seq), ret_chunk=CHUNK, tm=min(256, gb * seq),
                           ranges=([(lo, lo + step) for lo in range(0, seq, step)] if split and g == 0
                                   else [(0, seq)]))
                   for g in range(nb // gb)]
        streams.append(_Stream(hs, tabs_s, state_retention[l], nbatch=db, ret_rows=dseq, ret_chunk=dseq,
                               tm=min(256, db * dseq), ranges=[(0, dseq)],
                               cache=(cache_mla_ckv[l], cache_mla_krope[l])))
        results = _layer(streams, w, lg, peer_u[l], peer_v[l], lnf, dims, final_norm=last)
        hp, c1, k1, s1 = (jnp.concatenate(p, axis=0) for p in zip(*results[:-1]))
        hs, c2, k2, s2 = results[-1]
        for acc, val in zip(outs, (c1.reshape(nb, seq, -1), k1.reshape(nb, seq, -1), s1,
                                   c2.reshape(db, dseq, -1), k2.reshape(db, dseq, -1), s2)):
            acc.append(val)
    return (hp.reshape(nb, seq, d), hs.reshape(db, dseq, d), *[jnp.stack(o) for o in outs])
```

```python
import functools

import jax
import jax.numpy as jnp
from jax import lax
from jax.experimental import pallas as pl
from jax.experimental.pallas import tpu as pltpu
from jax.experimental.pallas import tpu_sc as plsc

EPS = 1e-6
ROPE_BASE = 10000.0
CHUNK = 64
PEER_TOPK = 16
PROMPT_GROUPS = 8
PROMPT_HEAD_SPLIT = 4
PROMPT_SPLIT = 2
LANES = 128
MXU_DTYPE = jnp.bfloat16
VMEM_LIMIT_BYTES = 56 * 1024 * 1024

F32 = jnp.float32
NEG_INF = float("-inf")


def _mm(a, b):
    return jnp.dot(a.astype(MXU_DTYPE), b.astype(MXU_DTYPE), preferred_element_type=F32)


def _mm_nt(a, b):
    return lax.dot_general(a.astype(MXU_DTYPE), b.astype(MXU_DTYPE),
                           (((1,), (1,)), ((), ())), preferred_element_type=F32)


def _mm_tn(a, b):
    return lax.dot_general(a.astype(MXU_DTYPE), b.astype(MXU_DTYPE),
                           (((0,), (0,)), ((), ())), preferred_element_type=F32)


def _rms(x, w):
    return x * lax.rsqrt(jnp.mean(x * x, axis=-1, keepdims=True) + EPS) * w


def _rope(t, c, sa, sb, half):
    n = t.shape[1]
    return t * c + pltpu.roll(t, n - half, 1) * sa + pltpu.roll(t, half, 1) * sb


def _params(*sem):
    return pltpu.CompilerParams(dimension_semantics=sem, vmem_limit_bytes=VMEM_LIMIT_BYTES)


ORDER_SPEC = pl.BlockSpec(memory_space=pl.ANY)


def _ordered(body):
    def run(_order_ref, *refs):
        body(*refs)
    return run


def _inproj_body(x_ref, ln1_ref, win_ref, cr_ref, sar_ref, sbr_ref, cm_ref, sam_ref, sbm_ref,
                 qnw_ref, kvnw_ref, wuq_ref, wuk_ref, wuv_ref,
                 qr_ref, kr_ref, vr_ref, gate_ref, qm_ref, km_ref, vm_ref, ckv_ref, kpe_ref,
                 *, rw, qrank, kvrank, ret_half, mla_half, k_scale, nheads):
    n1 = _rms(x_ref[...], ln1_ref[...])
    proj = _mm(n1, win_ref[...])
    cr, sar, sbr = cr_ref[...], sar_ref[...], sbr_ref[...]
    qr_ref[...] = _rope(proj[:, 0:rw], cr, sar, sbr, ret_half)
    kr_ref[...] = _rope(proj[:, rw:2 * rw], cr, sar, sbr, ret_half) * k_scale
    vr_ref[...] = proj[:, 2 * rw:3 * rw]
    gate_ref[...] = proj[:, 3 * rw:4 * rw]
    o4 = 4 * rw
    o5 = o4 + qrank
    o6 = o5 + kvrank
    cm, sam, sbm = cm_ref[...], sam_ref[...], sbm_ref[...]
    tile = lambda t: jnp.concatenate([t] * nheads, axis=1)
    cq = _rms(proj[:, o4:o5], qnw_ref[...])
    qm = _rope(_mm(cq, wuq_ref[...]), tile(cm), tile(sam), tile(sbm), mla_half)
    qm_ref[...] = qm.astype(qm_ref.dtype)
    ckv = _rms(proj[:, o5:o6], kvnw_ref[...])
    ckv_ref[...] = ckv
    kpe = _rope(proj[:, o6:o6 + LANES], cm, sam, sbm, mla_half)
    kpe_ref[...] = kpe
    km_ref[...] = (_mm(ckv, wuk_ref[...]) + tile(kpe)).astype(km_ref.dtype)
    vm_ref[...] = _mm(ckv, wuv_ref[...]).astype(vm_ref.dtype)


def _inproj(order, x, tabs, ln1, win_p, qnw, kvnw, wuq_p, wuk_p, wuv_p, *, tm, dims):
    t, d = x.shape
    rw, nheads = dims["rw"], dims["nheads"]
    hp = nheads * LANES
    nblk_tab = tabs[0].shape[0] // tm
    row = lambda i: (i, 0)
    tab = lambda i: (i % nblk_tab, 0)
    full = lambda i: (0, 0)
    fs = lambda a: pl.BlockSpec(a.shape, full)
    in_specs = [pl.BlockSpec((tm, d), row), fs(ln1), fs(win_p)]
    in_specs += [pl.BlockSpec((tm, rw), tab)] * 3 + [pl.BlockSpec((tm, LANES), tab)] * 3
    in_specs += [fs(qnw), fs(kvnw), fs(wuq_p), fs(wuk_p), fs(wuv_p)]
    out_shape = [jax.ShapeDtypeStruct((t, rw), F32)] * 4
    out_shape += [jax.ShapeDtypeStruct((t, hp), MXU_DTYPE)] * 3
    out_shape += [jax.ShapeDtypeStruct((t, dims["kvrank"]), F32), jax.ShapeDtypeStruct((t, LANES), F32)]
    out_specs = [pl.BlockSpec((tm, rw), row)] * 4 + [pl.BlockSpec((tm, hp), row)] * 3
    out_specs += [pl.BlockSpec((tm, dims["kvrank"]), row), pl.BlockSpec((tm, LANES), row)]
    body = functools.partial(
        _inproj_body, rw=rw, qrank=dims["qrank"], kvrank=dims["kvrank"], ret_half=dims["ret_dk"] // 2,
        mla_half=dims["mla_rope"] // 2, k_scale=dims["ret_dk"] ** -0.5, nheads=nheads)
    return pl.pallas_call(
        _ordered(body), grid=(t // tm,), in_specs=[ORDER_SPEC] + in_specs, out_specs=out_specs,
        out_shape=out_shape, compiler_params=_params("parallel"), name="inproj",
    )(order, x, ln1, win_p, *tabs, qnw, kvnw, wuq_p, wuk_p, wuv_p)


def _retention_body(lg_ref, q_ref, k_ref, v_ref, s0_ref, o_ref, sout_ref, s_scr, *, rows, chunk, dk):
    hp = pl.program_id(1)
    j = pl.program_id(2)

    @pl.when(j == 0)
    def _():
        s_scr[...] = s0_ref[0, 0]

    lane = lax.broadcasted_iota(jnp.int32, (1, LANES), 1)
    is_a = lane < dk
    lg_a = lg_ref[2 * hp]
    lg_b = lg_ref[2 * hp + 1]
    lgl = jnp.where(is_a, lg_a, lg_b)
    r = lax.broadcasted_iota(jnp.int32, (rows, 1), 0).astype(F32)
    q, k, v = q_ref[...], k_ref[...], v_ref[...]
    q_dec = q * jnp.exp(lgl * (r + 1.0))
    k_dec = k * jnp.exp(lgl * (float(rows) - 1.0 - r))
    ri = lax.broadcasted_iota(jnp.int32, (rows, rows), 0)
    ci = lax.broadcasted_iota(jnp.int32, (rows, rows), 1)
    dist = jnp.abs(ri - ci).astype(F32)
    visible = (ci // chunk) <= (ri // chunk)
    o = _mm(q_dec, s_scr[...])
    for first, lg in ((True, lg_a), (False, lg_b)):
        sel = is_a if first else jnp.logical_not(is_a)
        qh = jnp.where(sel, q, 0.0)
        vh = jnp.where(sel, v, 0.0)
        decay = jnp.where(visible, jnp.exp(lg * dist), 0.0)
        o = o + _mm(_mm_nt(qh, k) * decay, vh)
    o_ref[...] = o
    sr = lax.broadcasted_iota(jnp.int32, (LANES, LANES), 0) < dk
    sc = lax.broadcasted_iota(jnp.int32, (LANES, LANES), 1) < dk
    kv = jnp.where(sr == sc, _mm_tn(k_dec, v), 0.0)
    s_new = jnp.exp(lgl * float(rows)) * s_scr[...] + kv
    s_scr[...] = s_new

    @pl.when(j == pl.num_programs(2) - 1)
    def _():
        sout_ref[0, 0] = s_new


def _retention(lg, q, k, v, s0_pairs, *, nbatch, rows, chunk, dk):
    t, w = q.shape
    npairs = w // LANES
    nblk = t // (nbatch * rows)
    blk = pl.BlockSpec((rows, LANES), lambda b, p, j: (b * nblk + j, p))
    st = pl.BlockSpec((1, 1, LANES, LANES), lambda b, p, j: (b, p, 0, 0))
    return pl.pallas_call(
        functools.partial(_retention_body, rows=rows, chunk=chunk, dk=dk),
        grid=(nbatch, npairs, nblk),
        in_specs=[pl.BlockSpec(memory_space=pltpu.SMEM), blk, blk, blk, st],
        out_specs=[blk, st],
        out_shape=[jax.ShapeDtypeStruct((t, w), F32),
                   jax.ShapeDtypeStruct((nbatch, npairs, LANES, LANES), F32)],
        scratch_shapes=[pltpu.VMEM((LANES, LANES), F32)],
        compiler_params=_params("parallel", "parallel", "arbitrary"), name="retention",
    )(lg, q, k, v, s0_pairs)


def _flash_body(q_ref, k_ref, v_ref, o_ref, *, tq, tile0, scale, nheads, chunk):
    i = pl.program_id(1) + tile0
    ri = lax.broadcasted_iota(jnp.int32, (tq, tq), 0) // chunk
    ci = lax.broadcasted_iota(jnp.int32, (tq, tq), 1) // chunk
    visible = ci <= ri

    def head(h):
        cols = slice(h * LANES, (h + 1) * LANES)
        q = q_ref[:, cols]

        def step(j, carry, diagonal):
            m, l, acc = carry
            off = pl.multiple_of(j * tq, tq)
            s = _mm_nt(q, k_ref[pl.ds(off, tq), cols]) * scale
            if diagonal:
                s = jnp.where(visible, s, NEG_INF)
            m_new = jnp.maximum(m, jnp.max(s, axis=1, keepdims=True))
            alpha = jnp.exp(m - m_new)
            p = jnp.exp(s - m_new)
            l = alpha * l + jnp.sum(p, axis=1, keepdims=True)
            acc = alpha * acc + _mm(p, v_ref[pl.ds(off, tq), cols])
            return m_new, l, acc

        init = (jnp.full((tq, 1), NEG_INF, F32), jnp.zeros((tq, 1), F32), jnp.zeros((tq, LANES), F32))
        carry = lax.fori_loop(0, i, functools.partial(step, diagonal=False), init)
        _, l, acc = step(i, carry, True)
        return acc / l

    for p in range(nheads // 2):
        o_ref[:, p * LANES:(p + 1) * LANES] = head(2 * p) + head(2 * p + 1)


def _flash(order, qm, km, vm, *, nbatch, tq, lo, hi, scale, nheads, chunk):
    t, hp = qm.shape
    s = t // nbatch
    nq = s // tq
    tile0 = lo // tq
    nqr = (hi - lo) // tq
    ow = nheads // 2 * LANES
    return pl.pallas_call(
        _ordered(functools.partial(_flash_body, tq=tq, tile0=tile0, scale=scale, nheads=nheads, chunk=chunk)),
        grid=(nbatch, nqr),
        in_specs=[ORDER_SPEC,
                  pl.BlockSpec((tq, hp), lambda b, i: (b * nq + tile0 + i, 0)),
                  pl.BlockSpec((s, hp), lambda b, i: (b, 0)),
                  pl.BlockSpec((s, hp), lambda b, i: (b, 0))],
        out_specs=pl.BlockSpec((tq, ow), lambda b, i: (b * nqr + i, 0)),
        out_shape=jax.ShapeDtypeStruct((nbatch * (hi - lo), ow), F32),
        compiler_params=_params("parallel", "arbitrary"), name="flash_mla",
    )(order, qm, km, vm)


def _decode_attn_body(q_ref, cpast_ref, kpast_ref, cnew_ref, knew_ref, wuk_ref, wuv_ref, o_ref,
                      *, nheads, nope, rope, scale):
    c_past = cpast_ref[0]
    k_past = kpast_ref[0]
    c_new = cnew_ref[...]
    k_new = knew_ref[:, nope:nope + rope]
    outs = []
    for h in range(nheads):
        q = q_ref[:, h * LANES:(h + 1) * LANES]
        q_lat = _mm(q, wuk_ref[h])
        q_pe = q[:, nope:nope + rope]
        s_p = (_mm_nt(q_lat, c_past) + _mm_nt(q_pe, k_past)) * scale
        s_n = (_mm_nt(q_lat, c_new) + _mm_nt(q_pe, k_new)) * scale
        m = jnp.maximum(jnp.max(s_p, axis=1, keepdims=True), jnp.max(s_n, axis=1, keepdims=True))
        p_p = jnp.exp(s_p - m)
        p_n = jnp.exp(s_n - m)
        l = jnp.sum(p_p, axis=1, keepdims=True) + jnp.sum(p_n, axis=1, keepdims=True)
        o_lat = (_mm(p_p, c_past) + _mm(p_n, c_new)) / l
        outs.append(_mm(o_lat, wuv_ref[h]))
    o_ref[...] = jnp.concatenate(outs, axis=1)


def _decode_attn(qm, c_past, k_past, c_new, kpe_new, wuk3, wuv3, *, nq, nope, rope, scale):
    nb, past, kvr = c_past.shape
    nheads, _, vdim = wuv3.shape
    t, hp = qm.shape
    row = lambda b: (b, 0)
    full3 = lambda b: (0, 0, 0)
    return pl.pallas_call(
        functools.partial(_decode_attn_body, nheads=nheads, nope=nope, rope=rope, scale=scale),
        grid=(nb,),
        in_specs=[pl.BlockSpec((nq, hp), row),
                  pl.BlockSpec((1, past, kvr), lambda b: (b, 0, 0)),
                  pl.BlockSpec((1, past, rope), lambda b: (b, 0, 0)),
                  pl.BlockSpec((nq, kvr), row),
                  pl.BlockSpec((nq, LANES), row),
                  pl.BlockSpec(wuk3.shape, full3),
                  pl.BlockSpec(wuv3.shape, full3)],
        out_specs=pl.BlockSpec((nq, nheads * vdim), row),
        out_shape=jax.ShapeDtypeStruct((t, nheads * vdim), F32),
        compiler_params=_params("parallel"), name="decode_mla",
    )(qm, c_past, k_past, c_new, kpe_new, wuk3, wuv3)


def _split3(x):
    a = x.astype(MXU_DTYPE)
    r = x - a.astype(F32)
    b = r.astype(MXU_DTYPE)
    c = (r - b.astype(F32)).astype(MXU_DTYPE)
    return a, b, c


def _group_mean(x, avg):
    a, b, c = _split3(x)
    dot = lambda t: jnp.dot(t, avg, preferred_element_type=F32)
    return dot(a) + dot(b) + dot(c)


def _topk_rows(s, payload, kk):
    n = s.shape[0]
    rid = lax.broadcasted_iota(jnp.int32, s.shape, 0)
    vals, pays = [], []
    for _ in range(kk):
        mx = jnp.max(s, axis=0, keepdims=True)
        first = jnp.min(jnp.where(s == mx, rid, n), axis=0, keepdims=True)
        hit = rid == first
        vals.append(mx)
        pays.append(jnp.max(jnp.where(hit, payload, -1), axis=0, keepdims=True))
        s = jnp.where(hit, NEG_INF, s)
    return jnp.concatenate(vals, axis=0), jnp.concatenate(pays, axis=0)


def _mix_route_body(x_ref, ret_ref, gate_ref, mla_ref, gnw_ref, avg_ref, wo_ref, ln2_ref, wq_ref, keys_ref,
                    h_ref, hn_ref, idx_ref, g_ref, *, rw, pheads, nkeys, topk):
    ret = ret_ref[...]
    avg = avg_ref[...]
    mu = _group_mean(ret, avg)
    cen = ret - mu
    var = _group_mean(cen * cen, avg)
    gate = gate_ref[...]
    y = cen * lax.rsqrt(var + EPS) * gnw_ref[...] * (gate * jax.nn.sigmoid(gate))
    h = x_ref[...] + _mm(y, wo_ref[0:rw, :]) + _mm(mla_ref[...], wo_ref[rw:, :])
    h_ref[...] = h
    hn = _rms(h, ln2_ref[...])
    hn_ref[...] = hn
    qp = _mm(hn, wq_ref[...])
    kid = lax.broadcasted_iota(jnp.int32, (nkeys, qp.shape[0]), 0)
    idx_rows, g_rows = [], []
    for hd in range(pheads):
        ts, ti = [], []
        for half in range(2):
            c = (2 * hd + half) * LANES
            st = _mm_nt(keys_ref[2 * hd + half], qp[:, c:c + LANES])
            v, i = _topk_rows(st, kid, topk)
            ts.append(v)
            ti.append(i)
        cand = jnp.concatenate([ts[0][a:a + 1, :] + ts[1] for a in range(topk)], axis=0)
        cidx = jnp.concatenate([ti[0][a:a + 1, :] * nkeys + ti[1] for a in range(topk)], axis=0)
        best, expert = _topk_rows(cand, cidx, topk)
        e = jnp.exp(best - best[0:1, :])
        g_rows.append(e / jnp.sum(e, axis=0, keepdims=True))
        idx_rows.append(expert)
    idx_ref[...] = jnp.concatenate(idx_rows, axis=0).T
    g_ref[...] = jnp.concatenate(g_rows, axis=0).T


def _mix_route(x, ret_o, gate, mla_o, gnw, avg, wo, ln2, wq, keys, *, tm, row0, pheads, nkeys, topk):
    t = mla_o.shape[0]
    d = x.shape[1]
    rw = ret_o.shape[1]
    nsel = pheads * topk
    blk0 = row0 // tm
    row = lambda i: (i, 0)
    src = lambda i: (blk0 + i, 0)
    fs = lambda a: pl.BlockSpec(a.shape, lambda i: (0,) * a.ndim)
    return pl.pallas_call(
        functools.partial(_mix_route_body, rw=rw, pheads=pheads, nkeys=nkeys, topk=topk),
        grid=(t // tm,),
        in_specs=[pl.BlockSpec((tm, d), src), pl.BlockSpec((tm, rw), src), pl.BlockSpec((tm, rw), src),
                  pl.BlockSpec((tm, mla_o.shape[1]), row), fs(gnw), fs(avg), fs(wo), fs(ln2), fs(wq), fs(keys)],
        out_specs=[pl.BlockSpec((tm, d), row), pl.BlockSpec((tm, d), row),
                   pl.BlockSpec((tm, nsel), row), pl.BlockSpec((tm, nsel), row)],
        out_shape=[jax.ShapeDtypeStruct((t, d), F32), jax.ShapeDtypeStruct((t, d), F32),
                   jax.ShapeDtypeStruct((t, nsel), jnp.int32),
                   jax.ShapeDtypeStruct((t, nsel), F32)],
        compiler_params=_params("parallel"), name="mix_route",
    )(x, ret_o, gate, mla_o, gnw, avg, wo, ln2, wq, keys)


def _gelu_gate_body(hid_ref, g_ref, a_ref):
    hid = hid_ref[...]
    a_ref[...] = 0.5 * hid * (1.0 + lax.erf(hid * (2.0 ** -0.5))) * g_ref[...]


def _gelu_gate(order, hid, g, *, tm):
    t, n = hid.shape
    blk = pl.BlockSpec((tm, n), lambda i: (i, 0))
    return pl.pallas_call(
        _ordered(_gelu_gate_body), grid=(t // tm,), in_specs=[ORDER_SPEC, blk, blk], out_specs=blk,
        out_shape=jax.ShapeDtypeStruct((t, n), F32), compiler_params=_params("parallel"), name="gelu_gate",
    )(order, hid, g)


def _residual_body(h_ref, p_ref, lnf_ref, o_ref, *, final_norm):
    out = h_ref[...] + p_ref[...]
    if final_norm:
        out = _rms(out, lnf_ref[...])
    o_ref[...] = out


def _residual(order, h, peer, lnf, *, tm, final_norm):
    t, d = h.shape
    blk = pl.BlockSpec((tm, d), lambda i: (i, 0))
    return pl.pallas_call(
        _ordered(functools.partial(_residual_body, final_norm=final_norm)), grid=(t // tm,),
        in_specs=[ORDER_SPEC, blk, blk, pl.BlockSpec((1, d), lambda i: (0, 0))], out_specs=blk,
        out_shape=jax.ShapeDtypeStruct((t, d), F32), compiler_params=_params("parallel"), name="residual_norm",
    )(order, h, peer, lnf)


SC_CORES = 2
SC_SUBCORES = 16
SC_LANES = 16
SC_RING = 4
SC_BATCH = 32


def _sc_worker_id():
    return lax.axis_index("s") * SC_CORES + lax.axis_index("c")


def _sc_ring(nq, start, wait, compute):
    for s in range(SC_RING - 1):
        start(s, s)

    @pl.loop(0, nq, step=SC_RING)
    def _(q0):
        for s in range(SC_RING):
            q = q0 + s
            nxt = q + SC_RING - 1

            @pl.when(nxt < nq)
            def _():
                start(nxt, (s + SC_RING - 1) % SC_RING)

            wait(q, s)
            compute(q, s)


def _peer_hidden_sc(order, xn, idx, u_tab):
    t, d = xn.shape
    nsel = idx.shape[1]
    nw = SC_CORES * SC_SUBCORES
    per_w = t // nw
    tb = min(SC_BATCH, per_w)
    nchunk = nsel // SC_LANES
    shift = nchunk.bit_length() - 1
    ncol = d // SC_LANES
    nq = tb * nchunk
    assert per_w * nw == t and per_w % tb == 0 and nchunk == 1 << shift and nq % SC_RING == 0
    mesh = plsc.VectorSubcoreMesh(core_axis_name="c", subcore_axis_name="s")

    def body(_order_hbm, x_hbm, idx_hbm, u_hbm, out_hbm, idx_v, x_v, ubuf, hid_v, sem):
        wid = _sc_worker_id()
        lane = lax.iota(jnp.int32, SC_LANES)

        def gather(q, slot):
            tok = lax.shift_right_logical(q, shift)
            ch = q & (nchunk - 1)
            rows = idx_v.at[tok, pl.ds(ch * SC_LANES, SC_LANES)]
            return pltpu.make_async_copy(u_hbm.at[rows], ubuf.at[slot], sem.at[slot])

        def compute(q, slot):
            tok = lax.shift_right_logical(q, shift)
            ch = q & (nchunk - 1)

            @plsc.parallel_loop(0, ncol, carry=tuple(jnp.zeros((SC_LANES,), F32) for _ in range(SC_LANES)))
            def accs(c, acc):
                cs = pl.ds(pl.multiple_of(c * SC_LANES, SC_LANES), SC_LANES)
                xc = x_v[tok, cs]
                return tuple(a + xc * ubuf[slot, k, cs] for k, a in enumerate(acc))

            out = jnp.zeros((SC_LANES,), F32)
            for k in range(SC_LANES):
                out = jnp.where(lane == k, jnp.sum(accs[k]), out)
            hid_v[tok, pl.ds(ch * SC_LANES, SC_LANES)] = out

        @pl.loop(0, per_w // tb)
        def _(b):
            base = wid * per_w + b * tb
            pltpu.sync_copy(idx_hbm.at[pl.ds(base, tb)], idx_v)
            pltpu.sync_copy(x_hbm.at[pl.ds(base, tb)], x_v)
            _sc_ring(nq, lambda q, s: gather(q, s).start(), lambda q, s: gather(q, s).wait(), compute)
            pltpu.sync_copy(hid_v, out_hbm.at[pl.ds(base, tb)])

    return pl.kernel(
        body, out_type=jax.ShapeDtypeStruct((t, nsel), F32), mesh=mesh,
        scratch_types=[pltpu.VMEM((tb, nsel), jnp.int32), pltpu.VMEM((tb, d), F32),
                       pltpu.VMEM((SC_RING, SC_LANES, d), F32), pltpu.VMEM((tb, nsel), F32),
                       pltpu.SemaphoreType.DMA((SC_RING,))],
        compiler_params=pltpu.CompilerParams(needs_layout_passes=False), name="peer_hidden_sc",
    )(order, xn, idx, u_tab)


def _peer_mix_sc(act, idx, v_tab):
    t, nsel = act.shape
    d = v_tab.shape[1]
    nw = SC_CORES * SC_SUBCORES
    per_w = t // nw
    tb = min(SC_BATCH, per_w)
    nchunk = nsel // SC_LANES
    shift = nchunk.bit_length() - 1
    ncol = d // SC_LANES
    nq = tb * nchunk
    assert per_w * nw == t and per_w % tb == 0 and nchunk == 1 << shift and nq % SC_RING == 0
    mesh = plsc.VectorSubcoreMesh(core_axis_name="c", subcore_axis_name="s")

    def body(a_hbm, idx_hbm, v_hbm, out_hbm, idx_v, a_v, vbuf, o_v, sem):
        wid = _sc_worker_id()
        zero = jnp.zeros((SC_LANES,), F32)

        def gather(q, slot):
            tok = lax.shift_right_logical(q, shift)
            ch = q & (nchunk - 1)
            rows = idx_v.at[tok, pl.ds(ch * SC_LANES, SC_LANES)]
            return pltpu.make_async_copy(v_hbm.at[rows], vbuf.at[slot], sem.at[slot])

        def compute(q, slot):
            tok = lax.shift_right_logical(q, shift)
            ch = q & (nchunk - 1)
            tok_v = jnp.full((SC_LANES,), tok, jnp.int32)
            col_v = jnp.full((SC_LANES,), ch * SC_LANES, jnp.int32)
            w = [plsc.load_gather(a_v, [tok_v, col_v + k]) for k in range(SC_LANES)]

            @plsc.parallel_loop(0, ncol)
            def _(c):
                cs = pl.ds(pl.multiple_of(c * SC_LANES, SC_LANES), SC_LANES)
                terms = [w[k] * vbuf[slot, k, cs] for k in range(SC_LANES)]
                while len(terms) > 1:
                    terms = [a + b for a, b in zip(terms[0::2], terms[1::2])]
                o_v[tok, cs] = o_v[tok, cs] + terms[0]

        @pl.loop(0, per_w // tb)
        def _(b):
            base = wid * per_w + b * tb
            pltpu.sync_copy(idx_hbm.at[pl.ds(base, tb)], idx_v)
            pltpu.sync_copy(a_hbm.at[pl.ds(base, tb)], a_v)

            @pl.loop(0, tb)
            def _(r):
                @pl.loop(0, ncol)
                def _(c):
                    o_v[r, pl.ds(pl.multiple_of(c * SC_LANES, SC_LANES), SC_LANES)] = zero

            _sc_ring(nq, lambda q, s: gather(q, s).start(), lambda q, s: gather(q, s).wait(), compute)
            pltpu.sync_copy(o_v, out_hbm.at[pl.ds(base, tb)])

    return pl.kernel(
        body, out_type=jax.ShapeDtypeStruct((t, d), F32), mesh=mesh,
        scratch_types=[pltpu.VMEM((tb, nsel), jnp.int32), pltpu.VMEM((tb, nsel), F32),
                       pltpu.VMEM((SC_RING, SC_LANES, d), F32), pltpu.VMEM((tb, d), F32),
                       pltpu.SemaphoreType.DMA((SC_RING,))],
        compiler_params=pltpu.CompilerParams(needs_layout_passes=False), name="peer_mix_sc",
    )(act, idx, v_tab)


def _rope_tables(pos, half, group, width, lo):
    inv = ROPE_BASE ** (-jnp.arange(half, dtype=F32) / half)
    ang = pos.astype(F32)[:, None] * inv[None, :]
    cos, sin = jnp.cos(ang), jnp.sin(ang)
    n = pos.shape[0]
    reps = width // group
    pad_hi = group - lo - 2 * half
    blk = lambda a, b, fill: jnp.concatenate(
        [jnp.full((n, lo), fill, F32), a, b, jnp.full((n, pad_hi), fill, F32)], axis=1)
    z = jnp.zeros_like(sin)
    c = blk(cos, cos, 1.0)
    sa = blk(-sin, z, 0.0)
    sb = blk(z, sin, 0.0)
    return [jnp.tile(a, (1, reps)) for a in (c, sa, sb)]


def _ret_log_decay(nheads):
    return jnp.log(1.0 - jnp.exp2(-5.0 - jnp.arange(nheads, dtype=F32)))


def _pair_states(s):
    b, h, dk, dv = s.shape
    s = s.reshape(b, h // 2, 2, dk, dv)
    z = jnp.zeros_like(s[:, :, 0])
    top = jnp.concatenate([s[:, :, 0], z], axis=-1)
    bot = jnp.concatenate([z, s[:, :, 1]], axis=-1)
    return jnp.concatenate([top, bot], axis=-2)


def _unpair_states(sp, dk, dv):
    b, hp = sp.shape[:2]
    return jnp.stack([sp[:, :, :dk, :dv], sp[:, :, dk:, dv:]], axis=2).reshape(b, 2 * hp, dk, dv)


def _layer_weights(ln1_w, w_in, ret_gn_w, q_norm_w, w_uq, kv_norm_w, w_uk, w_uv, w_o, ln2_w,
                   peer_w_q, peer_sub_keys, dims):
    d = w_in.shape[0]
    nheads, nope, rope, vdim = dims["nheads"], dims["nope"], dims["mla_rope"], dims["vdim"]
    o6 = 4 * dims["rw"] + dims["qrank"] + dims["kvrank"]
    zc = lambda r, c: jnp.zeros((r, c), F32)
    win_p = jnp.concatenate([w_in[:, :o6], zc(d, nope), w_in[:, o6:], zc(d, LANES - nope - rope)], axis=1)
    qr, kr = w_uq.shape[0], w_uk.shape[0]
    wuq_p = jnp.concatenate([w_uq, jnp.zeros((qr, nheads, LANES - nope - rope), F32)], axis=2).reshape(qr, -1)
    wuk_p = jnp.concatenate([w_uk, jnp.zeros((kr, nheads, LANES - nope), F32)], axis=2).reshape(kr, -1)
    zv = jnp.zeros((kr, nheads // 2, LANES - vdim), F32)
    wv = w_uv.reshape(kr, nheads // 2, 2, vdim)
    wuv_p = jnp.concatenate([wv[:, :, 0], zv, zv, wv[:, :, 1]], axis=2).reshape(kr, -1)
    wuk3 = jnp.concatenate([jnp.transpose(w_uk, (1, 2, 0)),
                            jnp.zeros((nheads, LANES - nope, kr), F32)], axis=1)
    wuv3 = jnp.transpose(w_uv, (1, 0, 2))
    gidx = jnp.arange(dims["rw"]) // dims["ret_dv"]
    avg = (gidx[:, None] == gidx[None, :]).astype(F32) / dims["ret_dv"]
    keys = peer_sub_keys.reshape(-1, peer_sub_keys.shape[2], peer_sub_keys.shape[3])
    c = lambda a: a.astype(MXU_DTYPE)
    r2 = lambda a: a.reshape(1, -1)
    return dict(ln1=r2(ln1_w), win_p=c(win_p), gnw=r2(ret_gn_w), qnw=r2(q_norm_w), kvnw=r2(kv_norm_w),
                wuq_p=c(wuq_p), wuk_p=c(wuk_p), wuv_p=c(wuv_p), wuk3=c(wuk3), wuv3=c(wuv3), avg=c(avg),
                wo=c(w_o), ln2=r2(ln2_w), wq=c(peer_w_q), keys=c(keys))


class _Stream:
    def __init__(self, x, tabs, s0, *, nbatch, ret_rows, ret_chunk, tm, ranges, cache=None):
        self.x, self.tabs, self.s0, self.cache = x, tabs, s0, cache
        self.nbatch, self.ret_rows, self.ret_chunk, self.tm, self.ranges = nbatch, ret_rows, ret_chunk, tm, ranges
        self.seq = x.shape[0] // nbatch
        assert nbatch == 1 or ranges == [(0, self.seq)]
        self.pre = None
        self.outs = []


def _layer(streams, w, lg, u_tab, v_tab, lnf, dims, *, final_norm):
    units = [(st, lo, hi) for st in streams for lo, hi in st.ranges]
    n = len(units)
    nheads, dk = dims["nheads"], dims["ret_dk"]
    scale = (dims["nope"] + dims["mla_rope"]) ** -0.5
    built = [None] * n

    def build(i, order):
        st, lo, hi = units[i]
        if st.pre is None:
            proj = _inproj(order, st.x, st.tabs, w["ln1"], w["win_p"], w["qnw"], w["kvnw"], w["wuq_p"], w["wuk_p"],
                           w["wuv_p"], tm=st.tm, dims=dims)
            ret_o, s_pairs = _retention(lg, *proj[:3], _pair_states(st.s0), nbatch=st.nbatch, rows=st.ret_rows,
                                        chunk=st.ret_chunk, dk=dk)
            st.pre = list(proj) + [ret_o, s_pairs]
        qr, kr, vr, gate, qm, km, vm, ckv, kpe, ret_o, s_pairs = st.pre
        if st.cache is None:
            mla_o = _flash(order, qm, km, vm, nbatch=st.nbatch, tq=min(256, st.seq), lo=lo, hi=hi, scale=scale,
                           nheads=nheads, chunk=CHUNK)
        else:
            mla_o = _decode_attn(qm, st.cache[0], st.cache[1], ckv, kpe, w["wuk3"], w["wuv3"], nq=st.seq,
                                 nope=dims["nope"], rope=dims["mla_rope"], scale=scale)
        h, hn, idx, g = _mix_route(st.x, ret_o, gate, mla_o, w["gnw"], w["avg"], w["wo"], w["ln2"], w["wq"],
                                   w["keys"], tm=st.tm, row0=lo, pheads=dims["pheads"], nkeys=dims["nkeys"],
                                   topk=dims["topk"])
        built[i] = (h, idx, g, _peer_hidden_sc(peers[i - 2] if i >= 2 else lnf, hn, idx, u_tab))

    for i in range(min(2, n)):
        build(i, lnf)
    acts, peers = [], []
    for i in range(n):
        h, idx, g, hid = built[i]
        acts.append(_gelu_gate(built[i + 1][1] if i + 1 < n else lnf, hid, g, tm=units[i][0].tm))
        peers.append(_peer_mix_sc(acts[i], idx, v_tab))
        if i + 2 < n:
            build(i + 2, acts[i])
    for i, (st, lo, hi) in enumerate(units):
        st.outs.append(_residual(acts[min(i + 2, n - 1)], built[i][0], peers[i], lnf, tm=st.tm,
                                 final_norm=final_norm))
    nope, rope, dv = dims["nope"], dims["mla_rope"], dims["ret_dv"]
    results = []
    for st in streams:
        out = st.outs[0] if len(st.outs) == 1 else jnp.concatenate(st.outs, axis=0)
        results.append((out, st.pre[7], st.pre[8][:, nope:nope + rope], _unpair_states(st.pre[10], dk, dv)))
    return results


def kernel(x_prompt, x_sample, cache_mla_ckv, cache_mla_krope, state_retention, ln1_w, w_in, ret_gn_w,
           mla_q_norm_w, mla_w_uq, mla_kv_norm_w, mla_w_uk, mla_w_uv, w_o, ln2_w, peer_w_q, peer_sub_keys,
           peer_u, peer_v, lnf_w):
    depth = w_in.shape[0]
    nb, seq, d = x_prompt.shape
    db, dseq, _ = x_sample.shape
    past = cache_mla_ckv.shape[2]
    rheads, dk, dv = state_retention.shape[2:]
    nkeys = peer_sub_keys.shape[3]
    dims = dict(rw=rheads * dk, ret_dk=dk, ret_dv=dv, qrank=mla_w_uq.shape[1], kvrank=mla_w_uk.shape[1],
                nheads=mla_w_uq.shape[2], nope=mla_w_uk.shape[3], vdim=mla_w_uv.shape[3],
                mla_rope=mla_w_uq.shape[3] - mla_w_uk.shape[3], pheads=peer_sub_keys.shape[1], nkeys=nkeys,
                topk=PEER_TOPK)
    assert rheads * dk == rheads * dv and dims["nheads"] % 2 == 0 and dk * 2 == LANES and dims["vdim"] * 2 == LANES

    def tables(pos):
        return (_rope_tables(pos, dk // 2, dk, dims["rw"], 0)
                + _rope_tables(pos, dims["mla_rope"] // 2, LANES, LANES, dims["nope"]))

    tabs_p = tables(jnp.arange(seq))
    tabs_s = tables(jnp.tile(past + jnp.arange(dseq), db))
    lg = _ret_log_decay(rheads)
    lnf = lnf_w.reshape(1, -1)
    hp = x_prompt.reshape(nb * seq, d)
    hs = x_sample.reshape(db * dseq, d)
    outs = [[] for _ in range(6)]
    for l in range(depth):
        w = _layer_weights(ln1_w[l], w_in[l], ret_gn_w[l], mla_q_norm_w[l], mla_w_uq[l], mla_kv_norm_w[l],
                           mla_w_uk[l], mla_w_uv[l], w_o[l], ln2_w[l], peer_w_q[l], peer_sub_keys[l], dims)
        last = l == depth - 1
        gb = nb // PROMPT_GROUPS if nb % PROMPT_GROUPS == 0 else nb

        def frame_ranges(g):
            step = seq // (PROMPT_HEAD_SPLIT if g == 0 else PROMPT_SPLIT)
            ok = gb == 1 and step > 0 and step % 256 == 0
            return [(lo, lo + step) for lo in range(0, seq, step)] if ok else [(0, seq)]

        streams = [_Stream(hp[g * gb * seq:(g + 1) * gb * seq], tabs_p, jnp.zeros((gb, rheads, dk, dv), F32),
                           nbatch=gb, ret_rows=min(256, seq), ret_chunk=CHUNK, tm=min(256, gb * seq),
                           ranges=frame_ranges(g))
                   for g in range(nb // gb)]
        streams.append(_Stream(hs, tabs_s, state_retention[l], nbatch=db, ret_rows=dseq, ret_chunk=dseq,
                               tm=min(256, db * dseq), ranges=[(0, dseq)],
                               cache=(cache_mla_ckv[l], cache_mla_krope[l])))
        results = _layer(streams, w, lg, peer_u[l], peer_v[l], lnf, dims, final_norm=last)
        hp, c1, k1, s1 = (jnp.concatenate(p, axis=0) for p in zip(*results[:-1]))
        hs, c2, k2, s2 = results[-1]
        for acc, val in zip(outs, (c1.reshape(nb, seq, -1), k1.reshape(nb, seq, -1), s1,
                                   c2.reshape(db, dseq, -1), k2.reshape(db, dseq, -1), s2)):
            acc.append(val)
    return (hp.reshape(nb, seq, d), hs.reshape(db, dseq, d), *[jnp.stack(o) for o in outs])
```

```python
import functools

import jax
import jax.numpy as jnp
from jax import lax
from jax.experimental import pallas as pl
from jax.experimental.pallas import tpu as pltpu
from jax.experimental.pallas import tpu_sc as plsc

EPS = 1e-6
ROPE_BASE = 10000.0
CHUNK = 64
PEER_TOPK = 16
PROMPT_GROUPS = 8
PROMPT_HEAD_SPLIT = 4
PROMPT_SPLIT = 2
LANES = 128
MXU_DTYPE = jnp.bfloat16
VMEM_LIMIT_BYTES = 56 * 1024 * 1024

F32 = jnp.float32
NEG_INF = float("-inf")


def _mm(a, b):
    return jnp.dot(a.astype(MXU_DTYPE), b.astype(MXU_DTYPE), preferred_element_type=F32)


def _mm_nt(a, b):
    return lax.dot_general(a.astype(MXU_DTYPE), b.astype(MXU_DTYPE),
                           (((1,), (1,)), ((), ())), preferred_element_type=F32)


def _mm_tn(a, b):
    return lax.dot_general(a.astype(MXU_DTYPE), b.astype(MXU_DTYPE),
                           (((0,), (0,)), ((), ())), preferred_element_type=F32)


def _rms(x, w):
    return x * lax.rsqrt(jnp.mean(x * x, axis=-1, keepdims=True) + EPS) * w


def _rope(t, c, sa, sb, half):
    n = t.shape[1]
    return t * c + pltpu.roll(t, n - half, 1) * sa + pltpu.roll(t, half, 1) * sb


def _params(*sem):
    return pltpu.CompilerParams(dimension_semantics=sem, vmem_limit_bytes=VMEM_LIMIT_BYTES)


ORDER_SPEC = pl.BlockSpec(memory_space=pl.ANY)


def _ordered(body):
    def run(_order_ref, *refs):
        body(*refs)
    return run


def _inproj_body(x_ref, ln1_ref, win_ref, cr_ref, sar_ref, sbr_ref, cm_ref, sam_ref, sbm_ref,
                 qnw_ref, kvnw_ref, wuq_ref, wuk_ref, wuv_ref,
                 qr_ref, kr_ref, vr_ref, gate_ref, qm_ref, km_ref, vm_ref, ckv_ref, kpe_ref,
                 *, rw, qrank, kvrank, ret_half, mla_half, k_scale, nheads):
    n1 = _rms(x_ref[...], ln1_ref[...])
    proj = _mm(n1, win_ref[...])
    cr, sar, sbr = cr_ref[...], sar_ref[...], sbr_ref[...]
    qr_ref[...] = _rope(proj[:, 0:rw], cr, sar, sbr, ret_half)
    kr_ref[...] = _rope(proj[:, rw:2 * rw], cr, sar, sbr, ret_half) * k_scale
    vr_ref[...] = proj[:, 2 * rw:3 * rw]
    gate_ref[...] = proj[:, 3 * rw:4 * rw]
    o4 = 4 * rw
    o5 = o4 + qrank
    o6 = o5 + kvrank
    cm, sam, sbm = cm_ref[...], sam_ref[...], sbm_ref[...]
    tile = lambda t: jnp.concatenate([t] * nheads, axis=1)
    cq = _rms(proj[:, o4:o5], qnw_ref[...])
    qm = _rope(_mm(cq, wuq_ref[...]), tile(cm), tile(sam), tile(sbm), mla_half)
    qm_ref[...] = qm.astype(qm_ref.dtype)
    ckv = _rms(proj[:, o5:o6], kvnw_ref[...])
    ckv_ref[...] = ckv
    kpe = _rope(proj[:, o6:o6 + LANES], cm, sam, sbm, mla_half)
    kpe_ref[...] = kpe
    km_ref[...] = (_mm(ckv, wuk_ref[...]) + tile(kpe)).astype(km_ref.dtype)
    vm_ref[...] = _mm(ckv, wuv_ref[...]).astype(vm_ref.dtype)


def _inproj(order, x, tabs, ln1, win_p, qnw, kvnw, wuq_p, wuk_p, wuv_p, *, tm, dims):
    t, d = x.shape
    rw, nheads = dims["rw"], dims["nheads"]
    hp = nheads * LANES
    nblk_tab = tabs[0].shape[0] // tm
    row = lambda i: (i, 0)
    tab = lambda i: (i % nblk_tab, 0)
    full = lambda i: (0, 0)
    fs = lambda a: pl.BlockSpec(a.shape, full)
    in_specs = [pl.BlockSpec((tm, d), row), fs(ln1), fs(win_p)]
    in_specs += [pl.BlockSpec((tm, rw), tab)] * 3 + [pl.BlockSpec((tm, LANES), tab)] * 3
    in_specs += [fs(qnw), fs(kvnw), fs(wuq_p), fs(wuk_p), fs(wuv_p)]
    out_shape = [jax.ShapeDtypeStruct((t, rw), F32)] * 4
    out_shape += [jax.ShapeDtypeStruct((t, hp), MXU_DTYPE)] * 3
    out_shape += [jax.ShapeDtypeStruct((t, dims["kvrank"]), F32), jax.ShapeDtypeStruct((t, LANES), F32)]
    out_specs = [pl.BlockSpec((tm, rw), row)] * 4 + [pl.BlockSpec((tm, hp), row)] * 3
    out_specs += [pl.BlockSpec((tm, dims["kvrank"]), row), pl.BlockSpec((tm, LANES), row)]
    body = functools.partial(
        _inproj_body, rw=rw, qrank=dims["qrank"], kvrank=dims["kvrank"], ret_half=dims["ret_dk"] // 2,
        mla_half=dims["mla_rope"] // 2, k_scale=dims["ret_dk"] ** -0.5, nheads=nheads)
    return pl.pallas_call(
        _ordered(body), grid=(t // tm,), in_specs=[ORDER_SPEC] + in_specs, out_specs=out_specs,
        out_shape=out_shape, compiler_params=_params("parallel"), name="inproj",
    )(order, x, ln1, win_p, *tabs, qnw, kvnw, wuq_p, wuk_p, wuv_p)


def _retention_body(lg_ref, q_ref, k_ref, v_ref, s0_ref, o_ref, sout_ref, s_scr, *, rows, chunk, dk):
    hp = pl.program_id(1)
    j = pl.program_id(2)

    @pl.when(j == 0)
    def _():
        s_scr[...] = s0_ref[0, 0]

    lane = lax.broadcasted_iota(jnp.int32, (1, LANES), 1)
    is_a = lane < dk
    lg_a = lg_ref[2 * hp]
    lg_b = lg_ref[2 * hp + 1]
    lgl = jnp.where(is_a, lg_a, lg_b)
    r = lax.broadcasted_iota(jnp.int32, (rows, 1), 0).astype(F32)
    q, k, v = q_ref[...], k_ref[...], v_ref[...]
    q_dec = q * jnp.exp(lgl * (r + 1.0))
    k_dec = k * jnp.exp(lgl * (float(rows) - 1.0 - r))
    ri = lax.broadcasted_iota(jnp.int32, (rows, rows), 0)
    ci = lax.broadcasted_iota(jnp.int32, (rows, rows), 1)
    dist = jnp.abs(ri - ci).astype(F32)
    visible = (ci // chunk) <= (ri // chunk)
    o = _mm(q_dec, s_scr[...])
    for first, lg in ((True, lg_a), (False, lg_b)):
        sel = is_a if first else jnp.logical_not(is_a)
        qh = jnp.where(sel, q, 0.0)
        vh = jnp.where(sel, v, 0.0)
        decay = jnp.where(visible, jnp.exp(lg * dist), 0.0)
        o = o + _mm(_mm_nt(qh, k) * decay, vh)
    o_ref[...] = o
    sr = lax.broadcasted_iota(jnp.int32, (LANES, LANES), 0) < dk
    sc = lax.broadcasted_iota(jnp.int32, (LANES, LANES), 1) < dk
    kv = jnp.where(sr == sc, _mm_tn(k_dec, v), 0.0)
    s_new = jnp.exp(lgl * float(rows)) * s_scr[...] + kv
    s_scr[...] = s_new

    @pl.when(j == pl.num_programs(2) - 1)
    def _():
        sout_ref[0, 0] = s_new


def _retention(lg, q, k, v, s0_pairs, *, nbatch, rows, chunk, dk):
    t, w = q.shape
    npairs = w // LANES
    nblk = t // (nbatch * rows)
    blk = pl.BlockSpec((rows, LANES), lambda b, p, j: (b * nblk + j, p))
    st = pl.BlockSpec((1, 1, LANES, LANES), lambda b, p, j: (b, p, 0, 0))
    return pl.pallas_call(
        functools.partial(_retention_body, rows=rows, chunk=chunk, dk=dk),
        grid=(nbatch, npairs, nblk),
        in_specs=[pl.BlockSpec(memory_space=pltpu.SMEM), blk, blk, blk, st],
        out_specs=[blk, st],
        out_shape=[jax.ShapeDtypeStruct((t, w), F32),
                   jax.ShapeDtypeStruct((nbatch, npairs, LANES, LANES), F32)],
        scratch_shapes=[pltpu.VMEM((LANES, LANES), F32)],
        compiler_params=_params("parallel", "parallel", "arbitrary"), name="retention",
    )(lg, q, k, v, s0_pairs)


def _flash_body(q_ref, k_ref, v_ref, o_ref, *, tq, tile0, scale, nheads, chunk):
    i = pl.program_id(1) + tile0
    ri = lax.broadcasted_iota(jnp.int32, (tq, tq), 0) // chunk
    ci = lax.broadcasted_iota(jnp.int32, (tq, tq), 1) // chunk
    visible = ci <= ri

    def head(h):
        cols = slice(h * LANES, (h + 1) * LANES)
        q = q_ref[:, cols]

        def step(j, carry, diagonal):
            m, l, acc = carry
            off = pl.multiple_of(j * tq, tq)
            s = _mm_nt(q, k_ref[pl.ds(off, tq), cols]) * scale
            if diagonal:
                s = jnp.where(visible, s, NEG_INF)
            m_new = jnp.maximum(m, jnp.max(s, axis=1, keepdims=True))
            alpha = jnp.exp(m - m_new)
            p = jnp.exp(s - m_new)
            l = alpha * l + jnp.sum(p, axis=1, keepdims=True)
            acc = alpha * acc + _mm(p, v_ref[pl.ds(off, tq), cols])
            return m_new, l, acc

        init = (jnp.full((tq, 1), NEG_INF, F32), jnp.zeros((tq, 1), F32), jnp.zeros((tq, LANES), F32))
        carry = lax.fori_loop(0, i, functools.partial(step, diagonal=False), init)
        _, l, acc = step(i, carry, True)
        return acc / l

    for p in range(nheads // 2):
        o_ref[:, p * LANES:(p + 1) * LANES] = head(2 * p) + head(2 * p + 1)


def _flash(order, qm, km, vm, *, nbatch, tq, lo, hi, scale, nheads, chunk):
    t, hp = qm.shape
    s = t // nbatch
    nq = s // tq
    tile0 = lo // tq
    nqr = (hi - lo) // tq
    ow = nheads // 2 * LANES
    return pl.pallas_call(
        _ordered(functools.partial(_flash_body, tq=tq, tile0=tile0, scale=scale, nheads=nheads, chunk=chunk)),
        grid=(nbatch, nqr),
        in_specs=[ORDER_SPEC,
                  pl.BlockSpec((tq, hp), lambda b, i: (b * nq + tile0 + i, 0)),
                  pl.BlockSpec((s, hp), lambda b, i: (b, 0)),
                  pl.BlockSpec((s, hp), lambda b, i: (b, 0))],
        out_specs=pl.BlockSpec((tq, ow), lambda b, i: (b * nqr + i, 0)),
        out_shape=jax.ShapeDtypeStruct((nbatch * (hi - lo), ow), F32),
        compiler_params=_params("parallel", "arbitrary"), name="flash_mla",
    )(order, qm, km, vm)


def _decode_attn_body(q_ref, cpast_ref, kpast_ref, cnew_ref, knew_ref, wuk_ref, wuv_ref, o_ref,
                      *, nheads, nope, rope, scale):
    c_past = cpast_ref[0]
    k_past = kpast_ref[0]
    c_new = cnew_ref[...]
    k_new = knew_ref[:, nope:nope + rope]
    outs = []
    for h in range(nheads):
        q = q_ref[:, h * LANES:(h + 1) * LANES]
        q_lat = _mm(q, wuk_ref[h])
        q_pe = q[:, nope:nope + rope]
        s_p = (_mm_nt(q_lat, c_past) + _mm_nt(q_pe, k_past)) * scale
        s_n = (_mm_nt(q_lat, c_new) + _mm_nt(q_pe, k_new)) * scale
        m = jnp.maximum(jnp.max(s_p, axis=1, keepdims=True), jnp.max(s_n, axis=1, keepdims=True))
        p_p = jnp.exp(s_p - m)
        p_n = jnp.exp(s_n - m)
        l = jnp.sum(p_p, axis=1, keepdims=True) + jnp.sum(p_n, axis=1, keepdims=True)
        o_lat = (_mm(p_p, c_past) + _mm(p_n, c_new)) / l
        outs.append(_mm(o_lat, wuv_ref[h]))
    o_ref[...] = jnp.concatenate(outs, axis=1)


def _decode_attn(qm, c_past, k_past, c_new, kpe_new, wuk3, wuv3, *, nq, nope, rope, scale):
    nb, past, kvr = c_past.shape
    nheads, _, vdim = wuv3.shape
    t, hp = qm.shape
    row = lambda b: (b, 0)
    full3 = lambda b: (0, 0, 0)
    return pl.pallas_call(
        functools.partial(_decode_attn_body, nheads=nheads, nope=nope, rope=rope, scale=scale),
        grid=(nb,),
        in_specs=[pl.BlockSpec((nq, hp), row),
                  pl.BlockSpec((1, past, kvr), lambda b: (b, 0, 0)),
                  pl.BlockSpec((1, past, rope), lambda b: (b, 0, 0)),
                  pl.BlockSpec((nq, kvr), row),
                  pl.BlockSpec((nq, LANES), row),
                  pl.BlockSpec(wuk3.shape, full3),
                  pl.BlockSpec(wuv3.shape, full3)],
        out_specs=pl.BlockSpec((nq, nheads * vdim), row),
        out_shape=jax.ShapeDtypeStruct((t, nheads * vdim), F32),
        compiler_params=_params("parallel"), name="decode_mla",
    )(qm, c_past, k_past, c_new, kpe_new, wuk3, wuv3)


def _split3(x):
    a = x.astype(MXU_DTYPE)
    r = x - a.astype(F32)
    b = r.astype(MXU_DTYPE)
    c = (r - b.astype(F32)).astype(MXU_DTYPE)
    return a, b, c


def _group_mean(x, avg):
    a, b, c = _split3(x)
    dot = lambda t: jnp.dot(t, avg, preferred_element_type=F32)
    return dot(a) + dot(b) + dot(c)


def _topk_rows(s, payload, kk):
    n = s.shape[0]
    rid = lax.broadcasted_iota(jnp.int32, s.shape, 0)
    vals, pays = [], []
    for _ in range(kk):
        mx = jnp.max(s, axis=0, keepdims=True)
        first = jnp.min(jnp.where(s == mx, rid, n), axis=0, keepdims=True)
        hit = rid == first
        vals.append(mx)
        pays.append(jnp.max(jnp.where(hit, payload, -1), axis=0, keepdims=True))
        s = jnp.where(hit, NEG_INF, s)
    return jnp.concatenate(vals, axis=0), jnp.concatenate(pays, axis=0)


def _mix_route_body(x_ref, ret_ref, gate_ref, mla_ref, gnw_ref, avg_ref, wo_ref, ln2_ref, wq_ref, keys_ref,
                    h_ref, hn_ref, idx_ref, g_ref, *, rw, pheads, nkeys, topk):
    ret = ret_ref[...]
    avg = avg_ref[...]
    mu = _group_mean(ret, avg)
    cen = ret - mu
    var = _group_mean(cen * cen, avg)
    gate = gate_ref[...]
    y = cen * lax.rsqrt(var + EPS) * gnw_ref[...] * (gate * jax.nn.sigmoid(gate))
    h = x_ref[...] + _mm(y, wo_ref[0:rw, :]) + _mm(mla_ref[...], wo_ref[rw:, :])
    h_ref[...] = h
    hn = _rms(h, ln2_ref[...])
    hn_ref[...] = hn
    qp = _mm(hn, wq_ref[...])
    kid = lax.broadcasted_iota(jnp.int32, (nkeys, qp.shape[0]), 0)
    idx_rows, g_rows = [], []
    for hd in range(pheads):
        ts, ti = [], []
        for half in range(2):
            c = (2 * hd + half) * LANES
            st = _mm_nt(keys_ref[2 * hd + half], qp[:, c:c + LANES])
            v, i = _topk_rows(st, kid, topk)
            ts.append(v)
            ti.append(i)
        cand = jnp.concatenate([ts[0][a:a + 1, :] + ts[1] for a in range(topk)], axis=0)
        cidx = jnp.concatenate([ti[0][a:a + 1, :] * nkeys + ti[1] for a in range(topk)], axis=0)
        best, expert = _topk_rows(cand, cidx, topk)
        e = jnp.exp(best - best[0:1, :])
        g_rows.append(e / jnp.sum(e, axis=0, keepdims=True))
        idx_rows.append(expert)
    idx_ref[...] = jnp.concatenate(idx_rows, axis=0).T
    g_ref[...] = jnp.concatenate(g_rows, axis=0).T


def _mix_route(x, ret_o, gate, mla_o, gnw, avg, wo, ln2, wq, keys, *, tm, row0, pheads, nkeys, topk):
    t = mla_o.shape[0]
    d = x.shape[1]
    rw = ret_o.shape[1]
    nsel = pheads * topk
    blk0 = row0 // tm
    row = lambda i: (i, 0)
    src = lambda i: (blk0 + i, 0)
    fs = lambda a: pl.BlockSpec(a.shape, lambda i: (0,) * a.ndim)
    return pl.pallas_call(
        functools.partial(_mix_route_body, rw=rw, pheads=pheads, nkeys=nkeys, topk=topk),
        grid=(t // tm,),
        in_specs=[pl.BlockSpec((tm, d), src), pl.BlockSpec((tm, rw), src), pl.BlockSpec((tm, rw), src),
                  pl.BlockSpec((tm, mla_o.shape[1]), row), fs(gnw), fs(avg), fs(wo), fs(ln2), fs(wq), fs(keys)],
        out_specs=[pl.BlockSpec((tm, d), row), pl.BlockSpec((tm, d), row),
                   pl.BlockSpec((tm, nsel), row), pl.BlockSpec((tm, nsel), row)],
        out_shape=[jax.ShapeDtypeStruct((t, d), F32), jax.ShapeDtypeStruct((t, d), F32),
                   jax.ShapeDtypeStruct((t, nsel), jnp.int32),
                   jax.ShapeDtypeStruct((t, nsel), F32)],
        compiler_params=_params("parallel"), name="mix_route",
    )(x, ret_o, gate, mla_o, gnw, avg, wo, ln2, wq, keys)


def _gelu_gate_body(hid_ref, g_ref, a_ref):
    hid = hid_ref[...]
    a_ref[...] = 0.5 * hid * (1.0 + lax.erf(hid * (2.0 ** -0.5))) * g_ref[...]


def _gelu_gate(order, hid, g, *, tm):
    t, n = hid.shape
    blk = pl.BlockSpec((tm, n), lambda i: (i, 0))
    return pl.pallas_call(
        _ordered(_gelu_gate_body), grid=(t // tm,), in_specs=[ORDER_SPEC, blk, blk], out_specs=blk,
        out_shape=jax.ShapeDtypeStruct((t, n), F32), compiler_params=_params("parallel"), name="gelu_gate",
    )(order, hid, g)


def _residual_body(h_ref, p_ref, lnf_ref, o_ref, *, final_norm):
    out = h_ref[...] + p_ref[...]
    if final_norm:
        out = _rms(out, lnf_ref[...])
    o_ref[...] = out


def _residual(order, h, peer, lnf, *, tm, final_norm):
    t, d = h.shape
    blk = pl.BlockSpec((tm, d), lambda i: (i, 0))
    return pl.pallas_call(
        _ordered(functools.partial(_residual_body, final_norm=final_norm)), grid=(t // tm,),
        in_specs=[ORDER_SPEC, blk, blk, pl.BlockSpec((1, d), lambda i: (0, 0))], out_specs=blk,
        out_shape=jax.ShapeDtypeStruct((t, d), F32), compiler_params=_params("parallel"), name="residual_norm",
    )(order, h, peer, lnf)


SC_CORES = 2
SC_SUBCORES = 16
SC_LANES = 16
SC_RING = 4
SC_BATCH = 32


def _sc_worker_id():
    return lax.axis_index("s") * SC_CORES + lax.axis_index("c")


def _sc_ring(nq, start, wait, compute):
    for s in range(SC_RING - 1):
        start(s, s)

    def step(q, s):
        nxt = q + SC_RING - 1

        @pl.when(nxt < nq)
        def _():
            start(nxt, (s + SC_RING - 1) % SC_RING)

        wait(q, s)
        compute(q, s)

    full = nq // SC_RING * SC_RING

    @pl.loop(0, full, step=SC_RING)
    def _(q0):
        for s in range(SC_RING):
            step(q0 + s, s)

    for s in range(nq - full):
        step(jnp.int32(full + s), s)


def _pack_rows(tab):
    bits = lax.bitcast_convert_type(tab.astype(jnp.bfloat16), jnp.uint16).astype(jnp.uint32)
    half = tab.shape[1] // 2
    return lax.bitcast_convert_type(bits[:, :half] | (bits[:, half:] << 16), jnp.int32)


def _unpack_pair(w):
    lo = lax.bitcast_convert_type(lax.shift_left(w, jnp.int32(16)), F32)
    hi = lax.bitcast_convert_type(w & jnp.int32(-65536), F32)
    return lo, hi


def _peer_hidden_sc(order, xn, idx, u_tab):
    t, d = xn.shape
    nsel = idx.shape[1]
    nw = SC_CORES * SC_SUBCORES
    per_w = t // nw
    tb = min(SC_BATCH, per_w)
    nchunk = nsel // SC_LANES
    shift = nchunk.bit_length() - 1
    half = d // 2
    nword = half // SC_LANES
    nq = tb * nchunk
    assert per_w * nw == t and per_w % tb == 0 and nchunk == 1 << shift and nq >= SC_RING
    assert u_tab.shape[1] == half
    mesh = plsc.VectorSubcoreMesh(core_axis_name="c", subcore_axis_name="s")

    def body(_order_hbm, x_hbm, idx_hbm, u_hbm, out_hbm, idx_v, x_v, ubuf, hid_v, sem):
        wid = _sc_worker_id()
        lane = lax.iota(jnp.int32, SC_LANES)

        def gather(q, slot):
            tok = lax.shift_right_logical(q, shift)
            ch = q & (nchunk - 1)
            rows = idx_v.at[tok, pl.ds(ch * SC_LANES, SC_LANES)]
            return pltpu.make_async_copy(u_hbm.at[rows], ubuf.at[slot], sem.at[slot])

        def compute(q, slot):
            tok = lax.shift_right_logical(q, shift)
            ch = q & (nchunk - 1)

            @plsc.parallel_loop(0, nword, carry=tuple(jnp.zeros((SC_LANES,), F32) for _ in range(SC_LANES)))
            def accs(c, acc):
                off = pl.multiple_of(c * SC_LANES, SC_LANES)
                x_lo = x_v[tok, pl.ds(off, SC_LANES)]
                x_hi = x_v[tok, pl.ds(pl.multiple_of(half + off, SC_LANES), SC_LANES)]
                new = []
                for k, a in enumerate(acc):
                    lo, hi = _unpack_pair(ubuf[slot, k, pl.ds(off, SC_LANES)])
                    new.append(a + x_lo * lo + x_hi * hi)
                return tuple(new)

            out = jnp.zeros((SC_LANES,), F32)
            for k in range(SC_LANES):
                out = jnp.where(lane == k, jnp.sum(accs[k]), out)
            hid_v[tok, pl.ds(ch * SC_LANES, SC_LANES)] = out

        @pl.loop(0, per_w // tb)
        def _(b):
            base = wid * per_w + b * tb
            pltpu.sync_copy(idx_hbm.at[pl.ds(base, tb)], idx_v)
            pltpu.sync_copy(x_hbm.at[pl.ds(base, tb)], x_v)
            _sc_ring(nq, lambda q, s: gather(q, s).start(), lambda q, s: gather(q, s).wait(), compute)
            pltpu.sync_copy(hid_v, out_hbm.at[pl.ds(base, tb)])

    return pl.kernel(
        body, out_type=jax.ShapeDtypeStruct((t, nsel), F32), mesh=mesh,
        scratch_types=[pltpu.VMEM((tb, nsel), jnp.int32), pltpu.VMEM((tb, d), F32),
                       pltpu.VMEM((SC_RING, SC_LANES, half), jnp.int32), pltpu.VMEM((tb, nsel), F32),
                       pltpu.SemaphoreType.DMA((SC_RING,))],
        compiler_params=pltpu.CompilerParams(needs_layout_passes=False), name="peer_hidden_sc",
    )(order, xn, idx, u_tab)


def _peer_mix_sc(act, idx, v_tab):
    t, nsel = act.shape
    half = v_tab.shape[1]
    d = 2 * half
    nw = SC_CORES * SC_SUBCORES
    per_w = t // nw
    tb = min(SC_BATCH, per_w)
    nchunk = nsel // SC_LANES
    shift = nchunk.bit_length() - 1
    ncol = d // SC_LANES
    nword = half // SC_LANES
    nq = tb * nchunk
    assert per_w * nw == t and per_w % tb == 0 and nchunk == 1 << shift and nq >= SC_RING
    mesh = plsc.VectorSubcoreMesh(core_axis_name="c", subcore_axis_name="s")

    def body(a_hbm, idx_hbm, v_hbm, out_hbm, idx_v, a_v, vbuf, o_v, sem):
        wid = _sc_worker_id()
        zero = jnp.zeros((SC_LANES,), F32)

        def gather(q, slot):
            tok = lax.shift_right_logical(q, shift)
            ch = q & (nchunk - 1)
            rows = idx_v.at[tok, pl.ds(ch * SC_LANES, SC_LANES)]
            return pltpu.make_async_copy(v_hbm.at[rows], vbuf.at[slot], sem.at[slot])

        def compute(q, slot):
            tok = lax.shift_right_logical(q, shift)
            ch = q & (nchunk - 1)
            tok_v = jnp.full((SC_LANES,), tok, jnp.int32)
            col_v = jnp.full((SC_LANES,), ch * SC_LANES, jnp.int32)
            w = [plsc.load_gather(a_v, [tok_v, col_v + k]) for k in range(SC_LANES)]

            def tree_sum(terms):
                while len(terms) > 1:
                    terms = [a + b for a, b in zip(terms[0::2], terms[1::2])]
                return terms[0]

            @plsc.parallel_loop(0, nword)
            def _(c):
                off = pl.multiple_of(c * SC_LANES, SC_LANES)
                cs_lo = pl.ds(off, SC_LANES)
                cs_hi = pl.ds(pl.multiple_of(half + off, SC_LANES), SC_LANES)
                pairs = [_unpack_pair(vbuf[slot, k, cs_lo]) for k in range(SC_LANES)]
                o_v[tok, cs_lo] = o_v[tok, cs_lo] + tree_sum([w[k] * p[0] for k, p in enumerate(pairs)])
                o_v[tok, cs_hi] = o_v[tok, cs_hi] + tree_sum([w[k] * p[1] for k, p in enumerate(pairs)])

        @pl.loop(0, per_w // tb)
        def _(b):
            base = wid * per_w + b * tb
            pltpu.sync_copy(idx_hbm.at[pl.ds(base, tb)], idx_v)
            pltpu.sync_copy(a_hbm.at[pl.ds(base, tb)], a_v)

            @pl.loop(0, tb)
            def _(r):
                @pl.loop(0, ncol)
                def _(c):
                    o_v[r, pl.ds(pl.multiple_of(c * SC_LANES, SC_LANES), SC_LANES)] = zero

            _sc_ring(nq, lambda q, s: gather(q, s).start(), lambda q, s: gather(q, s).wait(), compute)
            pltpu.sync_copy(o_v, out_hbm.at[pl.ds(base, tb)])

    return pl.kernel(
        body, out_type=jax.ShapeDtypeStruct((t, d), F32), mesh=mesh,
        scratch_types=[pltpu.VMEM((tb, nsel), jnp.int32), pltpu.VMEM((tb, nsel), F32),
                       pltpu.VMEM((SC_RING, SC_LANES, half), jnp.int32), pltpu.VMEM((tb, d), F32),
                       pltpu.SemaphoreType.DMA((SC_RING,))],
        compiler_params=pltpu.CompilerParams(needs_layout_passes=False), name="peer_mix_sc",
    )(act, idx, v_tab)


def _rope_tables(pos, half, group, width, lo):
    inv = ROPE_BASE ** (-jnp.arange(half, dtype=F32) / half)
    ang = pos.astype(F32)[:, None] * inv[None, :]
    cos, sin = jnp.cos(ang), jnp.sin(ang)
    n = pos.shape[0]
    reps = width // group
    pad_hi = group - lo - 2 * half
    blk = lambda a, b, fill: jnp.concatenate(
        [jnp.full((n, lo), fill, F32), a, b, jnp.full((n, pad_hi), fill, F32)], axis=1)
    z = jnp.zeros_like(sin)
    c = blk(cos, cos, 1.0)
    sa = blk(-sin, z, 0.0)
    sb = blk(z, sin, 0.0)
    return [jnp.tile(a, (1, reps)) for a in (c, sa, sb)]


def _ret_log_decay(nheads):
    return jnp.log(1.0 - jnp.exp2(-5.0 - jnp.arange(nheads, dtype=F32)))


def _pair_states(s):
    b, h, dk, dv = s.shape
    s = s.reshape(b, h // 2, 2, dk, dv)
    z = jnp.zeros_like(s[:, :, 0])
    top = jnp.concatenate([s[:, :, 0], z], axis=-1)
    bot = jnp.concatenate([z, s[:, :, 1]], axis=-1)
    return jnp.concatenate([top, bot], axis=-2)


def _unpair_states(sp, dk, dv):
    b, hp = sp.shape[:2]
    return jnp.stack([sp[:, :, :dk, :dv], sp[:, :, dk:, dv:]], axis=2).reshape(b, 2 * hp, dk, dv)


def _layer_weights(ln1_w, w_in, ret_gn_w, q_norm_w, w_uq, kv_norm_w, w_uk, w_uv, w_o, ln2_w,
                   peer_w_q, peer_sub_keys, dims):
    d = w_in.shape[0]
    nheads, nope, rope, vdim = dims["nheads"], dims["nope"], dims["mla_rope"], dims["vdim"]
    o6 = 4 * dims["rw"] + dims["qrank"] + dims["kvrank"]
    zc = lambda r, c: jnp.zeros((r, c), F32)
    win_p = jnp.concatenate([w_in[:, :o6], zc(d, nope), w_in[:, o6:], zc(d, LANES - nope - rope)], axis=1)
    qr, kr = w_uq.shape[0], w_uk.shape[0]
    wuq_p = jnp.concatenate([w_uq, jnp.zeros((qr, nheads, LANES - nope - rope), F32)], axis=2).reshape(qr, -1)
    wuk_p = jnp.concatenate([w_uk, jnp.zeros((kr, nheads, LANES - nope), F32)], axis=2).reshape(kr, -1)
    zv = jnp.zeros((kr, nheads // 2, LANES - vdim), F32)
    wv = w_uv.reshape(kr, nheads // 2, 2, vdim)
    wuv_p = jnp.concatenate([wv[:, :, 0], zv, zv, wv[:, :, 1]], axis=2).reshape(kr, -1)
    wuk3 = jnp.concatenate([jnp.transpose(w_uk, (1, 2, 0)),
                            jnp.zeros((nheads, LANES - nope, kr), F32)], axis=1)
    wuv3 = jnp.transpose(w_uv, (1, 0, 2))
    gidx = jnp.arange(dims["rw"]) // dims["ret_dv"]
    avg = (gidx[:, None] == gidx[None, :]).astype(F32) / dims["ret_dv"]
    keys = peer_sub_keys.reshape(-1, peer_sub_keys.shape[2], peer_sub_keys.shape[3])
    c = lambda a: a.astype(MXU_DTYPE)
    r2 = lambda a: a.reshape(1, -1)
    return dict(ln1=r2(ln1_w), win_p=c(win_p), gnw=r2(ret_gn_w), qnw=r2(q_norm_w), kvnw=r2(kv_norm_w),
                wuq_p=c(wuq_p), wuk_p=c(wuk_p), wuv_p=c(wuv_p), wuk3=c(wuk3), wuv3=c(wuv3), avg=c(avg),
                wo=c(w_o), ln2=r2(ln2_w), wq=c(peer_w_q), keys=c(keys))


class _Stream:
    def __init__(self, x, tabs, s0, *, nbatch, ret_rows, ret_chunk, tm, ranges, cache=None):
        self.x, self.tabs, self.s0, self.cache = x, tabs, s0, cache
        self.nbatch, self.ret_rows, self.ret_chunk, self.tm, self.ranges = nbatch, ret_rows, ret_chunk, tm, ranges
        self.seq = x.shape[0] // nbatch
        assert nbatch == 1 or ranges == [(0, self.seq)]
        self.pre = None
        self.outs = []


def _layer(streams, w, lg, u_tab, v_tab, lnf, dims, *, final_norm):
    units = [(st, lo, hi) for st in streams for lo, hi in st.ranges]
    n = len(units)
    nheads, dk = dims["nheads"], dims["ret_dk"]
    scale = (dims["nope"] + dims["mla_rope"]) ** -0.5
    built = [None] * n

    def build(i, order):
        st, lo, hi = units[i]
        if st.pre is None:
            proj = _inproj(order, st.x, st.tabs, w["ln1"], w["win_p"], w["qnw"], w["kvnw"], w["wuq_p"], w["wuk_p"],
                           w["wuv_p"], tm=st.tm, dims=dims)
            ret_o, s_pairs = _retention(lg, *proj[:3], _pair_states(st.s0), nbatch=st.nbatch, rows=st.ret_rows,
                                        chunk=st.ret_chunk, dk=dk)
            st.pre = list(proj) + [ret_o, s_pairs]
        qr, kr, vr, gate, qm, km, vm, ckv, kpe, ret_o, s_pairs = st.pre
        if st.cache is None:
            mla_o = _flash(order, qm, km, vm, nbatch=st.nbatch, tq=min(256, st.seq), lo=lo, hi=hi, scale=scale,
                           nheads=nheads, chunk=CHUNK)
        else:
            mla_o = _decode_attn(qm, st.cache[0], st.cache[1], ckv, kpe, w["wuk3"], w["wuv3"], nq=st.seq,
                                 nope=dims["nope"], rope=dims["mla_rope"], scale=scale)
        h, hn, idx, g = _mix_route(st.x, ret_o, gate, mla_o, w["gnw"], w["avg"], w["wo"], w["ln2"], w["wq"],
                                   w["keys"], tm=st.tm, row0=lo, pheads=dims["pheads"], nkeys=dims["nkeys"],
                                   topk=dims["topk"])
        built[i] = (h, idx, g, _peer_hidden_sc(peers[i - 2] if i >= 2 else lnf, hn, idx, u_tab))

    for i in range(min(2, n)):
        build(i, lnf)
    acts, peers = [], []
    for i in range(n):
        h, idx, g, hid = built[i]
        acts.append(_gelu_gate(built[i + 1][1] if i + 1 < n else lnf, hid, g, tm=units[i][0].tm))
        peers.append(_peer_mix_sc(acts[i], idx, v_tab))
        if i + 2 < n:
            build(i + 2, acts[i])
    for i, (st, lo, hi) in enumerate(units):
        st.outs.append(_residual(acts[min(i + 2, n - 1)], built[i][0], peers[i], lnf, tm=st.tm,
                                 final_norm=final_norm))
    nope, rope, dv = dims["nope"], dims["mla_rope"], dims["ret_dv"]
    results = []
    for st in streams:
        out = st.outs[0] if len(st.outs) == 1 else jnp.concatenate(st.outs, axis=0)
        results.append((out, st.pre[7], st.pre[8][:, nope:nope + rope], _unpair_states(st.pre[10], dk, dv)))
    return results


def kernel(x_prompt, x_sample, cache_mla_ckv, cache_mla_krope, state_retention, ln1_w, w_in, ret_gn_w,
           mla_q_norm_w, mla_w_uq, mla_kv_norm_w, mla_w_uk, mla_w_uv, w_o, ln2_w, peer_w_q, peer_sub_keys,
           peer_u, peer_v, lnf_w):
    depth = w_in.shape[0]
    nb, seq, d = x_prompt.shape
    db, dseq, _ = x_sample.shape
    past = cache_mla_ckv.shape[2]
    rheads, dk, dv = state_retention.shape[2:]
    nkeys = peer_sub_keys.shape[3]
    dims = dict(rw=rheads * dk, ret_dk=dk, ret_dv=dv, qrank=mla_w_uq.shape[1], kvrank=mla_w_uk.shape[1],
                nheads=mla_w_uq.shape[2], nope=mla_w_uk.shape[3], vdim=mla_w_uv.shape[3],
                mla_rope=mla_w_uq.shape[3] - mla_w_uk.shape[3], pheads=peer_sub_keys.shape[1], nkeys=nkeys,
                topk=PEER_TOPK)
    assert rheads * dk == rheads * dv and dims["nheads"] % 2 == 0 and dk * 2 == LANES and dims["vdim"] * 2 == LANES

    def tables(pos):
        return (_rope_tables(pos, dk // 2, dk, dims["rw"], 0)
                + _rope_tables(pos, dims["mla_rope"] // 2, LANES, LANES, dims["nope"]))

    tabs_p = tables(jnp.arange(seq))
    tabs_s = tables(jnp.tile(past + jnp.arange(dseq), db))
    lg = _ret_log_decay(rheads)
    lnf = lnf_w.reshape(1, -1)
    hp = x_prompt.reshape(nb * seq, d)
    hs = x_sample.reshape(db * dseq, d)
    outs = [[] for _ in range(6)]
    for l in range(depth):
        w = _layer_weights(ln1_w[l], w_in[l], ret_gn_w[l], mla_q_norm_w[l], mla_w_uq[l], mla_kv_norm_w[l],
                           mla_w_uk[l], mla_w_uv[l], w_o[l], ln2_w[l], peer_w_q[l], peer_sub_keys[l], dims)
        last = l == depth - 1
        gb = nb // PROMPT_GROUPS if nb % PROMPT_GROUPS == 0 else nb

        def frame_ranges(g):
            step = seq // (PROMPT_HEAD_SPLIT if g == 0 else PROMPT_SPLIT)
            ok = gb == 1 and step > 0 and step % 256 == 0
            return [(lo, lo + step) for lo in range(0, seq, step)] if ok else [(0, seq)]

        streams = [_Stream(hp[g * gb * seq:(g + 1) * gb * seq], tabs_p, jnp.zeros((gb, rheads, dk, dv), F32),
                           nbatch=gb, ret_rows=min(256, seq), ret_chunk=CHUNK, tm=min(256, gb * seq),
                           ranges=frame_ranges(g))
                   for g in range(nb // gb)]
        streams.append(_Stream(hs, tabs_s, state_retention[l], nbatch=db, ret_rows=dseq, ret_chunk=dseq,
                               tm=min(256, db * dseq), ranges=[(0, dseq)],
                               cache=(cache_mla_ckv[l], cache_mla_krope[l])))
        results = _layer(streams, w, lg, _pack_rows(peer_u[l]), _pack_rows(peer_v[l]), lnf, dims,
                         final_norm=last)
        hp, c1, k1, s1 = (jnp.concatenate(p, axis=0) for p in zip(*results[:-1]))
        hs, c2, k2, s2 = results[-1]
        for acc, val in zip(outs, (c1.reshape(nb, seq, -1), k1.reshape(nb, seq, -1), s1,
                                   c2.reshape(db, dseq, -1), k2.reshape(db, dseq, -1), s2)):
            acc.append(val)
    return (hp.reshape(nb, seq, d), hs.reshape(db, dseq, d), *[jnp.stack(o) for o in outs])
```

```python
import functools

import jax
import jax.numpy as jnp
from jax import lax
from jax.experimental import pallas as pl
from jax.experimental.pallas import tpu as pltpu
from jax.experimental.pallas import tpu_sc as plsc

EPS = 1e-6
ROPE_BASE = 10000.0
CHUNK = 64
PEER_TOPK = 16
PROMPT_GROUPS = 8
PROMPT_HEAD_SPLIT = 4
PROMPT_SPLIT = 2
LANES = 128
MXU_DTYPE = jnp.bfloat16
VMEM_LIMIT_BYTES = 56 * 1024 * 1024

F32 = jnp.float32
NEG_INF = float("-inf")


def _mm(a, b):
    return jnp.dot(a.astype(MXU_DTYPE), b.astype(MXU_DTYPE), preferred_element_type=F32)


def _mm_nt(a, b):
    return lax.dot_general(a.astype(MXU_DTYPE), b.astype(MXU_DTYPE),
                           (((1,), (1,)), ((), ())), preferred_element_type=F32)


def _mm_tn(a, b):
    return lax.dot_general(a.astype(MXU_DTYPE), b.astype(MXU_DTYPE),
                           (((0,), (0,)), ((), ())), preferred_element_type=F32)


def _rms(x, w):
    return x * lax.rsqrt(jnp.mean(x * x, axis=-1, keepdims=True) + EPS) * w


def _rope(t, c, sa, sb, half):
    n = t.shape[1]
    return t * c + pltpu.roll(t, n - half, 1) * sa + pltpu.roll(t, half, 1) * sb


def _params(*sem):
    return pltpu.CompilerParams(dimension_semantics=sem, vmem_limit_bytes=VMEM_LIMIT_BYTES)


ORDER_SPEC = pl.BlockSpec(memory_space=pl.ANY)


def _ordered(body):
    def run(_order_ref, *refs):
        body(*refs)
    return run


def _inproj_body(x_ref, ln1_ref, win_ref, cr_ref, sar_ref, sbr_ref, cm_ref, sam_ref, sbm_ref,
                 qnw_ref, kvnw_ref, wuq_ref, wuk_ref, wuv_ref,
                 qr_ref, kr_ref, vr_ref, gate_ref, qm_ref, km_ref, vm_ref, ckv_ref, kpe_ref,
                 *, rw, qrank, kvrank, ret_half, mla_half, k_scale, nheads):
    n1 = _rms(x_ref[...], ln1_ref[...])
    proj = _mm(n1, win_ref[...])
    cr, sar, sbr = cr_ref[...], sar_ref[...], sbr_ref[...]
    qr_ref[...] = _rope(proj[:, 0:rw], cr, sar, sbr, ret_half)
    kr_ref[...] = _rope(proj[:, rw:2 * rw], cr, sar, sbr, ret_half) * k_scale
    vr_ref[...] = proj[:, 2 * rw:3 * rw]
    gate_ref[...] = proj[:, 3 * rw:4 * rw]
    o4 = 4 * rw
    o5 = o4 + qrank
    o6 = o5 + kvrank
    cm, sam, sbm = cm_ref[...], sam_ref[...], sbm_ref[...]
    tile = lambda t: jnp.concatenate([t] * nheads, axis=1)
    cq = _rms(proj[:, o4:o5], qnw_ref[...])
    qm = _rope(_mm(cq, wuq_ref[...]), tile(cm), tile(sam), tile(sbm), mla_half)
    qm_ref[...] = qm.astype(qm_ref.dtype)
    ckv = _rms(proj[:, o5:o6], kvnw_ref[...])
    ckv_ref[...] = ckv
    kpe = _rope(proj[:, o6:o6 + LANES], cm, sam, sbm, mla_half)
    kpe_ref[...] = kpe
    km_ref[...] = (_mm(ckv, wuk_ref[...]) + tile(kpe)).astype(km_ref.dtype)
    vm_ref[...] = _mm(ckv, wuv_ref[...]).astype(vm_ref.dtype)


def _inproj(order, x, tabs, ln1, win_p, qnw, kvnw, wuq_p, wuk_p, wuv_p, *, tm, dims):
    t, d = x.shape
    rw, nheads = dims["rw"], dims["nheads"]
    hp = nheads * LANES
    nblk_tab = tabs[0].shape[0] // tm
    row = lambda i: (i, 0)
    tab = lambda i: (i % nblk_tab, 0)
    full = lambda i: (0, 0)
    fs = lambda a: pl.BlockSpec(a.shape, full)
    in_specs = [pl.BlockSpec((tm, d), row), fs(ln1), fs(win_p)]
    in_specs += [pl.BlockSpec((tm, rw), tab)] * 3 + [pl.BlockSpec((tm, LANES), tab)] * 3
    in_specs += [fs(qnw), fs(kvnw), fs(wuq_p), fs(wuk_p), fs(wuv_p)]
    out_shape = [jax.ShapeDtypeStruct((t, rw), F32)] * 4
    out_shape += [jax.ShapeDtypeStruct((t, hp), MXU_DTYPE)] * 3
    out_shape += [jax.ShapeDtypeStruct((t, dims["kvrank"]), F32), jax.ShapeDtypeStruct((t, LANES), F32)]
    out_specs = [pl.BlockSpec((tm, rw), row)] * 4 + [pl.BlockSpec((tm, hp), row)] * 3
    out_specs += [pl.BlockSpec((tm, dims["kvrank"]), row), pl.BlockSpec((tm, LANES), row)]
    body = functools.partial(
        _inproj_body, rw=rw, qrank=dims["qrank"], kvrank=dims["kvrank"], ret_half=dims["ret_dk"] // 2,
        mla_half=dims["mla_rope"] // 2, k_scale=dims["ret_dk"] ** -0.5, nheads=nheads)
    return pl.pallas_call(
        _ordered(body), grid=(t // tm,), in_specs=[ORDER_SPEC] + in_specs, out_specs=out_specs,
        out_shape=out_shape, compiler_params=_params("parallel"), name="inproj",
    )(order, x, ln1, win_p, *tabs, qnw, kvnw, wuq_p, wuk_p, wuv_p)


def _retention_body(lg_ref, q_ref, k_ref, v_ref, s0_ref, o_ref, sout_ref, s_scr, *, rows, chunk, dk):
    hp = pl.program_id(1)
    j = pl.program_id(2)

    @pl.when(j == 0)
    def _():
        s_scr[...] = s0_ref[0, 0]

    lane = lax.broadcasted_iota(jnp.int32, (1, LANES), 1)
    is_a = lane < dk
    lg_a = lg_ref[2 * hp]
    lg_b = lg_ref[2 * hp + 1]
    lgl = jnp.where(is_a, lg_a, lg_b)
    r = lax.broadcasted_iota(jnp.int32, (rows, 1), 0).astype(F32)
    q, k, v = q_ref[...], k_ref[...], v_ref[...]
    q_dec = q * jnp.exp(lgl * (r + 1.0))
    k_dec = k * jnp.exp(lgl * (float(rows) - 1.0 - r))
    ri = lax.broadcasted_iota(jnp.int32, (rows, rows), 0)
    ci = lax.broadcasted_iota(jnp.int32, (rows, rows), 1)
    dist = jnp.abs(ri - ci).astype(F32)
    visible = (ci // chunk) <= (ri // chunk)
    o = _mm(q_dec, s_scr[...])
    for first, lg in ((True, lg_a), (False, lg_b)):
        sel = is_a if first else jnp.logical_not(is_a)
        qh = jnp.where(sel, q, 0.0)
        vh = jnp.where(sel, v, 0.0)
        decay = jnp.where(visible, jnp.exp(lg * dist), 0.0)
        o = o + _mm(_mm_nt(qh, k) * decay, vh)
    o_ref[...] = o
    sr = lax.broadcasted_iota(jnp.int32, (LANES, LANES), 0) < dk
    sc = lax.broadcasted_iota(jnp.int32, (LANES, LANES), 1) < dk
    kv = jnp.where(sr == sc, _mm_tn(k_dec, v), 0.0)
    s_new = jnp.exp(lgl * float(rows)) * s_scr[...] + kv
    s_scr[...] = s_new

    @pl.when(j == pl.num_programs(2) - 1)
    def _():
        sout_ref[0, 0] = s_new


def _retention(lg, q, k, v, s0_pairs, *, nbatch, rows, chunk, dk):
    t, w = q.shape
    npairs = w // LANES
    nblk = t // (nbatch * rows)
    blk = pl.BlockSpec((rows, LANES), lambda b, p, j: (b * nblk + j, p))
    st = pl.BlockSpec((1, 1, LANES, LANES), lambda b, p, j: (b, p, 0, 0))
    return pl.pallas_call(
        functools.partial(_retention_body, rows=rows, chunk=chunk, dk=dk),
        grid=(nbatch, npairs, nblk),
        in_specs=[pl.BlockSpec(memory_space=pltpu.SMEM), blk, blk, blk, st],
        out_specs=[blk, st],
        out_shape=[jax.ShapeDtypeStruct((t, w), F32),
                   jax.ShapeDtypeStruct((nbatch, npairs, LANES, LANES), F32)],
        scratch_shapes=[pltpu.VMEM((LANES, LANES), F32)],
        compiler_params=_params("parallel", "parallel", "arbitrary"), name="retention",
    )(lg, q, k, v, s0_pairs)


def _flash_body(q_ref, k_ref, v_ref, o_ref, *, tq, tile0, scale, nheads, chunk):
    i = pl.program_id(1) + tile0
    ri = lax.broadcasted_iota(jnp.int32, (tq, tq), 0) // chunk
    ci = lax.broadcasted_iota(jnp.int32, (tq, tq), 1) // chunk
    visible = ci <= ri

    def head(h):
        cols = slice(h * LANES, (h + 1) * LANES)
        q = q_ref[:, cols]

        def step(j, carry, diagonal):
            m, l, acc = carry
            off = pl.multiple_of(j * tq, tq)
            s = _mm_nt(q, k_ref[pl.ds(off, tq), cols]) * scale
            if diagonal:
                s = jnp.where(visible, s, NEG_INF)
            m_new = jnp.maximum(m, jnp.max(s, axis=1, keepdims=True))
            alpha = jnp.exp(m - m_new)
            p = jnp.exp(s - m_new)
            l = alpha * l + jnp.sum(p, axis=1, keepdims=True)
            acc = alpha * acc + _mm(p, v_ref[pl.ds(off, tq), cols])
            return m_new, l, acc

        init = (jnp.full((tq, 1), NEG_INF, F32), jnp.zeros((tq, 1), F32), jnp.zeros((tq, LANES), F32))
        carry = lax.fori_loop(0, i, functools.partial(step, diagonal=False), init)
        _, l, acc = step(i, carry, True)
        return acc / l

    for p in range(nheads // 2):
        o_ref[:, p * LANES:(p + 1) * LANES] = head(2 * p) + head(2 * p + 1)


def _flash(order, qm, km, vm, *, nbatch, tq, lo, hi, scale, nheads, chunk):
    t, hp = qm.shape
    s = t // nbatch
    nq = s // tq
    tile0 = lo // tq
    nqr = (hi - lo) // tq
    ow = nheads // 2 * LANES
    return pl.pallas_call(
        _ordered(functools.partial(_flash_body, tq=tq, tile0=tile0, scale=scale, nheads=nheads, chunk=chunk)),
        grid=(nbatch, nqr),
        in_specs=[ORDER_SPEC,
                  pl.BlockSpec((tq, hp), lambda b, i: (b * nq + tile0 + i, 0)),
                  pl.BlockSpec((s, hp), lambda b, i: (b, 0)),
                  pl.BlockSpec((s, hp), lambda b, i: (b, 0))],
        out_specs=pl.BlockSpec((tq, ow), lambda b, i: (b * nqr + i, 0)),
        out_shape=jax.ShapeDtypeStruct((nbatch * (hi - lo), ow), F32),
        compiler_params=_params("parallel", "arbitrary"), name="flash_mla",
    )(order, qm, km, vm)


def _decode_attn_body(q_ref, cpast_ref, kpast_ref, cnew_ref, knew_ref, wuk_ref, wuv_ref, o_ref,
                      *, nheads, nope, rope, scale):
    c_past = cpast_ref[0]
    k_past = kpast_ref[0]
    c_new = cnew_ref[...]
    k_new = knew_ref[:, nope:nope + rope]
    outs = []
    for h in range(nheads):
        q = q_ref[:, h * LANES:(h + 1) * LANES]
        q_lat = _mm(q, wuk_ref[h])
        q_pe = q[:, nope:nope + rope]
        s_p = (_mm_nt(q_lat, c_past) + _mm_nt(q_pe, k_past)) * scale
        s_n = (_mm_nt(q_lat, c_new) + _mm_nt(q_pe, k_new)) * scale
        m = jnp.maximum(jnp.max(s_p, axis=1, keepdims=True), jnp.max(s_n, axis=1, keepdims=True))
        p_p = jnp.exp(s_p - m)
        p_n = jnp.exp(s_n - m)
        l = jnp.sum(p_p, axis=1, keepdims=True) + jnp.sum(p_n, axis=1, keepdims=True)
        o_lat = (_mm(p_p, c_past) + _mm(p_n, c_new)) / l
        outs.append(_mm(o_lat, wuv_ref[h]))
    o_ref[...] = jnp.concatenate(outs, axis=1)


def _decode_attn(qm, c_past, k_past, c_new, kpe_new, wuk3, wuv3, *, nq, nope, rope, scale):
    nb, past, kvr = c_past.shape
    nheads, _, vdim = wuv3.shape
    t, hp = qm.shape
    row = lambda b: (b, 0)
    full3 = lambda b: (0, 0, 0)
    return pl.pallas_call(
        functools.partial(_decode_attn_body, nheads=nheads, nope=nope, rope=rope, scale=scale),
        grid=(nb,),
        in_specs=[pl.BlockSpec((nq, hp), row),
                  pl.BlockSpec((1, past, kvr), lambda b: (b, 0, 0)),
                  pl.BlockSpec((1, past, rope), lambda b: (b, 0, 0)),
                  pl.BlockSpec((nq, kvr), row),
                  pl.BlockSpec((nq, LANES), row),
                  pl.BlockSpec(wuk3.shape, full3),
                  pl.BlockSpec(wuv3.shape, full3)],
        out_specs=pl.BlockSpec((nq, nheads * vdim), row),
        out_shape=jax.ShapeDtypeStruct((t, nheads * vdim), F32),
        compiler_params=_params("parallel"), name="decode_mla",
    )(qm, c_past, k_past, c_new, kpe_new, wuk3, wuv3)


def _split3(x):
    a = x.astype(MXU_DTYPE)
    r = x - a.astype(F32)
    b = r.astype(MXU_DTYPE)
    c = (r - b.astype(F32)).astype(MXU_DTYPE)
    return a, b, c


def _group_mean(x, avg):
    a, b, c = _split3(x)
    dot = lambda t: jnp.dot(t, avg, preferred_element_type=F32)
    return dot(a) + dot(b) + dot(c)


def _topk_rows(s, payload, kk):
    n = s.shape[0]
    rid = lax.broadcasted_iota(jnp.int32, s.shape, 0)
    vals, pays = [], []
    for _ in range(kk):
        mx = jnp.max(s, axis=0, keepdims=True)
        first = jnp.min(jnp.where(s == mx, rid, n), axis=0, keepdims=True)
        hit = rid == first
        vals.append(mx)
        pays.append(jnp.max(jnp.where(hit, payload, -1), axis=0, keepdims=True))
        s = jnp.where(hit, NEG_INF, s)
    return jnp.concatenate(vals, axis=0), jnp.concatenate(pays, axis=0)


def _mix_route_body(x_ref, ret_ref, gate_ref, mla_ref, gnw_ref, avg_ref, wo_ref, ln2_ref, wq_ref, keys_ref,
                    h_ref, hn_ref, idx_ref, g_ref, *, rw, pheads, nkeys, topk):
    ret = ret_ref[...]
    avg = avg_ref[...]
    mu = _group_mean(ret, avg)
    cen = ret - mu
    var = _group_mean(cen * cen, avg)
    gate = gate_ref[...]
    y = cen * lax.rsqrt(var + EPS) * gnw_ref[...] * (gate * jax.nn.sigmoid(gate))
    h = x_ref[...] + _mm(y, wo_ref[0:rw, :]) + _mm(mla_ref[...], wo_ref[rw:, :])
    h_ref[...] = h
    hn = _rms(h, ln2_ref[...])
    hn_ref[...] = hn
    qp = _mm(hn, wq_ref[...])
    kid = lax.broadcasted_iota(jnp.int32, (nkeys, qp.shape[0]), 0)
    idx_rows, g_rows = [], []
    for hd in range(pheads):
        ts, ti = [], []
        for half in range(2):
            c = (2 * hd + half) * LANES
            st = _mm_nt(keys_ref[2 * hd + half], qp[:, c:c + LANES])
            v, i = _topk_rows(st, kid, topk)
            ts.append(v)
            ti.append(i)
        cand = jnp.concatenate([ts[0][a:a + 1, :] + ts[1] for a in range(topk)], axis=0)
        cidx = jnp.concatenate([ti[0][a:a + 1, :] * nkeys + ti[1] for a in range(topk)], axis=0)
        best, expert = _topk_rows(cand, cidx, topk)
        e = jnp.exp(best - best[0:1, :])
        g_rows.append(e / jnp.sum(e, axis=0, keepdims=True))
        idx_rows.append(expert)
    idx_ref[...] = jnp.concatenate(idx_rows, axis=0).T
    g_ref[...] = jnp.concatenate(g_rows, axis=0).T


def _mix_route(x, ret_o, gate, mla_o, gnw, avg, wo, ln2, wq, keys, *, tm, row0, pheads, nkeys, topk):
    t = mla_o.shape[0]
    d = x.shape[1]
    rw = ret_o.shape[1]
    nsel = pheads * topk
    blk0 = row0 // tm
    row = lambda i: (i, 0)
    src = lambda i: (blk0 + i, 0)
    fs = lambda a: pl.BlockSpec(a.shape, lambda i: (0,) * a.ndim)
    return pl.pallas_call(
        functools.partial(_mix_route_body, rw=rw, pheads=pheads, nkeys=nkeys, topk=topk),
        grid=(t // tm,),
        in_specs=[pl.BlockSpec((tm, d), src), pl.BlockSpec((tm, rw), src), pl.BlockSpec((tm, rw), src),
                  pl.BlockSpec((tm, mla_o.shape[1]), row), fs(gnw), fs(avg), fs(wo), fs(ln2), fs(wq), fs(keys)],
        out_specs=[pl.BlockSpec((tm, d), row), pl.BlockSpec((tm, d), row),
                   pl.BlockSpec((tm, nsel), row), pl.BlockSpec((tm, nsel), row)],
        out_shape=[jax.ShapeDtypeStruct((t, d), F32), jax.ShapeDtypeStruct((t, d), F32),
                   jax.ShapeDtypeStruct((t, nsel), jnp.int32),
                   jax.ShapeDtypeStruct((t, nsel), F32)],
        compiler_params=_params("parallel"), name="mix_route",
    )(x, ret_o, gate, mla_o, gnw, avg, wo, ln2, wq, keys)


def _gelu_gate_body(hid_ref, g_ref, a_ref):
    hid = hid_ref[...]
    a_ref[...] = 0.5 * hid * (1.0 + lax.erf(hid * (2.0 ** -0.5))) * g_ref[...]


def _gelu_gate(order, hid, g, *, tm):
    t, n = hid.shape
    blk = pl.BlockSpec((tm, n), lambda i: (i, 0))
    return pl.pallas_call(
        _ordered(_gelu_gate_body), grid=(t // tm,), in_specs=[ORDER_SPEC, blk, blk], out_specs=blk,
        out_shape=jax.ShapeDtypeStruct((t, n), F32), compiler_params=_params("parallel"), name="gelu_gate",
    )(order, hid, g)


def _residual_body(h_ref, p_ref, lnf_ref, o_ref, *, final_norm):
    out = h_ref[...] + p_ref[...]
    if final_norm:
        out = _rms(out, lnf_ref[...])
    o_ref[...] = out


def _residual(order, h, peer, lnf, *, tm, final_norm):
    t, d = h.shape
    blk = pl.BlockSpec((tm, d), lambda i: (i, 0))
    return pl.pallas_call(
        _ordered(functools.partial(_residual_body, final_norm=final_norm)), grid=(t // tm,),
        in_specs=[ORDER_SPEC, blk, blk, pl.BlockSpec((1, d), lambda i: (0, 0))], out_specs=blk,
        out_shape=jax.ShapeDtypeStruct((t, d), F32), compiler_params=_params("parallel"), name="residual_norm",
    )(order, h, peer, lnf)


SC_CORES = 2
SC_SUBCORES = 16
SC_LANES = 16
SC_RING = 4
SC_BATCH = 32


def _sc_worker_id():
    return lax.axis_index("s") * SC_CORES + lax.axis_index("c")


def _sc_ring(nq, start, wait, compute):
    for s in range(SC_RING - 1):
        start(s, s)

    def step(q, s):
        nxt = q + SC_RING - 1

        @pl.when(nxt < nq)
        def _():
            start(nxt, (s + SC_RING - 1) % SC_RING)

        wait(q, s)
        compute(q, s)

    full = nq // SC_RING * SC_RING

    @pl.loop(0, full, step=SC_RING)
    def _(q0):
        for s in range(SC_RING):
            step(q0 + s, s)

    for s in range(nq - full):
        step(jnp.int32(full + s), s)


def _row_layout(d):
    return d // 2, d // 4


def _pack_rows(tab):
    nf, npk = _row_layout(tab.shape[1])
    plain = lax.bitcast_convert_type(tab[:, :nf], jnp.int32)
    bits = lax.bitcast_convert_type(tab[:, nf:].astype(jnp.bfloat16), jnp.uint16).astype(jnp.uint32)
    pairs = lax.bitcast_convert_type(bits[:, :npk] | (bits[:, npk:] << 16), jnp.int32)
    return jnp.concatenate([plain, pairs], axis=1)


def _unpack_pair(w):
    lo = lax.bitcast_convert_type(lax.shift_left(w, jnp.int32(16)), F32)
    hi = lax.bitcast_convert_type(w & jnp.int32(-65536), F32)
    return lo, hi


def _peer_hidden_sc(order, xn, idx, u_tab):
    t, d = xn.shape
    nsel = idx.shape[1]
    nw = SC_CORES * SC_SUBCORES
    per_w = t // nw
    tb = min(SC_BATCH, per_w)
    nchunk = nsel // SC_LANES
    shift = nchunk.bit_length() - 1
    nf, npk = _row_layout(d)
    roww = nf + npk
    nq = tb * nchunk
    assert per_w * nw == t and per_w % tb == 0 and nchunk == 1 << shift and nq >= SC_RING
    assert u_tab.shape[1] == roww
    mesh = plsc.VectorSubcoreMesh(core_axis_name="c", subcore_axis_name="s")

    def lanes(off):
        return pl.ds(pl.multiple_of(off, SC_LANES), SC_LANES)

    def body(_order_hbm, x_hbm, idx_hbm, u_hbm, out_hbm, idx_v, x_v, ubuf, hid_v, sem):
        wid = _sc_worker_id()
        lane = lax.iota(jnp.int32, SC_LANES)

        def gather(q, slot):
            tok = lax.shift_right_logical(q, shift)
            ch = q & (nchunk - 1)
            rows = idx_v.at[tok, pl.ds(ch * SC_LANES, SC_LANES)]
            return pltpu.make_async_copy(u_hbm.at[rows], ubuf.at[slot], sem.at[slot])

        def compute(q, slot):
            tok = lax.shift_right_logical(q, shift)
            ch = q & (nchunk - 1)

            @plsc.parallel_loop(0, nf // SC_LANES, carry=tuple(jnp.zeros((SC_LANES,), F32) for _ in range(SC_LANES)))
            def plain(c, acc):
                cs = lanes(c * SC_LANES)
                xc = x_v[tok, cs]
                return tuple(a + xc * lax.bitcast_convert_type(ubuf[slot, k, cs], F32) for k, a in enumerate(acc))

            @plsc.parallel_loop(0, npk // SC_LANES, carry=plain)
            def accs(c, acc):
                off = c * SC_LANES
                x_lo = x_v[tok, lanes(nf + off)]
                x_hi = x_v[tok, lanes(nf + npk + off)]
                new = []
                for k, a in enumerate(acc):
                    lo, hi = _unpack_pair(ubuf[slot, k, lanes(nf + off)])
                    new.append(a + x_lo * lo + x_hi * hi)
                return tuple(new)

            out = jnp.zeros((SC_LANES,), F32)
            for k in range(SC_LANES):
                out = jnp.where(lane == k, jnp.sum(accs[k]), out)
            hid_v[tok, pl.ds(ch * SC_LANES, SC_LANES)] = out

        @pl.loop(0, per_w // tb)
        def _(b):
            base = wid * per_w + b * tb
            pltpu.sync_copy(idx_hbm.at[pl.ds(base, tb)], idx_v)
            pltpu.sync_copy(x_hbm.at[pl.ds(base, tb)], x_v)
            _sc_ring(nq, lambda q, s: gather(q, s).start(), lambda q, s: gather(q, s).wait(), compute)
            pltpu.sync_copy(hid_v, out_hbm.at[pl.ds(base, tb)])

    return pl.kernel(
        body, out_type=jax.ShapeDtypeStruct((t, nsel), F32), mesh=mesh,
        scratch_types=[pltpu.VMEM((tb, nsel), jnp.int32), pltpu.VMEM((tb, d), F32),
                       pltpu.VMEM((SC_RING, SC_LANES, roww), jnp.int32), pltpu.VMEM((tb, nsel), F32),
                       pltpu.SemaphoreType.DMA((SC_RING,))],
        compiler_params=pltpu.CompilerParams(needs_layout_passes=False), name="peer_hidden_sc",
    )(order, xn, idx, u_tab)


def _peer_mix_sc(act, idx, v_tab, d):
    t, nsel = act.shape
    nf, npk = _row_layout(d)
    roww = nf + npk
    nw = SC_CORES * SC_SUBCORES
    per_w = t // nw
    tb = min(SC_BATCH, per_w)
    nchunk = nsel // SC_LANES
    shift = nchunk.bit_length() - 1
    ncol = d // SC_LANES
    nq = tb * nchunk
    assert per_w * nw == t and per_w % tb == 0 and nchunk == 1 << shift and nq >= SC_RING
    assert v_tab.shape[1] == roww
    mesh = plsc.VectorSubcoreMesh(core_axis_name="c", subcore_axis_name="s")

    def lanes(off):
        return pl.ds(pl.multiple_of(off, SC_LANES), SC_LANES)

    def body(a_hbm, idx_hbm, v_hbm, out_hbm, idx_v, a_v, vbuf, o_v, sem):
        wid = _sc_worker_id()
        zero = jnp.zeros((SC_LANES,), F32)

        def gather(q, slot):
            tok = lax.shift_right_logical(q, shift)
            ch = q & (nchunk - 1)
            rows = idx_v.at[tok, pl.ds(ch * SC_LANES, SC_LANES)]
            return pltpu.make_async_copy(v_hbm.at[rows], vbuf.at[slot], sem.at[slot])

        def compute(q, slot):
            tok = lax.shift_right_logical(q, shift)
            ch = q & (nchunk - 1)
            tok_v = jnp.full((SC_LANES,), tok, jnp.int32)
            col_v = jnp.full((SC_LANES,), ch * SC_LANES, jnp.int32)
            w = [plsc.load_gather(a_v, [tok_v, col_v + k]) for k in range(SC_LANES)]

            def tree_sum(terms):
                while len(terms) > 1:
                    terms = [a + b for a, b in zip(terms[0::2], terms[1::2])]
                return terms[0]

            @plsc.parallel_loop(0, nf // SC_LANES)
            def _(c):
                cs = lanes(c * SC_LANES)
                terms = [w[k] * lax.bitcast_convert_type(vbuf[slot, k, cs], F32) for k in range(SC_LANES)]
                o_v[tok, cs] = o_v[tok, cs] + tree_sum(terms)

            @plsc.parallel_loop(0, npk // SC_LANES)
            def _(c):
                off = c * SC_LANES
                cs_lo, cs_hi = lanes(nf + off), lanes(nf + npk + off)
                pairs = [_unpack_pair(vbuf[slot, k, cs_lo]) for k in range(SC_LANES)]
                o_v[tok, cs_lo] = o_v[tok, cs_lo] + tree_sum([w[k] * p[0] for k, p in enumerate(pairs)])
                o_v[tok, cs_hi] = o_v[tok, cs_hi] + tree_sum([w[k] * p[1] for k, p in enumerate(pairs)])

        @pl.loop(0, per_w // tb)
        def _(b):
            base = wid * per_w + b * tb
            pltpu.sync_copy(idx_hbm.at[pl.ds(base, tb)], idx_v)
            pltpu.sync_copy(a_hbm.at[pl.ds(base, tb)], a_v)

            @pl.loop(0, tb)
            def _(r):
                @pl.loop(0, ncol)
                def _(c):
                    o_v[r, pl.ds(pl.multiple_of(c * SC_LANES, SC_LANES), SC_LANES)] = zero

            _sc_ring(nq, lambda q, s: gather(q, s).start(), lambda q, s: gather(q, s).wait(), compute)
            pltpu.sync_copy(o_v, out_hbm.at[pl.ds(base, tb)])

    return pl.kernel(
        body, out_type=jax.ShapeDtypeStruct((t, d), F32), mesh=mesh,
        scratch_types=[pltpu.VMEM((tb, nsel), jnp.int32), pltpu.VMEM((tb, nsel), F32),
                       pltpu.VMEM((SC_RING, SC_LANES, roww), jnp.int32), pltpu.VMEM((tb, d), F32),
                       pltpu.SemaphoreType.DMA((SC_RING,))],
        compiler_params=pltpu.CompilerParams(needs_layout_passes=False), name="peer_mix_sc",
    )(act, idx, v_tab)


def _rope_tables(pos, half, group, width, lo):
    inv = ROPE_BASE ** (-jnp.arange(half, dtype=F32) / half)
    ang = pos.astype(F32)[:, None] * inv[None, :]
    cos, sin = jnp.cos(ang), jnp.sin(ang)
    n = pos.shape[0]
    reps = width // group
    pad_hi = group - lo - 2 * half
    blk = lambda a, b, fill: jnp.concatenate(
        [jnp.full((n, lo), fill, F32), a, b, jnp.full((n, pad_hi), fill, F32)], axis=1)
    z = jnp.zeros_like(sin)
    c = blk(cos, cos, 1.0)
    sa = blk(-sin, z, 0.0)
    sb = blk(z, sin, 0.0)
    return [jnp.tile(a, (1, reps)) for a in (c, sa, sb)]


def _ret_log_decay(nheads):
    return jnp.log(1.0 - jnp.exp2(-5.0 - jnp.arange(nheads, dtype=F32)))


def _pair_states(s):
    b, h, dk, dv = s.shape
    s = s.reshape(b, h // 2, 2, dk, dv)
    z = jnp.zeros_like(s[:, :, 0])
    top = jnp.concatenate([s[:, :, 0], z], axis=-1)
    bot = jnp.concatenate([z, s[:, :, 1]], axis=-1)
    return jnp.concatenate([top, bot], axis=-2)


def _unpair_states(sp, dk, dv):
    b, hp = sp.shape[:2]
    return jnp.stack([sp[:, :, :dk, :dv], sp[:, :, dk:, dv:]], axis=2).reshape(b, 2 * hp, dk, dv)


def _layer_weights(ln1_w, w_in, ret_gn_w, q_norm_w, w_uq, kv_norm_w, w_uk, w_uv, w_o, ln2_w,
                   peer_w_q, peer_sub_keys, dims):
    d = w_in.shape[0]
    nheads, nope, rope, vdim = dims["nheads"], dims["nope"], dims["mla_rope"], dims["vdim"]
    o6 = 4 * dims["rw"] + dims["qrank"] + dims["kvrank"]
    zc = lambda r, c: jnp.zeros((r, c), F32)
    win_p = jnp.concatenate([w_in[:, :o6], zc(d, nope), w_in[:, o6:], zc(d, LANES - nope - rope)], axis=1)
    qr, kr = w_uq.shape[0], w_uk.shape[0]
    wuq_p = jnp.concatenate([w_uq, jnp.zeros((qr, nheads, LANES - nope - rope), F32)], axis=2).reshape(qr, -1)
    wuk_p = jnp.concatenate([w_uk, jnp.zeros((kr, nheads, LANES - nope), F32)], axis=2).reshape(kr, -1)
    zv = jnp.zeros((kr, nheads // 2, LANES - vdim), F32)
    wv = w_uv.reshape(kr, nheads // 2, 2, vdim)
    wuv_p = jnp.concatenate([wv[:, :, 0], zv, zv, wv[:, :, 1]], axis=2).reshape(kr, -1)
    wuk3 = jnp.concatenate([jnp.transpose(w_uk, (1, 2, 0)),
                            jnp.zeros((nheads, LANES - nope, kr), F32)], axis=1)
    wuv3 = jnp.transpose(w_uv, (1, 0, 2))
    gidx = jnp.arange(dims["rw"]) // dims["ret_dv"]
    avg = (gidx[:, None] == gidx[None, :]).astype(F32) / dims["ret_dv"]
    keys = peer_sub_keys.reshape(-1, peer_sub_keys.shape[2], peer_sub_keys.shape[3])
    c = lambda a: a.astype(MXU_DTYPE)
    r2 = lambda a: a.reshape(1, -1)
    return dict(ln1=r2(ln1_w), win_p=c(win_p), gnw=r2(ret_gn_w), qnw=r2(q_norm_w), kvnw=r2(kv_norm_w),
                wuq_p=c(wuq_p), wuk_p=c(wuk_p), wuv_p=c(wuv_p), wuk3=c(wuk3), wuv3=c(wuv3), avg=c(avg),
                wo=c(w_o), ln2=r2(ln2_w), wq=c(peer_w_q), keys=c(keys))


class _Stream:
    def __init__(self, x, tabs, s0, *, nbatch, ret_rows, ret_chunk, tm, ranges, cache=None):
        self.x, self.tabs, self.s0, self.cache = x, tabs, s0, cache
        self.nbatch, self.ret_rows, self.ret_chunk, self.tm, self.ranges = nbatch, ret_rows, ret_chunk, tm, ranges
        self.seq = x.shape[0] // nbatch
        assert nbatch == 1 or ranges == [(0, self.seq)]
        self.pre = None
        self.outs = []


def _layer(streams, w, lg, u_tab, v_tab, lnf, dims, *, final_norm):
    units = [(st, lo, hi) for st in streams for lo, hi in st.ranges]
    n = len(units)
    nheads, dk = dims["nheads"], dims["ret_dk"]
    scale = (dims["nope"] + dims["mla_rope"]) ** -0.5
    built = [None] * n

    def build(i, order):
        st, lo, hi = units[i]
        if st.pre is None:
            proj = _inproj(order, st.x, st.tabs, w["ln1"], w["win_p"], w["qnw"], w["kvnw"], w["wuq_p"], w["wuk_p"],
                           w["wuv_p"], tm=st.tm, dims=dims)
            ret_o, s_pairs = _retention(lg, *proj[:3], _pair_states(st.s0), nbatch=st.nbatch, rows=st.ret_rows,
                                        chunk=st.ret_chunk, dk=dk)
            st.pre = list(proj) + [ret_o, s_pairs]
        qr, kr, vr, gate, qm, km, vm, ckv, kpe, ret_o, s_pairs = st.pre
        if st.cache is None:
            mla_o = _flash(order, qm, km, vm, nbatch=st.nbatch, tq=min(256, st.seq), lo=lo, hi=hi, scale=scale,
                           nheads=nheads, chunk=CHUNK)
        else:
            mla_o = _decode_attn(qm, st.cache[0], st.cache[1], ckv, kpe, w["wuk3"], w["wuv3"], nq=st.seq,
                                 nope=dims["nope"], rope=dims["mla_rope"], scale=scale)
        h, hn, idx, g = _mix_route(st.x, ret_o, gate, mla_o, w["gnw"], w["avg"], w["wo"], w["ln2"], w["wq"],
                                   w["keys"], tm=st.tm, row0=lo, pheads=dims["pheads"], nkeys=dims["nkeys"],
                                   topk=dims["topk"])
        built[i] = (h, idx, g, _peer_hidden_sc(peers[i - 2] if i >= 2 else lnf, hn, idx, u_tab))

    for i in range(min(2, n)):
        build(i, lnf)
    acts, peers = [], []
    for i in range(n):
        h, idx, g, hid = built[i]
        acts.append(_gelu_gate(built[i + 1][1] if i + 1 < n else lnf, hid, g, tm=units[i][0].tm))
        peers.append(_peer_mix_sc(acts[i], idx, v_tab, h.shape[1]))
        if i + 2 < n:
            build(i + 2, acts[i])
    for i, (st, lo, hi) in enumerate(units):
        st.outs.append(_residual(acts[min(i + 2, n - 1)], built[i][0], peers[i], lnf, tm=st.tm,
                                 final_norm=final_norm))
    nope, rope, dv = dims["nope"], dims["mla_rope"], dims["ret_dv"]
    results = []
    for st in streams:
        out = st.outs[0] if len(st.outs) == 1 else jnp.concatenate(st.outs, axis=0)
        results.append((out, st.pre[7], st.pre[8][:, nope:nope + rope], _unpair_states(st.pre[10], dk, dv)))
    return results


def kernel(x_prompt, x_sample, cache_mla_ckv, cache_mla_krope, state_retention, ln1_w, w_in, ret_gn_w,
           mla_q_norm_w, mla_w_uq, mla_kv_norm_w, mla_w_uk, mla_w_uv, w_o, ln2_w, peer_w_q, peer_sub_keys,
           peer_u, peer_v, lnf_w):
    depth = w_in.shape[0]
    nb, seq, d = x_prompt.shape
    db, dseq, _ = x_sample.shape
    past = cache_mla_ckv.shape[2]
    rheads, dk, dv = state_retention.shape[2:]
    nkeys = peer_sub_keys.shape[3]
    dims = dict(rw=rheads * dk, ret_dk=dk, ret_dv=dv, qrank=mla_w_uq.shape[1], kvrank=mla_w_uk.shape[1],
                nheads=mla_w_uq.shape[2], nope=mla_w_uk.shape[3], vdim=mla_w_uv.shape[3],
                mla_rope=mla_w_uq.shape[3] - mla_w_uk.shape[3], pheads=peer_sub_keys.shape[1], nkeys=nkeys,
                topk=PEER_TOPK)
    assert rheads * dk == rheads * dv and dims["nheads"] % 2 == 0 and dk * 2 == LANES and dims["vdim"] * 2 == LANES

    def tables(pos):
        return (_rope_tables(pos, dk // 2, dk, dims["rw"], 0)
                + _rope_tables(pos, dims["mla_rope"] // 2, LANES, LANES, dims["nope"]))

    tabs_p = tables(jnp.arange(seq))
    tabs_s = tables(jnp.tile(past + jnp.arange(dseq), db))
    lg = _ret_log_decay(rheads)
    lnf = lnf_w.reshape(1, -1)
    hp = x_prompt.reshape(nb * seq, d)
    hs = x_sample.reshape(db * dseq, d)
    outs = [[] for _ in range(6)]
    for l in range(depth):
        w = _layer_weights(ln1_w[l], w_in[l], ret_gn_w[l], mla_q_norm_w[l], mla_w_uq[l], mla_kv_norm_w[l],
                           mla_w_uk[l], mla_w_uv[l], w_o[l], ln2_w[l], peer_w_q[l], peer_sub_keys[l], dims)
        last = l == depth - 1
        gb = nb // PROMPT_GROUPS if nb % PROMPT_GROUPS == 0 else nb

        def frame_ranges(g):
            step = seq // (PROMPT_HEAD_SPLIT if g == 0 else PROMPT_SPLIT)
            ok = gb == 1 and step > 0 and step % 256 == 0
            return [(lo, lo + step) for lo in range(0, seq, step)] if ok else [(0, seq)]

        streams = [_Stream(hp[g * gb * seq:(g + 1) * gb * seq], tabs_p, jnp.zeros((gb, rheads, dk, dv), F32),
                           nbatch=gb, ret_rows=min(256, seq), ret_chunk=CHUNK, tm=min(256, gb * seq),
                           ranges=frame_ranges(g))
                   for g in range(nb // gb)]
        streams.append(_Stream(hs, tabs_s, state_retention[l], nbatch=db, ret_rows=dseq, ret_chunk=dseq,
                               tm=min(256, db * dseq), ranges=[(0, dseq)],
                               cache=(cache_mla_ckv[l], cache_mla_krope[l])))
        results = _layer(streams, w, lg, _pack_rows(peer_u[l]), _pack_rows(peer_v[l]), lnf, dims,
                         final_norm=last)
        hp, c1, k1, s1 = (jnp.concatenate(p, axis=0) for p in zip(*results[:-1]))
        hs, c2, k2, s2 = results[-1]
        for acc, val in zip(outs, (c1.reshape(nb, seq, -1), k1.reshape(nb, seq, -1), s1,
                                   c2.reshape(db, dseq, -1), k2.reshape(db, dseq, -1), s2)):
            acc.append(val)
    return (hp.reshape(nb, seq, d), hs.reshape(db, dseq, d), *[jnp.stack(o) for o in outs])
```

```python
import functools

import jax
import jax.numpy as jnp
from jax import lax
from jax.experimental import pallas as pl
from jax.experimental.pallas import tpu as pltpu
from jax.experimental.pallas import tpu_sc as plsc

EPS = 1e-6
ROPE_BASE = 10000.0
CHUNK = 64
PEER_TOPK = 16
PROMPT_GROUPS = 8
PROMPT_HEAD_SPLIT = 4
PROMPT_SPLIT = 2
LANES = 128
MXU_DTYPE = jnp.bfloat16
VMEM_LIMIT_BYTES = 56 * 1024 * 1024

F32 = jnp.float32
NEG_INF = float("-inf")


def _mm(a, b):
    return jnp.dot(a.astype(MXU_DTYPE), b.astype(MXU_DTYPE), preferred_element_type=F32)


def _mm_nt(a, b):
    return lax.dot_general(a.astype(MXU_DTYPE), b.astype(MXU_DTYPE),
                           (((1,), (1,)), ((), ())), preferred_element_type=F32)


def _mm_tn(a, b):
    return lax.dot_general(a.astype(MXU_DTYPE), b.astype(MXU_DTYPE),
                           (((0,), (0,)), ((), ())), preferred_element_type=F32)


def _rms(x, w):
    return x * lax.rsqrt(jnp.mean(x * x, axis=-1, keepdims=True) + EPS) * w


def _rope(t, c, sa, sb, half):
    n = t.shape[1]
    return t * c + pltpu.roll(t, n - half, 1) * sa + pltpu.roll(t, half, 1) * sb


def _params(*sem):
    return pltpu.CompilerParams(dimension_semantics=sem, vmem_limit_bytes=VMEM_LIMIT_BYTES)


ORDER_SPEC = pl.BlockSpec(memory_space=pl.ANY)


def _ordered(body):
    def run(_order_ref, *refs):
        body(*refs)
    return run


def _inproj_body(x_ref, ln1_ref, win_ref, cr_ref, sar_ref, sbr_ref, cm_ref, sam_ref, sbm_ref,
                 qnw_ref, kvnw_ref, wuq_ref, wuk_ref, wuv_ref,
                 qr_ref, kr_ref, vr_ref, gate_ref, qm_ref, km_ref, vm_ref, ckv_ref, kpe_ref,
                 *, rw, qrank, kvrank, ret_half, mla_half, k_scale, nheads):
    n1 = _rms(x_ref[...], ln1_ref[...])
    proj = _mm(n1, win_ref[...])
    cr, sar, sbr = cr_ref[...], sar_ref[...], sbr_ref[...]
    qr_ref[...] = _rope(proj[:, 0:rw], cr, sar, sbr, ret_half)
    kr_ref[...] = _rope(proj[:, rw:2 * rw], cr, sar, sbr, ret_half) * k_scale
    vr_ref[...] = proj[:, 2 * rw:3 * rw]
    gate_ref[...] = proj[:, 3 * rw:4 * rw]
    o4 = 4 * rw
    o5 = o4 + qrank
    o6 = o5 + kvrank
    cm, sam, sbm = cm_ref[...], sam_ref[...], sbm_ref[...]
    tile = lambda t: jnp.concatenate([t] * nheads, axis=1)
    cq = _rms(proj[:, o4:o5], qnw_ref[...])
    qm = _rope(_mm(cq, wuq_ref[...]), tile(cm), tile(sam), tile(sbm), mla_half)
    qm_ref[...] = qm.astype(qm_ref.dtype)
    ckv = _rms(proj[:, o5:o6], kvnw_ref[...])
    ckv_ref[...] = ckv
    kpe = _rope(proj[:, o6:o6 + LANES], cm, sam, sbm, mla_half)
    kpe_ref[...] = kpe
    km_ref[...] = (_mm(ckv, wuk_ref[...]) + tile(kpe)).astype(km_ref.dtype)
    vm_ref[...] = _mm(ckv, wuv_ref[...]).astype(vm_ref.dtype)


def _inproj(order, x, tabs, ln1, win_p, qnw, kvnw, wuq_p, wuk_p, wuv_p, *, tm, dims):
    t, d = x.shape
    rw, nheads = dims["rw"], dims["nheads"]
    hp = nheads * LANES
    nblk_tab = tabs[0].shape[0] // tm
    row = lambda i: (i, 0)
    tab = lambda i: (i % nblk_tab, 0)
    full = lambda i: (0, 0)
    fs = lambda a: pl.BlockSpec(a.shape, full)
    in_specs = [pl.BlockSpec((tm, d), row), fs(ln1), fs(win_p)]
    in_specs += [pl.BlockSpec((tm, rw), tab)] * 3 + [pl.BlockSpec((tm, LANES), tab)] * 3
    in_specs += [fs(qnw), fs(kvnw), fs(wuq_p), fs(wuk_p), fs(wuv_p)]
    out_shape = [jax.ShapeDtypeStruct((t, rw), F32)] * 4
    out_shape += [jax.ShapeDtypeStruct((t, hp), MXU_DTYPE)] * 3
    out_shape += [jax.ShapeDtypeStruct((t, dims["kvrank"]), F32), jax.ShapeDtypeStruct((t, LANES), F32)]
    out_specs = [pl.BlockSpec((tm, rw), row)] * 4 + [pl.BlockSpec((tm, hp), row)] * 3
    out_specs += [pl.BlockSpec((tm, dims["kvrank"]), row), pl.BlockSpec((tm, LANES), row)]
    body = functools.partial(
        _inproj_body, rw=rw, qrank=dims["qrank"], kvrank=dims["kvrank"], ret_half=dims["ret_dk"] // 2,
        mla_half=dims["mla_rope"] // 2, k_scale=dims["ret_dk"] ** -0.5, nheads=nheads)
    return pl.pallas_call(
        _ordered(body), grid=(t // tm,), in_specs=[ORDER_SPEC] + in_specs, out_specs=out_specs,
        out_shape=out_shape, compiler_params=_params("parallel"), name="inproj",
    )(order, x, ln1, win_p, *tabs, qnw, kvnw, wuq_p, wuk_p, wuv_p)


def _retention_body(lg_ref, q_ref, k_ref, v_ref, s0_ref, o_ref, sout_ref, s_scr, *, rows, chunk, dk):
    hp = pl.program_id(1)
    j = pl.program_id(2)

    @pl.when(j == 0)
    def _():
        s_scr[...] = s0_ref[0, 0]

    lane = lax.broadcasted_iota(jnp.int32, (1, LANES), 1)
    is_a = lane < dk
    lg_a = lg_ref[2 * hp]
    lg_b = lg_ref[2 * hp + 1]
    lgl = jnp.where(is_a, lg_a, lg_b)
    r = lax.broadcasted_iota(jnp.int32, (rows, 1), 0).astype(F32)
    q, k, v = q_ref[...], k_ref[...], v_ref[...]
    q_dec = q * jnp.exp(lgl * (r + 1.0))
    k_dec = k * jnp.exp(lgl * (float(rows) - 1.0 - r))
    ri = lax.broadcasted_iota(jnp.int32, (rows, rows), 0)
    ci = lax.broadcasted_iota(jnp.int32, (rows, rows), 1)
    dist = jnp.abs(ri - ci).astype(F32)
    visible = (ci // chunk) <= (ri // chunk)
    o = _mm(q_dec, s_scr[...])
    for first, lg in ((True, lg_a), (False, lg_b)):
        sel = is_a if first else jnp.logical_not(is_a)
        qh = jnp.where(sel, q, 0.0)
        vh = jnp.where(sel, v, 0.0)
        decay = jnp.where(visible, jnp.exp(lg * dist), 0.0)
        o = o + _mm(_mm_nt(qh, k) * decay, vh)
    o_ref[...] = o
    sr = lax.broadcasted_iota(jnp.int32, (LANES, LANES), 0) < dk
    sc = lax.broadcasted_iota(jnp.int32, (LANES, LANES), 1) < dk
    kv = jnp.where(sr == sc, _mm_tn(k_dec, v), 0.0)
    s_new = jnp.exp(lgl * float(rows)) * s_scr[...] + kv
    s_scr[...] = s_new

    @pl.when(j == pl.num_programs(2) - 1)
    def _():
        sout_ref[0, 0] = s_new


def _retention(lg, q, k, v, s0_pairs, *, nbatch, rows, chunk, dk):
    t, w = q.shape
    npairs = w // LANES
    nblk = t // (nbatch * rows)
    blk = pl.BlockSpec((rows, LANES), lambda b, p, j: (b * nblk + j, p))
    st = pl.BlockSpec((1, 1, LANES, LANES), lambda b, p, j: (b, p, 0, 0))
    return pl.pallas_call(
        functools.partial(_retention_body, rows=rows, chunk=chunk, dk=dk),
        grid=(nbatch, npairs, nblk),
        in_specs=[pl.BlockSpec(memory_space=pltpu.SMEM), blk, blk, blk, st],
        out_specs=[blk, st],
        out_shape=[jax.ShapeDtypeStruct((t, w), F32),
                   jax.ShapeDtypeStruct((nbatch, npairs, LANES, LANES), F32)],
        scratch_shapes=[pltpu.VMEM((LANES, LANES), F32)],
        compiler_params=_params("parallel", "parallel", "arbitrary"), name="retention",
    )(lg, q, k, v, s0_pairs)


def _flash_body(q_ref, k_ref, v_ref, o_ref, *, tq, tile0, scale, nheads, chunk):
    i = pl.program_id(1) + tile0
    ri = lax.broadcasted_iota(jnp.int32, (tq, tq), 0) // chunk
    ci = lax.broadcasted_iota(jnp.int32, (tq, tq), 1) // chunk
    visible = ci <= ri

    def head(h):
        cols = slice(h * LANES, (h + 1) * LANES)
        q = q_ref[:, cols]

        def step(j, carry, diagonal):
            m, l, acc = carry
            off = pl.multiple_of(j * tq, tq)
            s = _mm_nt(q, k_ref[pl.ds(off, tq), cols]) * scale
            if diagonal:
                s = jnp.where(visible, s, NEG_INF)
            m_new = jnp.maximum(m, jnp.max(s, axis=1, keepdims=True))
            alpha = jnp.exp(m - m_new)
            p = jnp.exp(s - m_new)
            l = alpha * l + jnp.sum(p, axis=1, keepdims=True)
            acc = alpha * acc + _mm(p, v_ref[pl.ds(off, tq), cols])
            return m_new, l, acc

        init = (jnp.full((tq, 1), NEG_INF, F32), jnp.zeros((tq, 1), F32), jnp.zeros((tq, LANES), F32))
        carry = lax.fori_loop(0, i, functools.partial(step, diagonal=False), init)
        _, l, acc = step(i, carry, True)
        return acc / l

    for p in range(nheads // 2):
        o_ref[:, p * LANES:(p + 1) * LANES] = head(2 * p) + head(2 * p + 1)


def _flash(order, qm, km, vm, *, nbatch, tq, lo, hi, scale, nheads, chunk):
    t, hp = qm.shape
    s = t // nbatch
    nq = s // tq
    tile0 = lo // tq
    nqr = (hi - lo) // tq
    ow = nheads // 2 * LANES
    return pl.pallas_call(
        _ordered(functools.partial(_flash_body, tq=tq, tile0=tile0, scale=scale, nheads=nheads, chunk=chunk)),
        grid=(nbatch, nqr),
        in_specs=[ORDER_SPEC,
                  pl.BlockSpec((tq, hp), lambda b, i: (b * nq + tile0 + i, 0)),
                  pl.BlockSpec((s, hp), lambda b, i: (b, 0)),
                  pl.BlockSpec((s, hp), lambda b, i: (b, 0))],
        out_specs=pl.BlockSpec((tq, ow), lambda b, i: (b * nqr + i, 0)),
        out_shape=jax.ShapeDtypeStruct((nbatch * (hi - lo), ow), F32),
        compiler_params=_params("parallel", "arbitrary"), name="flash_mla",
    )(order, qm, km, vm)


def _decode_attn_body(q_ref, cpast_ref, kpast_ref, cnew_ref, knew_ref, wuk_ref, wuv_ref, o_ref,
                      *, nheads, nope, rope, scale):
    c_past = cpast_ref[0]
    k_past = kpast_ref[0]
    c_new = cnew_ref[...]
    k_new = knew_ref[:, nope:nope + rope]
    outs = []
    for h in range(nheads):
        q = q_ref[:, h * LANES:(h + 1) * LANES]
        q_lat = _mm(q, wuk_ref[h])
        q_pe = q[:, nope:nope + rope]
        s_p = (_mm_nt(q_lat, c_past) + _mm_nt(q_pe, k_past)) * scale
        s_n = (_mm_nt(q_lat, c_new) + _mm_nt(q_pe, k_new)) * scale
        m = jnp.maximum(jnp.max(s_p, axis=1, keepdims=True), jnp.max(s_n, axis=1, keepdims=True))
        p_p = jnp.exp(s_p - m)
        p_n = jnp.exp(s_n - m)
        l = jnp.sum(p_p, axis=1, keepdims=True) + jnp.sum(p_n, axis=1, keepdims=True)
        o_lat = (_mm(p_p, c_past) + _mm(p_n, c_new)) / l
        outs.append(_mm(o_lat, wuv_ref[h]))
    o_ref[...] = jnp.concatenate(outs, axis=1)


def _decode_attn(qm, c_past, k_past, c_new, kpe_new, wuk3, wuv3, *, nq, nope, rope, scale):
    nb, past, kvr = c_past.shape
    nheads, _, vdim = wuv3.shape
    t, hp = qm.shape
    row = lambda b: (b, 0)
    full3 = lambda b: (0, 0, 0)
    return pl.pallas_call(
        functools.partial(_decode_attn_body, nheads=nheads, nope=nope, rope=rope, scale=scale),
        grid=(nb,),
        in_specs=[pl.BlockSpec((nq, hp), row),
                  pl.BlockSpec((1, past, kvr), lambda b: (b, 0, 0)),
                  pl.BlockSpec((1, past, rope), lambda b: (b, 0, 0)),
                  pl.BlockSpec((nq, kvr), row),
                  pl.BlockSpec((nq, LANES), row),
                  pl.BlockSpec(wuk3.shape, full3),
                  pl.BlockSpec(wuv3.shape, full3)],
        out_specs=pl.BlockSpec((nq, nheads * vdim), row),
        out_shape=jax.ShapeDtypeStruct((t, nheads * vdim), F32),
        compiler_params=_params("parallel"), name="decode_mla",
    )(qm, c_past, k_past, c_new, kpe_new, wuk3, wuv3)


def _split3(x):
    a = x.astype(MXU_DTYPE)
    r = x - a.astype(F32)
    b = r.astype(MXU_DTYPE)
    c = (r - b.astype(F32)).astype(MXU_DTYPE)
    return a, b, c


def _group_mean(x, avg):
    a, b, c = _split3(x)
    dot = lambda t: jnp.dot(t, avg, preferred_element_type=F32)
    return dot(a) + dot(b) + dot(c)


def _topk_rows(s, payload, kk):
    n = s.shape[0]
    rid = lax.broadcasted_iota(jnp.int32, s.shape, 0)
    vals, pays = [], []
    for _ in range(kk):
        mx = jnp.max(s, axis=0, keepdims=True)
        first = jnp.min(jnp.where(s == mx, rid, n), axis=0, keepdims=True)
        hit = rid == first
        vals.append(mx)
        pays.append(jnp.max(jnp.where(hit, payload, -1), axis=0, keepdims=True))
        s = jnp.where(hit, NEG_INF, s)
    return jnp.concatenate(vals, axis=0), jnp.concatenate(pays, axis=0)


def _mix_route_body(x_ref, ret_ref, gate_ref, mla_ref, gnw_ref, avg_ref, wo_ref, ln2_ref, wq_ref, keys_ref,
                    h_ref, hn_ref, idx_ref, g_ref, *, rw, pheads, nkeys, topk):
    ret = ret_ref[...]
    avg = avg_ref[...]
    mu = _group_mean(ret, avg)
    cen = ret - mu
    var = _group_mean(cen * cen, avg)
    gate = gate_ref[...]
    y = cen * lax.rsqrt(var + EPS) * gnw_ref[...] * (gate * jax.nn.sigmoid(gate))
    h = x_ref[...] + _mm(y, wo_ref[0:rw, :]) + _mm(mla_ref[...], wo_ref[rw:, :])
    h_ref[...] = h
    hn = _rms(h, ln2_ref[...])
    hn_ref[...] = hn
    qp = _mm(hn, wq_ref[...])
    kid = lax.broadcasted_iota(jnp.int32, (nkeys, qp.shape[0]), 0)
    idx_rows, g_rows = [], []
    for hd in range(pheads):
        ts, ti = [], []
        for half in range(2):
            c = (2 * hd + half) * LANES
            st = _mm_nt(keys_ref[2 * hd + half], qp[:, c:c + LANES])
            v, i = _topk_rows(st, kid, topk)
            ts.append(v)
            ti.append(i)
        cand = jnp.concatenate([ts[0][a:a + 1, :] + ts[1] for a in range(topk)], axis=0)
        cidx = jnp.concatenate([ti[0][a:a + 1, :] * nkeys + ti[1] for a in range(topk)], axis=0)
        best, expert = _topk_rows(cand, cidx, topk)
        e = jnp.exp(best - best[0:1, :])
        g_rows.append(e / jnp.sum(e, axis=0, keepdims=True))
        idx_rows.append(expert)
    idx_ref[...] = jnp.concatenate(idx_rows, axis=0).T
    g_ref[...] = jnp.concatenate(g_rows, axis=0).T


def _mix_route(x, ret_o, gate, mla_o, gnw, avg, wo, ln2, wq, keys, *, tm, row0, pheads, nkeys, topk):
    t = mla_o.shape[0]
    d = x.shape[1]
    rw = ret_o.shape[1]
    nsel = pheads * topk
    blk0 = row0 // tm
    row = lambda i: (i, 0)
    src = lambda i: (blk0 + i, 0)
    fs = lambda a: pl.BlockSpec(a.shape, lambda i: (0,) * a.ndim)
    return pl.pallas_call(
        functools.partial(_mix_route_body, rw=rw, pheads=pheads, nkeys=nkeys, topk=topk),
        grid=(t // tm,),
        in_specs=[pl.BlockSpec((tm, d), src), pl.BlockSpec((tm, rw), src), pl.BlockSpec((tm, rw), src),
                  pl.BlockSpec((tm, mla_o.shape[1]), row), fs(gnw), fs(avg), fs(wo), fs(ln2), fs(wq), fs(keys)],
        out_specs=[pl.BlockSpec((tm, d), row), pl.BlockSpec((tm, d), row),
                   pl.BlockSpec((tm, nsel), row), pl.BlockSpec((tm, nsel), row)],
        out_shape=[jax.ShapeDtypeStruct((t, d), F32), jax.ShapeDtypeStruct((t, d), F32),
                   jax.ShapeDtypeStruct((t, nsel), jnp.int32),
                   jax.ShapeDtypeStruct((t, nsel), F32)],
        compiler_params=_params("parallel"), name="mix_route",
    )(x, ret_o, gate, mla_o, gnw, avg, wo, ln2, wq, keys)


def _gelu_gate_body(hid_ref, g_ref, a_ref):
    hid = hid_ref[...]
    a_ref[...] = 0.5 * hid * (1.0 + lax.erf(hid * (2.0 ** -0.5))) * g_ref[...]


def _gelu_gate(order, hid, g, *, tm):
    t, n = hid.shape
    blk = pl.BlockSpec((tm, n), lambda i: (i, 0))
    return pl.pallas_call(
        _ordered(_gelu_gate_body), grid=(t // tm,), in_specs=[ORDER_SPEC, blk, blk], out_specs=blk,
        out_shape=jax.ShapeDtypeStruct((t, n), F32), compiler_params=_params("parallel"), name="gelu_gate",
    )(order, hid, g)


def _residual_body(h_ref, p_ref, lnf_ref, o_ref, *, final_norm):
    out = h_ref[...] + p_ref[...]
    if final_norm:
        out = _rms(out, lnf_ref[...])
    o_ref[...] = out


def _residual(order, h, peer, lnf, *, tm, final_norm):
    t, d = h.shape
    blk = pl.BlockSpec((tm, d), lambda i: (i, 0))
    return pl.pallas_call(
        _ordered(functools.partial(_residual_body, final_norm=final_norm)), grid=(t // tm,),
        in_specs=[ORDER_SPEC, blk, blk, pl.BlockSpec((1, d), lambda i: (0, 0))], out_specs=blk,
        out_shape=jax.ShapeDtypeStruct((t, d), F32), compiler_params=_params("parallel"), name="residual_norm",
    )(order, h, peer, lnf)


SC_CORES = 2
SC_SUBCORES = 16
SC_LANES = 16
SC_RING = 4
SC_BATCH = 32


def _sc_worker_id():
    return lax.axis_index("s") * SC_CORES + lax.axis_index("c")


def _sc_ring(nq, start, wait, compute):
    for s in range(SC_RING - 1):
        start(s, s)

    def step(q, s):
        nxt = q + SC_RING - 1

        @pl.when(nxt < nq)
        def _():
            start(nxt, (s + SC_RING - 1) % SC_RING)

        wait(q, s)
        compute(q, s)

    full = nq // SC_RING * SC_RING

    @pl.loop(0, full, step=SC_RING)
    def _(q0):
        for s in range(SC_RING):
            step(q0 + s, s)

    for s in range(nq - full):
        step(jnp.int32(full + s), s)


def _tree_sum(terms):
    while len(terms) > 1:
        terms = [a + b for a, b in zip(terms[0::2], terms[1::2])]
    return terms[0]


def _pack_rows(tab):
    bits = lax.bitcast_convert_type(tab.astype(jnp.bfloat16), jnp.uint16).astype(jnp.uint32)
    half = tab.shape[1] // 2
    return lax.bitcast_convert_type(bits[:, :half] | (bits[:, half:] << 16), jnp.int32)


def _unpack_pair(w):
    lo = lax.bitcast_convert_type(lax.shift_left(w, jnp.int32(16)), F32)
    hi = lax.bitcast_convert_type(w & jnp.int32(-65536), F32)
    return lo, hi


def _peer_hidden_sc(order, xn, idx, u_tab):
    t, d = xn.shape
    nsel = idx.shape[1]
    nw = SC_CORES * SC_SUBCORES
    per_w = t // nw
    tb = min(SC_BATCH, per_w)
    nchunk = nsel // SC_LANES
    shift = nchunk.bit_length() - 1
    half = d // 2
    nword = half // SC_LANES
    nq = tb * nchunk
    assert per_w * nw == t and per_w % tb == 0 and nchunk == 1 << shift and nq >= SC_RING
    assert u_tab.shape[1] == half
    mesh = plsc.VectorSubcoreMesh(core_axis_name="c", subcore_axis_name="s")

    def body(_order_hbm, x_hbm, idx_hbm, u_hbm, out_hbm, idx_v, x_v, ubuf, hid_v, tr_v, sem):
        wid = _sc_worker_id()
        lane = lax.iota(jnp.int32, SC_LANES)

        def gather(q, slot):
            tok = lax.shift_right_logical(q, shift)
            ch = q & (nchunk - 1)
            rows = idx_v.at[tok, pl.ds(ch * SC_LANES, SC_LANES)]
            return pltpu.make_async_copy(u_hbm.at[rows], ubuf.at[slot], sem.at[slot])

        def compute(q, slot):
            tok = lax.shift_right_logical(q, shift)
            ch = q & (nchunk - 1)

            @plsc.parallel_loop(0, nword, carry=tuple(jnp.zeros((SC_LANES,), F32) for _ in range(SC_LANES)))
            def accs(c, acc):
                off = pl.multiple_of(c * SC_LANES, SC_LANES)
                x_lo = x_v[tok, pl.ds(off, SC_LANES)]
                x_hi = x_v[tok, pl.ds(pl.multiple_of(half + off, SC_LANES), SC_LANES)]
                new = []
                for k, a in enumerate(acc):
                    lo, hi = _unpack_pair(ubuf[slot, k, pl.ds(off, SC_LANES)])
                    new.append(a + x_lo * lo + x_hi * hi)
                return tuple(new)

            for k in range(SC_LANES):
                tr_v[k, pl.ds(0, SC_LANES)] = accs[k]
            cols = [plsc.load_gather(tr_v, [lane, jnp.full((SC_LANES,), l, jnp.int32)]) for l in range(SC_LANES)]
            hid_v[tok, pl.ds(ch * SC_LANES, SC_LANES)] = _tree_sum(cols)

        @pl.loop(0, per_w // tb)
        def _(b):
            base = wid * per_w + b * tb
            pltpu.sync_copy(idx_hbm.at[pl.ds(base, tb)], idx_v)
            pltpu.sync_copy(x_hbm.at[pl.ds(base, tb)], x_v)
            _sc_ring(nq, lambda q, s: gather(q, s).start(), lambda q, s: gather(q, s).wait(), compute)
            pltpu.sync_copy(hid_v, out_hbm.at[pl.ds(base, tb)])

    return pl.kernel(
        body, out_type=jax.ShapeDtypeStruct((t, nsel), F32), mesh=mesh,
        scratch_types=[pltpu.VMEM((tb, nsel), jnp.int32), pltpu.VMEM((tb, d), F32),
                       pltpu.VMEM((SC_RING, SC_LANES, half), jnp.int32), pltpu.VMEM((tb, nsel), F32),
                       pltpu.VMEM((SC_LANES, SC_LANES), F32), pltpu.SemaphoreType.DMA((SC_RING,))],
        compiler_params=pltpu.CompilerParams(needs_layout_passes=False), name="peer_hidden_sc",
    )(order, xn, idx, u_tab)


def _peer_mix_sc(act, idx, v_tab):
    t, nsel = act.shape
    half = v_tab.shape[1]
    d = 2 * half
    nw = SC_CORES * SC_SUBCORES
    per_w = t // nw
    tb = min(SC_BATCH, per_w)
    nchunk = nsel // SC_LANES
    shift = nchunk.bit_length() - 1
    ncol = d // SC_LANES
    nword = half // SC_LANES
    nq = tb * nchunk
    assert per_w * nw == t and per_w % tb == 0 and nchunk == 1 << shift and nq >= SC_RING
    mesh = plsc.VectorSubcoreMesh(core_axis_name="c", subcore_axis_name="s")

    def body(a_hbm, idx_hbm, v_hbm, out_hbm, idx_v, a_v, vbuf, o_v, sem):
        wid = _sc_worker_id()
        zero = jnp.zeros((SC_LANES,), F32)

        def gather(q, slot):
            tok = lax.shift_right_logical(q, shift)
            ch = q & (nchunk - 1)
            rows = idx_v.at[tok, pl.ds(ch * SC_LANES, SC_LANES)]
            return pltpu.make_async_copy(v_hbm.at[rows], vbuf.at[slot], sem.at[slot])

        def compute(q, slot):
            tok = lax.shift_right_logical(q, shift)
            ch = q & (nchunk - 1)
            tok_v = jnp.full((SC_LANES,), tok, jnp.int32)
            col_v = jnp.full((SC_LANES,), ch * SC_LANES, jnp.int32)
            w = [plsc.load_gather(a_v, [tok_v, col_v + k]) for k in range(SC_LANES)]

            @plsc.parallel_loop(0, nword)
            def _(c):
                off = pl.multiple_of(c * SC_LANES, SC_LANES)
                cs_lo = pl.ds(off, SC_LANES)
                cs_hi = pl.ds(pl.multiple_of(half + off, SC_LANES), SC_LANES)
                pairs = [_unpack_pair(vbuf[slot, k, cs_lo]) for k in range(SC_LANES)]
                o_v[tok, cs_lo] = o_v[tok, cs_lo] + _tree_sum([w[k] * p[0] for k, p in enumerate(pairs)])
                o_v[tok, cs_hi] = o_v[tok, cs_hi] + _tree_sum([w[k] * p[1] for k, p in enumerate(pairs)])

        @pl.loop(0, per_w // tb)
        def _(b):
            base = wid * per_w + b * tb
            pltpu.sync_copy(idx_hbm.at[pl.ds(base, tb)], idx_v)
            pltpu.sync_copy(a_hbm.at[pl.ds(base, tb)], a_v)

            @pl.loop(0, tb)
            def _(r):
                @pl.loop(0, ncol)
                def _(c):
                    o_v[r, pl.ds(pl.multiple_of(c * SC_LANES, SC_LANES), SC_LANES)] = zero

            _sc_ring(nq, lambda q, s: gather(q, s).start(), lambda q, s: gather(q, s).wait(), compute)
            pltpu.sync_copy(o_v, out_hbm.at[pl.ds(base, tb)])

    return pl.kernel(
        body, out_type=jax.ShapeDtypeStruct((t, d), F32), mesh=mesh,
        scratch_types=[pltpu.VMEM((tb, nsel), jnp.int32), pltpu.VMEM((tb, nsel), F32),
                       pltpu.VMEM((SC_RING, SC_LANES, half), jnp.int32), pltpu.VMEM((tb, d), F32),
                       pltpu.SemaphoreType.DMA((SC_RING,))],
        compiler_params=pltpu.CompilerParams(needs_layout_passes=False), name="peer_mix_sc",
    )(act, idx, v_tab)


def _rope_tables(pos, half, group, width, lo):
    inv = ROPE_BASE ** (-jnp.arange(half, dtype=F32) / half)
    ang = pos.astype(F32)[:, None] * inv[None, :]
    cos, sin = jnp.cos(ang), jnp.sin(ang)
    n = pos.shape[0]
    reps = width // group
    pad_hi = group - lo - 2 * half
    blk = lambda a, b, fill: jnp.concatenate(
        [jnp.full((n, lo), fill, F32), a, b, jnp.full((n, pad_hi), fill, F32)], axis=1)
    z = jnp.zeros_like(sin)
    c = blk(cos, cos, 1.0)
    sa = blk(-sin, z, 0.0)
    sb = blk(z, sin, 0.0)
    return [jnp.tile(a, (1, reps)) for a in (c, sa, sb)]


def _ret_log_decay(nheads):
    return jnp.log(1.0 - jnp.exp2(-5.0 - jnp.arange(nheads, dtype=F32)))


def _pair_states(s):
    b, h, dk, dv = s.shape
    s = s.reshape(b, h // 2, 2, dk, dv)
    z = jnp.zeros_like(s[:, :, 0])
    top = jnp.concatenate([s[:, :, 0], z], axis=-1)
    bot = jnp.concatenate([z, s[:, :, 1]], axis=-1)
    return jnp.concatenate([top, bot], axis=-2)


def _unpair_states(sp, dk, dv):
    b, hp = sp.shape[:2]
    return jnp.stack([sp[:, :, :dk, :dv], sp[:, :, dk:, dv:]], axis=2).reshape(b, 2 * hp, dk, dv)


def _layer_weights(ln1_w, w_in, ret_gn_w, q_norm_w, w_uq, kv_norm_w, w_uk, w_uv, w_o, ln2_w,
                   peer_w_q, peer_sub_keys, dims):
    d = w_in.shape[0]
    nheads, nope, rope, vdim = dims["nheads"], dims["nope"], dims["mla_rope"], dims["vdim"]
    o6 = 4 * dims["rw"] + dims["qrank"] + dims["kvrank"]
    zc = lambda r, c: jnp.zeros((r, c), F32)
    win_p = jnp.concatenate([w_in[:, :o6], zc(d, nope), w_in[:, o6:], zc(d, LANES - nope - rope)], axis=1)
    qr, kr = w_uq.shape[0], w_uk.shape[0]
    wuq_p = jnp.concatenate([w_uq, jnp.zeros((qr, nheads, LANES - nope - rope), F32)], axis=2).reshape(qr, -1)
    wuk_p = jnp.concatenate([w_uk, jnp.zeros((kr, nheads, LANES - nope), F32)], axis=2).reshape(kr, -1)
    zv = jnp.zeros((kr, nheads // 2, LANES - vdim), F32)
    wv = w_uv.reshape(kr, nheads // 2, 2, vdim)
    wuv_p = jnp.concatenate([wv[:, :, 0], zv, zv, wv[:, :, 1]], axis=2).reshape(kr, -1)
    wuk3 = jnp.concatenate([jnp.transpose(w_uk, (1, 2, 0)),
                            jnp.zeros((nheads, LANES - nope, kr), F32)], axis=1)
    wuv3 = jnp.transpose(w_uv, (1, 0, 2))
    gidx = jnp.arange(dims["rw"]) // dims["ret_dv"]
    avg = (gidx[:, None] == gidx[None, :]).astype(F32) / dims["ret_dv"]
    keys = peer_sub_keys.reshape(-1, peer_sub_keys.shape[2], peer_sub_keys.shape[3])
    c = lambda a: a.astype(MXU_DTYPE)
    r2 = lambda a: a.reshape(1, -1)
    return dict(ln1=r2(ln1_w), win_p=c(win_p), gnw=r2(ret_gn_w), qnw=r2(q_norm_w), kvnw=r2(kv_norm_w),
                wuq_p=c(wuq_p), wuk_p=c(wuk_p), wuv_p=c(wuv_p), wuk3=c(wuk3), wuv3=c(wuv3), avg=c(avg),
                wo=c(w_o), ln2=r2(ln2_w), wq=c(peer_w_q), keys=c(keys))


class _Stream:
    def __init__(self, x, tabs, s0, *, nbatch, ret_rows, ret_chunk, tm, ranges, cache=None):
        self.x, self.tabs, self.s0, self.cache = x, tabs, s0, cache
        self.nbatch, self.ret_rows, self.ret_chunk, self.tm, self.ranges = nbatch, ret_rows, ret_chunk, tm, ranges
        self.seq = x.shape[0] // nbatch
        assert nbatch == 1 or ranges == [(0, self.seq)]
        self.pre = None
        self.outs = []


def _layer(streams, w, lg, u_tab, v_tab, lnf, dims, *, final_norm):
    units = [(st, lo, hi) for st in streams for lo, hi in st.ranges]
    n = len(units)
    nheads, dk = dims["nheads"], dims["ret_dk"]
    scale = (dims["nope"] + dims["mla_rope"]) ** -0.5
    built = [None] * n

    def build(i, order):
        st, lo, hi = units[i]
        if st.pre is None:
            proj = _inproj(order, st.x, st.tabs, w["ln1"], w["win_p"], w["qnw"], w["kvnw"], w["wuq_p"], w["wuk_p"],
                           w["wuv_p"], tm=st.tm, dims=dims)
            ret_o, s_pairs = _retention(lg, *proj[:3], _pair_states(st.s0), nbatch=st.nbatch, rows=st.ret_rows,
                                        chunk=st.ret_chunk, dk=dk)
            st.pre = list(proj) + [ret_o, s_pairs]
        qr, kr, vr, gate, qm, km, vm, ckv, kpe, ret_o, s_pairs = st.pre
        if st.cache is None:
            mla_o = _flash(order, qm, km, vm, nbatch=st.nbatch, tq=min(256, st.seq), lo=lo, hi=hi, scale=scale,
                           nheads=nheads, chunk=CHUNK)
        else:
            mla_o = _decode_attn(qm, st.cache[0], st.cache[1], ckv, kpe, w["wuk3"], w["wuv3"], nq=st.seq,
                                 nope=dims["nope"], rope=dims["mla_rope"], scale=scale)
        h, hn, idx, g = _mix_route(st.x, ret_o, gate, mla_o, w["gnw"], w["avg"], w["wo"], w["ln2"], w["wq"],
                                   w["keys"], tm=st.tm, row0=lo, pheads=dims["pheads"], nkeys=dims["nkeys"],
                                   topk=dims["topk"])
        built[i] = (h, idx, g, _peer_hidden_sc(peers[i - 2] if i >= 2 else lnf, hn, idx, u_tab))

    for i in range(min(2, n)):
        build(i, lnf)
    acts, peers = [], []
    for i in range(n):
        h, idx, g, hid = built[i]
        acts.append(_gelu_gate(built[i + 1][1] if i + 1 < n else lnf, hid, g, tm=units[i][0].tm))
        peers.append(_peer_mix_sc(acts[i], idx, v_tab))
        if i + 2 < n:
            build(i + 2, acts[i])
    for i, (st, lo, hi) in enumerate(units):
        st.outs.append(_residual(acts[min(i + 2, n - 1)], built[i][0], peers[i], lnf, tm=st.tm,
                                 final_norm=final_norm))
    nope, rope, dv = dims["nope"], dims["mla_rope"], dims["ret_dv"]
    results = []
    for st in streams:
        out = st.outs[0] if len(st.outs) == 1 else jnp.concatenate(st.outs, axis=0)
        results.append((out, st.pre[7], st.pre[8][:, nope:nope + rope], _unpair_states(st.pre[10], dk, dv)))
    return results


def kernel(x_prompt, x_sample, cache_mla_ckv, cache_mla_krope, state_retention, ln1_w, w_in, ret_gn_w,
           mla_q_norm_w, mla_w_uq, mla_kv_norm_w, mla_w_uk, mla_w_uv, w_o, ln2_w, peer_w_q, peer_sub_keys,
           peer_u, peer_v, lnf_w):
    depth = w_in.shape[0]
    nb, seq, d = x_prompt.shape
    db, dseq, _ = x_sample.shape
    past = cache_mla_ckv.shape[2]
    rheads, dk, dv = state_retention.shape[2:]
    nkeys = peer_sub_keys.shape[3]
    dims = dict(rw=rheads * dk, ret_dk=dk, ret_dv=dv, qrank=mla_w_uq.shape[1], kvrank=mla_w_uk.shape[1],
                nheads=mla_w_uq.shape[2], nope=mla_w_uk.shape[3], vdim=mla_w_uv.shape[3],
                mla_rope=mla_w_uq.shape[3] - mla_w_uk.shape[3], pheads=peer_sub_keys.shape[1], nkeys=nkeys,
                topk=PEER_TOPK)
    assert rheads * dk == rheads * dv and dims["nheads"] % 2 == 0 and dk * 2 == LANES and dims["vdim"] * 2 == LANES

    def tables(pos):
        return (_rope_tables(pos, dk // 2, dk, dims["rw"], 0)
                + _rope_tables(pos, dims["mla_rope"] // 2, LANES, LANES, dims["nope"]))

    tabs_p = tables(jnp.arange(seq))
    tabs_s = tables(jnp.tile(past + jnp.arange(dseq), db))
    lg = _ret_log_decay(rheads)
    lnf = lnf_w.reshape(1, -1)
    hp = x_prompt.reshape(nb * seq, d)
    hs = x_sample.reshape(db * dseq, d)
    outs = [[] for _ in range(6)]
    for l in range(depth):
        w = _layer_weights(ln1_w[l], w_in[l], ret_gn_w[l], mla_q_norm_w[l], mla_w_uq[l], mla_kv_norm_w[l],
                           mla_w_uk[l], mla_w_uv[l], w_o[l], ln2_w[l], peer_w_q[l], peer_sub_keys[l], dims)
        last = l == depth - 1
        gb = nb // PROMPT_GROUPS if nb % PROMPT_GROUPS == 0 else nb

        def frame_ranges(g):
            step = seq // (PROMPT_HEAD_SPLIT if g == 0 else PROMPT_SPLIT)
            ok = gb == 1 and step > 0 and step % 256 == 0
            return [(lo, lo + step) for lo in range(0, seq, step)] if ok else [(0, seq)]

        streams = [_Stream(hp[g * gb * seq:(g + 1) * gb * seq], tabs_p, jnp.zeros((gb, rheads, dk, dv), F32),
                           nbatch=gb, ret_rows=min(256, seq), ret_chunk=CHUNK, tm=min(256, gb * seq),
                           ranges=frame_ranges(g))
                   for g in range(nb // gb)]
        streams.append(_Stream(hs, tabs_s, state_retention[l], nbatch=db, ret_rows=dseq, ret_chunk=dseq,
                               tm=min(256, db * dseq), ranges=[(0, dseq)],
                               cache=(cache_mla_ckv[l], cache_mla_krope[l])))
        results = _layer(streams, w, lg, _pack_rows(peer_u[l]), _pack_rows(peer_v[l]), lnf, dims,
                         final_norm=last)
        hp, c1, k1, s1 = (jnp.concatenate(p, axis=0) for p in zip(*results[:-1]))
        hs, c2, k2, s2 = results[-1]
        for acc, val in zip(outs, (c1.reshape(nb, seq, -1), k1.reshape(nb, seq, -1), s1,
                                   c2.reshape(db, dseq, -1), k2.reshape(db, dseq, -1), s2)):
            acc.append(val)
    return (hp.reshape(nb, seq, d), hs.reshape(db, dseq, d), *[jnp.stack(o) for o in outs])
```

```python
import functools

import jax
import jax.numpy as jnp
from jax import lax
from jax.experimental import pallas as pl
from jax.experimental.pallas import tpu as pltpu
from jax.experimental.pallas import tpu_sc as plsc

EPS = 1e-6
ROPE_BASE = 10000.0
CHUNK = 64
PEER_TOPK = 16
PROMPT_GROUPS = 8
PROMPT_HEAD_SPLIT = 4
PROMPT_SPLIT = 2
LANES = 128
MXU_DTYPE = jnp.bfloat16
VMEM_LIMIT_BYTES = 56 * 1024 * 1024

F32 = jnp.float32
NEG_INF = float("-inf")


def _mm(a, b):
    return jnp.dot(a.astype(MXU_DTYPE), b.astype(MXU_DTYPE), preferred_element_type=F32)


def _mm_nt(a, b):
    return lax.dot_general(a.astype(MXU_DTYPE), b.astype(MXU_DTYPE),
                           (((1,), (1,)), ((), ())), preferred_element_type=F32)


def _mm_tn(a, b):
    return lax.dot_general(a.astype(MXU_DTYPE), b.astype(MXU_DTYPE),
                           (((0,), (0,)), ((), ())), preferred_element_type=F32)


def _rms(x, w):
    return x * lax.rsqrt(jnp.mean(x * x, axis=-1, keepdims=True) + EPS) * w


def _rope(t, c, sa, sb, half):
    n = t.shape[1]
    return t * c + pltpu.roll(t, n - half, 1) * sa + pltpu.roll(t, half, 1) * sb


def _params(*sem):
    return pltpu.CompilerParams(dimension_semantics=sem, vmem_limit_bytes=VMEM_LIMIT_BYTES)


ORDER_SPEC = pl.BlockSpec(memory_space=pl.ANY)


def _ordered(body):
    def run(_order_ref, *refs):
        body(*refs)
    return run


def _inproj_body(x_ref, ln1_ref, win_ref, cr_ref, sar_ref, sbr_ref, cm_ref, sam_ref, sbm_ref,
                 qnw_ref, kvnw_ref, wuq_ref, wuk_ref, wuv_ref,
                 qr_ref, kr_ref, vr_ref, gate_ref, qm_ref, km_ref, vm_ref, ckv_ref, kpe_ref,
                 *, rw, qrank, kvrank, ret_half, mla_half, k_scale, nheads):
    n1 = _rms(x_ref[...], ln1_ref[...])
    proj = _mm(n1, win_ref[...])
    cr, sar, sbr = cr_ref[...], sar_ref[...], sbr_ref[...]
    qr_ref[...] = _rope(proj[:, 0:rw], cr, sar, sbr, ret_half)
    kr_ref[...] = _rope(proj[:, rw:2 * rw], cr, sar, sbr, ret_half) * k_scale
    vr_ref[...] = proj[:, 2 * rw:3 * rw]
    gate_ref[...] = proj[:, 3 * rw:4 * rw]
    o4 = 4 * rw
    o5 = o4 + qrank
    o6 = o5 + kvrank
    cm, sam, sbm = cm_ref[...], sam_ref[...], sbm_ref[...]
    tile = lambda t: jnp.concatenate([t] * nheads, axis=1)
    cq = _rms(proj[:, o4:o5], qnw_ref[...])
    qm = _rope(_mm(cq, wuq_ref[...]), tile(cm), tile(sam), tile(sbm), mla_half)
    qm_ref[...] = qm.astype(qm_ref.dtype)
    ckv = _rms(proj[:, o5:o6], kvnw_ref[...])
    ckv_ref[...] = ckv
    kpe = _rope(proj[:, o6:o6 + LANES], cm, sam, sbm, mla_half)
    kpe_ref[...] = kpe
    km_ref[...] = (_mm(ckv, wuk_ref[...]) + tile(kpe)).astype(km_ref.dtype)
    vm_ref[...] = _mm(ckv, wuv_ref[...]).astype(vm_ref.dtype)


def _inproj(order, x, tabs, ln1, win_p, qnw, kvnw, wuq_p, wuk_p, wuv_p, *, tm, dims):
    t, d = x.shape
    rw, nheads = dims["rw"], dims["nheads"]
    hp = nheads * LANES
    nblk_tab = tabs[0].shape[0] // tm
    row = lambda i: (i, 0)
    tab = lambda i: (i % nblk_tab, 0)
    full = lambda i: (0, 0)
    fs = lambda a: pl.BlockSpec(a.shape, full)
    in_specs = [pl.BlockSpec((tm, d), row), fs(ln1), fs(win_p)]
    in_specs += [pl.BlockSpec((tm, rw), tab)] * 3 + [pl.BlockSpec((tm, LANES), tab)] * 3
    in_specs += [fs(qnw), fs(kvnw), fs(wuq_p), fs(wuk_p), fs(wuv_p)]
    out_shape = [jax.ShapeDtypeStruct((t, rw), F32)] * 4
    out_shape += [jax.ShapeDtypeStruct((t, hp), MXU_DTYPE)] * 3
    out_shape += [jax.ShapeDtypeStruct((t, dims["kvrank"]), F32), jax.ShapeDtypeStruct((t, LANES), F32)]
    out_specs = [pl.BlockSpec((tm, rw), row)] * 4 + [pl.BlockSpec((tm, hp), row)] * 3
    out_specs += [pl.BlockSpec((tm, dims["kvrank"]), row), pl.BlockSpec((tm, LANES), row)]
    body = functools.partial(
        _inproj_body, rw=rw, qrank=dims["qrank"], kvrank=dims["kvrank"], ret_half=dims["ret_dk"] // 2,
        mla_half=dims["mla_rope"] // 2, k_scale=dims["ret_dk"] ** -0.5, nheads=nheads)
    return pl.pallas_call(
        _ordered(body), grid=(t // tm,), in_specs=[ORDER_SPEC] + in_specs, out_specs=out_specs,
        out_shape=out_shape, compiler_params=_params("parallel"), name="inproj",
    )(order, x, ln1, win_p, *tabs, qnw, kvnw, wuq_p, wuk_p, wuv_p)


def _retention_body(lg_ref, q_ref, k_ref, v_ref, s0_ref, o_ref, sout_ref, s_scr, *, rows, chunk, dk):
    hp = pl.program_id(1)
    j = pl.program_id(2)

    @pl.when(j == 0)
    def _():
        s_scr[...] = s0_ref[0, 0]

    lane = lax.broadcasted_iota(jnp.int32, (1, LANES), 1)
    is_a = lane < dk
    lg_a = lg_ref[2 * hp]
    lg_b = lg_ref[2 * hp + 1]
    lgl = jnp.where(is_a, lg_a, lg_b)
    r = lax.broadcasted_iota(jnp.int32, (rows, 1), 0).astype(F32)
    q, k, v = q_ref[...], k_ref[...], v_ref[...]
    q_dec = q * jnp.exp(lgl * (r + 1.0))
    k_dec = k * jnp.exp(lgl * (float(rows) - 1.0 - r))
    ri = lax.broadcasted_iota(jnp.int32, (rows, rows), 0)
    ci = lax.broadcasted_iota(jnp.int32, (rows, rows), 1)
    dist = jnp.abs(ri - ci).astype(F32)
    visible = (ci // chunk) <= (ri // chunk)
    o = _mm(q_dec, s_scr[...])
    for first, lg in ((True, lg_a), (False, lg_b)):
        sel = is_a if first else jnp.logical_not(is_a)
        qh = jnp.where(sel, q, 0.0)
        vh = jnp.where(sel, v, 0.0)
        decay = jnp.where(visible, jnp.exp(lg * dist), 0.0)
        o = o + _mm(_mm_nt(qh, k) * decay, vh)
    o_ref[...] = o
    sr = lax.broadcasted_iota(jnp.int32, (LANES, LANES), 0) < dk
    sc = lax.broadcasted_iota(jnp.int32, (LANES, LANES), 1) < dk
    kv = jnp.where(sr == sc, _mm_tn(k_dec, v), 0.0)
    s_new = jnp.exp(lgl * float(rows)) * s_scr[...] + kv
    s_scr[...] = s_new

    @pl.when(j == pl.num_programs(2) - 1)
    def _():
        sout_ref[0, 0] = s_new


def _retention(lg, q, k, v, s0_pairs, *, nbatch, rows, chunk, dk):
    t, w = q.shape
    npairs = w // LANES
    nblk = t // (nbatch * rows)
    blk = pl.BlockSpec((rows, LANES), lambda b, p, j: (b * nblk + j, p))
    st = pl.BlockSpec((1, 1, LANES, LANES), lambda b, p, j: (b, p, 0, 0))
    return pl.pallas_call(
        functools.partial(_retention_body, rows=rows, chunk=chunk, dk=dk),
        grid=(nbatch, npairs, nblk),
        in_specs=[pl.BlockSpec(memory_space=pltpu.SMEM), blk, blk, blk, st],
        out_specs=[blk, st],
        out_shape=[jax.ShapeDtypeStruct((t, w), F32),
                   jax.ShapeDtypeStruct((nbatch, npairs, LANES, LANES), F32)],
        scratch_shapes=[pltpu.VMEM((LANES, LANES), F32)],
        compiler_params=_params("parallel", "parallel", "arbitrary"), name="retention",
    )(lg, q, k, v, s0_pairs)


def _flash_body(q_ref, k_ref, v_ref, o_ref, *, tq, tile0, scale, nheads, chunk):
    i = pl.program_id(1) + tile0
    ri = lax.broadcasted_iota(jnp.int32, (tq, tq), 0) // chunk
    ci = lax.broadcasted_iota(jnp.int32, (tq, tq), 1) // chunk
    visible = ci <= ri

    def head(h):
        cols = slice(h * LANES, (h + 1) * LANES)
        q = q_ref[:, cols]

        def step(j, carry, diagonal):
            m, l, acc = carry
            off = pl.multiple_of(j * tq, tq)
            s = _mm_nt(q, k_ref[pl.ds(off, tq), cols]) * scale
            if diagonal:
                s = jnp.where(visible, s, NEG_INF)
            m_new = jnp.maximum(m, jnp.max(s, axis=1, keepdims=True))
            alpha = jnp.exp(m - m_new)
            p = jnp.exp(s - m_new)
            l = alpha * l + jnp.sum(p, axis=1, keepdims=True)
            acc = alpha * acc + _mm(p, v_ref[pl.ds(off, tq), cols])
            return m_new, l, acc

        init = (jnp.full((tq, 1), NEG_INF, F32), jnp.zeros((tq, 1), F32), jnp.zeros((tq, LANES), F32))
        carry = lax.fori_loop(0, i, functools.partial(step, diagonal=False), init)
        _, l, acc = step(i, carry, True)
        return acc / l

    for p in range(nheads // 2):
        o_ref[:, p * LANES:(p + 1) * LANES] = head(2 * p) + head(2 * p + 1)


def _flash(order, qm, km, vm, *, nbatch, tq, lo, hi, scale, nheads, chunk):
    t, hp = qm.shape
    s = t // nbatch
    nq = s // tq
    tile0 = lo // tq
    nqr = (hi - lo) // tq
    ow = nheads // 2 * LANES
    return pl.pallas_call(
        _ordered(functools.partial(_flash_body, tq=tq, tile0=tile0, scale=scale, nheads=nheads, chunk=chunk)),
        grid=(nbatch, nqr),
        in_specs=[ORDER_SPEC,
                  pl.BlockSpec((tq, hp), lambda b, i: (b * nq + tile0 + i, 0)),
                  pl.BlockSpec((s, hp), lambda b, i: (b, 0)),
                  pl.BlockSpec((s, hp), lambda b, i: (b, 0))],
        out_specs=pl.BlockSpec((tq, ow), lambda b, i: (b * nqr + i, 0)),
        out_shape=jax.ShapeDtypeStruct((nbatch * (hi - lo), ow), F32),
        compiler_params=_params("parallel", "arbitrary"), name="flash_mla",
    )(order, qm, km, vm)


def _decode_attn_body(q_ref, cpast_ref, kpast_ref, cnew_ref, knew_ref, wuk_ref, wuv_ref, o_ref,
                      *, nheads, nope, rope, scale):
    c_past = cpast_ref[0]
    k_past = kpast_ref[0]
    c_new = cnew_ref[...]
    k_new = knew_ref[:, nope:nope + rope]
    outs = []
    for h in range(nheads):
        q = q_ref[:, h * LANES:(h + 1) * LANES]
        q_lat = _mm(q, wuk_ref[h])
        q_pe = q[:, nope:nope + rope]
        s_p = (_mm_nt(q_lat, c_past) + _mm_nt(q_pe, k_past)) * scale
        s_n = (_mm_nt(q_lat, c_new) + _mm_nt(q_pe, k_new)) * scale
        m = jnp.maximum(jnp.max(s_p, axis=1, keepdims=True), jnp.max(s_n, axis=1, keepdims=True))
        p_p = jnp.exp(s_p - m)
        p_n = jnp.exp(s_n - m)
        l = jnp.sum(p_p, axis=1, keepdims=True) + jnp.sum(p_n, axis=1, keepdims=True)
        o_lat = (_mm(p_p, c_past) + _mm(p_n, c_new)) / l
        outs.append(_mm(o_lat, wuv_ref[h]))
    o_ref[...] = jnp.concatenate(outs, axis=1)


def _decode_attn(qm, c_past, k_past, c_new, kpe_new, wuk3, wuv3, *, nq, nope, rope, scale):
    nb, past, kvr = c_past.shape
    nheads, _, vdim = wuv3.shape
    t, hp = qm.shape
    row = lambda b: (b, 0)
    full3 = lambda b: (0, 0, 0)
    return pl.pallas_call(
        functools.partial(_decode_attn_body, nheads=nheads, nope=nope, rope=rope, scale=scale),
        grid=(nb,),
        in_specs=[pl.BlockSpec((nq, hp), row),
                  pl.BlockSpec((1, past, kvr), lambda b: (b, 0, 0)),
                  pl.BlockSpec((1, past, rope), lambda b: (b, 0, 0)),
                  pl.BlockSpec((nq, kvr), row),
                  pl.BlockSpec((nq, LANES), row),
                  pl.BlockSpec(wuk3.shape, full3),
                  pl.BlockSpec(wuv3.shape, full3)],
        out_specs=pl.BlockSpec((nq, nheads * vdim), row),
        out_shape=jax.ShapeDtypeStruct((t, nheads * vdim), F32),
        compiler_params=_params("parallel"), name="decode_mla",
    )(qm, c_past, k_past, c_new, kpe_new, wuk3, wuv3)


def _split3(x):
    a = x.astype(MXU_DTYPE)
    r = x - a.astype(F32)
    b = r.astype(MXU_DTYPE)
    c = (r - b.astype(F32)).astype(MXU_DTYPE)
    return a, b, c


def _group_mean(x, avg):
    a, b, c = _split3(x)
    dot = lambda t: jnp.dot(t, avg, preferred_element_type=F32)
    return dot(a) + dot(b) + dot(c)


def _topk_rows(s, payload, kk):
    n = s.shape[0]
    rid = lax.broadcasted_iota(jnp.int32, s.shape, 0)
    vals, pays = [], []
    for _ in range(kk):
        mx = jnp.max(s, axis=0, keepdims=True)
        first = jnp.min(jnp.where(s == mx, rid, n), axis=0, keepdims=True)
        hit = rid == first
        vals.append(mx)
        pays.append(jnp.max(jnp.where(hit, payload, -1), axis=0, keepdims=True))
        s = jnp.where(hit, NEG_INF, s)
    return jnp.concatenate(vals, axis=0), jnp.concatenate(pays, axis=0)


def _mix_route_body(x_ref, ret_ref, gate_ref, mla_ref, gnw_ref, avg_ref, wo_ref, ln2_ref, wq_ref, keys_ref,
                    h_ref, hn_ref, idx_ref, g_ref, *, rw, pheads, nkeys, topk):
    ret = ret_ref[...]
    avg = avg_ref[...]
    mu = _group_mean(ret, avg)
    cen = ret - mu
    var = _group_mean(cen * cen, avg)
    gate = gate_ref[...]
    y = cen * lax.rsqrt(var + EPS) * gnw_ref[...] * (gate * jax.nn.sigmoid(gate))
    h = x_ref[...] + _mm(y, wo_ref[0:rw, :]) + _mm(mla_ref[...], wo_ref[rw:, :])
    h_ref[...] = h
    hn = _rms(h, ln2_ref[...])
    hn_ref[...] = hn
    qp = _mm(hn, wq_ref[...])
    kid = lax.broadcasted_iota(jnp.int32, (nkeys, qp.shape[0]), 0)
    idx_rows, g_rows = [], []
    for hd in range(pheads):
        ts, ti = [], []
        for half in range(2):
            c = (2 * hd + half) * LANES
            st = _mm_nt(keys_ref[2 * hd + half], qp[:, c:c + LANES])
            v, i = _topk_rows(st, kid, topk)
            ts.append(v)
            ti.append(i)
        cand = jnp.concatenate([ts[0][a:a + 1, :] + ts[1] for a in range(topk)], axis=0)
        cidx = jnp.concatenate([ti[0][a:a + 1, :] * nkeys + ti[1] for a in range(topk)], axis=0)
        best, expert = _topk_rows(cand, cidx, topk)
        e = jnp.exp(best - best[0:1, :])
        g_rows.append(e / jnp.sum(e, axis=0, keepdims=True))
        idx_rows.append(expert)
    idx_ref[...] = jnp.concatenate(idx_rows, axis=0).T
    g_ref[...] = jnp.concatenate(g_rows, axis=0).T


def _mix_route(x, ret_o, gate, mla_o, gnw, avg, wo, ln2, wq, keys, *, tm, row0, pheads, nkeys, topk):
    t = mla_o.shape[0]
    d = x.shape[1]
    rw = ret_o.shape[1]
    nsel = pheads * topk
    blk0 = row0 // tm
    row = lambda i: (i, 0)
    src = lambda i: (blk0 + i, 0)
    fs = lambda a: pl.BlockSpec(a.shape, lambda i: (0,) * a.ndim)
    return pl.pallas_call(
        functools.partial(_mix_route_body, rw=rw, pheads=pheads, nkeys=nkeys, topk=topk),
        grid=(t // tm,),
        in_specs=[pl.BlockSpec((tm, d), src), pl.BlockSpec((tm, rw), src), pl.BlockSpec((tm, rw), src),
                  pl.BlockSpec((tm, mla_o.shape[1]), row), fs(gnw), fs(avg), fs(wo), fs(ln2), fs(wq), fs(keys)],
        out_specs=[pl.BlockSpec((tm, d), row), pl.BlockSpec((tm, d), row),
                   pl.BlockSpec((tm, nsel), row), pl.BlockSpec((tm, nsel), row)],
        out_shape=[jax.ShapeDtypeStruct((t, d), F32), jax.ShapeDtypeStruct((t, d), F32),
                   jax.ShapeDtypeStruct((t, nsel), jnp.int32),
                   jax.ShapeDtypeStruct((t, nsel), F32)],
        compiler_params=_params("parallel"), name="mix_route",
    )(x, ret_o, gate, mla_o, gnw, avg, wo, ln2, wq, keys)


def _gelu_gate_body(part_ref, g_ref, fold_ref, a_ref):
    hid = _group_mean(part_ref[...], fold_ref[...])
    a_ref[...] = 0.5 * hid * (1.0 + lax.erf(hid * (2.0 ** -0.5))) * g_ref[...]


def _gelu_gate(order, part, g, fold, *, tm):
    t, n = g.shape
    blk = pl.BlockSpec((tm, n), lambda i: (i, 0))
    return pl.pallas_call(
        _ordered(_gelu_gate_body), grid=(t // tm,),
        in_specs=[ORDER_SPEC, pl.BlockSpec((tm, part.shape[1]), lambda i: (i, 0)), blk,
                  pl.BlockSpec(fold.shape, lambda i: (0, 0))],
        out_specs=blk,
        out_shape=jax.ShapeDtypeStruct((t, n), F32), compiler_params=_params("parallel"), name="gelu_gate",
    )(order, part, g, fold)


def _residual_body(h_ref, p_ref, lnf_ref, o_ref, *, final_norm):
    out = h_ref[...] + p_ref[...]
    if final_norm:
        out = _rms(out, lnf_ref[...])
    o_ref[...] = out


def _residual(order, h, peer, lnf, *, tm, final_norm):
    t, d = h.shape
    blk = pl.BlockSpec((tm, d), lambda i: (i, 0))
    return pl.pallas_call(
        _ordered(functools.partial(_residual_body, final_norm=final_norm)), grid=(t // tm,),
        in_specs=[ORDER_SPEC, blk, blk, pl.BlockSpec((1, d), lambda i: (0, 0))], out_specs=blk,
        out_shape=jax.ShapeDtypeStruct((t, d), F32), compiler_params=_params("parallel"), name="residual_norm",
    )(order, h, peer, lnf)


SC_CORES = 2
SC_SUBCORES = 16
SC_LANES = 16
SC_RING = 4
SC_BATCH = 32
SC_HIDDEN_BATCH = 16


def _sc_worker_id():
    return lax.axis_index("s") * SC_CORES + lax.axis_index("c")


def _sc_ring(nq, start, wait, compute):
    for s in range(SC_RING - 1):
        start(s, s)

    def step(q, s):
        nxt = q + SC_RING - 1

        @pl.when(nxt < nq)
        def _():
            start(nxt, (s + SC_RING - 1) % SC_RING)

        wait(q, s)
        compute(q, s)

    full = nq // SC_RING * SC_RING

    @pl.loop(0, full, step=SC_RING)
    def _(q0):
        for s in range(SC_RING):
            step(q0 + s, s)

    for s in range(nq - full):
        step(jnp.int32(full + s), s)


def _tree_sum(terms):
    while len(terms) > 1:
        terms = [a + b for a, b in zip(terms[0::2], terms[1::2])]
    return terms[0]


def _pack_rows(tab):
    bits = lax.bitcast_convert_type(tab.astype(jnp.bfloat16), jnp.uint16).astype(jnp.uint32)
    half = tab.shape[1] // 2
    return lax.bitcast_convert_type(bits[:, :half] | (bits[:, half:] << 16), jnp.int32)


def _unpack_pair(w):
    lo = lax.bitcast_convert_type(lax.shift_left(w, jnp.int32(16)), F32)
    hi = lax.bitcast_convert_type(w & jnp.int32(-65536), F32)
    return lo, hi


def _peer_hidden_sc(order, xn, idx, u_tab):
    t, d = xn.shape
    nsel = idx.shape[1]
    nw = SC_CORES * SC_SUBCORES
    per_w = t // nw
    tb = min(SC_HIDDEN_BATCH, per_w)
    nchunk = nsel // SC_LANES
    shift = nchunk.bit_length() - 1
    half = d // 2
    nword = half // SC_LANES
    nq = tb * nchunk
    assert per_w * nw == t and per_w % tb == 0 and nchunk == 1 << shift and nq >= SC_RING
    assert u_tab.shape[1] == half
    mesh = plsc.VectorSubcoreMesh(core_axis_name="c", subcore_axis_name="s")

    def body(_order_hbm, x_hbm, idx_hbm, u_hbm, out_hbm, idx_v, x_v, ubuf, hid_v, sem):
        wid = _sc_worker_id()

        def gather(q, slot):
            tok = lax.shift_right_logical(q, shift)
            ch = q & (nchunk - 1)
            rows = idx_v.at[tok, pl.ds(ch * SC_LANES, SC_LANES)]
            return pltpu.make_async_copy(u_hbm.at[rows], ubuf.at[slot], sem.at[slot])

        def compute(q, slot):
            tok = lax.shift_right_logical(q, shift)
            ch = q & (nchunk - 1)

            @plsc.parallel_loop(0, nword, carry=tuple(jnp.zeros((SC_LANES,), F32) for _ in range(SC_LANES)))
            def accs(c, acc):
                off = pl.multiple_of(c * SC_LANES, SC_LANES)
                x_lo = x_v[tok, pl.ds(off, SC_LANES)]
                x_hi = x_v[tok, pl.ds(pl.multiple_of(half + off, SC_LANES), SC_LANES)]
                new = []
                for k, a in enumerate(acc):
                    lo, hi = _unpack_pair(ubuf[slot, k, pl.ds(off, SC_LANES)])
                    new.append(a + x_lo * lo + x_hi * hi)
                return tuple(new)

            base = ch * (SC_LANES * SC_LANES)
            for k in range(SC_LANES):
                hid_v[tok, pl.ds(pl.multiple_of(base + k * SC_LANES, SC_LANES), SC_LANES)] = accs[k]

        @pl.loop(0, per_w // tb)
        def _(b):
            base = wid * per_w + b * tb
            pltpu.sync_copy(idx_hbm.at[pl.ds(base, tb)], idx_v)
            pltpu.sync_copy(x_hbm.at[pl.ds(base, tb)], x_v)
            _sc_ring(nq, lambda q, s: gather(q, s).start(), lambda q, s: gather(q, s).wait(), compute)
            pltpu.sync_copy(hid_v, out_hbm.at[pl.ds(base, tb)])

    return pl.kernel(
        body, out_type=jax.ShapeDtypeStruct((t, nsel * SC_LANES), F32), mesh=mesh,
        scratch_types=[pltpu.VMEM((tb, nsel), jnp.int32), pltpu.VMEM((tb, d), F32),
                       pltpu.VMEM((SC_RING, SC_LANES, half), jnp.int32), pltpu.VMEM((tb, nsel * SC_LANES), F32),
                       pltpu.SemaphoreType.DMA((SC_RING,))],
        compiler_params=pltpu.CompilerParams(needs_layout_passes=False), name="peer_hidden_sc",
    )(order, xn, idx, u_tab)


def _peer_mix_sc(act, idx, v_tab):
    t, nsel = act.shape
    half = v_tab.shape[1]
    d = 2 * half
    nw = SC_CORES * SC_SUBCORES
    per_w = t // nw
    tb = min(SC_BATCH, per_w)
    nchunk = nsel // SC_LANES
    shift = nchunk.bit_length() - 1
    ncol = d // SC_LANES
    nword = half // SC_LANES
    nq = tb * nchunk
    assert per_w * nw == t and per_w % tb == 0 and nchunk == 1 << shift and nq >= SC_RING
    mesh = plsc.VectorSubcoreMesh(core_axis_name="c", subcore_axis_name="s")

    def body(a_hbm, idx_hbm, v_hbm, out_hbm, idx_v, a_v, vbuf, o_v, sem):
        wid = _sc_worker_id()
        zero = jnp.zeros((SC_LANES,), F32)

        def gather(q, slot):
            tok = lax.shift_right_logical(q, shift)
            ch = q & (nchunk - 1)
            rows = idx_v.at[tok, pl.ds(ch * SC_LANES, SC_LANES)]
            return pltpu.make_async_copy(v_hbm.at[rows], vbuf.at[slot], sem.at[slot])

        def compute(q, slot):
            tok = lax.shift_right_logical(q, shift)
            ch = q & (nchunk - 1)
            tok_v = jnp.full((SC_LANES,), tok, jnp.int32)
            col_v = jnp.full((SC_LANES,), ch * SC_LANES, jnp.int32)
            w = [plsc.load_gather(a_v, [tok_v, col_v + k]) for k in range(SC_LANES)]

            @plsc.parallel_loop(0, nword)
            def _(c):
                off = pl.multiple_of(c * SC_LANES, SC_LANES)
                cs_lo = pl.ds(off, SC_LANES)
                cs_hi = pl.ds(pl.multiple_of(half + off, SC_LANES), SC_LANES)
                pairs = [_unpack_pair(vbuf[slot, k, cs_lo]) for k in range(SC_LANES)]
                o_v[tok, cs_lo] = o_v[tok, cs_lo] + _tree_sum([w[k] * p[0] for k, p in enumerate(pairs)])
                o_v[tok, cs_hi] = o_v[tok, cs_hi] + _tree_sum([w[k] * p[1] for k, p in enumerate(pairs)])

        @pl.loop(0, per_w // tb)
        def _(b):
            base = wid * per_w + b * tb
            pltpu.sync_copy(idx_hbm.at[pl.ds(base, tb)], idx_v)
            pltpu.sync_copy(a_hbm.at[pl.ds(base, tb)], a_v)

            @pl.loop(0, tb)
            def _(r):
                @pl.loop(0, ncol)
                def _(c):
                    o_v[r, pl.ds(pl.multiple_of(c * SC_LANES, SC_LANES), SC_LANES)] = zero

            _sc_ring(nq, lambda q, s: gather(q, s).start(), lambda q, s: gather(q, s).wait(), compute)
            pltpu.sync_copy(o_v, out_hbm.at[pl.ds(base, tb)])

    return pl.kernel(
        body, out_type=jax.ShapeDtypeStruct((t, d), F32), mesh=mesh,
        scratch_types=[pltpu.VMEM((tb, nsel), jnp.int32), pltpu.VMEM((tb, nsel), F32),
                       pltpu.VMEM((SC_RING, SC_LANES, half), jnp.int32), pltpu.VMEM((tb, d), F32),
                       pltpu.SemaphoreType.DMA((SC_RING,))],
        compiler_params=pltpu.CompilerParams(needs_layout_passes=False), name="peer_mix_sc",
    )(act, idx, v_tab)


def _rope_tables(pos, half, group, width, lo):
    inv = ROPE_BASE ** (-jnp.arange(half, dtype=F32) / half)
    ang = pos.astype(F32)[:, None] * inv[None, :]
    cos, sin = jnp.cos(ang), jnp.sin(ang)
    n = pos.shape[0]
    reps = width // group
    pad_hi = group - lo - 2 * half
    blk = lambda a, b, fill: jnp.concatenate(
        [jnp.full((n, lo), fill, F32), a, b, jnp.full((n, pad_hi), fill, F32)], axis=1)
    z = jnp.zeros_like(sin)
    c = blk(cos, cos, 1.0)
    sa = blk(-sin, z, 0.0)
    sb = blk(z, sin, 0.0)
    return [jnp.tile(a, (1, reps)) for a in (c, sa, sb)]


def _ret_log_decay(nheads):
    return jnp.log(1.0 - jnp.exp2(-5.0 - jnp.arange(nheads, dtype=F32)))


def _pair_states(s):
    b, h, dk, dv = s.shape
    s = s.reshape(b, h // 2, 2, dk, dv)
    z = jnp.zeros_like(s[:, :, 0])
    top = jnp.concatenate([s[:, :, 0], z], axis=-1)
    bot = jnp.concatenate([z, s[:, :, 1]], axis=-1)
    return jnp.concatenate([top, bot], axis=-2)


def _unpair_states(sp, dk, dv):
    b, hp = sp.shape[:2]
    return jnp.stack([sp[:, :, :dk, :dv], sp[:, :, dk:, dv:]], axis=2).reshape(b, 2 * hp, dk, dv)


def _layer_weights(ln1_w, w_in, ret_gn_w, q_norm_w, w_uq, kv_norm_w, w_uk, w_uv, w_o, ln2_w,
                   peer_w_q, peer_sub_keys, dims):
    d = w_in.shape[0]
    nheads, nope, rope, vdim = dims["nheads"], dims["nope"], dims["mla_rope"], dims["vdim"]
    o6 = 4 * dims["rw"] + dims["qrank"] + dims["kvrank"]
    zc = lambda r, c: jnp.zeros((r, c), F32)
    win_p = jnp.concatenate([w_in[:, :o6], zc(d, nope), w_in[:, o6:], zc(d, LANES - nope - rope)], axis=1)
    qr, kr = w_uq.shape[0], w_uk.shape[0]
    wuq_p = jnp.concatenate([w_uq, jnp.zeros((qr, nheads, LANES - nope - rope), F32)], axis=2).reshape(qr, -1)
    wuk_p = jnp.concatenate([w_uk, jnp.zeros((kr, nheads, LANES - nope), F32)], axis=2).reshape(kr, -1)
    zv = jnp.zeros((kr, nheads // 2, LANES - vdim), F32)
    wv = w_uv.reshape(kr, nheads // 2, 2, vdim)
    wuv_p = jnp.concatenate([wv[:, :, 0], zv, zv, wv[:, :, 1]], axis=2).reshape(kr, -1)
    wuk3 = jnp.concatenate([jnp.transpose(w_uk, (1, 2, 0)),
                            jnp.zeros((nheads, LANES - nope, kr), F32)], axis=1)
    wuv3 = jnp.transpose(w_uv, (1, 0, 2))
    gidx = jnp.arange(dims["rw"]) // dims["ret_dv"]
    avg = (gidx[:, None] == gidx[None, :]).astype(F32) / dims["ret_dv"]
    nsel = dims["pheads"] * dims["topk"]
    fold = (jnp.arange(nsel * SC_LANES)[:, None] // SC_LANES == jnp.arange(nsel)[None, :]).astype(F32)
    keys = peer_sub_keys.reshape(-1, peer_sub_keys.shape[2], peer_sub_keys.shape[3])
    c = lambda a: a.astype(MXU_DTYPE)
    r2 = lambda a: a.reshape(1, -1)
    return dict(ln1=r2(ln1_w), win_p=c(win_p), gnw=r2(ret_gn_w), qnw=r2(q_norm_w), kvnw=r2(kv_norm_w),
                wuq_p=c(wuq_p), wuk_p=c(wuk_p), wuv_p=c(wuv_p), wuk3=c(wuk3), wuv3=c(wuv3), avg=c(avg),
                wo=c(w_o), ln2=r2(ln2_w), wq=c(peer_w_q), keys=c(keys), fold=c(fold))


class _Stream:
    def __init__(self, x, tabs, s0, *, nbatch, ret_rows, ret_chunk, tm, ranges, cache=None):
        self.x, self.tabs, self.s0, self.cache = x, tabs, s0, cache
        self.nbatch, self.ret_rows, self.ret_chunk, self.tm, self.ranges = nbatch, ret_rows, ret_chunk, tm, ranges
        self.seq = x.shape[0] // nbatch
        assert nbatch == 1 or ranges == [(0, self.seq)]
        self.pre = None
        self.outs = []


def _layer(streams, w, lg, u_tab, v_tab, lnf, dims, *, final_norm):
    units = [(st, lo, hi) for st in streams for lo, hi in st.ranges]
    n = len(units)
    nheads, dk = dims["nheads"], dims["ret_dk"]
    scale = (dims["nope"] + dims["mla_rope"]) ** -0.5
    built = [None] * n

    def build(i, order):
        st, lo, hi = units[i]
        if st.pre is None:
            proj = _inproj(order, st.x, st.tabs, w["ln1"], w["win_p"], w["qnw"], w["kvnw"], w["wuq_p"], w["wuk_p"],
                           w["wuv_p"], tm=st.tm, dims=dims)
            ret_o, s_pairs = _retention(lg, *proj[:3], _pair_states(st.s0), nbatch=st.nbatch, rows=st.ret_rows,
                                        chunk=st.ret_chunk, dk=dk)
            st.pre = list(proj) + [ret_o, s_pairs]
        qr, kr, vr, gate, qm, km, vm, ckv, kpe, ret_o, s_pairs = st.pre
        if st.cache is None:
            mla_o = _flash(order, qm, km, vm, nbatch=st.nbatch, tq=min(256, st.seq), lo=lo, hi=hi, scale=scale,
                           nheads=nheads, chunk=CHUNK)
        else:
            mla_o = _decode_attn(qm, st.cache[0], st.cache[1], ckv, kpe, w["wuk3"], w["wuv3"], nq=st.seq,
                                 nope=dims["nope"], rope=dims["mla_rope"], scale=scale)
        h, hn, idx, g = _mix_route(st.x, ret_o, gate, mla_o, w["gnw"], w["avg"], w["wo"], w["ln2"], w["wq"],
                                   w["keys"], tm=st.tm, row0=lo, pheads=dims["pheads"], nkeys=dims["nkeys"],
                                   topk=dims["topk"])
        built[i] = (h, idx, g, _peer_hidden_sc(peers[i - 2] if i >= 2 else lnf, hn, idx, u_tab))

    for i in range(min(2, n)):
        build(i, lnf)
    acts, peers = [], []
    for i in range(n):
        h, idx, g, hid = built[i]
        acts.append(_gelu_gate(built[i + 1][1] if i + 1 < n else lnf, hid, g, w["fold"], tm=units[i][0].tm))
        peers.append(_peer_mix_sc(acts[i], idx, v_tab))
        if i + 2 < n:
            build(i + 2, acts[i])
    for i, (st, lo, hi) in enumerate(units):
        st.outs.append(_residual(acts[min(i + 2, n - 1)], built[i][0], peers[i], lnf, tm=st.tm,
                                 final_norm=final_norm))
    nope, rope, dv = dims["nope"], dims["mla_rope"], dims["ret_dv"]
    results = []
    for st in streams:
        out = st.outs[0] if len(st.outs) == 1 else jnp.concatenate(st.outs, axis=0)
        results.append((out, st.pre[7], st.pre[8][:, nope:nope + rope], _unpair_states(st.pre[10], dk, dv)))
    return results


def kernel(x_prompt, x_sample, cache_mla_ckv, cache_mla_krope, state_retention, ln1_w, w_in, ret_gn_w,
           mla_q_norm_w, mla_w_uq, mla_kv_norm_w, mla_w_uk, mla_w_uv, w_o, ln2_w, peer_w_q, peer_sub_keys,
           peer_u, peer_v, lnf_w):
    depth = w_in.shape[0]
    nb, seq, d = x_prompt.shape
    db, dseq, _ = x_sample.shape
    past = cache_mla_ckv.shape[2]
    rheads, dk, dv = state_retention.shape[2:]
    nkeys = peer_sub_keys.shape[3]
    dims = dict(rw=rheads * dk, ret_dk=dk, ret_dv=dv, qrank=mla_w_uq.shape[1], kvrank=mla_w_uk.shape[1],
                nheads=mla_w_uq.shape[2], nope=mla_w_uk.shape[3], vdim=mla_w_uv.shape[3],
                mla_rope=mla_w_uq.shape[3] - mla_w_uk.shape[3], pheads=peer_sub_keys.shape[1], nkeys=nkeys,
                topk=PEER_TOPK)
    assert rheads * dk == rheads * dv and dims["nheads"] % 2 == 0 and dk * 2 == LANES and dims["vdim"] * 2 == LANES

    def tables(pos):
        return (_rope_tables(pos, dk // 2, dk, dims["rw"], 0)
                + _rope_tables(pos, dims["mla_rope"] // 2, LANES, LANES, dims["nope"]))

    tabs_p = tables(jnp.arange(seq))
    tabs_s = tables(jnp.tile(past + jnp.arange(dseq), db))
    lg = _ret_log_decay(rheads)
    lnf = lnf_w.reshape(1, -1)
    hp = x_prompt.reshape(nb * seq, d)
    hs = x_sample.reshape(db * dseq, d)
    outs = [[] for _ in range(6)]
    for l in range(depth):
        w = _layer_weights(ln1_w[l], w_in[l], ret_gn_w[l], mla_q_norm_w[l], mla_w_uq[l], mla_kv_norm_w[l],
                           mla_w_uk[l], mla_w_uv[l], w_o[l], ln2_w[l], peer_w_q[l], peer_sub_keys[l], dims)
        last = l == depth - 1
        gb = nb // PROMPT_GROUPS if nb % PROMPT_GROUPS == 0 else nb

        def frame_ranges(g):
            step = seq // (PROMPT_HEAD_SPLIT if g == 0 else PROMPT_SPLIT)
            ok = gb == 1 and step > 0 and step % 256 == 0
            return [(lo, lo + step) for lo in range(0, seq, step)] if ok else [(0, seq)]

        streams = [_Stream(hp[g * gb * seq:(g + 1) * gb * seq], tabs_p, jnp.zeros((gb, rheads, dk, dv), F32),
                           nbatch=gb, ret_rows=min(256, seq), ret_chunk=CHUNK, tm=min(256, gb * seq),
                           ranges=frame_ranges(g))
                   for g in range(nb // gb)]
        streams.append(_Stream(hs, tabs_s, state_retention[l], nbatch=db, ret_rows=dseq, ret_chunk=dseq,
                               tm=min(256, db * dseq), ranges=[(0, dseq)],
                               cache=(cache_mla_ckv[l], cache_mla_krope[l])))
        results = _layer(streams, w, lg, _pack_rows(peer_u[l]), _pack_rows(peer_v[l]), lnf, dims,
                         final_norm=last)
        hp, c1, k1, s1 = (jnp.concatenate(p, axis=0) for p in zip(*results[:-1]))
        hs, c2, k2, s2 = results[-1]
        for acc, val in zip(outs, (c1.reshape(nb, seq, -1), k1.reshape(nb, seq, -1), s1,
                                   c2.reshape(db, dseq, -1), k2.reshape(db, dseq, -1), s2)):
            acc.append(val)
    return (hp.reshape(nb, seq, d), hs.reshape(db, dseq, d), *[jnp.stack(o) for o in outs])
```

```python
import functools

import jax
import jax.numpy as jnp
from jax import lax
from jax.experimental import pallas as pl
from jax.experimental.pallas import tpu as pltpu
from jax.experimental.pallas import tpu_sc as plsc

EPS = 1e-6
ROPE_BASE = 10000.0
CHUNK = 64
PEER_TOPK = 16
PROMPT_GROUPS = 8
PROMPT_HEAD_SPLIT = 8
PROMPT_SPLIT = 2
LANES = 128
MXU_DTYPE = jnp.bfloat16
VMEM_LIMIT_BYTES = 56 * 1024 * 1024

F32 = jnp.float32
NEG_INF = float("-inf")


def _mm(a, b):
    return jnp.dot(a.astype(MXU_DTYPE), b.astype(MXU_DTYPE), preferred_element_type=F32)


def _mm_nt(a, b):
    return lax.dot_general(a.astype(MXU_DTYPE), b.astype(MXU_DTYPE),
                           (((1,), (1,)), ((), ())), preferred_element_type=F32)


def _mm_tn(a, b):
    return lax.dot_general(a.astype(MXU_DTYPE), b.astype(MXU_DTYPE),
                           (((0,), (0,)), ((), ())), preferred_element_type=F32)


def _rms(x, w):
    return x * lax.rsqrt(jnp.mean(x * x, axis=-1, keepdims=True) + EPS) * w


def _rope(t, c, sa, sb, half):
    n = t.shape[1]
    return t * c + pltpu.roll(t, n - half, 1) * sa + pltpu.roll(t, half, 1) * sb


def _params(*sem):
    return pltpu.CompilerParams(dimension_semantics=sem, vmem_limit_bytes=VMEM_LIMIT_BYTES)


ORDER_SPEC = pl.BlockSpec(memory_space=pl.ANY)


def _ordered(body):
    def run(_order_ref, *refs):
        body(*refs)
    return run


def _inproj_body(x_ref, ln1_ref, win_ref, cr_ref, sar_ref, sbr_ref, cm_ref, sam_ref, sbm_ref,
                 qnw_ref, kvnw_ref, wuq_ref, wuk_ref, wuv_ref,
                 qr_ref, kr_ref, vr_ref, gate_ref, qm_ref, km_ref, vm_ref, ckv_ref, kpe_ref,
                 *, rw, qrank, kvrank, ret_half, mla_half, k_scale, nheads):
    n1 = _rms(x_ref[...], ln1_ref[...])
    proj = _mm(n1, win_ref[...])
    cr, sar, sbr = cr_ref[...], sar_ref[...], sbr_ref[...]
    qr_ref[...] = _rope(proj[:, 0:rw], cr, sar, sbr, ret_half)
    kr_ref[...] = _rope(proj[:, rw:2 * rw], cr, sar, sbr, ret_half) * k_scale
    vr_ref[...] = proj[:, 2 * rw:3 * rw]
    gate_ref[...] = proj[:, 3 * rw:4 * rw]
    o4 = 4 * rw
    o5 = o4 + qrank
    o6 = o5 + kvrank
    cm, sam, sbm = cm_ref[...], sam_ref[...], sbm_ref[...]
    tile = lambda t: jnp.concatenate([t] * nheads, axis=1)
    cq = _rms(proj[:, o4:o5], qnw_ref[...])
    qm = _rope(_mm(cq, wuq_ref[...]), tile(cm), tile(sam), tile(sbm), mla_half)
    qm_ref[...] = qm.astype(qm_ref.dtype)
    ckv = _rms(proj[:, o5:o6], kvnw_ref[...])
    ckv_ref[...] = ckv
    kpe = _rope(proj[:, o6:o6 + LANES], cm, sam, sbm, mla_half)
    kpe_ref[...] = kpe
    km_ref[...] = (_mm(ckv, wuk_ref[...]) + tile(kpe)).astype(km_ref.dtype)
    vm_ref[...] = _mm(ckv, wuv_ref[...]).astype(vm_ref.dtype)


def _inproj(order, x, tabs, ln1, win_p, qnw, kvnw, wuq_p, wuk_p, wuv_p, *, tm, dims):
    t, d = x.shape
    rw, nheads = dims["rw"], dims["nheads"]
    hp = nheads * LANES
    nblk_tab = tabs[0].shape[0] // tm
    row = lambda i: (i, 0)
    tab = lambda i: (i % nblk_tab, 0)
    full = lambda i: (0, 0)
    fs = lambda a: pl.BlockSpec(a.shape, full)
    in_specs = [pl.BlockSpec((tm, d), row), fs(ln1), fs(win_p)]
    in_specs += [pl.BlockSpec((tm, rw), tab)] * 3 + [pl.BlockSpec((tm, LANES), tab)] * 3
    in_specs += [fs(qnw), fs(kvnw), fs(wuq_p), fs(wuk_p), fs(wuv_p)]
    out_shape = [jax.ShapeDtypeStruct((t, rw), F32)] * 4
    out_shape += [jax.ShapeDtypeStruct((t, hp), MXU_DTYPE)] * 3
    out_shape += [jax.ShapeDtypeStruct((t, dims["kvrank"]), F32), jax.ShapeDtypeStruct((t, LANES), F32)]
    out_specs = [pl.BlockSpec((tm, rw), row)] * 4 + [pl.BlockSpec((tm, hp), row)] * 3
    out_specs += [pl.BlockSpec((tm, dims["kvrank"]), row), pl.BlockSpec((tm, LANES), row)]
    body = functools.partial(
        _inproj_body, rw=rw, qrank=dims["qrank"], kvrank=dims["kvrank"], ret_half=dims["ret_dk"] // 2,
        mla_half=dims["mla_rope"] // 2, k_scale=dims["ret_dk"] ** -0.5, nheads=nheads)
    return pl.pallas_call(
        _ordered(body), grid=(t // tm,), in_specs=[ORDER_SPEC] + in_specs, out_specs=out_specs,
        out_shape=out_shape, compiler_params=_params("parallel"), name="inproj",
    )(order, x, ln1, win_p, *tabs, qnw, kvnw, wuq_p, wuk_p, wuv_p)


def _retention_body(lg_ref, q_ref, k_ref, v_ref, s0_ref, o_ref, sout_ref, s_scr, *, rows, chunk, dk):
    hp = pl.program_id(1)
    j = pl.program_id(2)

    @pl.when(j == 0)
    def _():
        s_scr[...] = s0_ref[0, 0]

    lane = lax.broadcasted_iota(jnp.int32, (1, LANES), 1)
    is_a = lane < dk
    lg_a = lg_ref[2 * hp]
    lg_b = lg_ref[2 * hp + 1]
    lgl = jnp.where(is_a, lg_a, lg_b)
    r = lax.broadcasted_iota(jnp.int32, (rows, 1), 0).astype(F32)
    q, k, v = q_ref[...], k_ref[...], v_ref[...]
    q_dec = q * jnp.exp(lgl * (r + 1.0))
    k_dec = k * jnp.exp(lgl * (float(rows) - 1.0 - r))
    ri = lax.broadcasted_iota(jnp.int32, (rows, rows), 0)
    ci = lax.broadcasted_iota(jnp.int32, (rows, rows), 1)
    dist = jnp.abs(ri - ci).astype(F32)
    visible = (ci // chunk) <= (ri // chunk)
    o = _mm(q_dec, s_scr[...])
    for first, lg in ((True, lg_a), (False, lg_b)):
        sel = is_a if first else jnp.logical_not(is_a)
        qh = jnp.where(sel, q, 0.0)
        vh = jnp.where(sel, v, 0.0)
        decay = jnp.where(visible, jnp.exp(lg * dist), 0.0)
        o = o + _mm(_mm_nt(qh, k) * decay, vh)
    o_ref[...] = o
    sr = lax.broadcasted_iota(jnp.int32, (LANES, LANES), 0) < dk
    sc = lax.broadcasted_iota(jnp.int32, (LANES, LANES), 1) < dk
    kv = jnp.where(sr == sc, _mm_tn(k_dec, v), 0.0)
    s_new = jnp.exp(lgl * float(rows)) * s_scr[...] + kv
    s_scr[...] = s_new

    @pl.when(j == pl.num_programs(2) - 1)
    def _():
        sout_ref[0, 0] = s_new


def _retention(lg, q, k, v, s0_pairs, *, nbatch, rows, chunk, dk):
    t, w = q.shape
    npairs = w // LANES
    nblk = t // (nbatch * rows)
    blk = pl.BlockSpec((rows, LANES), lambda b, p, j: (b * nblk + j, p))
    st = pl.BlockSpec((1, 1, LANES, LANES), lambda b, p, j: (b, p, 0, 0))
    return pl.pallas_call(
        functools.partial(_retention_body, rows=rows, chunk=chunk, dk=dk),
        grid=(nbatch, npairs, nblk),
        in_specs=[pl.BlockSpec(memory_space=pltpu.SMEM), blk, blk, blk, st],
        out_specs=[blk, st],
        out_shape=[jax.ShapeDtypeStruct((t, w), F32),
                   jax.ShapeDtypeStruct((nbatch, npairs, LANES, LANES), F32)],
        scratch_shapes=[pltpu.VMEM((LANES, LANES), F32)],
        compiler_params=_params("parallel", "parallel", "arbitrary"), name="retention",
    )(lg, q, k, v, s0_pairs)


def _flash_body(q_ref, k_ref, v_ref, o_ref, *, tq, tile0, scale, nheads, chunk):
    i = pl.program_id(1) + tile0
    ri = lax.broadcasted_iota(jnp.int32, (tq, tq), 0) // chunk
    ci = lax.broadcasted_iota(jnp.int32, (tq, tq), 1) // chunk
    visible = ci <= ri

    def head(h):
        cols = slice(h * LANES, (h + 1) * LANES)
        q = q_ref[:, cols]

        def step(j, carry, diagonal):
            m, l, acc = carry
            off = pl.multiple_of(j * tq, tq)
            s = _mm_nt(q, k_ref[pl.ds(off, tq), cols]) * scale
            if diagonal:
                s = jnp.where(visible, s, NEG_INF)
            m_new = jnp.maximum(m, jnp.max(s, axis=1, keepdims=True))
            alpha = jnp.exp(m - m_new)
            p = jnp.exp(s - m_new)
            l = alpha * l + jnp.sum(p, axis=1, keepdims=True)
            acc = alpha * acc + _mm(p, v_ref[pl.ds(off, tq), cols])
            return m_new, l, acc

        init = (jnp.full((tq, 1), NEG_INF, F32), jnp.zeros((tq, 1), F32), jnp.zeros((tq, LANES), F32))
        carry = lax.fori_loop(0, i, functools.partial(step, diagonal=False), init)
        _, l, acc = step(i, carry, True)
        return acc / l

    for p in range(nheads // 2):
        o_ref[:, p * LANES:(p + 1) * LANES] = head(2 * p) + head(2 * p + 1)


def _flash(order, qm, km, vm, *, nbatch, tq, lo, hi, scale, nheads, chunk):
    t, hp = qm.shape
    s = t // nbatch
    nq = s // tq
    tile0 = lo // tq
    nqr = (hi - lo) // tq
    ow = nheads // 2 * LANES
    return pl.pallas_call(
        _ordered(functools.partial(_flash_body, tq=tq, tile0=tile0, scale=scale, nheads=nheads, chunk=chunk)),
        grid=(nbatch, nqr),
        in_specs=[ORDER_SPEC,
                  pl.BlockSpec((tq, hp), lambda b, i: (b * nq + tile0 + i, 0)),
                  pl.BlockSpec((s, hp), lambda b, i: (b, 0)),
                  pl.BlockSpec((s, hp), lambda b, i: (b, 0))],
        out_specs=pl.BlockSpec((tq, ow), lambda b, i: (b * nqr + i, 0)),
        out_shape=jax.ShapeDtypeStruct((nbatch * (hi - lo), ow), F32),
        compiler_params=_params("parallel", "arbitrary"), name="flash_mla",
    )(order, qm, km, vm)


def _decode_attn_body(q_ref, cpast_ref, kpast_ref, cnew_ref, knew_ref, wuk_ref, wuv_ref, o_ref,
                      *, nheads, nope, rope, scale):
    c_past = cpast_ref[0]
    k_past = kpast_ref[0]
    c_new = cnew_ref[...]
    k_new = knew_ref[:, nope:nope + rope]
    outs = []
    for h in range(nheads):
        q = q_ref[:, h * LANES:(h + 1) * LANES]
        q_lat = _mm(q, wuk_ref[h])
        q_pe = q[:, nope:nope + rope]
        s_p = (_mm_nt(q_lat, c_past) + _mm_nt(q_pe, k_past)) * scale
        s_n = (_mm_nt(q_lat, c_new) + _mm_nt(q_pe, k_new)) * scale
        m = jnp.maximum(jnp.max(s_p, axis=1, keepdims=True), jnp.max(s_n, axis=1, keepdims=True))
        p_p = jnp.exp(s_p - m)
        p_n = jnp.exp(s_n - m)
        l = jnp.sum(p_p, axis=1, keepdims=True) + jnp.sum(p_n, axis=1, keepdims=True)
        o_lat = (_mm(p_p, c_past) + _mm(p_n, c_new)) / l
        outs.append(_mm(o_lat, wuv_ref[h]))
    o_ref[...] = jnp.concatenate(outs, axis=1)


def _decode_attn(qm, c_past, k_past, c_new, kpe_new, wuk3, wuv3, *, nq, nope, rope, scale):
    nb, past, kvr = c_past.shape
    nheads, _, vdim = wuv3.shape
    t, hp = qm.shape
    row = lambda b: (b, 0)
    full3 = lambda b: (0, 0, 0)
    return pl.pallas_call(
        functools.partial(_decode_attn_body, nheads=nheads, nope=nope, rope=rope, scale=scale),
        grid=(nb,),
        in_specs=[pl.BlockSpec((nq, hp), row),
                  pl.BlockSpec((1, past, kvr), lambda b: (b, 0, 0)),
                  pl.BlockSpec((1, past, rope), lambda b: (b, 0, 0)),
                  pl.BlockSpec((nq, kvr), row),
                  pl.BlockSpec((nq, LANES), row),
                  pl.BlockSpec(wuk3.shape, full3),
                  pl.BlockSpec(wuv3.shape, full3)],
        out_specs=pl.BlockSpec((nq, nheads * vdim), row),
        out_shape=jax.ShapeDtypeStruct((t, nheads * vdim), F32),
        compiler_params=_params("parallel"), name="decode_mla",
    )(qm, c_past, k_past, c_new, kpe_new, wuk3, wuv3)


def _split3(x):
    a = x.astype(MXU_DTYPE)
    r = x - a.astype(F32)
    b = r.astype(MXU_DTYPE)
    c = (r - b.astype(F32)).astype(MXU_DTYPE)
    return a, b, c


def _group_mean(x, avg):
    a, b, c = _split3(x)
    dot = lambda t: jnp.dot(t, avg, preferred_element_type=F32)
    return dot(a) + dot(b) + dot(c)


def _topk_rows(s, payload, kk):
    n = s.shape[0]
    rid = lax.broadcasted_iota(jnp.int32, s.shape, 0)
    vals, pays = [], []
    for _ in range(kk):
        mx = jnp.max(s, axis=0, keepdims=True)
        first = jnp.min(jnp.where(s == mx, rid, n), axis=0, keepdims=True)
        hit = rid == first
        vals.append(mx)
        pays.append(jnp.max(jnp.where(hit, payload, -1), axis=0, keepdims=True))
        s = jnp.where(hit, NEG_INF, s)
    return jnp.concatenate(vals, axis=0), jnp.concatenate(pays, axis=0)


def _mix_route_body(x_ref, ret_ref, gate_ref, mla_ref, gnw_ref, avg_ref, wo_ref, ln2_ref, wq_ref, keys_ref,
                    h_ref, hn_ref, idx_ref, g_ref, *, rw, pheads, nkeys, topk):
    ret = ret_ref[...]
    avg = avg_ref[...]
    mu = _group_mean(ret, avg)
    cen = ret - mu
    var = _group_mean(cen * cen, avg)
    gate = gate_ref[...]
    y = cen * lax.rsqrt(var + EPS) * gnw_ref[...] * (gate * jax.nn.sigmoid(gate))
    h = x_ref[...] + _mm(y, wo_ref[0:rw, :]) + _mm(mla_ref[...], wo_ref[rw:, :])
    h_ref[...] = h
    hn = _rms(h, ln2_ref[...])
    hn_ref[...] = hn
    qp = _mm(hn, wq_ref[...])
    kid = lax.broadcasted_iota(jnp.int32, (nkeys, qp.shape[0]), 0)
    idx_rows, g_rows = [], []
    for hd in range(pheads):
        ts, ti = [], []
        for half in range(2):
            c = (2 * hd + half) * LANES
            st = _mm_nt(keys_ref[2 * hd + half], qp[:, c:c + LANES])
            v, i = _topk_rows(st, kid, topk)
            ts.append(v)
            ti.append(i)
        cand = jnp.concatenate([ts[0][a:a + 1, :] + ts[1] for a in range(topk)], axis=0)
        cidx = jnp.concatenate([ti[0][a:a + 1, :] * nkeys + ti[1] for a in range(topk)], axis=0)
        best, expert = _topk_rows(cand, cidx, topk)
        e = jnp.exp(best - best[0:1, :])
        g_rows.append(e / jnp.sum(e, axis=0, keepdims=True))
        idx_rows.append(expert)
    idx_ref[...] = jnp.concatenate(idx_rows, axis=0).T
    g_ref[...] = jnp.concatenate(g_rows, axis=0).T


def _mix_route(x, ret_o, gate, mla_o, gnw, avg, wo, ln2, wq, keys, *, tm, row0, pheads, nkeys, topk):
    t = mla_o.shape[0]
    d = x.shape[1]
    rw = ret_o.shape[1]
    nsel = pheads * topk
    blk0 = row0 // tm
    row = lambda i: (i, 0)
    src = lambda i: (blk0 + i, 0)
    fs = lambda a: pl.BlockSpec(a.shape, lambda i: (0,) * a.ndim)
    return pl.pallas_call(
        functools.partial(_mix_route_body, rw=rw, pheads=pheads, nkeys=nkeys, topk=topk),
        grid=(t // tm,),
        in_specs=[pl.BlockSpec((tm, d), src), pl.BlockSpec((tm, rw), src), pl.BlockSpec((tm, rw), src),
                  pl.BlockSpec((tm, mla_o.shape[1]), row), fs(gnw), fs(avg), fs(wo), fs(ln2), fs(wq), fs(keys)],
        out_specs=[pl.BlockSpec((tm, d), row), pl.BlockSpec((tm, d), row),
                   pl.BlockSpec((tm, nsel), row), pl.BlockSpec((tm, nsel), row)],
        out_shape=[jax.ShapeDtypeStruct((t, d), F32), jax.ShapeDtypeStruct((t, d), F32),
                   jax.ShapeDtypeStruct((t, nsel), jnp.int32),
                   jax.ShapeDtypeStruct((t, nsel), F32)],
        compiler_params=_params("parallel"), name="mix_route",
    )(x, ret_o, gate, mla_o, gnw, avg, wo, ln2, wq, keys)


def _gelu_gate_body(hid_ref, g_ref, a_ref):
    hid = hid_ref[...]
    a_ref[...] = 0.5 * hid * (1.0 + lax.erf(hid * (2.0 ** -0.5))) * g_ref[...]


def _gelu_gate(order, hid, g, *, tm):
    t, n = hid.shape
    blk = pl.BlockSpec((tm, n), lambda i: (i, 0))
    return pl.pallas_call(
        _ordered(_gelu_gate_body), grid=(t // tm,), in_specs=[ORDER_SPEC, blk, blk], out_specs=blk,
        out_shape=jax.ShapeDtypeStruct((t, n), F32), compiler_params=_params("parallel"), name="gelu_gate",
    )(order, hid, g)


def _residual_body(h_ref, p_ref, lnf_ref, o_ref, *, final_norm):
    out = h_ref[...] + p_ref[...]
    if final_norm:
        out = _rms(out, lnf_ref[...])
    o_ref[...] = out


def _residual(order, h, peer, lnf, *, tm, final_norm):
    t, d = h.shape
    blk = pl.BlockSpec((tm, d), lambda i: (i, 0))
    return pl.pallas_call(
        _ordered(functools.partial(_residual_body, final_norm=final_norm)), grid=(t // tm,),
        in_specs=[ORDER_SPEC, blk, blk, pl.BlockSpec((1, d), lambda i: (0, 0))], out_specs=blk,
        out_shape=jax.ShapeDtypeStruct((t, d), F32), compiler_params=_params("parallel"), name="residual_norm",
    )(order, h, peer, lnf)


SC_CORES = 2
SC_SUBCORES = 16
SC_LANES = 16
SC_RING = 4
SC_BATCH = 64


def _sc_worker_id():
    return lax.axis_index("s") * SC_CORES + lax.axis_index("c")


def _sc_ring(nq, start, wait, compute):
    for s in range(SC_RING - 1):
        start(s, s)

    def step(q, s):
        nxt = q + SC_RING - 1

        @pl.when(nxt < nq)
        def _():
            start(nxt, (s + SC_RING - 1) % SC_RING)

        wait(q, s)
        compute(q, s)

    full = nq // SC_RING * SC_RING

    @pl.loop(0, full, step=SC_RING)
    def _(q0):
        for s in range(SC_RING):
            step(q0 + s, s)

    for s in range(nq - full):
        step(jnp.int32(full + s), s)


def _tree_sum(terms):
    while len(terms) > 1:
        terms = [a + b for a, b in zip(terms[0::2], terms[1::2])]
    return terms[0]


def _pack_rows(tab):
    bits = lax.bitcast_convert_type(tab.astype(jnp.bfloat16), jnp.uint16).astype(jnp.uint32)
    half = tab.shape[1] // 2
    return lax.bitcast_convert_type(bits[:, :half] | (bits[:, half:] << 16), jnp.int32)


def _unpack_pair(w):
    lo = lax.bitcast_convert_type(lax.shift_left(w, jnp.int32(16)), F32)
    hi = lax.bitcast_convert_type(w & jnp.int32(-65536), F32)
    return lo, hi


def _peer_hidden_sc(order, xn, idx, u_tab):
    t, d = xn.shape
    nsel = idx.shape[1]
    nw = SC_CORES * SC_SUBCORES
    per_w = t // nw
    tb = min(SC_BATCH, per_w)
    nchunk = nsel // SC_LANES
    shift = nchunk.bit_length() - 1
    half = d // 2
    nword = half // SC_LANES
    nq = tb * nchunk
    assert per_w * nw == t and per_w % tb == 0 and nchunk == 1 << shift and nq >= SC_RING
    assert u_tab.shape[1] == half
    mesh = plsc.VectorSubcoreMesh(core_axis_name="c", subcore_axis_name="s")

    def body(_order_hbm, x_hbm, idx_hbm, u_hbm, out_hbm, idx_v, x_v, ubuf, hid_v, sem):
        wid = _sc_worker_id()
        lane = lax.iota(jnp.int32, SC_LANES)

        def gather(q, slot):
            tok = lax.shift_right_logical(q, shift)
            ch = q & (nchunk - 1)
            rows = idx_v.at[tok, pl.ds(ch * SC_LANES, SC_LANES)]
            return pltpu.make_async_copy(u_hbm.at[rows], ubuf.at[slot], sem.at[slot])

        def compute(q, slot):
            tok = lax.shift_right_logical(q, shift)
            ch = q & (nchunk - 1)

            @plsc.parallel_loop(0, nword, carry=tuple(jnp.zeros((SC_LANES,), F32) for _ in range(SC_LANES)))
            def accs(c, acc):
                off = pl.multiple_of(c * SC_LANES, SC_LANES)
                x_lo = x_v[tok, pl.ds(off, SC_LANES)]
                x_hi = x_v[tok, pl.ds(pl.multiple_of(half + off, SC_LANES), SC_LANES)]
                new = []
                for k, a in enumerate(acc):
                    lo, hi = _unpack_pair(ubuf[slot, k, pl.ds(off, SC_LANES)])
                    new.append(a + x_lo * lo + x_hi * hi)
                return tuple(new)

            out = jnp.zeros((SC_LANES,), F32)
            for k in range(SC_LANES):
                out = jnp.where(lane == k, jnp.sum(accs[k]), out)
            hid_v[tok, pl.ds(ch * SC_LANES, SC_LANES)] = out

        @pl.loop(0, per_w // tb)
        def _(b):
            base = wid * per_w + b * tb
            pltpu.sync_copy(idx_hbm.at[pl.ds(base, tb)], idx_v)
            pltpu.sync_copy(x_hbm.at[pl.ds(base, tb)], x_v)
            _sc_ring(nq, lambda q, s: gather(q, s).start(), lambda q, s: gather(q, s).wait(), compute)
            pltpu.sync_copy(hid_v, out_hbm.at[pl.ds(base, tb)])

    return pl.kernel(
        body, out_type=jax.ShapeDtypeStruct((t, nsel), F32), mesh=mesh,
        scratch_types=[pltpu.VMEM((tb, nsel), jnp.int32), pltpu.VMEM((tb, d), F32),
                       pltpu.VMEM((SC_RING, SC_LANES, half), jnp.int32), pltpu.VMEM((tb, nsel), F32),
                       pltpu.SemaphoreType.DMA((SC_RING,))],
        compiler_params=pltpu.CompilerParams(needs_layout_passes=False), name="peer_hidden_sc",
    )(order, xn, idx, u_tab)


def _peer_mix_sc(act, idx, v_tab):
    t, nsel = act.shape
    half = v_tab.shape[1]
    d = 2 * half
    nw = SC_CORES * SC_SUBCORES
    per_w = t // nw
    tb = min(SC_BATCH, per_w)
    nchunk = nsel // SC_LANES
    shift = nchunk.bit_length() - 1
    ncol = d // SC_LANES
    nword = half // SC_LANES
    nq = tb * nchunk
    assert per_w * nw == t and per_w % tb == 0 and nchunk == 1 << shift and nq >= SC_RING
    mesh = plsc.VectorSubcoreMesh(core_axis_name="c", subcore_axis_name="s")

    def body(a_hbm, idx_hbm, v_hbm, out_hbm, idx_v, a_v, vbuf, o_v, sem):
        wid = _sc_worker_id()
        zero = jnp.zeros((SC_LANES,), F32)

        def gather(q, slot):
            tok = lax.shift_right_logical(q, shift)
            ch = q & (nchunk - 1)
            rows = idx_v.at[tok, pl.ds(ch * SC_LANES, SC_LANES)]
            return pltpu.make_async_copy(v_hbm.at[rows], vbuf.at[slot], sem.at[slot])

        def compute(q, slot):
            tok = lax.shift_right_logical(q, shift)
            ch = q & (nchunk - 1)
            tok_v = jnp.full((SC_LANES,), tok, jnp.int32)
            col_v = jnp.full((SC_LANES,), ch * SC_LANES, jnp.int32)
            w = [plsc.load_gather(a_v, [tok_v, col_v + k]) for k in range(SC_LANES)]

            @plsc.parallel_loop(0, nword)
            def _(c):
                off = pl.multiple_of(c * SC_LANES, SC_LANES)
                cs_lo = pl.ds(off, SC_LANES)
                cs_hi = pl.ds(pl.multiple_of(half + off, SC_LANES), SC_LANES)
                pairs = [_unpack_pair(vbuf[slot, k, cs_lo]) for k in range(SC_LANES)]
                o_v[tok, cs_lo] = o_v[tok, cs_lo] + _tree_sum([w[k] * p[0] for k, p in enumerate(pairs)])
                o_v[tok, cs_hi] = o_v[tok, cs_hi] + _tree_sum([w[k] * p[1] for k, p in enumerate(pairs)])

        @pl.loop(0, per_w // tb)
        def _(b):
            base = wid * per_w + b * tb
            pltpu.sync_copy(idx_hbm.at[pl.ds(base, tb)], idx_v)
            pltpu.sync_copy(a_hbm.at[pl.ds(base, tb)], a_v)

            @pl.loop(0, tb)
            def _(r):
                @pl.loop(0, ncol)
                def _(c):
                    o_v[r, pl.ds(pl.multiple_of(c * SC_LANES, SC_LANES), SC_LANES)] = zero

            _sc_ring(nq, lambda q, s: gather(q, s).start(), lambda q, s: gather(q, s).wait(), compute)
            pltpu.sync_copy(o_v, out_hbm.at[pl.ds(base, tb)])

    return pl.kernel(
        body, out_type=jax.ShapeDtypeStruct((t, d), F32), mesh=mesh,
        scratch_types=[pltpu.VMEM((tb, nsel), jnp.int32), pltpu.VMEM((tb, nsel), F32),
                       pltpu.VMEM((SC_RING, SC_LANES, half), jnp.int32), pltpu.VMEM((tb, d), F32),
                       pltpu.SemaphoreType.DMA((SC_RING,))],
        compiler_params=pltpu.CompilerParams(needs_layout_passes=False), name="peer_mix_sc",
    )(act, idx, v_tab)


def _rope_tables(pos, half, group, width, lo):
    inv = ROPE_BASE ** (-jnp.arange(half, dtype=F32) / half)
    ang = pos.astype(F32)[:, None] * inv[None, :]
    cos, sin = jnp.cos(ang), jnp.sin(ang)
    n = pos.shape[0]
    reps = width // group
    pad_hi = group - lo - 2 * half
    blk = lambda a, b, fill: jnp.concatenate(
        [jnp.full((n, lo), fill, F32), a, b, jnp.full((n, pad_hi), fill, F32)], axis=1)
    z = jnp.zeros_like(sin)
    c = blk(cos, cos, 1.0)
    sa = blk(-sin, z, 0.0)
    sb = blk(z, sin, 0.0)
    return [jnp.tile(a, (1, reps)) for a in (c, sa, sb)]


def _ret_log_decay(nheads):
    return jnp.log(1.0 - jnp.exp2(-5.0 - jnp.arange(nheads, dtype=F32)))


def _pair_states(s):
    b, h, dk, dv = s.shape
    s = s.reshape(b, h // 2, 2, dk, dv)
    z = jnp.zeros_like(s[:, :, 0])
    top = jnp.concatenate([s[:, :, 0], z], axis=-1)
    bot = jnp.concatenate([z, s[:, :, 1]], axis=-1)
    return jnp.concatenate([top, bot], axis=-2)


def _unpair_states(sp, dk, dv):
    b, hp = sp.shape[:2]
    return jnp.stack([sp[:, :, :dk, :dv], sp[:, :, dk:, dv:]], axis=2).reshape(b, 2 * hp, dk, dv)


def _layer_weights(ln1_w, w_in, ret_gn_w, q_norm_w, w_uq, kv_norm_w, w_uk, w_uv, w_o, ln2_w,
                   peer_w_q, peer_sub_keys, dims):
    d = w_in.shape[0]
    nheads, nope, rope, vdim = dims["nheads"], dims["nope"], dims["mla_rope"], dims["vdim"]
    o6 = 4 * dims["rw"] + dims["qrank"] + dims["kvrank"]
    zc = lambda r, c: jnp.zeros((r, c), F32)
    win_p = jnp.concatenate([w_in[:, :o6], zc(d, nope), w_in[:, o6:], zc(d, LANES - nope - rope)], axis=1)
    qr, kr = w_uq.shape[0], w_uk.shape[0]
    wuq_p = jnp.concatenate([w_uq, jnp.zeros((qr, nheads, LANES - nope - rope), F32)], axis=2).reshape(qr, -1)
    wuk_p = jnp.concatenate([w_uk, jnp.zeros((kr, nheads, LANES - nope), F32)], axis=2).reshape(kr, -1)
    zv = jnp.zeros((kr, nheads // 2, LANES - vdim), F32)
    wv = w_uv.reshape(kr, nheads // 2, 2, vdim)
    wuv_p = jnp.concatenate([wv[:, :, 0], zv, zv, wv[:, :, 1]], axis=2).reshape(kr, -1)
    wuk3 = jnp.concatenate([jnp.transpose(w_uk, (1, 2, 0)),
                            jnp.zeros((nheads, LANES - nope, kr), F32)], axis=1)
    wuv3 = jnp.transpose(w_uv, (1, 0, 2))
    gidx = jnp.arange(dims["rw"]) // dims["ret_dv"]
    avg = (gidx[:, None] == gidx[None, :]).astype(F32) / dims["ret_dv"]
    keys = peer_sub_keys.reshape(-1, peer_sub_keys.shape[2], peer_sub_keys.shape[3])
    c = lambda a: a.astype(MXU_DTYPE)
    r2 = lambda a: a.reshape(1, -1)
    return dict(ln1=r2(ln1_w), win_p=c(win_p), gnw=r2(ret_gn_w), qnw=r2(q_norm_w), kvnw=r2(kv_norm_w),
                wuq_p=c(wuq_p), wuk_p=c(wuk_p), wuv_p=c(wuv_p), wuk3=c(wuk3), wuv3=c(wuv3), avg=c(avg),
                wo=c(w_o), ln2=r2(ln2_w), wq=c(peer_w_q), keys=c(keys))


class _Stream:
    def __init__(self, x, tabs, s0, *, nbatch, ret_rows, ret_chunk, tm, ranges, cache=None):
        self.x, self.tabs, self.s0, self.cache = x, tabs, s0, cache
        self.nbatch, self.ret_rows, self.ret_chunk, self.tm, self.ranges = nbatch, ret_rows, ret_chunk, tm, ranges
        self.seq = x.shape[0] // nbatch
        assert nbatch == 1 or ranges == [(0, self.seq)]
        self.pre = None
        self.outs = []


def _layer(streams, w, lg, u_tab, v_tab, lnf, dims, *, final_norm):
    units = [(st, lo, hi) for st in streams for lo, hi in st.ranges]
    n = len(units)
    nheads, dk = dims["nheads"], dims["ret_dk"]
    scale = (dims["nope"] + dims["mla_rope"]) ** -0.5
    built = [None] * n

    def build(i, order):
        st, lo, hi = units[i]
        if st.pre is None:
            proj = _inproj(order, st.x, st.tabs, w["ln1"], w["win_p"], w["qnw"], w["kvnw"], w["wuq_p"], w["wuk_p"],
                           w["wuv_p"], tm=st.tm, dims=dims)
            ret_o, s_pairs = _retention(lg, *proj[:3], _pair_states(st.s0), nbatch=st.nbatch, rows=st.ret_rows,
                                        chunk=st.ret_chunk, dk=dk)
            st.pre = list(proj) + [ret_o, s_pairs]
        qr, kr, vr, gate, qm, km, vm, ckv, kpe, ret_o, s_pairs = st.pre
        if st.cache is None:
            mla_o = _flash(order, qm, km, vm, nbatch=st.nbatch, tq=min(256, st.seq), lo=lo, hi=hi, scale=scale,
                           nheads=nheads, chunk=CHUNK)
        else:
            mla_o = _decode_attn(qm, st.cache[0], st.cache[1], ckv, kpe, w["wuk3"], w["wuv3"], nq=st.seq,
                                 nope=dims["nope"], rope=dims["mla_rope"], scale=scale)
        h, hn, idx, g = _mix_route(st.x, ret_o, gate, mla_o, w["gnw"], w["avg"], w["wo"], w["ln2"], w["wq"],
                                   w["keys"], tm=st.tm, row0=lo, pheads=dims["pheads"], nkeys=dims["nkeys"],
                                   topk=dims["topk"])
        built[i] = (h, idx, g, _peer_hidden_sc(peers[i - 2] if i >= 2 else lnf, hn, idx, u_tab))

    for i in range(min(2, n)):
        build(i, lnf)
    acts, peers = [], []
    for i in range(n):
        h, idx, g, hid = built[i]
        acts.append(_gelu_gate(built[i + 1][1] if i + 1 < n else lnf, hid, g, tm=units[i][0].tm))
        peers.append(_peer_mix_sc(acts[i], idx, v_tab))
        if i + 2 < n:
            build(i + 2, acts[i])
    for i, (st, lo, hi) in enumerate(units):
        st.outs.append(_residual(acts[min(i + 2, n - 1)], built[i][0], peers[i], lnf, tm=st.tm,
                                 final_norm=final_norm))
    nope, rope, dv = dims["nope"], dims["mla_rope"], dims["ret_dv"]
    results = []
    for st in streams:
        out = st.outs[0] if len(st.outs) == 1 else jnp.concatenate(st.outs, axis=0)
        results.append((out, st.pre[7], st.pre[8][:, nope:nope + rope], _unpair_states(st.pre[10], dk, dv)))
    return results


def kernel(x_prompt, x_sample, cache_mla_ckv, cache_mla_krope, state_retention, ln1_w, w_in, ret_gn_w,
           mla_q_norm_w, mla_w_uq, mla_kv_norm_w, mla_w_uk, mla_w_uv, w_o, ln2_w, peer_w_q, peer_sub_keys,
           peer_u, peer_v, lnf_w):
    depth = w_in.shape[0]
    nb, seq, d = x_prompt.shape
    db, dseq, _ = x_sample.shape
    past = cache_mla_ckv.shape[2]
    rheads, dk, dv = state_retention.shape[2:]
    nkeys = peer_sub_keys.shape[3]
    dims = dict(rw=rheads * dk, ret_dk=dk, ret_dv=dv, qrank=mla_w_uq.shape[1], kvrank=mla_w_uk.shape[1],
                nheads=mla_w_uq.shape[2], nope=mla_w_uk.shape[3], vdim=mla_w_uv.shape[3],
                mla_rope=mla_w_uq.shape[3] - mla_w_uk.shape[3], pheads=peer_sub_keys.shape[1], nkeys=nkeys,
                topk=PEER_TOPK)
    assert rheads * dk == rheads * dv and dims["nheads"] % 2 == 0 and dk * 2 == LANES and dims["vdim"] * 2 == LANES

    def tables(pos):
        return (_rope_tables(pos, dk // 2, dk, dims["rw"], 0)
                + _rope_tables(pos, dims["mla_rope"] // 2, LANES, LANES, dims["nope"]))

    tabs_p = tables(jnp.arange(seq))
    tabs_s = tables(jnp.tile(past + jnp.arange(dseq), db))
    lg = _ret_log_decay(rheads)
    lnf = lnf_w.reshape(1, -1)
    hp = x_prompt.reshape(nb * seq, d)
    hs = x_sample.reshape(db * dseq, d)
    outs = [[] for _ in range(6)]
    for l in range(depth):
        w = _layer_weights(ln1_w[l], w_in[l], ret_gn_w[l], mla_q_norm_w[l], mla_w_uq[l], mla_kv_norm_w[l],
                           mla_w_uk[l], mla_w_uv[l], w_o[l], ln2_w[l], peer_w_q[l], peer_sub_keys[l], dims)
        last = l == depth - 1
        gb = nb // PROMPT_GROUPS if nb % PROMPT_GROUPS == 0 else nb

        def frame_ranges(g):
            step = seq // (PROMPT_HEAD_SPLIT if g == 0 else PROMPT_SPLIT)
            ok = gb == 1 and step > 0 and step % 256 == 0
            return [(lo, lo + step) for lo in range(0, seq, step)] if ok else [(0, seq)]

        streams = [_Stream(hp[g * gb * seq:(g + 1) * gb * seq], tabs_p, jnp.zeros((gb, rheads, dk, dv), F32),
                           nbatch=gb, ret_rows=min(256, seq), ret_chunk=CHUNK, tm=min(256, gb * seq),
                           ranges=frame_ranges(g))
                   for g in range(nb // gb)]
        streams.append(_Stream(hs, tabs_s, state_retention[l], nbatch=db, ret_rows=dseq, ret_chunk=dseq,
                               tm=min(256, db * dseq), ranges=[(0, dseq)],
                               cache=(cache_mla_ckv[l], cache_mla_krope[l])))
        results = _layer(streams, w, lg, _pack_rows(peer_u[l]), _pack_rows(peer_v[l]), lnf, dims,
                         final_norm=last)
        hp, c1, k1, s1 = (jnp.concatenate(p, axis=0) for p in zip(*results[:-1]))
        hs, c2, k2, s2 = results[-1]
        for acc, val in zip(outs, (c1.reshape(nb, seq, -1), k1.reshape(nb, seq, -1), s1,
                                   c2.reshape(db, dseq, -1), k2.reshape(db, dseq, -1), s2)):
            acc.append(val)
    return (hp.reshape(nb, seq, d), hs.reshape(db, dseq, d), *[jnp.stack(o) for o in outs])
```

```python
import functools

import jax
import jax.numpy as jnp
from jax import lax
from jax.experimental import pallas as pl
from jax.experimental.pallas import tpu as pltpu
from jax.experimental.pallas import tpu_sc as plsc

EPS = 1e-6
ROPE_BASE = 10000.0
CHUNK = 64
PEER_TOPK = 16
PROMPT_GROUPS = 8
PROMPT_HEAD_SPLIT = 4
PROMPT_SPLIT = 2
LANES = 128
MXU_DTYPE = jnp.bfloat16
VMEM_LIMIT_BYTES = 56 * 1024 * 1024

F32 = jnp.float32
NEG_INF = float("-inf")


def _mm(a, b):
    return jnp.dot(a.astype(MXU_DTYPE), b.astype(MXU_DTYPE), preferred_element_type=F32)


def _mm_nt(a, b):
    return lax.dot_general(a.astype(MXU_DTYPE), b.astype(MXU_DTYPE),
                           (((1,), (1,)), ((), ())), preferred_element_type=F32)


def _mm_tn(a, b):
    return lax.dot_general(a.astype(MXU_DTYPE), b.astype(MXU_DTYPE),
                           (((0,), (0,)), ((), ())), preferred_element_type=F32)


def _rms(x, w):
    return x * lax.rsqrt(jnp.mean(x * x, axis=-1, keepdims=True) + EPS) * w


def _rope(t, c, sa, sb, half):
    n = t.shape[1]
    return t * c + pltpu.roll(t, n - half, 1) * sa + pltpu.roll(t, half, 1) * sb


def _params(*sem):
    return pltpu.CompilerParams(dimension_semantics=sem, vmem_limit_bytes=VMEM_LIMIT_BYTES)


ORDER_SPEC = pl.BlockSpec(memory_space=pl.ANY)


def _ordered(body):
    def run(_order_ref, *refs):
        body(*refs)
    return run


def _inproj_body(x_ref, ln1_ref, win_ref, cr_ref, sar_ref, sbr_ref, cm_ref, sam_ref, sbm_ref,
                 qnw_ref, kvnw_ref, wuq_ref, wuk_ref, wuv_ref,
                 qr_ref, kr_ref, vr_ref, gate_ref, qm_ref, km_ref, vm_ref, ckv_ref, kpe_ref,
                 *, rw, qrank, kvrank, ret_half, mla_half, k_scale, nheads):
    n1 = _rms(x_ref[...], ln1_ref[...])
    proj = _mm(n1, win_ref[...])
    cr, sar, sbr = cr_ref[...], sar_ref[...], sbr_ref[...]
    qr_ref[...] = _rope(proj[:, 0:rw], cr, sar, sbr, ret_half)
    kr_ref[...] = _rope(proj[:, rw:2 * rw], cr, sar, sbr, ret_half) * k_scale
    vr_ref[...] = proj[:, 2 * rw:3 * rw]
    gate_ref[...] = proj[:, 3 * rw:4 * rw]
    o4 = 4 * rw
    o5 = o4 + qrank
    o6 = o5 + kvrank
    cm, sam, sbm = cm_ref[...], sam_ref[...], sbm_ref[...]
    tile = lambda t: jnp.concatenate([t] * nheads, axis=1)
    cq = _rms(proj[:, o4:o5], qnw_ref[...])
    qm = _rope(_mm(cq, wuq_ref[...]), tile(cm), tile(sam), tile(sbm), mla_half)
    qm_ref[...] = qm.astype(qm_ref.dtype)
    ckv = _rms(proj[:, o5:o6], kvnw_ref[...])
    ckv_ref[...] = ckv
    kpe = _rope(proj[:, o6:o6 + LANES], cm, sam, sbm, mla_half)
    kpe_ref[...] = kpe
    km_ref[...] = (_mm(ckv, wuk_ref[...]) + tile(kpe)).astype(km_ref.dtype)
    vm_ref[...] = _mm(ckv, wuv_ref[...]).astype(vm_ref.dtype)


def _inproj(order, x, tabs, ln1, win_p, qnw, kvnw, wuq_p, wuk_p, wuv_p, *, tm, dims):
    t, d = x.shape
    rw, nheads = dims["rw"], dims["nheads"]
    hp = nheads * LANES
    nblk_tab = tabs[0].shape[0] // tm
    row = lambda i: (i, 0)
    tab = lambda i: (i % nblk_tab, 0)
    full = lambda i: (0, 0)
    fs = lambda a: pl.BlockSpec(a.shape, full)
    in_specs = [pl.BlockSpec((tm, d), row), fs(ln1), fs(win_p)]
    in_specs += [pl.BlockSpec((tm, rw), tab)] * 3 + [pl.BlockSpec((tm, LANES), tab)] * 3
    in_specs += [fs(qnw), fs(kvnw), fs(wuq_p), fs(wuk_p), fs(wuv_p)]
    out_shape = [jax.ShapeDtypeStruct((t, rw), F32)] * 4
    out_shape += [jax.ShapeDtypeStruct((t, hp), MXU_DTYPE)] * 3
    out_shape += [jax.ShapeDtypeStruct((t, dims["kvrank"]), F32), jax.ShapeDtypeStruct((t, LANES), F32)]
    out_specs = [pl.BlockSpec((tm, rw), row)] * 4 + [pl.BlockSpec((tm, hp), row)] * 3
    out_specs += [pl.BlockSpec((tm, dims["kvrank"]), row), pl.BlockSpec((tm, LANES), row)]
    body = functools.partial(
        _inproj_body, rw=rw, qrank=dims["qrank"], kvrank=dims["kvrank"], ret_half=dims["ret_dk"] // 2,
        mla_half=dims["mla_rope"] // 2, k_scale=dims["ret_dk"] ** -0.5, nheads=nheads)
    return pl.pallas_call(
        _ordered(body), grid=(t // tm,), in_specs=[ORDER_SPEC] + in_specs, out_specs=out_specs,
        out_shape=out_shape, compiler_params=_params("parallel"), name="inproj",
    )(order, x, ln1, win_p, *tabs, qnw, kvnw, wuq_p, wuk_p, wuv_p)


def _retention_body(lg_ref, q_ref, k_ref, v_ref, s0_ref, o_ref, sout_ref, s_scr, *, rows, chunk, dk):
    hp = pl.program_id(1)
    j = pl.program_id(2)

    @pl.when(j == 0)
    def _():
        s_scr[...] = s0_ref[0, 0]

    lane = lax.broadcasted_iota(jnp.int32, (1, LANES), 1)
    is_a = lane < dk
    lg_a = lg_ref[2 * hp]
    lg_b = lg_ref[2 * hp + 1]
    lgl = jnp.where(is_a, lg_a, lg_b)
    r = lax.broadcasted_iota(jnp.int32, (rows, 1), 0).astype(F32)
    q, k, v = q_ref[...], k_ref[...], v_ref[...]
    q_dec = q * jnp.exp(lgl * (r + 1.0))
    k_dec = k * jnp.exp(lgl * (float(rows) - 1.0 - r))
    ri = lax.broadcasted_iota(jnp.int32, (rows, rows), 0)
    ci = lax.broadcasted_iota(jnp.int32, (rows, rows), 1)
    dist = jnp.abs(ri - ci).astype(F32)
    visible = (ci // chunk) <= (ri // chunk)
    o = _mm(q_dec, s_scr[...])
    for first, lg in ((True, lg_a), (False, lg_b)):
        sel = is_a if first else jnp.logical_not(is_a)
        qh = jnp.where(sel, q, 0.0)
        vh = jnp.where(sel, v, 0.0)
        decay = jnp.where(visible, jnp.exp(lg * dist), 0.0)
        o = o + _mm(_mm_nt(qh, k) * decay, vh)
    o_ref[...] = o
    sr = lax.broadcasted_iota(jnp.int32, (LANES, LANES), 0) < dk
    sc = lax.broadcasted_iota(jnp.int32, (LANES, LANES), 1) < dk
    kv = jnp.where(sr == sc, _mm_tn(k_dec, v), 0.0)
    s_new = jnp.exp(lgl * float(rows)) * s_scr[...] + kv
    s_scr[...] = s_new

    @pl.when(j == pl.num_programs(2) - 1)
    def _():
        sout_ref[0, 0] = s_new


def _retention(lg, q, k, v, s0_pairs, *, nbatch, rows, chunk, dk):
    t, w = q.shape
    npairs = w // LANES
    nblk = t // (nbatch * rows)
    blk = pl.BlockSpec((rows, LANES), lambda b, p, j: (b * nblk + j, p))
    st = pl.BlockSpec((1, 1, LANES, LANES), lambda b, p, j: (b, p, 0, 0))
    return pl.pallas_call(
        functools.partial(_retention_body, rows=rows, chunk=chunk, dk=dk),
        grid=(nbatch, npairs, nblk),
        in_specs=[pl.BlockSpec(memory_space=pltpu.SMEM), blk, blk, blk, st],
        out_specs=[blk, st],
        out_shape=[jax.ShapeDtypeStruct((t, w), F32),
                   jax.ShapeDtypeStruct((nbatch, npairs, LANES, LANES), F32)],
        scratch_shapes=[pltpu.VMEM((LANES, LANES), F32)],
        compiler_params=_params("parallel", "parallel", "arbitrary"), name="retention",
    )(lg, q, k, v, s0_pairs)


def _flash_body(q_ref, k_ref, v_ref, o_ref, *, tq, tile0, scale, nheads, chunk):
    i = pl.program_id(1) + tile0
    ri = lax.broadcasted_iota(jnp.int32, (tq, tq), 0) // chunk
    ci = lax.broadcasted_iota(jnp.int32, (tq, tq), 1) // chunk
    visible = ci <= ri

    def head(h):
        cols = slice(h * LANES, (h + 1) * LANES)
        q = q_ref[:, cols]

        def step(j, carry, diagonal):
            m, l, acc = carry
            off = pl.multiple_of(j * tq, tq)
            s = _mm_nt(q, k_ref[pl.ds(off, tq), cols]) * scale
            if diagonal:
                s = jnp.where(visible, s, NEG_INF)
            m_new = jnp.maximum(m, jnp.max(s, axis=1, keepdims=True))
            alpha = jnp.exp(m - m_new)
            p = jnp.exp(s - m_new)
            l = alpha * l + jnp.sum(p, axis=1, keepdims=True)
            acc = alpha * acc + _mm(p, v_ref[pl.ds(off, tq), cols])
            return m_new, l, acc

        init = (jnp.full((tq, 1), NEG_INF, F32), jnp.zeros((tq, 1), F32), jnp.zeros((tq, LANES), F32))
        carry = lax.fori_loop(0, i, functools.partial(step, diagonal=False), init)
        _, l, acc = step(i, carry, True)
        return acc / l

    for p in range(nheads // 2):
        o_ref[:, p * LANES:(p + 1) * LANES] = head(2 * p) + head(2 * p + 1)


def _flash(order, qm, km, vm, *, nbatch, tq, lo, hi, scale, nheads, chunk):
    t, hp = qm.shape
    s = t // nbatch
    nq = s // tq
    tile0 = lo // tq
    nqr = (hi - lo) // tq
    ow = nheads // 2 * LANES
    return pl.pallas_call(
        _ordered(functools.partial(_flash_body, tq=tq, tile0=tile0, scale=scale, nheads=nheads, chunk=chunk)),
        grid=(nbatch, nqr),
        in_specs=[ORDER_SPEC,
                  pl.BlockSpec((tq, hp), lambda b, i: (b * nq + tile0 + i, 0)),
                  pl.BlockSpec((s, hp), lambda b, i: (b, 0)),
                  pl.BlockSpec((s, hp), lambda b, i: (b, 0))],
        out_specs=pl.BlockSpec((tq, ow), lambda b, i: (b * nqr + i, 0)),
        out_shape=jax.ShapeDtypeStruct((nbatch * (hi - lo), ow), F32),
        compiler_params=_params("parallel", "arbitrary"), name="flash_mla",
    )(order, qm, km, vm)


def _decode_attn_body(q_ref, cpast_ref, kpast_ref, cnew_ref, knew_ref, wuk_ref, wuv_ref, o_ref,
                      *, nheads, nope, rope, scale):
    c_past = cpast_ref[0]
    k_past = kpast_ref[0]
    c_new = cnew_ref[...]
    k_new = knew_ref[:, nope:nope + rope]
    outs = []
    for h in range(nheads):
        q = q_ref[:, h * LANES:(h + 1) * LANES]
        q_lat = _mm(q, wuk_ref[h])
        q_pe = q[:, nope:nope + rope]
        s_p = (_mm_nt(q_lat, c_past) + _mm_nt(q_pe, k_past)) * scale
        s_n = (_mm_nt(q_lat, c_new) + _mm_nt(q_pe, k_new)) * scale
        m = jnp.maximum(jnp.max(s_p, axis=1, keepdims=True), jnp.max(s_n, axis=1, keepdims=True))
        p_p = jnp.exp(s_p - m)
        p_n = jnp.exp(s_n - m)
        l = jnp.sum(p_p, axis=1, keepdims=True) + jnp.sum(p_n, axis=1, keepdims=True)
        o_lat = (_mm(p_p, c_past) + _mm(p_n, c_new)) / l
        outs.append(_mm(o_lat, wuv_ref[h]))
    o_ref[...] = jnp.concatenate(outs, axis=1)


def _decode_attn(qm, c_past, k_past, c_new, kpe_new, wuk3, wuv3, *, nq, nope, rope, scale):
    nb, past, kvr = c_past.shape
    nheads, _, vdim = wuv3.shape
    t, hp = qm.shape
    row = lambda b: (b, 0)
    full3 = lambda b: (0, 0, 0)
    return pl.pallas_call(
        functools.partial(_decode_attn_body, nheads=nheads, nope=nope, rope=rope, scale=scale),
        grid=(nb,),
        in_specs=[pl.BlockSpec((nq, hp), row),
                  pl.BlockSpec((1, past, kvr), lambda b: (b, 0, 0)),
                  pl.BlockSpec((1, past, rope), lambda b: (b, 0, 0)),
                  pl.BlockSpec((nq, kvr), row),
                  pl.BlockSpec((nq, LANES), row),
                  pl.BlockSpec(wuk3.shape, full3),
                  pl.BlockSpec(wuv3.shape, full3)],
        out_specs=pl.BlockSpec((nq, nheads * vdim), row),
        out_shape=jax.ShapeDtypeStruct((t, nheads * vdim), F32),
        compiler_params=_params("parallel"), name="decode_mla",
    )(qm, c_past, k_past, c_new, kpe_new, wuk3, wuv3)


def _split3(x):
    a = x.astype(MXU_DTYPE)
    r = x - a.astype(F32)
    b = r.astype(MXU_DTYPE)
    c = (r - b.astype(F32)).astype(MXU_DTYPE)
    return a, b, c


def _group_mean(x, avg):
    a, b, c = _split3(x)
    dot = lambda t: jnp.dot(t, avg, preferred_element_type=F32)
    return dot(a) + dot(b) + dot(c)


def _topk_rows(s, payload, kk):
    n = s.shape[0]
    rid = lax.broadcasted_iota(jnp.int32, s.shape, 0)
    vals, pays = [], []
    for _ in range(kk):
        mx = jnp.max(s, axis=0, keepdims=True)
        first = jnp.min(jnp.where(s == mx, rid, n), axis=0, keepdims=True)
        hit = rid == first
        vals.append(mx)
        pays.append(jnp.max(jnp.where(hit, payload, -1), axis=0, keepdims=True))
        s = jnp.where(hit, NEG_INF, s)
    return jnp.concatenate(vals, axis=0), jnp.concatenate(pays, axis=0)


def _mix_route_body(x_ref, ret_ref, gate_ref, mla_ref, gnw_ref, avg_ref, wo_ref, ln2_ref, wq_ref, keys_ref,
                    h_ref, hn_ref, idx_ref, g_ref, *, rw, pheads, nkeys, topk):
    ret = ret_ref[...]
    avg = avg_ref[...]
    mu = _group_mean(ret, avg)
    cen = ret - mu
    var = _group_mean(cen * cen, avg)
    gate = gate_ref[...]
    y = cen * lax.rsqrt(var + EPS) * gnw_ref[...] * (gate * jax.nn.sigmoid(gate))
    h = x_ref[...] + _mm(y, wo_ref[0:rw, :]) + _mm(mla_ref[...], wo_ref[rw:, :])
    h_ref[...] = h
    hn = _rms(h, ln2_ref[...])
    hn_ref[...] = hn
    qp = _mm(hn, wq_ref[...])
    kid = lax.broadcasted_iota(jnp.int32, (nkeys, qp.shape[0]), 0)
    idx_rows, g_rows = [], []
    for hd in range(pheads):
        ts, ti = [], []
        for half in range(2):
            c = (2 * hd + half) * LANES
            st = _mm_nt(keys_ref[2 * hd + half], qp[:, c:c + LANES])
            v, i = _topk_rows(st, kid, topk)
            ts.append(v)
            ti.append(i)
        cand = jnp.concatenate([ts[0][a:a + 1, :] + ts[1] for a in range(topk)], axis=0)
        cidx = jnp.concatenate([ti[0][a:a + 1, :] * nkeys + ti[1] for a in range(topk)], axis=0)
        best, expert = _topk_rows(cand, cidx, topk)
        e = jnp.exp(best - best[0:1, :])
        g_rows.append(e / jnp.sum(e, axis=0, keepdims=True))
        idx_rows.append(expert)
    idx_ref[...] = jnp.concatenate(idx_rows, axis=0).T
    g_ref[...] = jnp.concatenate(g_rows, axis=0).T


def _mix_route(x, ret_o, gate, mla_o, gnw, avg, wo, ln2, wq, keys, *, tm, row0, pheads, nkeys, topk):
    t = mla_o.shape[0]
    d = x.shape[1]
    rw = ret_o.shape[1]
    nsel = pheads * topk
    blk0 = row0 // tm
    row = lambda i: (i, 0)
    src = lambda i: (blk0 + i, 0)
    fs = lambda a: pl.BlockSpec(a.shape, lambda i: (0,) * a.ndim)
    return pl.pallas_call(
        functools.partial(_mix_route_body, rw=rw, pheads=pheads, nkeys=nkeys, topk=topk),
        grid=(t // tm,),
        in_specs=[pl.BlockSpec((tm, d), src), pl.BlockSpec((tm, rw), src), pl.BlockSpec((tm, rw), src),
                  pl.BlockSpec((tm, mla_o.shape[1]), row), fs(gnw), fs(avg), fs(wo), fs(ln2), fs(wq), fs(keys)],
        out_specs=[pl.BlockSpec((tm, d), row), pl.BlockSpec((tm, d), row),
                   pl.BlockSpec((tm, nsel), row), pl.BlockSpec((tm, nsel), row)],
        out_shape=[jax.ShapeDtypeStruct((t, d), F32), jax.ShapeDtypeStruct((t, d), F32),
                   jax.ShapeDtypeStruct((t, nsel), jnp.int32),
                   jax.ShapeDtypeStruct((t, nsel), F32)],
        compiler_params=_params("parallel"), name="mix_route",
    )(x, ret_o, gate, mla_o, gnw, avg, wo, ln2, wq, keys)


def _gelu_gate_body(hid_ref, g_ref, a_ref):
    hid = hid_ref[...]
    a_ref[...] = 0.5 * hid * (1.0 + lax.erf(hid * (2.0 ** -0.5))) * g_ref[...]


def _gelu_gate(order, hid, g, *, tm):
    t, n = hid.shape
    blk = pl.BlockSpec((tm, n), lambda i: (i, 0))
    return pl.pallas_call(
        _ordered(_gelu_gate_body), grid=(t // tm,), in_specs=[ORDER_SPEC, blk, blk], out_specs=blk,
        out_shape=jax.ShapeDtypeStruct((t, n), F32), compiler_params=_params("parallel"), name="gelu_gate",
    )(order, hid, g)


def _residual_body(h_ref, p_ref, lnf_ref, o_ref, *, final_norm):
    out = h_ref[...] + p_ref[...]
    if final_norm:
        out = _rms(out, lnf_ref[...])
    o_ref[...] = out


def _residual(order, h, peer, lnf, *, tm, final_norm):
    t, d = h.shape
    blk = pl.BlockSpec((tm, d), lambda i: (i, 0))
    return pl.pallas_call(
        _ordered(functools.partial(_residual_body, final_norm=final_norm)), grid=(t // tm,),
        in_specs=[ORDER_SPEC, blk, blk, pl.BlockSpec((1, d), lambda i: (0, 0))], out_specs=blk,
        out_shape=jax.ShapeDtypeStruct((t, d), F32), compiler_params=_params("parallel"), name="residual_norm",
    )(order, h, peer, lnf)


SC_CORES = 2
SC_SUBCORES = 16
SC_LANES = 16
SC_RING = 4
SC_BATCH = 64


def _sc_worker_id():
    return lax.axis_index("s") * SC_CORES + lax.axis_index("c")


def _sc_ring(nq, start, wait, compute):
    for s in range(SC_RING - 1):
        start(s, s)

    def step(q, s):
        nxt = q + SC_RING - 1

        @pl.when(nxt < nq)
        def _():
            start(nxt, (s + SC_RING - 1) % SC_RING)

        wait(q, s)
        compute(q, s)

    full = nq // SC_RING * SC_RING

    @pl.loop(0, full, step=SC_RING)
    def _(q0):
        for s in range(SC_RING):
            step(q0 + s, s)

    for s in range(nq - full):
        step(jnp.int32(full + s), s)


def _tree_sum(terms):
    while len(terms) > 1:
        terms = [a + b for a, b in zip(terms[0::2], terms[1::2])]
    return terms[0]


def _pack_rows(tab):
    bits = lax.bitcast_convert_type(tab.astype(jnp.bfloat16), jnp.uint16).astype(jnp.uint32)
    half = tab.shape[1] // 2
    return lax.bitcast_convert_type(bits[:, :half] | (bits[:, half:] << 16), jnp.int32)


def _unpack_pair(w):
    lo = lax.bitcast_convert_type(lax.shift_left(w, jnp.int32(16)), F32)
    hi = lax.bitcast_convert_type(w & jnp.int32(-65536), F32)
    return lo, hi


def _peer_hidden_sc(order, xn, idx, u_tab):
    t, d = xn.shape
    nsel = idx.shape[1]
    nw = SC_CORES * SC_SUBCORES
    per_w = t // nw
    tb = min(SC_BATCH, per_w)
    nchunk = nsel // SC_LANES
    shift = nchunk.bit_length() - 1
    half = d // 2
    nword = half // SC_LANES
    nq = tb * nchunk
    assert per_w * nw == t and per_w % tb == 0 and nchunk == 1 << shift and nq >= SC_RING
    assert u_tab.shape[1] == half
    mesh = plsc.VectorSubcoreMesh(core_axis_name="c", subcore_axis_name="s")

    def body(_order_hbm, x_hbm, idx_hbm, u_hbm, out_hbm, idx_v, x_v, ubuf, hid_v, sem):
        wid = _sc_worker_id()
        lane = lax.iota(jnp.int32, SC_LANES)

        def gather(q, slot):
            tok = lax.shift_right_logical(q, shift)
            ch = q & (nchunk - 1)
            rows = idx_v.at[tok, pl.ds(ch * SC_LANES, SC_LANES)]
            return pltpu.make_async_copy(u_hbm.at[rows], ubuf.at[slot], sem.at[slot])

        def compute(q, slot):
            tok = lax.shift_right_logical(q, shift)
            ch = q & (nchunk - 1)

            @plsc.parallel_loop(0, nword, carry=tuple(jnp.zeros((SC_LANES,), F32) for _ in range(SC_LANES)))
            def accs(c, acc):
                off = pl.multiple_of(c * SC_LANES, SC_LANES)
                x_lo = x_v[tok, pl.ds(off, SC_LANES)]
                x_hi = x_v[tok, pl.ds(pl.multiple_of(half + off, SC_LANES), SC_LANES)]
                new = []
                for k, a in enumerate(acc):
                    lo, hi = _unpack_pair(ubuf[slot, k, pl.ds(off, SC_LANES)])
                    new.append(a + x_lo * lo + x_hi * hi)
                return tuple(new)

            out = jnp.zeros((SC_LANES,), F32)
            for k in range(SC_LANES):
                out = jnp.where(lane == k, jnp.sum(accs[k]), out)
            hid_v[tok, pl.ds(ch * SC_LANES, SC_LANES)] = out

        @pl.loop(0, per_w // tb)
        def _(b):
            base = wid * per_w + b * tb
            pltpu.sync_copy(idx_hbm.at[pl.ds(base, tb)], idx_v)
            pltpu.sync_copy(x_hbm.at[pl.ds(base, tb)], x_v)
            _sc_ring(nq, lambda q, s: gather(q, s).start(), lambda q, s: gather(q, s).wait(), compute)
            pltpu.sync_copy(hid_v, out_hbm.at[pl.ds(base, tb)])

    return pl.kernel(
        body, out_type=jax.ShapeDtypeStruct((t, nsel), F32), mesh=mesh,
        scratch_types=[pltpu.VMEM((tb, nsel), jnp.int32), pltpu.VMEM((tb, d), F32),
                       pltpu.VMEM((SC_RING, SC_LANES, half), jnp.int32), pltpu.VMEM((tb, nsel), F32),
                       pltpu.SemaphoreType.DMA((SC_RING,))],
        compiler_params=pltpu.CompilerParams(needs_layout_passes=False), name="peer_hidden_sc",
    )(order, xn, idx, u_tab)


def _peer_mix_sc(act, idx, v_tab):
    t, nsel = act.shape
    half = v_tab.shape[1]
    d = 2 * half
    nw = SC_CORES * SC_SUBCORES
    per_w = t // nw
    tb = min(SC_BATCH, per_w)
    nchunk = nsel // SC_LANES
    shift = nchunk.bit_length() - 1
    ncol = d // SC_LANES
    nword = half // SC_LANES
    nq = tb * nchunk
    assert per_w * nw == t and per_w % tb == 0 and nchunk == 1 << shift and nq >= SC_RING
    mesh = plsc.VectorSubcoreMesh(core_axis_name="c", subcore_axis_name="s")

    def body(a_hbm, idx_hbm, v_hbm, out_hbm, idx_v, a_v, vbuf, o_v, sem):
        wid = _sc_worker_id()
        zero = jnp.zeros((SC_LANES,), F32)

        def gather(q, slot):
            tok = lax.shift_right_logical(q, shift)
            ch = q & (nchunk - 1)
            rows = idx_v.at[tok, pl.ds(ch * SC_LANES, SC_LANES)]
            return pltpu.make_async_copy(v_hbm.at[rows], vbuf.at[slot], sem.at[slot])

        def compute(q, slot):
            tok = lax.shift_right_logical(q, shift)
            ch = q & (nchunk - 1)
            tok_v = jnp.full((SC_LANES,), tok, jnp.int32)
            col_v = jnp.full((SC_LANES,), ch * SC_LANES, jnp.int32)
            w = [plsc.load_gather(a_v, [tok_v, col_v + k]) for k in range(SC_LANES)]

            @plsc.parallel_loop(0, nword)
            def _(c):
                off = pl.multiple_of(c * SC_LANES, SC_LANES)
                cs_lo = pl.ds(off, SC_LANES)
                cs_hi = pl.ds(pl.multiple_of(half + off, SC_LANES), SC_LANES)
                pairs = [_unpack_pair(vbuf[slot, k, cs_lo]) for k in range(SC_LANES)]
                o_v[tok, cs_lo] = o_v[tok, cs_lo] + _tree_sum([w[k] * p[0] for k, p in enumerate(pairs)])
                o_v[tok, cs_hi] = o_v[tok, cs_hi] + _tree_sum([w[k] * p[1] for k, p in enumerate(pairs)])

        @pl.loop(0, per_w // tb)
        def _(b):
            base = wid * per_w + b * tb
            pltpu.sync_copy(idx_hbm.at[pl.ds(base, tb)], idx_v)
            pltpu.sync_copy(a_hbm.at[pl.ds(base, tb)], a_v)

            @pl.loop(0, tb)
            def _(r):
                @pl.loop(0, ncol)
                def _(c):
                    o_v[r, pl.ds(pl.multiple_of(c * SC_LANES, SC_LANES), SC_LANES)] = zero

            _sc_ring(nq, lambda q, s: gather(q, s).start(), lambda q, s: gather(q, s).wait(), compute)
            pltpu.sync_copy(o_v, out_hbm.at[pl.ds(base, tb)])

    return pl.kernel(
        body, out_type=jax.ShapeDtypeStruct((t, d), F32), mesh=mesh,
        scratch_types=[pltpu.VMEM((tb, nsel), jnp.int32), pltpu.VMEM((tb, nsel), F32),
                       pltpu.VMEM((SC_RING, SC_LANES, half), jnp.int32), pltpu.VMEM((tb, d), F32),
                       pltpu.SemaphoreType.DMA((SC_RING,))],
        compiler_params=pltpu.CompilerParams(needs_layout_passes=False), name="peer_mix_sc",
    )(act, idx, v_tab)


def _rope_tables(pos, half, group, width, lo):
    inv = ROPE_BASE ** (-jnp.arange(half, dtype=F32) / half)
    ang = pos.astype(F32)[:, None] * inv[None, :]
    cos, sin = jnp.cos(ang), jnp.sin(ang)
    n = pos.shape[0]
    reps = width // group
    pad_hi = group - lo - 2 * half
    blk = lambda a, b, fill: jnp.concatenate(
        [jnp.full((n, lo), fill, F32), a, b, jnp.full((n, pad_hi), fill, F32)], axis=1)
    z = jnp.zeros_like(sin)
    c = blk(cos, cos, 1.0)
    sa = blk(-sin, z, 0.0)
    sb = blk(z, sin, 0.0)
    return [jnp.tile(a, (1, reps)) for a in (c, sa, sb)]


def _ret_log_decay(nheads):
    return jnp.log(1.0 - jnp.exp2(-5.0 - jnp.arange(nheads, dtype=F32)))


def _pair_states(s):
    b, h, dk, dv = s.shape
    s = s.reshape(b, h // 2, 2, dk, dv)
    z = jnp.zeros_like(s[:, :, 0])
    top = jnp.concatenate([s[:, :, 0], z], axis=-1)
    bot = jnp.concatenate([z, s[:, :, 1]], axis=-1)
    return jnp.concatenate([top, bot], axis=-2)


def _unpair_states(sp, dk, dv):
    b, hp = sp.shape[:2]
    return jnp.stack([sp[:, :, :dk, :dv], sp[:, :, dk:, dv:]], axis=2).reshape(b, 2 * hp, dk, dv)


def _layer_weights(ln1_w, w_in, ret_gn_w, q_norm_w, w_uq, kv_norm_w, w_uk, w_uv, w_o, ln2_w,
                   peer_w_q, peer_sub_keys, dims):
    d = w_in.shape[0]
    nheads, nope, rope, vdim = dims["nheads"], dims["nope"], dims["mla_rope"], dims["vdim"]
    o6 = 4 * dims["rw"] + dims["qrank"] + dims["kvrank"]
    zc = lambda r, c: jnp.zeros((r, c), F32)
    win_p = jnp.concatenate([w_in[:, :o6], zc(d, nope), w_in[:, o6:], zc(d, LANES - nope - rope)], axis=1)
    qr, kr = w_uq.shape[0], w_uk.shape[0]
    wuq_p = jnp.concatenate([w_uq, jnp.zeros((qr, nheads, LANES - nope - rope), F32)], axis=2).reshape(qr, -1)
    wuk_p = jnp.concatenate([w_uk, jnp.zeros((kr, nheads, LANES - nope), F32)], axis=2).reshape(kr, -1)
    zv = jnp.zeros((kr, nheads // 2, LANES - vdim), F32)
    wv = w_uv.reshape(kr, nheads // 2, 2, vdim)
    wuv_p = jnp.concatenate([wv[:, :, 0], zv, zv, wv[:, :, 1]], axis=2).reshape(kr, -1)
    wuk3 = jnp.concatenate([jnp.transpose(w_uk, (1, 2, 0)),
                            jnp.zeros((nheads, LANES - nope, kr), F32)], axis=1)
    wuv3 = jnp.transpose(w_uv, (1, 0, 2))
    gidx = jnp.arange(dims["rw"]) // dims["ret_dv"]
    avg = (gidx[:, None] == gidx[None, :]).astype(F32) / dims["ret_dv"]
    keys = peer_sub_keys.reshape(-1, peer_sub_keys.shape[2], peer_sub_keys.shape[3])
    c = lambda a: a.astype(MXU_DTYPE)
    r2 = lambda a: a.reshape(1, -1)
    return dict(ln1=r2(ln1_w), win_p=c(win_p), gnw=r2(ret_gn_w), qnw=r2(q_norm_w), kvnw=r2(kv_norm_w),
                wuq_p=c(wuq_p), wuk_p=c(wuk_p), wuv_p=c(wuv_p), wuk3=c(wuk3), wuv3=c(wuv3), avg=c(avg),
                wo=c(w_o), ln2=r2(ln2_w), wq=c(peer_w_q), keys=c(keys))


class _Stream:
    def __init__(self, x, tabs, s0, *, nbatch, ret_rows, ret_chunk, tm, ranges, cache=None):
        self.x, self.tabs, self.s0, self.cache = x, tabs, s0, cache
        self.nbatch, self.ret_rows, self.ret_chunk, self.tm, self.ranges = nbatch, ret_rows, ret_chunk, tm, ranges
        self.seq = x.shape[0] // nbatch
        assert nbatch == 1 or ranges == [(0, self.seq)]
        self.pre = None
        self.outs = []


def _layer(streams, w, lg, u_tab, v_tab, lnf, dims, *, final_norm):
    units = [(st, lo, hi) for st in streams for lo, hi in st.ranges]
    n = len(units)
    nheads, dk = dims["nheads"], dims["ret_dk"]
    scale = (dims["nope"] + dims["mla_rope"]) ** -0.5
    built = [None] * n

    def build(i, order):
        st, lo, hi = units[i]
        if st.pre is None:
            proj = _inproj(order, st.x, st.tabs, w["ln1"], w["win_p"], w["qnw"], w["kvnw"], w["wuq_p"], w["wuk_p"],
                           w["wuv_p"], tm=st.tm, dims=dims)
            ret_o, s_pairs = _retention(lg, *proj[:3], _pair_states(st.s0), nbatch=st.nbatch, rows=st.ret_rows,
                                        chunk=st.ret_chunk, dk=dk)
            st.pre = list(proj) + [ret_o, s_pairs]
        qr, kr, vr, gate, qm, km, vm, ckv, kpe, ret_o, s_pairs = st.pre
        if st.cache is None:
            mla_o = _flash(order, qm, km, vm, nbatch=st.nbatch, tq=min(256, st.seq), lo=lo, hi=hi, scale=scale,
                           nheads=nheads, chunk=CHUNK)
        else:
            mla_o = _decode_attn(qm, st.cache[0], st.cache[1], ckv, kpe, w["wuk3"], w["wuv3"], nq=st.seq,
                                 nope=dims["nope"], rope=dims["mla_rope"], scale=scale)
        h, hn, idx, g = _mix_route(st.x, ret_o, gate, mla_o, w["gnw"], w["avg"], w["wo"], w["ln2"], w["wq"],
                                   w["keys"], tm=st.tm, row0=lo, pheads=dims["pheads"], nkeys=dims["nkeys"],
                                   topk=dims["topk"])
        built[i] = (h, idx, g, _peer_hidden_sc(peers[i - 2] if i >= 2 else lnf, hn, idx, u_tab))

    for i in range(min(2, n)):
        build(i, lnf)
    acts, peers = [], []
    for i in range(n):
        h, idx, g, hid = built[i]
        acts.append(_gelu_gate(built[i + 1][1] if i + 1 < n else lnf, hid, g, tm=units[i][0].tm))
        peers.append(_peer_mix_sc(acts[i], idx, v_tab))
        if i + 2 < n:
            build(i + 2, acts[i])
    for i, (st, lo, hi) in enumerate(units):
        st.outs.append(_residual(acts[min(i + 2, n - 1)], built[i][0], peers[i], lnf, tm=st.tm,
                                 final_norm=final_norm))
    nope, rope, dv = dims["nope"], dims["mla_rope"], dims["ret_dv"]
    results = []
    for st in streams:
        out = st.outs[0] if len(st.outs) == 1 else jnp.concatenate(st.outs, axis=0)
        results.append((out, st.pre[7], st.pre[8][:, nope:nope + rope], _unpair_states(st.pre[10], dk, dv)))
    return results


def kernel(x_prompt, x_sample, cache_mla_ckv, cache_mla_krope, state_retention, ln1_w, w_in, ret_gn_w,
           mla_q_norm_w, mla_w_uq, mla_kv_norm_w, mla_w_uk, mla_w_uv, w_o, ln2_w, peer_w_q, peer_sub_keys,
           peer_u, peer_v, lnf_w):
    depth = w_in.shape[0]
    nb, seq, d = x_prompt.shape
    db, dseq, _ = x_sample.shape
    past = cache_mla_ckv.shape[2]
    rheads, dk, dv = state_retention.shape[2:]
    nkeys = peer_sub_keys.shape[3]
    dims = dict(rw=rheads * dk, ret_dk=dk, ret_dv=dv, qrank=mla_w_uq.shape[1], kvrank=mla_w_uk.shape[1],
                nheads=mla_w_uq.shape[2], nope=mla_w_uk.shape[3], vdim=mla_w_uv.shape[3],
                mla_rope=mla_w_uq.shape[3] - mla_w_uk.shape[3], pheads=peer_sub_keys.shape[1], nkeys=nkeys,
                topk=PEER_TOPK)
    assert rheads * dk == rheads * dv and dims["nheads"] % 2 == 0 and dk * 2 == LANES and dims["vdim"] * 2 == LANES

    def tables(pos):
        return (_rope_tables(pos, dk // 2, dk, dims["rw"], 0)
                + _rope_tables(pos, dims["mla_rope"] // 2, LANES, LANES, dims["nope"]))

    tabs_p = tables(jnp.arange(seq))
    tabs_s = tables(jnp.tile(past + jnp.arange(dseq), db))
    lg = _ret_log_decay(rheads)
    lnf = lnf_w.reshape(1, -1)
    hp = x_prompt.reshape(nb * seq, d)
    hs = x_sample.reshape(db * dseq, d)
    outs = [[] for _ in range(6)]
    for l in range(depth):
        w = _layer_weights(ln1_w[l], w_in[l], ret_gn_w[l], mla_q_norm_w[l], mla_w_uq[l], mla_kv_norm_w[l],
                           mla_w_uk[l], mla_w_uv[l], w_o[l], ln2_w[l], peer_w_q[l], peer_sub_keys[l], dims)
        last = l == depth - 1
        gb = nb // PROMPT_GROUPS if nb % PROMPT_GROUPS == 0 else nb

        def frame_ranges(g):
            step = seq // (PROMPT_HEAD_SPLIT if g == 0 else PROMPT_SPLIT)
            ok = gb == 1 and step > 0 and step % 256 == 0
            return [(lo, lo + step) for lo in range(0, seq, step)] if ok else [(0, seq)]

        streams = [_Stream(hp[g * gb * seq:(g + 1) * gb * seq], tabs_p, jnp.zeros((gb, rheads, dk, dv), F32),
                           nbatch=gb, ret_rows=min(256, seq), ret_chunk=CHUNK, tm=min(256, gb * seq),
                           ranges=frame_ranges(g))
                   for g in range(nb // gb)]
        streams.append(_Stream(hs, tabs_s, state_retention[l], nbatch=db, ret_rows=dseq, ret_chunk=dseq,
                               tm=min(256, db * dseq), ranges=[(0, dseq)],
                               cache=(cache_mla_ckv[l], cache_mla_krope[l])))
        results = _layer(streams, w, lg, _pack_rows(peer_u[l]), _pack_rows(peer_v[l]), lnf, dims,
                         final_norm=last)
        hp, c1, k1, s1 = (jnp.concatenate(p, axis=0) for p in zip(*results[:-1]))
        hs, c2, k2, s2 = results[-1]
        for acc, val in zip(outs, (c1.reshape(nb, seq, -1), k1.reshape(nb, seq, -1), s1,
                                   c2.reshape(db, dseq, -1), k2.reshape(db, dseq, -1), s2)):
            acc.append(val)
    return (hp.reshape(nb, seq, d), hs.reshape(db, dseq, d), *[jnp.stack(o) for o in outs])
```

```python
import functools

import jax
import jax.numpy as jnp
from jax import lax
from jax.experimental import pallas as pl
from jax.experimental.pallas import tpu as pltpu
from jax.experimental.pallas import tpu_sc as plsc

EPS = 1e-6
ROPE_BASE = 10000.0
CHUNK = 64
PEER_TOPK = 16
PROMPT_GROUPS = 8
PROMPT_HEAD_SPLIT = 4
PROMPT_SPLIT = 2
LANES = 128
MXU_DTYPE = jnp.bfloat16
VMEM_LIMIT_BYTES = 56 * 1024 * 1024

F32 = jnp.float32
NEG_INF = float("-inf")


def _mm(a, b):
    return jnp.dot(a.astype(MXU_DTYPE), b.astype(MXU_DTYPE), preferred_element_type=F32)


def _mm_nt(a, b):
    return lax.dot_general(a.astype(MXU_DTYPE), b.astype(MXU_DTYPE),
                           (((1,), (1,)), ((), ())), preferred_element_type=F32)


def _mm_tn(a, b):
    return lax.dot_general(a.astype(MXU_DTYPE), b.astype(MXU_DTYPE),
                           (((0,), (0,)), ((), ())), preferred_element_type=F32)


def _rms(x, w):
    return x * lax.rsqrt(jnp.mean(x * x, axis=-1, keepdims=True) + EPS) * w


def _rope(t, c, sa, sb, half):
    n = t.shape[1]
    return t * c + pltpu.roll(t, n - half, 1) * sa + pltpu.roll(t, half, 1) * sb


def _params(*sem):
    return pltpu.CompilerParams(dimension_semantics=sem, vmem_limit_bytes=VMEM_LIMIT_BYTES)


ORDER_SPEC = pl.BlockSpec(memory_space=pl.ANY)


def _ordered(body):
    def run(_order_ref, *refs):
        body(*refs)
    return run


def _inproj_body(x_ref, ln1_ref, win_ref, cr_ref, sar_ref, sbr_ref, cm_ref, sam_ref, sbm_ref,
                 qnw_ref, kvnw_ref, wuq_ref, wuk_ref, wuv_ref,
                 qr_ref, kr_ref, vr_ref, gate_ref, qm_ref, km_ref, vm_ref, ckv_ref, kpe_ref,
                 *, rw, qrank, kvrank, ret_half, mla_half, k_scale, nheads):
    n1 = _rms(x_ref[...], ln1_ref[...])
    proj = _mm(n1, win_ref[...])
    cr, sar, sbr = cr_ref[...], sar_ref[...], sbr_ref[...]
    qr_ref[...] = _rope(proj[:, 0:rw], cr, sar, sbr, ret_half)
    kr_ref[...] = _rope(proj[:, rw:2 * rw], cr, sar, sbr, ret_half) * k_scale
    vr_ref[...] = proj[:, 2 * rw:3 * rw]
    gate_ref[...] = proj[:, 3 * rw:4 * rw]
    o4 = 4 * rw
    o5 = o4 + qrank
    o6 = o5 + kvrank
    cm, sam, sbm = cm_ref[...], sam_ref[...], sbm_ref[...]
    tile = lambda t: jnp.concatenate([t] * nheads, axis=1)
    cq = _rms(proj[:, o4:o5], qnw_ref[...])
    qm = _rope(_mm(cq, wuq_ref[...]), tile(cm), tile(sam), tile(sbm), mla_half)
    qm_ref[...] = qm.astype(qm_ref.dtype)
    ckv = _rms(proj[:, o5:o6], kvnw_ref[...])
    ckv_ref[...] = ckv
    kpe = _rope(proj[:, o6:o6 + LANES], cm, sam, sbm, mla_half)
    kpe_ref[...] = kpe
    km_ref[...] = (_mm(ckv, wuk_ref[...]) + tile(kpe)).astype(km_ref.dtype)
    vm_ref[...] = _mm(ckv, wuv_ref[...]).astype(vm_ref.dtype)


def _inproj(order, x, tabs, ln1, win_p, qnw, kvnw, wuq_p, wuk_p, wuv_p, *, tm, dims):
    t, d = x.shape
    rw, nheads = dims["rw"], dims["nheads"]
    hp = nheads * LANES
    nblk_tab = tabs[0].shape[0] // tm
    row = lambda i: (i, 0)
    tab = lambda i: (i % nblk_tab, 0)
    full = lambda i: (0, 0)
    fs = lambda a: pl.BlockSpec(a.shape, full)
    in_specs = [pl.BlockSpec((tm, d), row), fs(ln1), fs(win_p)]
    in_specs += [pl.BlockSpec((tm, rw), tab)] * 3 + [pl.BlockSpec((tm, LANES), tab)] * 3
    in_specs += [fs(qnw), fs(kvnw), fs(wuq_p), fs(wuk_p), fs(wuv_p)]
    out_shape = [jax.ShapeDtypeStruct((t, rw), F32)] * 4
    out_shape += [jax.ShapeDtypeStruct((t, hp), MXU_DTYPE)] * 3
    out_shape += [jax.ShapeDtypeStruct((t, dims["kvrank"]), F32), jax.ShapeDtypeStruct((t, LANES), F32)]
    out_specs = [pl.BlockSpec((tm, rw), row)] * 4 + [pl.BlockSpec((tm, hp), row)] * 3
    out_specs += [pl.BlockSpec((tm, dims["kvrank"]), row), pl.BlockSpec((tm, LANES), row)]
    body = functools.partial(
        _inproj_body, rw=rw, qrank=dims["qrank"], kvrank=dims["kvrank"], ret_half=dims["ret_dk"] // 2,
        mla_half=dims["mla_rope"] // 2, k_scale=dims["ret_dk"] ** -0.5, nheads=nheads)
    return pl.pallas_call(
        _ordered(body), grid=(t // tm,), in_specs=[ORDER_SPEC] + in_specs, out_specs=out_specs,
        out_shape=out_shape, compiler_params=_params("parallel"), name="inproj",
    )(order, x, ln1, win_p, *tabs, qnw, kvnw, wuq_p, wuk_p, wuv_p)


def _retention_body(lg_ref, q_ref, k_ref, v_ref, s0_ref, o_ref, sout_ref, s_scr, *, rows, chunk, dk):
    hp = pl.program_id(1)
    j = pl.program_id(2)

    @pl.when(j == 0)
    def _():
        s_scr[...] = s0_ref[0, 0]

    lane = lax.broadcasted_iota(jnp.int32, (1, LANES), 1)
    is_a = lane < dk
    lg_a = lg_ref[2 * hp]
    lg_b = lg_ref[2 * hp + 1]
    lgl = jnp.where(is_a, lg_a, lg_b)
    r = lax.broadcasted_iota(jnp.int32, (rows, 1), 0).astype(F32)
    q, k, v = q_ref[...], k_ref[...], v_ref[...]
    q_dec = q * jnp.exp(lgl * (r + 1.0))
    k_dec = k * jnp.exp(lgl * (float(rows) - 1.0 - r))
    ri = lax.broadcasted_iota(jnp.int32, (rows, rows), 0)
    ci = lax.broadcasted_iota(jnp.int32, (rows, rows), 1)
    dist = jnp.abs(ri - ci).astype(F32)
    visible = (ci // chunk) <= (ri // chunk)
    o = _mm(q_dec, s_scr[...])
    for first, lg in ((True, lg_a), (False, lg_b)):
        sel = is_a if first else jnp.logical_not(is_a)
        qh = jnp.where(sel, q, 0.0)
        vh = jnp.where(sel, v, 0.0)
        decay = jnp.where(visible, jnp.exp(lg * dist), 0.0)
        o = o + _mm(_mm_nt(qh, k) * decay, vh)
    o_ref[...] = o
    sr = lax.broadcasted_iota(jnp.int32, (LANES, LANES), 0) < dk
    sc = lax.broadcasted_iota(jnp.int32, (LANES, LANES), 1) < dk
    kv = jnp.where(sr == sc, _mm_tn(k_dec, v), 0.0)
    s_new = jnp.exp(lgl * float(rows)) * s_scr[...] + kv
    s_scr[...] = s_new

    @pl.when(j == pl.num_programs(2) - 1)
    def _():
        sout_ref[0, 0] = s_new


def _retention(lg, q, k, v, s0_pairs, *, nbatch, rows, chunk, dk):
    t, w = q.shape
    npairs = w // LANES
    nblk = t // (nbatch * rows)
    blk = pl.BlockSpec((rows, LANES), lambda b, p, j: (b * nblk + j, p))
    st = pl.BlockSpec((1, 1, LANES, LANES), lambda b, p, j: (b, p, 0, 0))
    return pl.pallas_call(
        functools.partial(_retention_body, rows=rows, chunk=chunk, dk=dk),
        grid=(nbatch, npairs, nblk),
        in_specs=[pl.BlockSpec(memory_space=pltpu.SMEM), blk, blk, blk, st],
        out_specs=[blk, st],
        out_shape=[jax.ShapeDtypeStruct((t, w), F32),
                   jax.ShapeDtypeStruct((nbatch, npairs, LANES, LANES), F32)],
        scratch_shapes=[pltpu.VMEM((LANES, LANES), F32)],
        compiler_params=_params("parallel", "parallel", "arbitrary"), name="retention",
    )(lg, q, k, v, s0_pairs)


def _flash_body(q_ref, k_ref, v_ref, o_ref, *, tq, tile0, scale, nheads, chunk):
    i = pl.program_id(1) + tile0
    ri = lax.broadcasted_iota(jnp.int32, (tq, tq), 0) // chunk
    ci = lax.broadcasted_iota(jnp.int32, (tq, tq), 1) // chunk
    visible = ci <= ri

    def head(h):
        cols = slice(h * LANES, (h + 1) * LANES)
        q = q_ref[:, cols]

        def step(j, carry, diagonal):
            m, l, acc = carry
            off = pl.multiple_of(j * tq, tq)
            s = _mm_nt(q, k_ref[pl.ds(off, tq), cols]) * scale
            if diagonal:
                s = jnp.where(visible, s, NEG_INF)
            m_new = jnp.maximum(m, jnp.max(s, axis=1, keepdims=True))
            alpha = jnp.exp(m - m_new)
            p = jnp.exp(s - m_new)
            l = alpha * l + jnp.sum(p, axis=1, keepdims=True)
            acc = alpha * acc + _mm(p, v_ref[pl.ds(off, tq), cols])
            return m_new, l, acc

        init = (jnp.full((tq, 1), NEG_INF, F32), jnp.zeros((tq, 1), F32), jnp.zeros((tq, LANES), F32))
        carry = lax.fori_loop(0, i, functools.partial(step, diagonal=False), init)
        _, l, acc = step(i, carry, True)
        return acc / l

    for p in range(nheads // 2):
        o_ref[:, p * LANES:(p + 1) * LANES] = head(2 * p) + head(2 * p + 1)


def _flash(order, qm, km, vm, *, nbatch, tq, lo, hi, scale, nheads, chunk):
    t, hp = qm.shape
    s = t // nbatch
    nq = s // tq
    tile0 = lo // tq
    nqr = (hi - lo) // tq
    ow = nheads // 2 * LANES
    return pl.pallas_call(
        _ordered(functools.partial(_flash_body, tq=tq, tile0=tile0, scale=scale, nheads=nheads, chunk=chunk)),
        grid=(nbatch, nqr),
        in_specs=[ORDER_SPEC,
                  pl.BlockSpec((tq, hp), lambda b, i: (b * nq + tile0 + i, 0)),
                  pl.BlockSpec((s, hp), lambda b, i: (b, 0)),
                  pl.BlockSpec((s, hp), lambda b, i: (b, 0))],
        out_specs=pl.BlockSpec((tq, ow), lambda b, i: (b * nqr + i, 0)),
        out_shape=jax.ShapeDtypeStruct((nbatch * (hi - lo), ow), F32),
        compiler_params=_params("parallel", "arbitrary"), name="flash_mla",
    )(order, qm, km, vm)


def _decode_attn_body(q_ref, cpast_ref, kpast_ref, cnew_ref, knew_ref, wuk_ref, wuv_ref, o_ref,
                      *, nheads, nope, rope, scale):
    c_past = cpast_ref[0]
    k_past = kpast_ref[0]
    c_new = cnew_ref[...]
    k_new = knew_ref[:, nope:nope + rope]
    outs = []
    for h in range(nheads):
        q = q_ref[:, h * LANES:(h + 1) * LANES]
        q_lat = _mm(q, wuk_ref[h])
        q_pe = q[:, nope:nope + rope]
        s_p = (_mm_nt(q_lat, c_past) + _mm_nt(q_pe, k_past)) * scale
        s_n = (_mm_nt(q_lat, c_new) + _mm_nt(q_pe, k_new)) * scale
        m = jnp.maximum(jnp.max(s_p, axis=1, keepdims=True), jnp.max(s_n, axis=1, keepdims=True))
        p_p = jnp.exp(s_p - m)
        p_n = jnp.exp(s_n - m)
        l = jnp.sum(p_p, axis=1, keepdims=True) + jnp.sum(p_n, axis=1, keepdims=True)
        o_lat = (_mm(p_p, c_past) + _mm(p_n, c_new)) / l
        outs.append(_mm(o_lat, wuv_ref[h]))
    o_ref[...] = jnp.concatenate(outs, axis=1)


def _decode_attn(qm, c_past, k_past, c_new, kpe_new, wuk3, wuv3, *, nq, nope, rope, scale):
    nb, past, kvr = c_past.shape
    nheads, _, vdim = wuv3.shape
    t, hp = qm.shape
    row = lambda b: (b, 0)
    full3 = lambda b: (0, 0, 0)
    return pl.pallas_call(
        functools.partial(_decode_attn_body, nheads=nheads, nope=nope, rope=rope, scale=scale),
        grid=(nb,),
        in_specs=[pl.BlockSpec((nq, hp), row),
                  pl.BlockSpec((1, past, kvr), lambda b: (b, 0, 0)),
                  pl.BlockSpec((1, past, rope), lambda b: (b, 0, 0)),
                  pl.BlockSpec((nq, kvr), row),
                  pl.BlockSpec((nq, LANES), row),
                  pl.BlockSpec(wuk3.shape, full3),
                  pl.BlockSpec(wuv3.shape, full3)],
        out_specs=pl.BlockSpec((nq, nheads * vdim), row),
        out_shape=jax.ShapeDtypeStruct((t, nheads * vdim), F32),
        compiler_params=_params("parallel"), name="decode_mla",
    )(qm, c_past, k_past, c_new, kpe_new, wuk3, wuv3)


def _split3(x):
    a = x.astype(MXU_DTYPE)
    r = x - a.astype(F32)
    b = r.astype(MXU_DTYPE)
    c = (r - b.astype(F32)).astype(MXU_DTYPE)
    return a, b, c


def _group_mean(x, avg):
    a, b, c = _split3(x)
    dot = lambda t: jnp.dot(t, avg, preferred_element_type=F32)
    return dot(a) + dot(b) + dot(c)


def _topk_rows(s, payload, kk):
    n = s.shape[0]
    rid = lax.broadcasted_iota(jnp.int32, s.shape, 0)
    vals, pays = [], []
    for _ in range(kk):
        mx = jnp.max(s, axis=0, keepdims=True)
        first = jnp.min(jnp.where(s == mx, rid, n), axis=0, keepdims=True)
        hit = rid == first
        vals.append(mx)
        pays.append(jnp.max(jnp.where(hit, payload, -1), axis=0, keepdims=True))
        s = jnp.where(hit, NEG_INF, s)
    return jnp.concatenate(vals, axis=0), jnp.concatenate(pays, axis=0)


def _mix_route_body(x_ref, ret_ref, gate_ref, mla_ref, gnw_ref, avg_ref, wo_ref, ln2_ref, wq_ref, keys_ref,
                    h_ref, hn_ref, idx_ref, g_ref, *, rw, pheads, nkeys, topk):
    ret = ret_ref[...]
    avg = avg_ref[...]
    mu = _group_mean(ret, avg)
    cen = ret - mu
    var = _group_mean(cen * cen, avg)
    gate = gate_ref[...]
    y = cen * lax.rsqrt(var + EPS) * gnw_ref[...] * (gate * jax.nn.sigmoid(gate))
    h = x_ref[...] + _mm(y, wo_ref[0:rw, :]) + _mm(mla_ref[...], wo_ref[rw:, :])
    h_ref[...] = h
    hn = _rms(h, ln2_ref[...])
    hn_ref[...] = hn
    qp = _mm(hn, wq_ref[...])
    kid = lax.broadcasted_iota(jnp.int32, (nkeys, qp.shape[0]), 0)
    idx_rows, g_rows = [], []
    for hd in range(pheads):
        ts, ti = [], []
        for half in range(2):
            c = (2 * hd + half) * LANES
            st = _mm_nt(keys_ref[2 * hd + half], qp[:, c:c + LANES])
            v, i = _topk_rows(st, kid, topk)
            ts.append(v)
            ti.append(i)
        cand = jnp.concatenate([ts[0][a:a + 1, :] + ts[1] for a in range(topk)], axis=0)
        cidx = jnp.concatenate([ti[0][a:a + 1, :] * nkeys + ti[1] for a in range(topk)], axis=0)
        best, expert = _topk_rows(cand, cidx, topk)
        e = jnp.exp(best - best[0:1, :])
        g_rows.append(e / jnp.sum(e, axis=0, keepdims=True))
        idx_rows.append(expert)
    idx_ref[...] = jnp.concatenate(idx_rows, axis=0).T
    g_ref[...] = jnp.concatenate(g_rows, axis=0).T


def _mix_route(x, ret_o, gate, mla_o, gnw, avg, wo, ln2, wq, keys, *, tm, row0, pheads, nkeys, topk):
    t = mla_o.shape[0]
    d = x.shape[1]
    rw = ret_o.shape[1]
    nsel = pheads * topk
    blk0 = row0 // tm
    row = lambda i: (i, 0)
    src = lambda i: (blk0 + i, 0)
    fs = lambda a: pl.BlockSpec(a.shape, lambda i: (0,) * a.ndim)
    return pl.pallas_call(
        functools.partial(_mix_route_body, rw=rw, pheads=pheads, nkeys=nkeys, topk=topk),
        grid=(t // tm,),
        in_specs=[pl.BlockSpec((tm, d), src), pl.BlockSpec((tm, rw), src), pl.BlockSpec((tm, rw), src),
                  pl.BlockSpec((tm, mla_o.shape[1]), row), fs(gnw), fs(avg), fs(wo), fs(ln2), fs(wq), fs(keys)],
        out_specs=[pl.BlockSpec((tm, d), row), pl.BlockSpec((tm, d), row),
                   pl.BlockSpec((tm, nsel), row), pl.BlockSpec((tm, nsel), row)],
        out_shape=[jax.ShapeDtypeStruct((t, d), F32), jax.ShapeDtypeStruct((t, d), F32),
                   jax.ShapeDtypeStruct((t, nsel), jnp.int32),
                   jax.ShapeDtypeStruct((t, nsel), F32)],
        compiler_params=_params("parallel"), name="mix_route",
    )(x, ret_o, gate, mla_o, gnw, avg, wo, ln2, wq, keys)


def _gelu_gate_body(hid_ref, g_ref, a_ref):
    hid = hid_ref[...]
    a_ref[...] = 0.5 * hid * (1.0 + lax.erf(hid * (2.0 ** -0.5))) * g_ref[...]


def _gelu_gate(order, hid, g, *, tm):
    t, n = hid.shape
    blk = pl.BlockSpec((tm, n), lambda i: (i, 0))
    return pl.pallas_call(
        _ordered(_gelu_gate_body), grid=(t // tm,), in_specs=[ORDER_SPEC, blk, blk], out_specs=blk,
        out_shape=jax.ShapeDtypeStruct((t, n), F32), compiler_params=_params("parallel"), name="gelu_gate",
    )(order, hid, g)


def _residual_body(h_ref, p_ref, lnf_ref, o_ref, *, final_norm):
    out = h_ref[...] + p_ref[...]
    if final_norm:
        out = _rms(out, lnf_ref[...])
    o_ref[...] = out


def _residual(order, h, peer, lnf, *, tm, final_norm):
    t, d = h.shape
    blk = pl.BlockSpec((tm, d), lambda i: (i, 0))
    return pl.pallas_call(
        _ordered(functools.partial(_residual_body, final_norm=final_norm)), grid=(t // tm,),
        in_specs=[ORDER_SPEC, blk, blk, pl.BlockSpec((1, d), lambda i: (0, 0))], out_specs=blk,
        out_shape=jax.ShapeDtypeStruct((t, d), F32), compiler_params=_params("parallel"), name="residual_norm",
    )(order, h, peer, lnf)


SC_CORES = 2
SC_SUBCORES = 16
SC_LANES = 16
SC_RING = 4
SC_BATCH = 64


def _sc_worker_id():
    return lax.axis_index("s") * SC_CORES + lax.axis_index("c")


def _sc_ring(nq, start, wait, compute):
    for s in range(SC_RING - 1):
        start(s, s)

    def step(q, s):
        nxt = q + SC_RING - 1

        @pl.when(nxt < nq)
        def _():
            start(nxt, (s + SC_RING - 1) % SC_RING)

        wait(q, s)
        compute(q, s)

    full = nq // SC_RING * SC_RING

    @pl.loop(0, full, step=SC_RING)
    def _(q0):
        for s in range(SC_RING):
            step(q0 + s, s)

    for s in range(nq - full):
        step(jnp.int32(full + s), s)


def _tree_sum(terms):
    while len(terms) > 1:
        terms = [a + b for a, b in zip(terms[0::2], terms[1::2])]
    return terms[0]


def _pack_rows(tab):
    half = tab.shape[1] // 2
    low = lax.bitcast_convert_type(tab[:, :half].astype(jnp.bfloat16), jnp.uint16).astype(jnp.int32)
    bits = lax.bitcast_convert_type(tab[:, half:], jnp.int32)
    sign = bits & jnp.int32(-2 ** 31)
    mag = bits & jnp.int32(2 ** 31 - 1)
    high = lax.shift_right_logical(jnp.maximum(mag - low + 2 ** 15, 0), 16)
    return sign | lax.shift_left(high, 16) | low


def _unpack_pair(w):
    return lax.bitcast_convert_type(lax.shift_left(w, jnp.int32(16)), F32), lax.bitcast_convert_type(w, F32)


def _peer_hidden_sc(order, xn, idx, u_tab):
    t, d = xn.shape
    nsel = idx.shape[1]
    nw = SC_CORES * SC_SUBCORES
    per_w = t // nw
    tb = min(SC_BATCH, per_w)
    nchunk = nsel // SC_LANES
    shift = nchunk.bit_length() - 1
    half = d // 2
    nword = half // SC_LANES
    nq = tb * nchunk
    assert per_w * nw == t and per_w % tb == 0 and nchunk == 1 << shift and nq >= SC_RING
    assert u_tab.shape[1] == half
    mesh = plsc.VectorSubcoreMesh(core_axis_name="c", subcore_axis_name="s")

    def body(_order_hbm, x_hbm, idx_hbm, u_hbm, out_hbm, idx_v, x_v, ubuf, hid_v, sem):
        wid = _sc_worker_id()
        lane = lax.iota(jnp.int32, SC_LANES)

        def gather(q, slot):
            tok = lax.shift_right_logical(q, shift)
            ch = q & (nchunk - 1)
            rows = idx_v.at[tok, pl.ds(ch * SC_LANES, SC_LANES)]
            return pltpu.make_async_copy(u_hbm.at[rows], ubuf.at[slot], sem.at[slot])

        def compute(q, slot):
            tok = lax.shift_right_logical(q, shift)
            ch = q & (nchunk - 1)

            @plsc.parallel_loop(0, nword, carry=tuple(jnp.zeros((SC_LANES,), F32) for _ in range(SC_LANES)))
            def accs(c, acc):
                off = pl.multiple_of(c * SC_LANES, SC_LANES)
                x_lo = x_v[tok, pl.ds(off, SC_LANES)]
                x_hi = x_v[tok, pl.ds(pl.multiple_of(half + off, SC_LANES), SC_LANES)]
                new = []
                for k, a in enumerate(acc):
                    lo, hi = _unpack_pair(ubuf[slot, k, pl.ds(off, SC_LANES)])
                    new.append(a + x_lo * lo + x_hi * hi)
                return tuple(new)

            out = jnp.zeros((SC_LANES,), F32)
            for k in range(SC_LANES):
                out = jnp.where(lane == k, jnp.sum(accs[k]), out)
            hid_v[tok, pl.ds(ch * SC_LANES, SC_LANES)] = out

        @pl.loop(0, per_w // tb)
        def _(b):
            base = wid * per_w + b * tb
            pltpu.sync_copy(idx_hbm.at[pl.ds(base, tb)], idx_v)
            pltpu.sync_copy(x_hbm.at[pl.ds(base, tb)], x_v)
            _sc_ring(nq, lambda q, s: gather(q, s).start(), lambda q, s: gather(q, s).wait(), compute)
            pltpu.sync_copy(hid_v, out_hbm.at[pl.ds(base, tb)])

    return pl.kernel(
        body, out_type=jax.ShapeDtypeStruct((t, nsel), F32), mesh=mesh,
        scratch_types=[pltpu.VMEM((tb, nsel), jnp.int32), pltpu.VMEM((tb, d), F32),
                       pltpu.VMEM((SC_RING, SC_LANES, half), jnp.int32), pltpu.VMEM((tb, nsel), F32),
                       pltpu.SemaphoreType.DMA((SC_RING,))],
        compiler_params=pltpu.CompilerParams(needs_layout_passes=False), name="peer_hidden_sc",
    )(order, xn, idx, u_tab)


def _peer_mix_sc(act, idx, v_tab):
    t, nsel = act.shape
    half = v_tab.shape[1]
    d = 2 * half
    nw = SC_CORES * SC_SUBCORES
    per_w = t // nw
    tb = min(SC_BATCH, per_w)
    nchunk = nsel // SC_LANES
    shift = nchunk.bit_length() - 1
    ncol = d // SC_LANES
    nword = half // SC_LANES
    nq = tb * nchunk
    assert per_w * nw == t and per_w % tb == 0 and nchunk == 1 << shift and nq >= SC_RING
    mesh = plsc.VectorSubcoreMesh(core_axis_name="c", subcore_axis_name="s")

    def body(a_hbm, idx_hbm, v_hbm, out_hbm, idx_v, a_v, vbuf, o_v, sem):
        wid = _sc_worker_id()
        zero = jnp.zeros((SC_LANES,), F32)

        def gather(q, slot):
            tok = lax.shift_right_logical(q, shift)
            ch = q & (nchunk - 1)
            rows = idx_v.at[tok, pl.ds(ch * SC_LANES, SC_LANES)]
            return pltpu.make_async_copy(v_hbm.at[rows], vbuf.at[slot], sem.at[slot])

        def compute(q, slot):
            tok = lax.shift_right_logical(q, shift)
            ch = q & (nchunk - 1)
            tok_v = jnp.full((SC_LANES,), tok, jnp.int32)
            col_v = jnp.full((SC_LANES,), ch * SC_LANES, jnp.int32)
            w = [plsc.load_gather(a_v, [tok_v, col_v + k]) for k in range(SC_LANES)]

            @plsc.parallel_loop(0, nword)
            def _(c):
                off = pl.multiple_of(c * SC_LANES, SC_LANES)
                cs_lo = pl.ds(off, SC_LANES)
                cs_hi = pl.ds(pl.multiple_of(half + off, SC_LANES), SC_LANES)
                pairs = [_unpack_pair(vbuf[slot, k, cs_lo]) for k in range(SC_LANES)]
                o_v[tok, cs_lo] = o_v[tok, cs_lo] + _tree_sum([w[k] * p[0] for k, p in enumerate(pairs)])
                o_v[tok, cs_hi] = o_v[tok, cs_hi] + _tree_sum([w[k] * p[1] for k, p in enumerate(pairs)])

        @pl.loop(0, per_w // tb)
        def _(b):
            base = wid * per_w + b * tb
            pltpu.sync_copy(idx_hbm.at[pl.ds(base, tb)], idx_v)
            pltpu.sync_copy(a_hbm.at[pl.ds(base, tb)], a_v)

            @pl.loop(0, tb)
            def _(r):
                @pl.loop(0, ncol)
                def _(c):
                    o_v[r, pl.ds(pl.multiple_of(c * SC_LANES, SC_LANES), SC_LANES)] = zero

            _sc_ring(nq, lambda q, s: gather(q, s).start(), lambda q, s: gather(q, s).wait(), compute)
            pltpu.sync_copy(o_v, out_hbm.at[pl.ds(base, tb)])

    return pl.kernel(
        body, out_type=jax.ShapeDtypeStruct((t, d), F32), mesh=mesh,
        scratch_types=[pltpu.VMEM((tb, nsel), jnp.int32), pltpu.VMEM((tb, nsel), F32),
                       pltpu.VMEM((SC_RING, SC_LANES, half), jnp.int32), pltpu.VMEM((tb, d), F32),
                       pltpu.SemaphoreType.DMA((SC_RING,))],
        compiler_params=pltpu.CompilerParams(needs_layout_passes=False), name="peer_mix_sc",
    )(act, idx, v_tab)


def _rope_tables(pos, half, group, width, lo):
    inv = ROPE_BASE ** (-jnp.arange(half, dtype=F32) / half)
    ang = pos.astype(F32)[:, None] * inv[None, :]
    cos, sin = jnp.cos(ang), jnp.sin(ang)
    n = pos.shape[0]
    reps = width // group
    pad_hi = group - lo - 2 * half
    blk = lambda a, b, fill: jnp.concatenate(
        [jnp.full((n, lo), fill, F32), a, b, jnp.full((n, pad_hi), fill, F32)], axis=1)
    z = jnp.zeros_like(sin)
    c = blk(cos, cos, 1.0)
    sa = blk(-sin, z, 0.0)
    sb = blk(z, sin, 0.0)
    return [jnp.tile(a, (1, reps)) for a in (c, sa, sb)]


def _ret_log_decay(nheads):
    return jnp.log(1.0 - jnp.exp2(-5.0 - jnp.arange(nheads, dtype=F32)))


def _pair_states(s):
    b, h, dk, dv = s.shape
    s = s.reshape(b, h // 2, 2, dk, dv)
    z = jnp.zeros_like(s[:, :, 0])
    top = jnp.concatenate([s[:, :, 0], z], axis=-1)
    bot = jnp.concatenate([z, s[:, :, 1]], axis=-1)
    return jnp.concatenate([top, bot], axis=-2)


def _unpair_states(sp, dk, dv):
    b, hp = sp.shape[:2]
    return jnp.stack([sp[:, :, :dk, :dv], sp[:, :, dk:, dv:]], axis=2).reshape(b, 2 * hp, dk, dv)


def _layer_weights(ln1_w, w_in, ret_gn_w, q_norm_w, w_uq, kv_norm_w, w_uk, w_uv, w_o, ln2_w,
                   peer_w_q, peer_sub_keys, dims):
    d = w_in.shape[0]
    nheads, nope, rope, vdim = dims["nheads"], dims["nope"], dims["mla_rope"], dims["vdim"]
    o6 = 4 * dims["rw"] + dims["qrank"] + dims["kvrank"]
    zc = lambda r, c: jnp.zeros((r, c), F32)
    win_p = jnp.concatenate([w_in[:, :o6], zc(d, nope), w_in[:, o6:], zc(d, LANES - nope - rope)], axis=1)
    qr, kr = w_uq.shape[0], w_uk.shape[0]
    wuq_p = jnp.concatenate([w_uq, jnp.zeros((qr, nheads, LANES - nope - rope), F32)], axis=2).reshape(qr, -1)
    wuk_p = jnp.concatenate([w_uk, jnp.zeros((kr, nheads, LANES - nope), F32)], axis=2).reshape(kr, -1)
    zv = jnp.zeros((kr, nheads // 2, LANES - vdim), F32)
    wv = w_uv.reshape(kr, nheads // 2, 2, vdim)
    wuv_p = jnp.concatenate([wv[:, :, 0], zv, zv, wv[:, :, 1]], axis=2).reshape(kr, -1)
    wuk3 = jnp.concatenate([jnp.transpose(w_uk, (1, 2, 0)),
                            jnp.zeros((nheads, LANES - nope, kr), F32)], axis=1)
    wuv3 = jnp.transpose(w_uv, (1, 0, 2))
    gidx = jnp.arange(dims["rw"]) // dims["ret_dv"]
    avg = (gidx[:, None] == gidx[None, :]).astype(F32) / dims["ret_dv"]
    keys = peer_sub_keys.reshape(-1, peer_sub_keys.shape[2], peer_sub_keys.shape[3])
    c = lambda a: a.astype(MXU_DTYPE)
    r2 = lambda a: a.reshape(1, -1)
    return dict(ln1=r2(ln1_w), win_p=c(win_p), gnw=r2(ret_gn_w), qnw=r2(q_norm_w), kvnw=r2(kv_norm_w),
                wuq_p=c(wuq_p), wuk_p=c(wuk_p), wuv_p=c(wuv_p), wuk3=c(wuk3), wuv3=c(wuv3), avg=c(avg),
                wo=c(w_o), ln2=r2(ln2_w), wq=c(peer_w_q), keys=c(keys))


class _Stream:
    def __init__(self, x, tabs, s0, *, nbatch, ret_rows, ret_chunk, tm, ranges, cache=None):
        self.x, self.tabs, self.s0, self.cache = x, tabs, s0, cache
        self.nbatch, self.ret_rows, self.ret_chunk, self.tm, self.ranges = nbatch, ret_rows, ret_chunk, tm, ranges
        self.seq = x.shape[0] // nbatch
        assert nbatch == 1 or ranges == [(0, self.seq)]
        self.pre = None
        self.outs = []


def _layer(streams, w, lg, u_tab, v_tab, lnf, dims, *, final_norm):
    units = [(st, lo, hi) for st in streams for lo, hi in st.ranges]
    n = len(units)
    nheads, dk = dims["nheads"], dims["ret_dk"]
    scale = (dims["nope"] + dims["mla_rope"]) ** -0.5
    built = [None] * n

    def build(i, order):
        st, lo, hi = units[i]
        if st.pre is None:
            proj = _inproj(order, st.x, st.tabs, w["ln1"], w["win_p"], w["qnw"], w["kvnw"], w["wuq_p"], w["wuk_p"],
                           w["wuv_p"], tm=st.tm, dims=dims)
            ret_o, s_pairs = _retention(lg, *proj[:3], _pair_states(st.s0), nbatch=st.nbatch, rows=st.ret_rows,
                                        chunk=st.ret_chunk, dk=dk)
            st.pre = list(proj) + [ret_o, s_pairs]
        qr, kr, vr, gate, qm, km, vm, ckv, kpe, ret_o, s_pairs = st.pre
        if st.cache is None:
            mla_o = _flash(order, qm, km, vm, nbatch=st.nbatch, tq=min(256, st.seq), lo=lo, hi=hi, scale=scale,
                           nheads=nheads, chunk=CHUNK)
        else:
            mla_o = _decode_attn(qm, st.cache[0], st.cache[1], ckv, kpe, w["wuk3"], w["wuv3"], nq=st.seq,
                                 nope=dims["nope"], rope=dims["mla_rope"], scale=scale)
        h, hn, idx, g = _mix_route(st.x, ret_o, gate, mla_o, w["gnw"], w["avg"], w["wo"], w["ln2"], w["wq"],
                                   w["keys"], tm=st.tm, row0=lo, pheads=dims["pheads"], nkeys=dims["nkeys"],
                                   topk=dims["topk"])
        built[i] = (h, idx, g, _peer_hidden_sc(peers[i - 2] if i >= 2 else lnf, hn, idx, u_tab))

    for i in range(min(2, n)):
        build(i, lnf)
    acts, peers = [], []
    for i in range(n):
        h, idx, g, hid = built[i]
        acts.append(_gelu_gate(built[i + 1][1] if i + 1 < n else lnf, hid, g, tm=units[i][0].tm))
        peers.append(_peer_mix_sc(acts[i], idx, v_tab))
        if i + 2 < n:
            build(i + 2, acts[i])
    for i, (st, lo, hi) in enumerate(units):
        st.outs.append(_residual(acts[min(i + 2, n - 1)], built[i][0], peers[i], lnf, tm=st.tm,
                                 final_norm=final_norm))
    nope, rope, dv = dims["nope"], dims["mla_rope"], dims["ret_dv"]
    results = []
    for st in streams:
        out = st.outs[0] if len(st.outs) == 1 else jnp.concatenate(st.outs, axis=0)
        results.append((out, st.pre[7], st.pre[8][:, nope:nope + rope], _unpair_states(st.pre[10], dk, dv)))
    return results


def kernel(x_prompt, x_sample, cache_mla_ckv, cache_mla_krope, state_retention, ln1_w, w_in, ret_gn_w,
           mla_q_norm_w, mla_w_uq, mla_kv_norm_w, mla_w_uk, mla_w_uv, w_o, ln2_w, peer_w_q, peer_sub_keys,
           peer_u, peer_v, lnf_w):
    depth = w_in.shape[0]
    nb, seq, d = x_prompt.shape
    db, dseq, _ = x_sample.shape
    past = cache_mla_ckv.shape[2]
    rheads, dk, dv = state_retention.shape[2:]
    nkeys = peer_sub_keys.shape[3]
    dims = dict(rw=rheads * dk, ret_dk=dk, ret_dv=dv, qrank=mla_w_uq.shape[1], kvrank=mla_w_uk.shape[1],
                nheads=mla_w_uq.shape[2], nope=mla_w_uk.shape[3], vdim=mla_w_uv.shape[3],
                mla_rope=mla_w_uq.shape[3] - mla_w_uk.shape[3], pheads=peer_sub_keys.shape[1], nkeys=nkeys,
                topk=PEER_TOPK)
    assert rheads * dk == rheads * dv and dims["nheads"] % 2 == 0 and dk * 2 == LANES and dims["vdim"] * 2 == LANES

    def tables(pos):
        return (_rope_tables(pos, dk // 2, dk, dims["rw"], 0)
                + _rope_tables(pos, dims["mla_rope"] // 2, LANES, LANES, dims["nope"]))

    tabs_p = tables(jnp.arange(seq))
    tabs_s = tables(jnp.tile(past + jnp.arange(dseq), db))
    lg = _ret_log_decay(rheads)
    lnf = lnf_w.reshape(1, -1)
    hp = x_prompt.reshape(nb * seq, d)
    hs = x_sample.reshape(db * dseq, d)
    outs = [[] for _ in range(6)]
    for l in range(depth):
        w = _layer_weights(ln1_w[l], w_in[l], ret_gn_w[l], mla_q_norm_w[l], mla_w_uq[l], mla_kv_norm_w[l],
                           mla_w_uk[l], mla_w_uv[l], w_o[l], ln2_w[l], peer_w_q[l], peer_sub_keys[l], dims)
        last = l == depth - 1
        gb = nb // PROMPT_GROUPS if nb % PROMPT_GROUPS == 0 else nb

        def frame_ranges(g):
            step = seq // (PROMPT_HEAD_SPLIT if g == 0 else PROMPT_SPLIT)
            ok = gb == 1 and step > 0 and step % 256 == 0
            return [(lo, lo + step) for lo in range(0, seq, step)] if ok else [(0, seq)]

        streams = [_Stream(hp[g * gb * seq:(g + 1) * gb * seq], tabs_p, jnp.zeros((gb, rheads, dk, dv), F32),
                           nbatch=gb, ret_rows=min(256, seq), ret_chunk=CHUNK, tm=min(256, gb * seq),
                           ranges=frame_ranges(g))
                   for g in range(nb // gb)]
        streams.append(_Stream(hs, tabs_s, state_retention[l], nbatch=db, ret_rows=dseq, ret_chunk=dseq,
                               tm=min(256, db * dseq), ranges=[(0, dseq)],
                               cache=(cache_mla_ckv[l], cache_mla_krope[l])))
        results = _layer(streams, w, lg, _pack_rows(peer_u[l]), _pack_rows(peer_v[l]), lnf, dims,
                         final_norm=last)
        hp, c1, k1, s1 = (jnp.concatenate(p, axis=0) for p in zip(*results[:-1]))
        hs, c2, k2, s2 = results[-1]
        for acc, val in zip(outs, (c1.reshape(nb, seq, -1), k1.reshape(nb, seq, -1), s1,
                                   c2.reshape(db, dseq, -1), k2.reshape(db, dseq, -1), s2)):
            acc.append(val)
    return (hp.reshape(nb, seq, d), hs.reshape(db, dseq, d), *[jnp.stack(o) for o in outs])
```

```python
import functools

import jax
import jax.numpy as jnp
from jax import lax
from jax.experimental import pallas as pl
from jax.experimental.pallas import tpu as pltpu
from jax.experimental.pallas import tpu_sc as plsc

EPS = 1e-6
ROPE_BASE = 10000.0
CHUNK = 64
PEER_TOPK = 16
PROMPT_GROUPS = 8
PROMPT_HEAD_SPLIT = 4
PROMPT_SPLIT = 2
LANES = 128
MXU_DTYPE = jnp.bfloat16
VMEM_LIMIT_BYTES = 56 * 1024 * 1024

F32 = jnp.float32
NEG_INF = float("-inf")


def _mm(a, b):
    return jnp.dot(a.astype(MXU_DTYPE), b.astype(MXU_DTYPE), preferred_element_type=F32)


def _mm_nt(a, b):
    return lax.dot_general(a.astype(MXU_DTYPE), b.astype(MXU_DTYPE),
                           (((1,), (1,)), ((), ())), preferred_element_type=F32)


def _mm_tn(a, b):
    return lax.dot_general(a.astype(MXU_DTYPE), b.astype(MXU_DTYPE),
                           (((0,), (0,)), ((), ())), preferred_element_type=F32)


def _rms(x, w):
    return x * lax.rsqrt(jnp.mean(x * x, axis=-1, keepdims=True) + EPS) * w


def _rope(t, c, sa, sb, half):
    n = t.shape[1]
    return t * c + pltpu.roll(t, n - half, 1) * sa + pltpu.roll(t, half, 1) * sb


def _params(*sem):
    return pltpu.CompilerParams(dimension_semantics=sem, vmem_limit_bytes=VMEM_LIMIT_BYTES)


ORDER_SPEC = pl.BlockSpec(memory_space=pl.ANY)


def _ordered(body):
    def run(_order_ref, *refs):
        body(*refs)
    return run


def _inproj_body(x_ref, ln1_ref, win_ref, cr_ref, sar_ref, sbr_ref, cm_ref, sam_ref, sbm_ref,
                 qnw_ref, kvnw_ref, wuq_ref, wuk_ref, wuv_ref,
                 qr_ref, kr_ref, vr_ref, gate_ref, qm_ref, km_ref, vm_ref, ckv_ref, kpe_ref,
                 *, rw, qrank, kvrank, ret_half, mla_half, k_scale, nheads):
    n1 = _rms(x_ref[...], ln1_ref[...])
    proj = _mm(n1, win_ref[...])
    cr, sar, sbr = cr_ref[...], sar_ref[...], sbr_ref[...]
    qr_ref[...] = _rope(proj[:, 0:rw], cr, sar, sbr, ret_half)
    kr_ref[...] = _rope(proj[:, rw:2 * rw], cr, sar, sbr, ret_half) * k_scale
    vr_ref[...] = proj[:, 2 * rw:3 * rw]
    gate_ref[...] = proj[:, 3 * rw:4 * rw]
    o4 = 4 * rw
    o5 = o4 + qrank
    o6 = o5 + kvrank
    cm, sam, sbm = cm_ref[...], sam_ref[...], sbm_ref[...]
    tile = lambda t: jnp.concatenate([t] * nheads, axis=1)
    cq = _rms(proj[:, o4:o5], qnw_ref[...])
    qm = _rope(_mm(cq, wuq_ref[...]), tile(cm), tile(sam), tile(sbm), mla_half)
    qm_ref[...] = qm.astype(qm_ref.dtype)
    ckv = _rms(proj[:, o5:o6], kvnw_ref[...])
    ckv_ref[...] = ckv
    kpe = _rope(proj[:, o6:o6 + LANES], cm, sam, sbm, mla_half)
    kpe_ref[...] = kpe
    km_ref[...] = (_mm(ckv, wuk_ref[...]) + tile(kpe)).astype(km_ref.dtype)
    vm_ref[...] = _mm(ckv, wuv_ref[...]).astype(vm_ref.dtype)


def _inproj(order, x, tabs, ln1, win_p, qnw, kvnw, wuq_p, wuk_p, wuv_p, *, tm, dims):
    t, d = x.shape
    rw, nheads = dims["rw"], dims["nheads"]
    hp = nheads * LANES
    nblk_tab = tabs[0].shape[0] // tm
    row = lambda i: (i, 0)
    tab = lambda i: (i % nblk_tab, 0)
    full = lambda i: (0, 0)
    fs = lambda a: pl.BlockSpec(a.shape, full)
    in_specs = [pl.BlockSpec((tm, d), row), fs(ln1), fs(win_p)]
    in_specs += [pl.BlockSpec((tm, rw), tab)] * 3 + [pl.BlockSpec((tm, LANES), tab)] * 3
    in_specs += [fs(qnw), fs(kvnw), fs(wuq_p), fs(wuk_p), fs(wuv_p)]
    out_shape = [jax.ShapeDtypeStruct((t, rw), F32)] * 4
    out_shape += [jax.ShapeDtypeStruct((t, hp), MXU_DTYPE)] * 3
    out_shape += [jax.ShapeDtypeStruct((t, dims["kvrank"]), F32), jax.ShapeDtypeStruct((t, LANES), F32)]
    out_specs = [pl.BlockSpec((tm, rw), row)] * 4 + [pl.BlockSpec((tm, hp), row)] * 3
    out_specs += [pl.BlockSpec((tm, dims["kvrank"]), row), pl.BlockSpec((tm, LANES), row)]
    body = functools.partial(
        _inproj_body, rw=rw, qrank=dims["qrank"], kvrank=dims["kvrank"], ret_half=dims["ret_dk"] // 2,
        mla_half=dims["mla_rope"] // 2, k_scale=dims["ret_dk"] ** -0.5, nheads=nheads)
    return pl.pallas_call(
        _ordered(body), grid=(t // tm,), in_specs=[ORDER_SPEC] + in_specs, out_specs=out_specs,
        out_shape=out_shape, compiler_params=_params("parallel"), name="inproj",
    )(order, x, ln1, win_p, *tabs, qnw, kvnw, wuq_p, wuk_p, wuv_p)


def _retention_body(lg_ref, q_ref, k_ref, v_ref, s0_ref, o_ref, sout_ref, s_scr, *, rows, chunk, dk):
    hp = pl.program_id(1)
    j = pl.program_id(2)

    @pl.when(j == 0)
    def _():
        s_scr[...] = s0_ref[0, 0]

    lane = lax.broadcasted_iota(jnp.int32, (1, LANES), 1)
    is_a = lane < dk
    lg_a = lg_ref[2 * hp]
    lg_b = lg_ref[2 * hp + 1]
    lgl = jnp.where(is_a, lg_a, lg_b)
    r = lax.broadcasted_iota(jnp.int32, (rows, 1), 0).astype(F32)
    q, k, v = q_ref[...], k_ref[...], v_ref[...]
    q_dec = q * jnp.exp(lgl * (r + 1.0))
    k_dec = k * jnp.exp(lgl * (float(rows) - 1.0 - r))
    ri = lax.broadcasted_iota(jnp.int32, (rows, rows), 0)
    ci = lax.broadcasted_iota(jnp.int32, (rows, rows), 1)
    dist = jnp.abs(ri - ci).astype(F32)
    visible = (ci // chunk) <= (ri // chunk)
    o = _mm(q_dec, s_scr[...])
    for first, lg in ((True, lg_a), (False, lg_b)):
        sel = is_a if first else jnp.logical_not(is_a)
        qh = jnp.where(sel, q, 0.0)
        vh = jnp.where(sel, v, 0.0)
        decay = jnp.where(visible, jnp.exp(lg * dist), 0.0)
        o = o + _mm(_mm_nt(qh, k) * decay, vh)
    o_ref[...] = o
    sr = lax.broadcasted_iota(jnp.int32, (LANES, LANES), 0) < dk
    sc = lax.broadcasted_iota(jnp.int32, (LANES, LANES), 1) < dk
    kv = jnp.where(sr == sc, _mm_tn(k_dec, v), 0.0)
    s_new = jnp.exp(lgl * float(rows)) * s_scr[...] + kv
    s_scr[...] = s_new

    @pl.when(j == pl.num_programs(2) - 1)
    def _():
        sout_ref[0, 0] = s_new


def _retention(lg, q, k, v, s0_pairs, *, nbatch, rows, chunk, dk):
    t, w = q.shape
    npairs = w // LANES
    nblk = t // (nbatch * rows)
    blk = pl.BlockSpec((rows, LANES), lambda b, p, j: (b * nblk + j, p))
    st = pl.BlockSpec((1, 1, LANES, LANES), lambda b, p, j: (b, p, 0, 0))
    return pl.pallas_call(
        functools.partial(_retention_body, rows=rows, chunk=chunk, dk=dk),
        grid=(nbatch, npairs, nblk),
        in_specs=[pl.BlockSpec(memory_space=pltpu.SMEM), blk, blk, blk, st],
        out_specs=[blk, st],
        out_shape=[jax.ShapeDtypeStruct((t, w), F32),
                   jax.ShapeDtypeStruct((nbatch, npairs, LANES, LANES), F32)],
        scratch_shapes=[pltpu.VMEM((LANES, LANES), F32)],
        compiler_params=_params("parallel", "parallel", "arbitrary"), name="retention",
    )(lg, q, k, v, s0_pairs)


def _flash_body(q_ref, k_ref, v_ref, o_ref, *, tq, tile0, scale, nheads, chunk):
    i = pl.program_id(1) + tile0
    ri = lax.broadcasted_iota(jnp.int32, (tq, tq), 0) // chunk
    ci = lax.broadcasted_iota(jnp.int32, (tq, tq), 1) // chunk
    visible = ci <= ri

    def head(h):
        cols = slice(h * LANES, (h + 1) * LANES)
        q = q_ref[:, cols]

        def step(j, carry, diagonal):
            m, l, acc = carry
            off = pl.multiple_of(j * tq, tq)
            s = _mm_nt(q, k_ref[pl.ds(off, tq), cols]) * scale
            if diagonal:
                s = jnp.where(visible, s, NEG_INF)
            m_new = jnp.maximum(m, jnp.max(s, axis=1, keepdims=True))
            alpha = jnp.exp(m - m_new)
            p = jnp.exp(s - m_new)
            l = alpha * l + jnp.sum(p, axis=1, keepdims=True)
            acc = alpha * acc + _mm(p, v_ref[pl.ds(off, tq), cols])
            return m_new, l, acc

        init = (jnp.full((tq, 1), NEG_INF, F32), jnp.zeros((tq, 1), F32), jnp.zeros((tq, LANES), F32))
        carry = lax.fori_loop(0, i, functools.partial(step, diagonal=False), init)
        _, l, acc = step(i, carry, True)
        return acc / l

    for p in range(nheads // 2):
        o_ref[:, p * LANES:(p + 1) * LANES] = head(2 * p) + head(2 * p + 1)


def _flash(order, qm, km, vm, *, nbatch, tq, lo, hi, scale, nheads, chunk):
    t, hp = qm.shape
    s = t // nbatch
    nq = s // tq
    tile0 = lo // tq
    nqr = (hi - lo) // tq
    ow = nheads // 2 * LANES
    return pl.pallas_call(
        _ordered(functools.partial(_flash_body, tq=tq, tile0=tile0, scale=scale, nheads=nheads, chunk=chunk)),
        grid=(nbatch, nqr),
        in_specs=[ORDER_SPEC,
                  pl.BlockSpec((tq, hp), lambda b, i: (b * nq + tile0 + i, 0)),
                  pl.BlockSpec((s, hp), lambda b, i: (b, 0)),
                  pl.BlockSpec((s, hp), lambda b, i: (b, 0))],
        out_specs=pl.BlockSpec((tq, ow), lambda b, i: (b * nqr + i, 0)),
        out_shape=jax.ShapeDtypeStruct((nbatch * (hi - lo), ow), F32),
        compiler_params=_params("parallel", "arbitrary"), name="flash_mla",
    )(order, qm, km, vm)


def _decode_attn_body(q_ref, cpast_ref, kpast_ref, cnew_ref, knew_ref, wuk_ref, wuv_ref, o_ref,
                      *, nheads, nope, rope, scale):
    c_past = cpast_ref[0]
    k_past = kpast_ref[0]
    c_new = cnew_ref[...]
    k_new = knew_ref[:, nope:nope + rope]
    outs = []
    for h in range(nheads):
        q = q_ref[:, h * LANES:(h + 1) * LANES]
        q_lat = _mm(q, wuk_ref[h])
        q_pe = q[:, nope:nope + rope]
        s_p = (_mm_nt(q_lat, c_past) + _mm_nt(q_pe, k_past)) * scale
        s_n = (_mm_nt(q_lat, c_new) + _mm_nt(q_pe, k_new)) * scale
        m = jnp.maximum(jnp.max(s_p, axis=1, keepdims=True), jnp.max(s_n, axis=1, keepdims=True))
        p_p = jnp.exp(s_p - m)
        p_n = jnp.exp(s_n - m)
        l = jnp.sum(p_p, axis=1, keepdims=True) + jnp.sum(p_n, axis=1, keepdims=True)
        o_lat = (_mm(p_p, c_past) + _mm(p_n, c_new)) / l
        outs.append(_mm(o_lat, wuv_ref[h]))
    o_ref[...] = jnp.concatenate(outs, axis=1)


def _decode_attn(qm, c_past, k_past, c_new, kpe_new, wuk3, wuv3, *, nq, nope, rope, scale):
    nb, past, kvr = c_past.shape
    nheads, _, vdim = wuv3.shape
    t, hp = qm.shape
    row = lambda b: (b, 0)
    full3 = lambda b: (0, 0, 0)
    return pl.pallas_call(
        functools.partial(_decode_attn_body, nheads=nheads, nope=nope, rope=rope, scale=scale),
        grid=(nb,),
        in_specs=[pl.BlockSpec((nq, hp), row),
                  pl.BlockSpec((1, past, kvr), lambda b: (b, 0, 0)),
                  pl.BlockSpec((1, past, rope), lambda b: (b, 0, 0)),
                  pl.BlockSpec((nq, kvr), row),
                  pl.BlockSpec((nq, LANES), row),
                  pl.BlockSpec(wuk3.shape, full3),
                  pl.BlockSpec(wuv3.shape, full3)],
        out_specs=pl.BlockSpec((nq, nheads * vdim), row),
        out_shape=jax.ShapeDtypeStruct((t, nheads * vdim), F32),
        compiler_params=_params("parallel"), name="decode_mla",
    )(qm, c_past, k_past, c_new, kpe_new, wuk3, wuv3)


def _split3(x):
    a = x.astype(MXU_DTYPE)
    r = x - a.astype(F32)
    b = r.astype(MXU_DTYPE)
    c = (r - b.astype(F32)).astype(MXU_DTYPE)
    return a, b, c


def _group_mean(x, avg):
    a, b, c = _split3(x)
    dot = lambda t: jnp.dot(t, avg, preferred_element_type=F32)
    return dot(a) + dot(b) + dot(c)


def _topk_rows(s, payload, kk):
    n = s.shape[0]
    rid = lax.broadcasted_iota(jnp.int32, s.shape, 0)
    vals, pays = [], []
    for _ in range(kk):
        mx = jnp.max(s, axis=0, keepdims=True)
        first = jnp.min(jnp.where(s == mx, rid, n), axis=0, keepdims=True)
        hit = rid == first
        vals.append(mx)
        pays.append(jnp.max(jnp.where(hit, payload, -1), axis=0, keepdims=True))
        s = jnp.where(hit, NEG_INF, s)
    return jnp.concatenate(vals, axis=0), jnp.concatenate(pays, axis=0)


def _topk_keyed(s, key, payload, kk):
    big = jnp.int32(2 ** 30)
    vals, pays = [], []
    for _ in range(kk):
        mx = jnp.max(s, axis=0, keepdims=True)
        first = jnp.min(jnp.where(s == mx, key, big), axis=0, keepdims=True)
        hit = key == first
        vals.append(mx)
        pays.append(jnp.max(jnp.where(hit, payload, -1), axis=0, keepdims=True))
        s = jnp.where(hit, NEG_INF, s)
    return jnp.concatenate(vals, axis=0), jnp.concatenate(pays, axis=0)


def _pair_shortlist(ts, ti, topk, nkeys):
    m = ts[0].shape[1]
    sums, keys, ids = [], [], []

    def block(rows_a, rows_b, keep):
        na, nb = len(rows_a), len(rows_b)
        n = max(na, nb)
        sa = ts[0][rows_a[0]:rows_a[0] + na, :]
        sb = ts[1][rows_b[0]:rows_b[0] + nb, :]
        ia = ti[0][rows_a[0]:rows_a[0] + na, :]
        ib = ti[1][rows_b[0]:rows_b[0] + nb, :]
        r = lax.broadcasted_iota(jnp.int32, (n, m), 0)
        a = r + rows_a[0] if na > 1 else jnp.full((n, m), rows_a[0], jnp.int32)
        b = r + rows_b[0] if nb > 1 else jnp.full((n, m), rows_b[0], jnp.int32)
        live = keep(a, b)
        sums.append(jnp.where(live, sa + sb, NEG_INF))
        keys.append(a * topk + b)
        ids.append(ia * nkeys + ib)

    lim = lambda a, b: (a + 1) * (b + 1) <= topk
    block(range(0, 1), range(0, topk), lim)
    block(range(1, 2), range(0, topk // 2), lim)
    block(range(0, topk), range(0, 1), lambda a, b: a >= 2)
    for b in range(1, topk // 3):
        block(range(0, topk // 2), range(b, b + 1), lambda a, b_: (a >= 2) & lim(a, b_))
    return jnp.concatenate(sums, axis=0), jnp.concatenate(keys, axis=0), jnp.concatenate(ids, axis=0)


def _mix_route_body(x_ref, ret_ref, gate_ref, mla_ref, gnw_ref, avg_ref, wo_ref, ln2_ref, wq_ref, keys_ref,
                    h_ref, hn_ref, idx_ref, g_ref, *, rw, pheads, nkeys, topk):
    ret = ret_ref[...]
    avg = avg_ref[...]
    mu = _group_mean(ret, avg)
    cen = ret - mu
    var = _group_mean(cen * cen, avg)
    gate = gate_ref[...]
    y = cen * lax.rsqrt(var + EPS) * gnw_ref[...] * (gate * jax.nn.sigmoid(gate))
    h = x_ref[...] + _mm(y, wo_ref[0:rw, :]) + _mm(mla_ref[...], wo_ref[rw:, :])
    h_ref[...] = h
    hn = _rms(h, ln2_ref[...])
    hn_ref[...] = hn
    qp = _mm(hn, wq_ref[...])
    kid = lax.broadcasted_iota(jnp.int32, (nkeys, qp.shape[0]), 0)
    idx_rows, g_rows = [], []
    for hd in range(pheads):
        ts, ti = [], []
        for half in range(2):
            c = (2 * hd + half) * LANES
            st = _mm_nt(keys_ref[2 * hd + half], qp[:, c:c + LANES])
            v, i = _topk_rows(st, kid, topk)
            ts.append(v)
            ti.append(i)
        best, expert = _topk_keyed(*_pair_shortlist(ts, ti, topk, nkeys), topk)
        e = jnp.exp(best - best[0:1, :])
        g_rows.append(e / jnp.sum(e, axis=0, keepdims=True))
        idx_rows.append(expert)
    idx_ref[...] = jnp.concatenate(idx_rows, axis=0).T
    g_ref[...] = jnp.concatenate(g_rows, axis=0).T


def _mix_route(x, ret_o, gate, mla_o, gnw, avg, wo, ln2, wq, keys, *, tm, row0, pheads, nkeys, topk):
    t = mla_o.shape[0]
    d = x.shape[1]
    rw = ret_o.shape[1]
    nsel = pheads * topk
    blk0 = row0 // tm
    row = lambda i: (i, 0)
    src = lambda i: (blk0 + i, 0)
    fs = lambda a: pl.BlockSpec(a.shape, lambda i: (0,) * a.ndim)
    return pl.pallas_call(
        functools.partial(_mix_route_body, rw=rw, pheads=pheads, nkeys=nkeys, topk=topk),
        grid=(t // tm,),
        in_specs=[pl.BlockSpec((tm, d), src), pl.BlockSpec((tm, rw), src), pl.BlockSpec((tm, rw), src),
                  pl.BlockSpec((tm, mla_o.shape[1]), row), fs(gnw), fs(avg), fs(wo), fs(ln2), fs(wq), fs(keys)],
        out_specs=[pl.BlockSpec((tm, d), row), pl.BlockSpec((tm, d), row),
                   pl.BlockSpec((tm, nsel), row), pl.BlockSpec((tm, nsel), row)],
        out_shape=[jax.ShapeDtypeStruct((t, d), F32), jax.ShapeDtypeStruct((t, d), F32),
                   jax.ShapeDtypeStruct((t, nsel), jnp.int32),
                   jax.ShapeDtypeStruct((t, nsel), F32)],
        compiler_params=_params("parallel"), name="mix_route",
    )(x, ret_o, gate, mla_o, gnw, avg, wo, ln2, wq, keys)


def _gelu_gate_body(hid_ref, g_ref, a_ref):
    hid = hid_ref[...]
    a_ref[...] = 0.5 * hid * (1.0 + lax.erf(hid * (2.0 ** -0.5))) * g_ref[...]


def _gelu_gate(order, hid, g, *, tm):
    t, n = hid.shape
    blk = pl.BlockSpec((tm, n), lambda i: (i, 0))
    return pl.pallas_call(
        _ordered(_gelu_gate_body), grid=(t // tm,), in_specs=[ORDER_SPEC, blk, blk], out_specs=blk,
        out_shape=jax.ShapeDtypeStruct((t, n), F32), compiler_params=_params("parallel"), name="gelu_gate",
    )(order, hid, g)


def _residual_body(h_ref, p_ref, lnf_ref, o_ref, *, final_norm):
    out = h_ref[...] + p_ref[...]
    if final_norm:
        out = _rms(out, lnf_ref[...])
    o_ref[...] = out


def _residual(order, h, peer, lnf, *, tm, final_norm):
    t, d = h.shape
    blk = pl.BlockSpec((tm, d), lambda i: (i, 0))
    return pl.pallas_call(
        _ordered(functools.partial(_residual_body, final_norm=final_norm)), grid=(t // tm,),
        in_specs=[ORDER_SPEC, blk, blk, pl.BlockSpec((1, d), lambda i: (0, 0))], out_specs=blk,
        out_shape=jax.ShapeDtypeStruct((t, d), F32), compiler_params=_params("parallel"), name="residual_norm",
    )(order, h, peer, lnf)


SC_CORES = 2
SC_SUBCORES = 16
SC_LANES = 16
SC_RING = 4
SC_BATCH = 64


def _sc_worker_id():
    return lax.axis_index("s") * SC_CORES + lax.axis_index("c")


def _sc_ring(nq, start, wait, compute, stage):
    for s in range(SC_RING - 1):
        start(s, s)
    stage()

    def step(q, s):
        nxt = q + SC_RING - 1

        @pl.when(nxt < nq)
        def _():
            start(nxt, (s + SC_RING - 1) % SC_RING)

        wait(q, s)
        compute(q, s)

    full = nq // SC_RING * SC_RING

    @pl.loop(0, full, step=SC_RING)
    def _(q0):
        for s in range(SC_RING):
            step(q0 + s, s)

    for s in range(nq - full):
        step(jnp.int32(full + s), s)


def _tree_sum(terms):
    while len(terms) > 1:
        terms = [a + b for a, b in zip(terms[0::2], terms[1::2])]
    return terms[0]


def _pack_rows(tab):
    half = tab.shape[1] // 2
    low = lax.bitcast_convert_type(tab[:, :half].astype(jnp.bfloat16), jnp.uint16).astype(jnp.int32)
    bits = lax.bitcast_convert_type(tab[:, half:], jnp.int32)
    sign = bits & jnp.int32(-2 ** 31)
    mag = bits & jnp.int32(2 ** 31 - 1)
    high = lax.shift_right_logical(jnp.maximum(mag - low + 2 ** 15, 0), 16)
    return sign | lax.shift_left(high, 16) | low


def _unpack_pair(w):
    return lax.bitcast_convert_type(lax.shift_left(w, jnp.int32(16)), F32), lax.bitcast_convert_type(w, F32)


def _peer_hidden_sc(order, xn, idx, u_tab):
    t, d = xn.shape
    nsel = idx.shape[1]
    nw = SC_CORES * SC_SUBCORES
    per_w = t // nw
    tb = min(SC_BATCH, per_w)
    nchunk = nsel // SC_LANES
    shift = nchunk.bit_length() - 1
    half = d // 2
    nword = half // SC_LANES
    nq = tb * nchunk
    assert per_w * nw == t and per_w % tb == 0 and nchunk == 1 << shift and nq >= SC_RING
    assert u_tab.shape[1] == half
    mesh = plsc.VectorSubcoreMesh(core_axis_name="c", subcore_axis_name="s")

    def body(_order_hbm, x_hbm, idx_hbm, u_hbm, out_hbm, idx_v, x_v, ubuf, hid_v, sem):
        wid = _sc_worker_id()
        lane = lax.iota(jnp.int32, SC_LANES)

        def gather(q, slot):
            tok = lax.shift_right_logical(q, shift)
            ch = q & (nchunk - 1)
            rows = idx_v.at[tok, pl.ds(ch * SC_LANES, SC_LANES)]
            return pltpu.make_async_copy(u_hbm.at[rows], ubuf.at[slot], sem.at[slot])

        def compute(q, slot):
            tok = lax.shift_right_logical(q, shift)
            ch = q & (nchunk - 1)

            @plsc.parallel_loop(0, nword, carry=tuple(jnp.zeros((SC_LANES,), F32) for _ in range(SC_LANES)))
            def accs(c, acc):
                off = pl.multiple_of(c * SC_LANES, SC_LANES)
                x_lo = x_v[tok, pl.ds(off, SC_LANES)]
                x_hi = x_v[tok, pl.ds(pl.multiple_of(half + off, SC_LANES), SC_LANES)]
                new = []
                for k, a in enumerate(acc):
                    lo, hi = _unpack_pair(ubuf[slot, k, pl.ds(off, SC_LANES)])
                    new.append(a + x_lo * lo + x_hi * hi)
                return tuple(new)

            out = jnp.zeros((SC_LANES,), F32)
            for k in range(SC_LANES):
                out = jnp.where(lane == k, jnp.sum(accs[k]), out)
            hid_v[tok, pl.ds(ch * SC_LANES, SC_LANES)] = out

        @pl.loop(0, per_w // tb)
        def _(b):
            base = wid * per_w + b * tb
            pltpu.sync_copy(idx_hbm.at[pl.ds(base, tb)], idx_v)
            _sc_ring(nq, lambda q, s: gather(q, s).start(), lambda q, s: gather(q, s).wait(), compute,
                     lambda: pltpu.sync_copy(x_hbm.at[pl.ds(base, tb)], x_v))
            pltpu.sync_copy(hid_v, out_hbm.at[pl.ds(base, tb)])

    return pl.kernel(
        body, out_type=jax.ShapeDtypeStruct((t, nsel), F32), mesh=mesh,
        scratch_types=[pltpu.VMEM((tb, nsel), jnp.int32), pltpu.VMEM((tb, d), F32),
                       pltpu.VMEM((SC_RING, SC_LANES, half), jnp.int32), pltpu.VMEM((tb, nsel), F32),
                       pltpu.SemaphoreType.DMA((SC_RING,))],
        compiler_params=pltpu.CompilerParams(needs_layout_passes=False), name="peer_hidden_sc",
    )(order, xn, idx, u_tab)


def _peer_mix_sc(act, idx, v_tab):
    t, nsel = act.shape
    half = v_tab.shape[1]
    d = 2 * half
    nw = SC_CORES * SC_SUBCORES
    per_w = t // nw
    tb = min(SC_BATCH, per_w)
    nchunk = nsel // SC_LANES
    shift = nchunk.bit_length() - 1
    ncol = d // SC_LANES
    nword = half // SC_LANES
    nq = tb * nchunk
    assert per_w * nw == t and per_w % tb == 0 and nchunk == 1 << shift and nq >= SC_RING
    mesh = plsc.VectorSubcoreMesh(core_axis_name="c", subcore_axis_name="s")

    def body(a_hbm, idx_hbm, v_hbm, out_hbm, idx_v, a_v, vbuf, o_v, sem):
        wid = _sc_worker_id()
        zero = jnp.zeros((SC_LANES,), F32)

        def gather(q, slot):
            tok = lax.shift_right_logical(q, shift)
            ch = q & (nchunk - 1)
            rows = idx_v.at[tok, pl.ds(ch * SC_LANES, SC_LANES)]
            return pltpu.make_async_copy(v_hbm.at[rows], vbuf.at[slot], sem.at[slot])

        def compute(q, slot):
            tok = lax.shift_right_logical(q, shift)
            ch = q & (nchunk - 1)
            tok_v = jnp.full((SC_LANES,), tok, jnp.int32)
            col_v = jnp.full((SC_LANES,), ch * SC_LANES, jnp.int32)
            w = [plsc.load_gather(a_v, [tok_v, col_v + k]) for k in range(SC_LANES)]

            @plsc.parallel_loop(0, nword)
            def _(c):
                off = pl.multiple_of(c * SC_LANES, SC_LANES)
                cs_lo = pl.ds(off, SC_LANES)
                cs_hi = pl.ds(pl.multiple_of(half + off, SC_LANES), SC_LANES)
                pairs = [_unpack_pair(vbuf[slot, k, cs_lo]) for k in range(SC_LANES)]
                o_v[tok, cs_lo] = o_v[tok, cs_lo] + _tree_sum([w[k] * p[0] for k, p in enumerate(pairs)])
                o_v[tok, cs_hi] = o_v[tok, cs_hi] + _tree_sum([w[k] * p[1] for k, p in enumerate(pairs)])

        @pl.loop(0, per_w // tb)
        def _(b):
            base = wid * per_w + b * tb
            pltpu.sync_copy(idx_hbm.at[pl.ds(base, tb)], idx_v)

            def stage():
                pltpu.sync_copy(a_hbm.at[pl.ds(base, tb)], a_v)

                @pl.loop(0, tb)
                def _(r):
                    @pl.loop(0, ncol)
                    def _(c):
                        o_v[r, pl.ds(pl.multiple_of(c * SC_LANES, SC_LANES), SC_LANES)] = zero

            _sc_ring(nq, lambda q, s: gather(q, s).start(), lambda q, s: gather(q, s).wait(), compute, stage)
            pltpu.sync_copy(o_v, out_hbm.at[pl.ds(base, tb)])

    return pl.kernel(
        body, out_type=jax.ShapeDtypeStruct((t, d), F32), mesh=mesh,
        scratch_types=[pltpu.VMEM((tb, nsel), jnp.int32), pltpu.VMEM((tb, nsel), F32),
                       pltpu.VMEM((SC_RING, SC_LANES, half), jnp.int32), pltpu.VMEM((tb, d), F32),
                       pltpu.SemaphoreType.DMA((SC_RING,))],
        compiler_params=pltpu.CompilerParams(needs_layout_passes=False), name="peer_mix_sc",
    )(act, idx, v_tab)


def _rope_tables(pos, half, group, width, lo):
    inv = ROPE_BASE ** (-jnp.arange(half, dtype=F32) / half)
    ang = pos.astype(F32)[:, None] * inv[None, :]
    cos, sin = jnp.cos(ang), jnp.sin(ang)
    n = pos.shape[0]
    reps = width // group
    pad_hi = group - lo - 2 * half
    blk = lambda a, b, fill: jnp.concatenate(
        [jnp.full((n, lo), fill, F32), a, b, jnp.full((n, pad_hi), fill, F32)], axis=1)
    z = jnp.zeros_like(sin)
    c = blk(cos, cos, 1.0)
    sa = blk(-sin, z, 0.0)
    sb = blk(z, sin, 0.0)
    return [jnp.tile(a, (1, reps)) for a in (c, sa, sb)]


def _ret_log_decay(nheads):
    return jnp.log(1.0 - jnp.exp2(-5.0 - jnp.arange(nheads, dtype=F32)))


def _pair_states(s):
    b, h, dk, dv = s.shape
    s = s.reshape(b, h // 2, 2, dk, dv)
    z = jnp.zeros_like(s[:, :, 0])
    top = jnp.concatenate([s[:, :, 0], z], axis=-1)
    bot = jnp.concatenate([z, s[:, :, 1]], axis=-1)
    return jnp.concatenate([top, bot], axis=-2)


def _unpair_states(sp, dk, dv):
    b, hp = sp.shape[:2]
    return jnp.stack([sp[:, :, :dk, :dv], sp[:, :, dk:, dv:]], axis=2).reshape(b, 2 * hp, dk, dv)


def _layer_weights(ln1_w, w_in, ret_gn_w, q_norm_w, w_uq, kv_norm_w, w_uk, w_uv, w_o, ln2_w,
                   peer_w_q, peer_sub_keys, dims):
    d = w_in.shape[0]
    nheads, nope, rope, vdim = dims["nheads"], dims["nope"], dims["mla_rope"], dims["vdim"]
    o6 = 4 * dims["rw"] + dims["qrank"] + dims["kvrank"]
    zc = lambda r, c: jnp.zeros((r, c), F32)
    win_p = jnp.concatenate([w_in[:, :o6], zc(d, nope), w_in[:, o6:], zc(d, LANES - nope - rope)], axis=1)
    qr, kr = w_uq.shape[0], w_uk.shape[0]
    wuq_p = jnp.concatenate([w_uq, jnp.zeros((qr, nheads, LANES - nope - rope), F32)], axis=2).reshape(qr, -1)
    wuk_p = jnp.concatenate([w_uk, jnp.zeros((kr, nheads, LANES - nope), F32)], axis=2).reshape(kr, -1)
    zv = jnp.zeros((kr, nheads // 2, LANES - vdim), F32)
    wv = w_uv.reshape(kr, nheads // 2, 2, vdim)
    wuv_p = jnp.concatenate([wv[:, :, 0], zv, zv, wv[:, :, 1]], axis=2).reshape(kr, -1)
    wuk3 = jnp.concatenate([jnp.transpose(w_uk, (1, 2, 0)),
                            jnp.zeros((nheads, LANES - nope, kr), F32)], axis=1)
    wuv3 = jnp.transpose(w_uv, (1, 0, 2))
    gidx = jnp.arange(dims["rw"]) // dims["ret_dv"]
    avg = (gidx[:, None] == gidx[None, :]).astype(F32) / dims["ret_dv"]
    keys = peer_sub_keys.reshape(-1, peer_sub_keys.shape[2], peer_sub_keys.shape[3])
    c = lambda a: a.astype(MXU_DTYPE)
    r2 = lambda a: a.reshape(1, -1)
    return dict(ln1=r2(ln1_w), win_p=c(win_p), gnw=r2(ret_gn_w), qnw=r2(q_norm_w), kvnw=r2(kv_norm_w),
                wuq_p=c(wuq_p), wuk_p=c(wuk_p), wuv_p=c(wuv_p), wuk3=c(wuk3), wuv3=c(wuv3), avg=c(avg),
                wo=c(w_o), ln2=r2(ln2_w), wq=c(peer_w_q), keys=c(keys))


class _Stream:
    def __init__(self, x, tabs, s0, *, nbatch, ret_rows, ret_chunk, tm, ranges, cache=None):
        self.x, self.tabs, self.s0, self.cache = x, tabs, s0, cache
        self.nbatch, self.ret_rows, self.ret_chunk, self.tm, self.ranges = nbatch, ret_rows, ret_chunk, tm, ranges
        self.seq = x.shape[0] // nbatch
        assert nbatch == 1 or ranges == [(0, self.seq)]
        self.pre = None
        self.outs = []


def _layer(streams, w, lg, u_tab, v_tab, lnf, dims, *, final_norm):
    units = [(st, lo, hi) for st in streams for lo, hi in st.ranges]
    n = len(units)
    nheads, dk = dims["nheads"], dims["ret_dk"]
    scale = (dims["nope"] + dims["mla_rope"]) ** -0.5
    built = [None] * n

    def build(i, order):
        st, lo, hi = units[i]
        if st.pre is None:
            proj = _inproj(order, st.x, st.tabs, w["ln1"], w["win_p"], w["qnw"], w["kvnw"], w["wuq_p"], w["wuk_p"],
                           w["wuv_p"], tm=st.tm, dims=dims)
            ret_o, s_pairs = _retention(lg, *proj[:3], _pair_states(st.s0), nbatch=st.nbatch, rows=st.ret_rows,
                                        chunk=st.ret_chunk, dk=dk)
            st.pre = list(proj) + [ret_o, s_pairs]
        qr, kr, vr, gate, qm, km, vm, ckv, kpe, ret_o, s_pairs = st.pre
        if st.cache is None:
            mla_o = _flash(order, qm, km, vm, nbatch=st.nbatch, tq=min(256, st.seq), lo=lo, hi=hi, scale=scale,
                           nheads=nheads, chunk=CHUNK)
        else:
            mla_o = _decode_attn(qm, st.cache[0], st.cache[1], ckv, kpe, w["wuk3"], w["wuv3"], nq=st.seq,
                                 nope=dims["nope"], rope=dims["mla_rope"], scale=scale)
        h, hn, idx, g = _mix_route(st.x, ret_o, gate, mla_o, w["gnw"], w["avg"], w["wo"], w["ln2"], w["wq"],
                                   w["keys"], tm=st.tm, row0=lo, pheads=dims["pheads"], nkeys=dims["nkeys"],
                                   topk=dims["topk"])
        built[i] = (h, idx, g, _peer_hidden_sc(peers[i - 2] if i >= 2 else lnf, hn, idx, u_tab))

    for i in range(min(2, n)):
        build(i, lnf)
    acts, peers = [], []
    for i in range(n):
        h, idx, g, hid = built[i]
        acts.append(_gelu_gate(built[i + 1][1] if i + 1 < n else lnf, hid, g, tm=units[i][0].tm))
        peers.append(_peer_mix_sc(acts[i], idx, v_tab))
        if i + 2 < n:
            build(i + 2, acts[i])
    for i, (st, lo, hi) in enumerate(units):
        st.outs.append(_residual(acts[min(i + 2, n - 1)], built[i][0], peers[i], lnf, tm=st.tm,
                                 final_norm=final_norm))
    nope, rope, dv = dims["nope"], dims["mla_rope"], dims["ret_dv"]
    results = []
    for st in streams:
        out = st.outs[0] if len(st.outs) == 1 else jnp.concatenate(st.outs, axis=0)
        results.append((out, st.pre[7], st.pre[8][:, nope:nope + rope], _unpair_states(st.pre[10], dk, dv)))
    return results


def kernel(x_prompt, x_sample, cache_mla_ckv, cache_mla_krope, state_retention, ln1_w, w_in, ret_gn_w,
           mla_q_norm_w, mla_w_uq, mla_kv_norm_w, mla_w_uk, mla_w_uv, w_o, ln2_w, peer_w_q, peer_sub_keys,
           peer_u, peer_v, lnf_w):
    depth = w_in.shape[0]
    nb, seq, d = x_prompt.shape
    db, dseq, _ = x_sample.shape
    past = cache_mla_ckv.shape[2]
    rheads, dk, dv = state_retention.shape[2:]
    nkeys = peer_sub_keys.shape[3]
    dims = dict(rw=rheads * dk, ret_dk=dk, ret_dv=dv, qrank=mla_w_uq.shape[1], kvrank=mla_w_uk.shape[1],
                nheads=mla_w_uq.shape[2], nope=mla_w_uk.shape[3], vdim=mla_w_uv.shape[3],
                mla_rope=mla_w_uq.shape[3] - mla_w_uk.shape[3], pheads=peer_sub_keys.shape[1], nkeys=nkeys,
                topk=PEER_TOPK)
    assert rheads * dk == rheads * dv and dims["nheads"] % 2 == 0 and dk * 2 == LANES and dims["vdim"] * 2 == LANES

    def tables(pos):
        return (_rope_tables(pos, dk // 2, dk, dims["rw"], 0)
                + _rope_tables(pos, dims["mla_rope"] // 2, LANES, LANES, dims["nope"]))

    tabs_p = tables(jnp.arange(seq))
    tabs_s = tables(jnp.tile(past + jnp.arange(dseq), db))
    lg = _ret_log_decay(rheads)
    lnf = lnf_w.reshape(1, -1)
    hp = x_prompt.reshape(nb * seq, d)
    hs = x_sample.reshape(db * dseq, d)
    outs = [[] for _ in range(6)]
    for l in range(depth):
        w = _layer_weights(ln1_w[l], w_in[l], ret_gn_w[l], mla_q_norm_w[l], mla_w_uq[l], mla_kv_norm_w[l],
                           mla_w_uk[l], mla_w_uv[l], w_o[l], ln2_w[l], peer_w_q[l], peer_sub_keys[l], dims)
        last = l == depth - 1
        gb = nb // PROMPT_GROUPS if nb % PROMPT_GROUPS == 0 else nb

        def frame_ranges(g):
            step = seq // (PROMPT_HEAD_SPLIT if g == 0 else PROMPT_SPLIT)
            ok = gb == 1 and step > 0 and step % 256 == 0
            return [(lo, lo + step) for lo in range(0, seq, step)] if ok else [(0, seq)]

        streams = [_Stream(hp[g * gb * seq:(g + 1) * gb * seq], tabs_p, jnp.zeros((gb, rheads, dk, dv), F32),
                           nbatch=gb, ret_rows=min(256, seq), ret_chunk=CHUNK, tm=min(256, gb * seq),
                           ranges=frame_ranges(g))
                   for g in range(nb // gb)]
        streams.append(_Stream(hs, tabs_s, state_retention[l], nbatch=db, ret_rows=dseq, ret_chunk=dseq,
                               tm=min(256, db * dseq), ranges=[(0, dseq)],
                               cache=(cache_mla_ckv[l], cache_mla_krope[l])))
        results = _layer(streams, w, lg, _pack_rows(peer_u[l]), _pack_rows(peer_v[l]), lnf, dims,
                         final_norm=last)
        hp, c1, k1, s1 = (jnp.concatenate(p, axis=0) for p in zip(*results[:-1]))
        hs, c2, k2, s2 = results[-1]
        for acc, val in zip(outs, (c1.reshape(nb, seq, -1), k1.reshape(nb, seq, -1), s1,
                                   c2.reshape(db, dseq, -1), k2.reshape(db, dseq, -1), s2)):
            acc.append(val)
    return (hp.reshape(nb, seq, d), hs.reshape(db, dseq, d), *[jnp.stack(o) for o in outs])
```
